```python
import math
import jax, jax.numpy as jnp
from jax import lax
import numpy as np

D_MODEL = 1024
BATCH = 16
SEQ = 2048
DEPTH = 1

D_MIX = D_MODEL
D_S5 = D_MIX // 2
S5_GROUP = 16
S5_GROUPS = D_S5 // S5_GROUP
S5_STATE = 64
D_GLA = D_MIX - D_S5
GLA_HEADS = 4
GLA_DK = D_GLA // 2 // GLA_HEADS
GLA_DV = D_GLA // GLA_HEADS
GLA_QK = GLA_HEADS * GLA_DK
GLA_LOWRANK = 16
GLA_TAU = 16.0
GLA_CHUNK = 64
D_IN_PROJ = D_S5 + 2 * GLA_QK + D_GLA + GLA_LOWRANK + D_GLA
N_EXPERTS = 256
TOP_K = 8
N_GROUPS = 8
TOPK_GROUPS = 4
D_EXPERT = 256
ROUTED_SCALE = 2.5
EXPERT_BLOCK = 128
EPS = 1e-6

kernel_name = 'hybrid_s5_gla_moe_block'


def rmsnorm(x, g):
    xf = x.astype(jnp.float32)
    r = lax.rsqrt(jnp.mean(xf * xf, axis=-1, keepdims=True) + EPS)
    return (xf * r).astype(x.dtype) * g


def modulate(h, shift, scale):
    return h * (1.0 + scale[:, None, :]) + shift[:, None, :]


def _cplx_scan_op(left, right):
    ar1, ai1, br1, bi1 = left
    ar2, ai2, br2, bi2 = right
    return (ar2 * ar1 - ai2 * ai1,
            ar2 * ai1 + ai2 * ar1,
            ar2 * br1 - ai2 * bi1 + br2,
            ar2 * bi1 + ai2 * br1 + bi2)


def s5_mixer(u, lam_re, lam_im, log_dt, b_re, b_im, c_re, c_im, d_skip, w_glu, b_glu, norm_g):
    bsz, L, _ = u.shape
    f32 = jnp.float32
    uf = u.astype(f32).reshape(bsz, L, S5_GROUPS, S5_GROUP)
    dt = jnp.exp(log_dt.astype(f32))[:, None]
    lr, li = lam_re.astype(f32), lam_im.astype(f32)
    mag = jnp.exp(lr * dt)
    ab_re, ab_im = mag * jnp.cos(li * dt), mag * jnp.sin(li * dt)
    den = lr * lr + li * li
    nr = ab_re - 1.0
    coef_re = ((nr * lr + ab_im * li) / den)[..., None]
    coef_im = ((ab_im * lr - nr * li) / den)[..., None]
    br, bi = b_re.astype(f32), b_im.astype(f32)
    bb_re = coef_re * br - coef_im * bi
    bb_im = coef_re * bi + coef_im * br
    bu_re = jnp.einsum('blgc,gnc->blgn', uf, bb_re)
    bu_im = jnp.einsum('blgc,gnc->blgn', uf, bb_im)
    a_re = jnp.broadcast_to(ab_re, bu_re.shape)
    a_im = jnp.broadcast_to(ab_im, bu_im.shape)
    _, _, h_re, h_im = lax.associative_scan(_cplx_scan_op, (a_re, a_im, bu_re, bu_im), axis=1)
    y = (jnp.einsum('blgn,gcn->blgc', h_re, c_re.astype(f32))
         - jnp.einsum('blgn,gcn->blgc', h_im, c_im.astype(f32))
         + d_skip.astype(f32) * uf)
    y = y.reshape(bsz, L, D_S5).astype(u.dtype)
    g = jax.nn.gelu(y)
    out = g * jax.nn.sigmoid(g @ w_glu + b_glu)
    return rmsnorm(out, norm_g)


def gla_mixer(q, k, v, glr, r, w_g2, b_g2, norm_g):
    bsz, L, _ = q.shape
    n, C = L // GLA_CHUNK, GLA_CHUNK
    f32 = jnp.float32
    qf = q.astype(f32).reshape(bsz, n, C, GLA_HEADS, GLA_DK) * (GLA_DK ** -0.5)
    kf = k.astype(f32).reshape(bsz, n, C, GLA_HEADS, GLA_DK)
    vf = v.astype(f32).reshape(bsz, n, C, GLA_HEADS, GLA_DV)
    log_a = jax.nn.log_sigmoid((glr @ w_g2 + b_g2).astype(f32)) / GLA_TAU
    log_a = log_a.reshape(bsz, n, C, GLA_HEADS, GLA_DK)
    bcum = jnp.cumsum(log_a, axis=2)
    b_last = bcum[:, :, -1]
    q_i = qf * jnp.exp(bcum)
    k_i = kf * jnp.exp(-bcum)
    mask = jnp.tril(jnp.ones((C, C), dtype=bool))
    scores = jnp.where(mask, jnp.einsum('bnchk,bnshk->bnhcs', q_i, k_i), 0.0)
    o_intra = jnp.einsum('bnhcs,bnshv->bnchv', scores, vf)
    k_dec = kf * jnp.exp(b_last[:, :, None] - bcum)
    kv = jnp.einsum('bnshk,bnshv->bnhkv', k_dec, vf)
    decay = jnp.exp(b_last)

    def step(S, inp):
        kv_n, dec_n = inp
        return dec_n[..., None] * S + kv_n, S

    S0 = jnp.zeros((bsz, GLA_HEADS, GLA_DK, GLA_DV), f32)
    _, S_prev = lax.scan(step, S0, (jnp.moveaxis(kv, 1, 0), jnp.moveaxis(decay, 1, 0)))
    S_prev = jnp.moveaxis(S_prev, 0, 1)
    o_inter = jnp.einsum('bnchk,bnhkv->bnchv', q_i, S_prev)
    o = (o_intra + o_inter).reshape(bsz, L, GLA_HEADS, GLA_DV)
    o = o * lax.rsqrt(jnp.mean(o * o, axis=-1, keepdims=True) + EPS)
    o = (o.astype(q.dtype) * norm_g).reshape(bsz, L, D_GLA)
    return o * jax.nn.silu(r)


def moe(h, w_router, e_bias, w_gate, w_up, w_down, ws_gate, ws_up, ws_down):
    bsz, L, D = h.shape
    T = bsz * L
    TK = T * TOP_K
    hf = h.reshape(T, D)
    scores = jax.nn.sigmoid((hf @ w_router).astype(jnp.float32))
    biased = scores + e_bias.astype(jnp.float32)
    per_group = N_EXPERTS // N_GROUPS
    grp_score = lax.top_k(biased.reshape(T, N_GROUPS, per_group), 2)[0].sum(-1)
    _, top_groups = lax.top_k(grp_score, TOPK_GROUPS)
    group_mask = jnp.any(top_groups[..., None] == jnp.arange(N_GROUPS)[None, None, :], axis=1)
    masked = jnp.where(jnp.repeat(group_mask, per_group, axis=-1), biased, -jnp.inf)
    _, top_e = lax.top_k(masked, TOP_K)
    sel = jnp.take_along_axis(scores, top_e, axis=-1)
    gates = sel / jnp.sum(sel, axis=-1, keepdims=True) * ROUTED_SCALE

    e_flat = top_e.reshape(TK).astype(jnp.int32)
    tok_flat = jnp.arange(TK, dtype=jnp.int32) // TOP_K
    g_flat = gates.reshape(TK)
    order = jnp.argsort(e_flat)
    e_sorted = e_flat[order]
    counts = jnp.zeros((N_EXPERTS,), jnp.int32).at[e_flat].add(1)
    padded = (counts + EXPERT_BLOCK - 1) // EXPERT_BLOCK * EXPERT_BLOCK
    starts = jnp.cumsum(counts) - counts
    pends = jnp.cumsum(padded)
    pstarts = pends - padded
    dest = pstarts[e_sorted] + (jnp.arange(TK, dtype=jnp.int32) - starts[e_sorted])
    n_blocks = (TK + N_EXPERTS * (EXPERT_BLOCK - 1) + EXPERT_BLOCK - 1) // EXPERT_BLOCK
    P = n_blocks * EXPERT_BLOCK
    slot_tok = jnp.full((P,), T, jnp.int32).at[dest].set(tok_flat[order])
    slot_gate = jnp.zeros((P,), h.dtype).at[dest].set(g_flat[order].astype(h.dtype))
    block_start = jnp.arange(n_blocks, dtype=jnp.int32) * EXPERT_BLOCK
    block_expert = jnp.minimum(jnp.searchsorted(pends, block_start, side='right'), N_EXPERTS - 1)
    x_pad = jnp.concatenate([hf, jnp.zeros((1, D), hf.dtype)], axis=0)

    def expert_block(args):
        idx, gb, e = args
        xb = x_pad[idx]
        hb = jax.nn.silu(xb @ w_gate[e]) * (xb @ w_up[e])
        return (hb @ w_down[e]) * gb[:, None]

    ys = lax.map(expert_block, (slot_tok.reshape(n_blocks, EXPERT_BLOCK),
                                slot_gate.reshape(n_blocks, EXPERT_BLOCK), block_expert))
    routed = jax.ops.segment_sum(ys.reshape(P, D), slot_tok, num_segments=T + 1)[:T]
    shared = (jax.nn.silu(hf @ ws_gate) * (hf @ ws_up)) @ ws_down
    return (routed + shared).reshape(bsz, L, D)


def setup_inputs(seed: int = 0) -> dict:
    key = jax.random.key(seed)
    ks = jax.random.split(key, 32)
    f32 = jnp.float32
    nrm = lambda k, shape, s: jax.random.normal(k, shape, f32) * s
    D, G, N = D_MODEL, S5_GROUPS, S5_STATE
    lam_im0 = jnp.pi * jnp.arange(N, dtype=f32)
    log_dt = jax.random.uniform(ks[6], (DEPTH, G), f32, math.log(0.001), math.log(0.1))
    return {
        'x': nrm(ks[0], (BATCH, SEQ, D), 1.0),
        'c': nrm(ks[1], (BATCH, D), 1.0),
        'w_ada': nrm(ks[2], (DEPTH, D, 6 * D), 0.5 * D ** -0.5),
        'b_ada': nrm(ks[3], (DEPTH, 6 * D), 0.02),
        'g_norm_mix': 1.0 + nrm(ks[4], (DEPTH, D), 0.02),
        'w_in': nrm(ks[5], (DEPTH, D, D_IN_PROJ), D ** -0.5),
        's5_lambda_re': -0.5 + nrm(ks[7], (DEPTH, G, N), 0.01),
        's5_lambda_im': lam_im0 + nrm(ks[8], (DEPTH, G, N), 0.01),
        's5_log_dt': log_dt,
        's5_b_re': nrm(ks[9], (DEPTH, G, N, S5_GROUP), (2 * S5_GROUP) ** -0.5),
        's5_b_im': nrm(ks[10], (DEPTH, G, N, S5_GROUP), (2 * S5_GROUP) ** -0.5),
        's5_c_re': nrm(ks[11], (DEPTH, G, S5_GROUP, N), N ** -0.5),
        's5_c_im': nrm(ks[12], (DEPTH, G, S5_GROUP, N), N ** -0.5),
        's5_d': nrm(ks[13], (DEPTH, G, S5_GROUP), 1.0),
        's5_w_glu': nrm(ks[14], (DEPTH, D_S5, D_S5), D_S5 ** -0.5),
        's5_b_glu': nrm(ks[15], (DEPTH, D_S5), 0.02),
        's5_norm_g': 1.0 + nrm(ks[16], (DEPTH, D_S5), 0.02),
        'gla_w_g2': nrm(ks[17], (DEPTH, GLA_LOWRANK, GLA_QK), GLA_LOWRANK ** -0.5),
        'gla_b_g2': nrm(ks[18], (DEPTH, GLA_QK), 0.02),
        'gla_norm_g': 1.0 + nrm(ks[19], (DEPTH, GLA_DV), 0.02),
        'w_out': nrm(ks[20], (DEPTH, D_MIX, D), D_MIX ** -0.5),
        'g_norm_moe': 1.0 + nrm(ks[21], (DEPTH, D), 0.02),
        'w_router': nrm(ks[22], (DEPTH, D, N_EXPERTS), D ** -0.5),
        'router_bias': nrm(ks[23], (DEPTH, N_EXPERTS), 0.01),
        'exp_w_gate': nrm(ks[24], (DEPTH, N_EXPERTS, D, D_EXPERT), D ** -0.5),
        'exp_w_up': nrm(ks[25], (DEPTH, N_EXPERTS, D, D_EXPERT), D ** -0.5),
        'exp_w_down': nrm(ks[26], (DEPTH, N_EXPERTS, D_EXPERT, D), D_EXPERT ** -0.5),
        'sh_w_gate': nrm(ks[27], (DEPTH, D, D_EXPERT), D ** -0.5),
        'sh_w_up': nrm(ks[28], (DEPTH, D, D_EXPERT), D ** -0.5),
        'sh_w_down': nrm(ks[29], (DEPTH, D_EXPERT, D), D_EXPERT ** -0.5),
        'g_final': 1.0 + nrm(ks[30], (D,), 0.02),
    }


def reference(x, c, w_ada, b_ada, g_norm_mix, w_in, s5_lambda_re, s5_lambda_im, s5_log_dt,
              s5_b_re, s5_b_im, s5_c_re, s5_c_im, s5_d, s5_w_glu, s5_b_glu, s5_norm_g,
              gla_w_g2, gla_b_g2, gla_norm_g, w_out, g_norm_moe, w_router, router_bias,
              exp_w_gate, exp_w_up, exp_w_down, sh_w_gate, sh_w_up, sh_w_down, g_final):
    splits = [D_S5, D_S5 + GLA_QK, D_S5 + 2 * GLA_QK, D_S5 + 2 * GLA_QK + D_GLA,
              D_S5 + 2 * GLA_QK + D_GLA + GLA_LOWRANK]
    for l in range(DEPTH):
        mod = jax.nn.silu(c) @ w_ada[l] + b_ada[l]
        sh1, sc1, gt1, sh2, sc2, gt2 = jnp.split(mod, 6, axis=-1)
        h = modulate(rmsnorm(x, g_norm_mix[l]), sh1, sc1)
        proj = h @ w_in[l]
        u, q, k, v, glr, r = jnp.split(proj, splits, axis=-1)
        y_s5 = s5_mixer(u, s5_lambda_re[l], s5_lambda_im[l], s5_log_dt[l], s5_b_re[l], s5_b_im[l],
                        s5_c_re[l], s5_c_im[l], s5_d[l], s5_w_glu[l], s5_b_glu[l], s5_norm_g[l])
        y_gla = gla_mixer(q, k, v, glr, r, gla_w_g2[l], gla_b_g2[l], gla_norm_g[l])
        mix = jnp.concatenate([y_s5, y_gla], axis=-1) @ w_out[l]
        x = x + gt1[:, None, :] * mix
        h2 = modulate(rmsnorm(x, g_norm_moe[l]), sh2, sc2)
        y_moe = moe(h2, w_router[l], router_bias[l], exp_w_gate[l], exp_w_up[l], exp_w_down[l],
                    sh_w_gate[l], sh_w_up[l], sh_w_down[l])
        x = x + gt2[:, None, :] * y_moe
    return rmsnorm(x, g_final)
```

```python
import functools

import jax
import jax.numpy as jnp
from jax import lax
from jax.experimental import pallas as pl
from jax.experimental.pallas import tpu as pltpu

F32 = jnp.float32
BF16 = jnp.bfloat16

D_MODEL = 1024
D_S5 = 512
S5_GROUP = 16
S5_GROUPS = 32
S5_STATE = 64
S5_CHUNK = 16
D_GLA = 512
GLA_HEADS = 4
GLA_DK = 64
GLA_DV = 128
GLA_QK = 256
GLA_LOWRANK = 16
GLA_TAU = 16.0
GLA_CHUNK = 64
LANE = 128
N_EXPERTS = 256
TOP_K = 8
N_GROUPS = 8
TOPK_GROUPS = 4
D_EXPERT = 256
ROUTED_SCALE = 2.5
EPS = 1e-6
MOE_BLOCK = 256
VMEM_LIMIT = 48 * 1024 * 1024


def _silu(x):
    return x * jax.nn.sigmoid(x)


def _params(*sem):
    return pltpu.CompilerParams(dimension_semantics=sem, vmem_limit_bytes=VMEM_LIMIT)


def _ada_kernel(c_ref, w_ref, b_ref, o_ref):
    s = _silu(c_ref[...]).astype(BF16)
    o_ref[...] = jnp.dot(s, w_ref[...].astype(BF16), preferred_element_type=F32) + b_ref[...]


def _ada(c, w, b):
    bsz, d = c.shape
    n = w.shape[1]
    tn = 1024
    return pl.pallas_call(
        _ada_kernel,
        grid=(n // tn,),
        in_specs=[pl.BlockSpec((bsz, d), lambda j: (0, 0)),
                  pl.BlockSpec((d, tn), lambda j: (0, j)),
                  pl.BlockSpec((1, tn), lambda j: (0, j))],
        out_specs=pl.BlockSpec((bsz, tn), lambda j: (0, j)),
        out_shape=jax.ShapeDtypeStruct((bsz, n), F32),
        compiler_params=_params("arbitrary"),
        name="ada",
    )(c, w, b.reshape(1, n))


def _inproj_kernel(x_ref, g_ref, sh_ref, sc_ref, w_ref,
                   u_ref, q_ref, k_ref, v_ref, r_ref, glr_ref):
    x = x_ref[0]
    ms = jnp.mean(x * x, axis=-1, keepdims=True)
    h = (x * lax.rsqrt(ms + EPS)) * g_ref[...]
    h = h * (1.0 + sc_ref[0]) + sh_ref[0]
    hb = h.astype(BF16)
    col = 0
    for ref in (u_ref, q_ref, k_ref, v_ref, r_ref, glr_ref):
        n = ref.shape[-1]
        ref[0] = jnp.dot(hb, w_ref[:, col:col + n], preferred_element_type=F32).astype(ref.dtype)
        col += n


def _inproj(x, g, sh, sc, w):
    bsz, L, d = x.shape
    tt = 512
    widths = (D_S5, GLA_QK, GLA_QK, D_GLA, D_GLA, LANE)
    tok = lambda n: pl.BlockSpec((1, tt, n), lambda b, i: (b, i, 0))
    vec = pl.BlockSpec((1, 1, d), lambda b, i: (b, 0, 0))
    return pl.pallas_call(
        _inproj_kernel,
        grid=(bsz, L // tt),
        in_specs=[tok(d), pl.BlockSpec((1, d), lambda b, i: (0, 0)), vec, vec,
                  pl.BlockSpec(w.shape, lambda b, i: (0, 0))],
        out_specs=[tok(n) for n in widths],
        out_shape=[jax.ShapeDtypeStruct((bsz, L, n), BF16) for n in widths],
        compiler_params=_params("arbitrary", "arbitrary"),
        name="inproj",
    )(x, g.reshape(1, d), sh, sc, w)


def _s5_prep(lam_re, lam_im, log_dt, b_re, b_im, c_re, c_im, d_skip):
    G, N, C, TC = S5_GROUPS, S5_STATE, S5_GROUP, S5_CHUNK
    hp = lax.Precision.HIGHEST
    dt = jnp.exp(log_dt)[:, None]
    lr, li = lam_re, lam_im
    mag = jnp.exp(lr * dt)
    ab_re, ab_im = mag * jnp.cos(li * dt), mag * jnp.sin(li * dt)
    den = lr * lr + li * li
    nr = ab_re - 1.0
    coef_re = ((nr * lr + ab_im * li) / den)[..., None]
    coef_im = ((ab_im * lr - nr * li) / den)[..., None]
    bb_re = coef_re * b_re - coef_im * b_im
    bb_im = coef_re * b_im + coef_im * b_re
    p = jnp.arange(TC + 1, dtype=F32)[:, None, None]
    pm = jnp.exp(lr * dt * p)
    pr, pi = pm * jnp.cos(li * dt * p), pm * jnp.sin(li * dt * p)
    ca_re = c_re[None] * pr[:, :, None, :] - c_im[None] * pi[:, :, None, :]
    ca_im = c_re[None] * pi[:, :, None, :] + c_im[None] * pr[:, :, None, :]
    kern = (jnp.einsum('tgon,gni->tgoi', ca_re, bb_re, precision=hp)
            - jnp.einsum('tgon,gni->tgoi', ca_im, bb_im, precision=hp))
    kern = kern.at[0].add(jnp.eye(C, dtype=F32)[None] * d_skip[:, :, None])
    s_idx = jnp.arange(TC)[:, None]
    t_idx = jnp.arange(TC)[None, :]
    lag = t_idx - s_idx
    toep = jnp.where((lag >= 0)[:, :, None, None, None], kern[jnp.maximum(lag, 0)], 0.0)
    toep = toep.transpose(2, 0, 4, 1, 3).reshape(G, TC * C, TC * C)
    rr, ri = pr[TC - 1 - jnp.arange(TC)], pi[TC - 1 - jnp.arange(TC)]
    binc_re = rr[..., None] * bb_re[None] - ri[..., None] * bb_im[None]
    binc_im = rr[..., None] * bb_im[None] + ri[..., None] * bb_re[None]
    binc_re = binc_re.transpose(1, 0, 3, 2).reshape(G, TC * C, N)
    binc_im = binc_im.transpose(1, 0, 3, 2).reshape(G, TC * C, N)
    cm_re = ca_re[1:].transpose(1, 3, 0, 2).reshape(G, N, TC * C)
    cm_im = (-ca_im[1:]).transpose(1, 3, 0, 2).reshape(G, N, TC * C)
    return (toep.astype(BF16), binc_re.astype(BF16), binc_im.astype(BF16),
            cm_re.astype(BF16), cm_im.astype(BF16), pr[TC], pi[TC])


def _s5_kernel(z_ref, toep_ref, bre_ref, bim_ref, cre_ref, cim_ref, ar_ref, ai_ref,
               y_ref, incr_ref, inci_ref, hsr_ref, hsi_ref, *, bsz, n_chunks):
    z = z_ref[0]
    incr_ref[...] = jnp.dot(z, bre_ref[0], preferred_element_type=F32)
    inci_ref[...] = jnp.dot(z, bim_ref[0], preferred_element_type=F32)
    ar = ar_ref[0]
    ai = ai_ref[0]

    def step(j, carry):
        hr, hi = carry
        rows = pl.ds(pl.multiple_of(j * bsz, bsz), bsz)
        hsr_ref[rows, :] = hr
        hsi_ref[rows, :] = hi
        return (ar * hr - ai * hi + incr_ref[rows, :], ar * hi + ai * hr + inci_ref[rows, :])

    zero = jnp.zeros((bsz, S5_STATE), F32)
    lax.fori_loop(0, n_chunks, step, (zero, zero), unroll=4)
    y = jnp.dot(z, toep_ref[0], preferred_element_type=F32)
    y += jnp.dot(hsr_ref[...].astype(BF16), cre_ref[0], preferred_element_type=F32)
    y += jnp.dot(hsi_ref[...].astype(BF16), cim_ref[0], preferred_element_type=F32)
    y_ref[0] = y.astype(y_ref.dtype)


def _s5_core(z, mats, bsz):
    toep, bre, bim, cre, cim, ar, ai = mats
    G, rows, w = z.shape
    n_chunks = rows // bsz
    N = S5_STATE
    arb = jnp.broadcast_to(ar[:, None, :], (G, bsz, N))
    aib = jnp.broadcast_to(ai[:, None, :], (G, bsz, N))
    grp = lambda a, b: pl.BlockSpec((1, a, b), lambda g: (g, 0, 0))
    return pl.pallas_call(
        functools.partial(_s5_kernel, bsz=bsz, n_chunks=n_chunks),
        grid=(G,),
        in_specs=[grp(rows, w), grp(w, w), grp(w, N), grp(w, N), grp(N, w), grp(N, w),
                  grp(bsz, N), grp(bsz, N)],
        out_specs=grp(rows, w),
        out_shape=jax.ShapeDtypeStruct((G, rows, w), BF16),
        scratch_shapes=[pltpu.VMEM((rows, N), F32)] * 4,
        compiler_params=_params("arbitrary"),
        name="s5_core",
    )(z, toep, bre, bim, cre, cim, arb, aib)


def _gla_kernel(q_ref, k_ref, v_ref, glr_ref, r_ref, wg_ref, bg_ref, ng_ref, o_ref, st_ref):
    lt = q_ref.shape[1]
    C = GLA_CHUNK

    @pl.when(pl.program_id(1) == 0)
    def _():
        st_ref[...] = jnp.zeros_like(st_ref)

    z = jnp.dot(glr_ref[0], wg_ref[...], preferred_element_type=F32) + bg_ref[...]
    log_a = (jnp.minimum(z, 0.0) - jnp.log(1.0 + jnp.exp(-jnp.abs(z)))) * (1.0 / GLA_TAU)
    ri = lax.broadcasted_iota(jnp.int32, (lt, lt), 0)
    ci = lax.broadcasted_iota(jnp.int32, (lt, lt), 1)
    tril = jnp.where(((ri >> 6) == (ci >> 6)) & (ci <= ri), 1.0, 0.0).astype(BF16)
    la_hi = log_a.astype(BF16)
    la_lo = (log_a - la_hi.astype(F32)).astype(BF16)
    bcum = (jnp.dot(tril, la_hi, preferred_element_type=F32)
            + jnp.dot(tril, la_lo, preferred_element_type=F32))
    q = q_ref[0].astype(F32) * (GLA_DK ** -0.5)
    k = k_ref[0].astype(F32)
    qi = q * jnp.exp(bcum)
    ki = k * jnp.exp(-bcum)
    lane_head = lax.broadcasted_iota(jnp.int32, (1, GLA_QK), 1) >> 6
    causal = ((lax.broadcasted_iota(jnp.int32, (GLA_HEADS * C, C), 0) & (C - 1))
              >= lax.broadcasted_iota(jnp.int32, (GLA_HEADS * C, C), 1))
    same_head = ((lax.broadcasted_iota(jnp.int32, (D_GLA, GLA_QK), 0) >> 7)
                 == (lax.broadcasted_iota(jnp.int32, (D_GLA, GLA_QK), 1) >> 6))
    nt = (((1,), (1,)), ((), ()))
    ng = ng_ref[...]
    for c in range(lt // C):
        sl = slice(c * C, (c + 1) * C)
        bc = bcum[sl]
        bl = bc[C - 1:C, :]
        kd = k[sl] * jnp.exp(bl - bc)
        qic = qi[sl]
        qs = jnp.concatenate([jnp.where(lane_head == h, qic, 0.0) for h in range(GLA_HEADS)],
                             axis=0).astype(BF16)
        sc = lax.dot_general(qs, ki[sl].astype(BF16), nt, preferred_element_type=F32)
        p = jnp.where(causal, sc, 0.0).astype(BF16)
        vc = v_ref[0, sl, :]
        o_intra = jnp.concatenate(
            [jnp.dot(p[h * C:(h + 1) * C], vc[:, h * GLA_DV:(h + 1) * GLA_DV], preferred_element_type=F32)
             for h in range(GLA_HEADS)], axis=1)
        st = st_ref[...]
        o_inter = lax.dot_general(qic.astype(BF16), st.astype(BF16), nt, preferred_element_type=F32)
        v_t = vc.astype(F32).T.astype(BF16)
        kv_t = jnp.dot(v_t, kd.astype(BF16), preferred_element_type=F32)
        st_ref[...] = st * jnp.exp(bl) + jnp.where(same_head, kv_t, 0.0)
        o = o_intra + o_inter
        parts = []
        for h in range(GLA_HEADS):
            oh = o[:, h * GLA_DV:(h + 1) * GLA_DV]
            oh = oh * lax.rsqrt(jnp.mean(oh * oh, axis=-1, keepdims=True) + EPS)
            parts.append(oh * ng)
        r = r_ref[0, sl, :].astype(F32)
        o_ref[0, sl, :] = (jnp.concatenate(parts, axis=1) * _silu(r)).astype(o_ref.dtype)


def _gla(q, k, v, glr, r, wg, bg, ng):
    bsz, L, _ = q.shape
    lt = 256
    tok = lambda n: pl.BlockSpec((1, lt, n), lambda b, i: (b, i, 0))
    full = lambda a: pl.BlockSpec(a.shape, lambda b, i: (0,) * a.ndim)
    return pl.pallas_call(
        _gla_kernel,
        grid=(bsz, L // lt),
        in_specs=[tok(GLA_QK), tok(GLA_QK), tok(D_GLA), tok(LANE), tok(D_GLA), full(wg), full(bg), full(ng)],
        out_specs=tok(D_GLA),
        out_shape=jax.ShapeDtypeStruct((bsz, L, D_GLA), BF16),
        scratch_shapes=[pltpu.VMEM((D_GLA, GLA_QK), F32)],
        compiler_params=_params("arbitrary", "arbitrary"),
        name="gla",
    )(q, k, v, glr, r, wg, bg, ng)


def _mix_kernel(ys_ref, yg_ref, x_ref, wglu_ref, bglu_ref, sng_ref, wo_ref, gt1_ref,
                gn_ref, sh2_ref, sc2_ref, gt2_ref, wr_ref, wsg_ref, wsu_ref, wsd_ref,
                x1_ref, h2_ref, lg_ref):
    y = ys_ref[0].astype(F32)
    g = y * (0.5 * (1.0 + jnp.tanh(0.7978845608028654 * (y + 0.044715 * (y * y * y)))))
    gate = jax.nn.sigmoid(jnp.dot(g.astype(BF16), wglu_ref[...], preferred_element_type=F32) + bglu_ref[...])
    out = g * gate
    out = out * lax.rsqrt(jnp.mean(out * out, axis=-1, keepdims=True) + EPS) * sng_ref[...]
    mix = (jnp.dot(out.astype(BF16), wo_ref[0:D_S5, :], preferred_element_type=F32)
           + jnp.dot(yg_ref[0], wo_ref[D_S5:, :], preferred_element_type=F32))
    x1 = x_ref[0] + gt1_ref[0] * mix
    h2 = x1 * lax.rsqrt(jnp.mean(x1 * x1, axis=-1, keepdims=True) + EPS) * gn_ref[...]
    h2 = h2 * (1.0 + sc2_ref[0]) + sh2_ref[0]
    hb = h2.astype(BF16)
    h2_ref[0] = hb
    lg_ref[0] = jnp.dot(hb, wr_ref[...], preferred_element_type=F32)
    a = _silu(jnp.dot(hb, wsg_ref[...], preferred_element_type=F32)) * jnp.dot(
        hb, wsu_ref[...], preferred_element_type=F32)
    shared = jnp.dot(a.astype(BF16), wsd_ref[...], preferred_element_type=F32)
    x1_ref[0] = x1 + gt2_ref[0] * shared


def _mix(ys, yg, x, wglu, bglu, sng, wo, gt1, gn, sh2, sc2, gt2, wr, wsg, wsu, wsd):
    bsz, L, d = x.shape
    tt = 512
    tok = lambda n: pl.BlockSpec((1, tt, n), lambda b, i: (b, i, 0))
    vec = pl.BlockSpec((1, 1, d), lambda b, i: (b, 0, 0))
    full = lambda a: pl.BlockSpec(a.shape, lambda b, i: (0,) * a.ndim)
    return pl.pallas_call(
        _mix_kernel,
        grid=(bsz, L // tt),
        in_specs=[tok(D_S5), tok(D_GLA), tok(d), full(wglu), full(bglu), full(sng), full(wo), vec,
                  full(gn), vec, vec, vec, full(wr), full(wsg), full(wsu), full(wsd)],
        out_specs=[tok(d), tok(d), tok(N_EXPERTS)],
        out_shape=[jax.ShapeDtypeStruct((bsz, L, d), F32),
                   jax.ShapeDtypeStruct((bsz, L, d), BF16),
                   jax.ShapeDtypeStruct((bsz, L, N_EXPERTS), F32)],
        compiler_params=_params("arbitrary", "arbitrary"),
        name="mix",
    )(ys, yg, x, wglu, bglu, sng, wo, gt1, gn, sh2, sc2, gt2, wr, wsg, wsu, wsd)


def _expert_kernel(be_ref, nv_ref, x_ref, gate_ref, wg_ref, wu_ref, wd_ref, y_ref):
    @pl.when(pl.program_id(0) < nv_ref[0])
    def _():
        x = x_ref[...]
        g = jnp.dot(x, wg_ref[0].astype(BF16), preferred_element_type=F32)
        u = jnp.dot(x, wu_ref[0].astype(BF16), preferred_element_type=F32)
        h = (_silu(g) * u).astype(BF16)
        y = jnp.dot(h, wd_ref[0].astype(BF16), preferred_element_type=F32) * gate_ref[...]
        y_ref[...] = y.astype(y_ref.dtype)


def _experts(block_expert, n_valid, xs, gates, wg, wu, wd):
    P, d = xs.shape
    n_blocks = P // MOE_BLOCK
    row = lambda i, be, nv: (jnp.minimum(i, nv[0] - 1), 0)
    wsel = lambda i, be, nv: (be[jnp.minimum(i, nv[0] - 1)], 0, 0)
    return pl.pallas_call(
        _expert_kernel,
        grid_spec=pltpu.PrefetchScalarGridSpec(
            num_scalar_prefetch=2,
            grid=(n_blocks,),
            in_specs=[pl.BlockSpec((MOE_BLOCK, d), row),
                      pl.BlockSpec((MOE_BLOCK, 1), row),
                      pl.BlockSpec((1, d, D_EXPERT), wsel),
                      pl.BlockSpec((1, d, D_EXPERT), wsel),
                      pl.BlockSpec((1, D_EXPERT, d), wsel)],
            out_specs=pl.BlockSpec((MOE_BLOCK, d), row)),
        out_shape=jax.ShapeDtypeStruct((P, d), BF16),
        compiler_params=_params("arbitrary"),
        name="experts",
    )(block_expert, n_valid, xs, gates, wg, wu, wd)


def _final_kernel(x_ref, y_ref, gt_ref, g_ref, o_ref):
    x = x_ref[0] + gt_ref[0] * y_ref[0].astype(F32)
    o_ref[0] = x * lax.rsqrt(jnp.mean(x * x, axis=-1, keepdims=True) + EPS) * g_ref[...]


def _final(x1, routed, gt2, g):
    bsz, L, d = x1.shape
    tt = 1024
    tok = pl.BlockSpec((1, tt, d), lambda b, i: (b, i, 0))
    return pl.pallas_call(
        _final_kernel,
        grid=(bsz, L // tt),
        in_specs=[tok, tok, pl.BlockSpec((1, 1, d), lambda b, i: (b, 0, 0)),
                  pl.BlockSpec((1, d), lambda b, i: (0, 0))],
        out_specs=tok,
        out_shape=jax.ShapeDtypeStruct((bsz, L, d), F32),
        compiler_params=_params("arbitrary", "arbitrary"),
        name="final",
    )(x1, routed, gt2, g.reshape(1, d))


def _route(logits, e_bias):
    T = logits.shape[0]
    scores = jax.nn.sigmoid(logits)
    biased = scores + e_bias
    per_group = N_EXPERTS // N_GROUPS
    grp_score = lax.top_k(biased.reshape(T, N_GROUPS, per_group), 2)[0].sum(-1)
    _, top_groups = lax.top_k(grp_score, TOPK_GROUPS)
    group_mask = jnp.any(top_groups[..., None] == jnp.arange(N_GROUPS)[None, None, :], axis=1)
    masked = jnp.where(jnp.repeat(group_mask, per_group, axis=-1), biased, -jnp.inf)
    _, top_e = lax.top_k(masked, TOP_K)
    sel = jnp.take_along_axis(scores, top_e, axis=-1)
    gates = sel / jnp.sum(sel, axis=-1, keepdims=True) * ROUTED_SCALE
    return top_e.astype(jnp.int32), gates


def _dispatch(top_e, gates):
    T = top_e.shape[0]
    TK = T * TOP_K
    e_flat = top_e.reshape(TK)
    order = jnp.argsort(e_flat)
    e_sorted = e_flat[order]
    counts = jnp.zeros((N_EXPERTS,), jnp.int32).at[e_flat].add(1)
    padded = (counts + MOE_BLOCK - 1) // MOE_BLOCK * MOE_BLOCK
    starts = jnp.cumsum(counts) - counts
    pends = jnp.cumsum(padded)
    pstarts = pends - padded
    dest = pstarts[e_sorted] + (jnp.arange(TK, dtype=jnp.int32) - starts[e_sorted])
    n_blocks = (TK + N_EXPERTS * (MOE_BLOCK - 1) + MOE_BLOCK - 1) // MOE_BLOCK
    P = n_blocks * MOE_BLOCK
    slot_tok = jnp.full((P,), T, jnp.int32).at[dest].set(order // TOP_K)
    slot_gate = jnp.zeros((P,), F32).at[dest].set(gates.reshape(TK)[order])
    pos = jnp.zeros((TK,), jnp.int32).at[order].set(dest).reshape(T, TOP_K)
    block_start = jnp.arange(n_blocks, dtype=jnp.int32) * MOE_BLOCK
    block_expert = jnp.minimum(jnp.searchsorted(pends, block_start, side='right'),
                               N_EXPERTS - 1).astype(jnp.int32)
    n_valid = (pends[-1] // MOE_BLOCK).astype(jnp.int32).reshape(1)
    return slot_tok, slot_gate, pos, block_expert, n_valid


def kernel(x, c, w_ada, b_ada, g_norm_mix, w_in, s5_lambda_re, s5_lambda_im, s5_log_dt, s5_b_re, s5_b_im, s5_c_re, s5_c_im, s5_d, s5_w_glu, s5_b_glu, s5_norm_g, gla_w_g2, gla_b_g2, gla_norm_g, w_out, g_norm_moe, w_router, router_bias, exp_w_gate, exp_w_up, exp_w_down, sh_w_gate, sh_w_up, sh_w_down, g_final):
    bsz, L, d = x.shape
    T = bsz * L
    assert w_ada.shape[0] == 1, "single-layer block"
    for l in range(1):
        mod = _ada(c, w_ada[l], b_ada[l])
        sh1, sc1, gt1, sh2, sc2, gt2 = [m.reshape(bsz, 1, d) for m in jnp.split(mod, 6, axis=-1)]

        wi = w_in[l]
        o_q, o_k, o_v, o_g, o_r = D_S5, D_S5 + GLA_QK, D_S5 + 2 * GLA_QK, D_S5 + 2 * GLA_QK + D_GLA, \
            D_S5 + 2 * GLA_QK + D_GLA + GLA_LOWRANK
        w_cat = jnp.concatenate([wi[:, :o_g], wi[:, o_r:], wi[:, o_g:o_r],
                                 jnp.zeros((d, LANE - GLA_LOWRANK), wi.dtype)], axis=1).astype(BF16)
        u, q, k, v, r, glr = _inproj(x, g_norm_mix[l], sh1, sc1, w_cat)

        n_chunks = L // S5_CHUNK
        mats = _s5_prep(s5_lambda_re[l], s5_lambda_im[l], s5_log_dt[l], s5_b_re[l], s5_b_im[l],
                        s5_c_re[l], s5_c_im[l], s5_d[l])
        z = u.reshape(bsz, n_chunks, S5_CHUNK, S5_GROUPS, S5_GROUP).transpose(3, 1, 0, 2, 4)
        z = z.reshape(S5_GROUPS, n_chunks * bsz, S5_CHUNK * S5_GROUP)
        ys = _s5_core(z, mats, bsz)
        ys = ys.reshape(S5_GROUPS, n_chunks, bsz, S5_CHUNK, S5_GROUP).transpose(2, 1, 3, 0, 4)
        ys = ys.reshape(bsz, L, D_S5)

        wg2 = jnp.concatenate([gla_w_g2[l], jnp.zeros((LANE - GLA_LOWRANK, GLA_QK), F32)], axis=0).astype(BF16)
        yg = _gla(q, k, v, glr, r, wg2, gla_b_g2[l].reshape(1, GLA_QK), gla_norm_g[l].reshape(1, GLA_DV))

        x1, h2, logits = _mix(
            ys, yg, x, s5_w_glu[l].astype(BF16), s5_b_glu[l].reshape(1, D_S5), s5_norm_g[l].reshape(1, D_S5),
            w_out[l].astype(BF16), gt1, g_norm_moe[l].reshape(1, d), sh2, sc2, gt2,
            w_router[l].astype(BF16), sh_w_gate[l].astype(BF16), sh_w_up[l].astype(BF16),
            sh_w_down[l].astype(BF16))

        top_e, gates = _route(logits.reshape(T, N_EXPERTS), router_bias[l])
        slot_tok, slot_gate, pos, block_expert, n_valid = _dispatch(top_e, gates)
        h2_pad = jnp.concatenate([h2.reshape(T, d), jnp.zeros((1, d), BF16)], axis=0)
        xs = h2_pad[slot_tok]
        ye = _experts(block_expert, n_valid, xs, slot_gate.reshape(-1, 1),
                      exp_w_gate[l], exp_w_up[l], exp_w_down[l])
        routed = jnp.sum(ye[pos].astype(F32), axis=1).reshape(bsz, L, d)
    return _final(x1, routed, gt2, g_final)
```

```python
import functools

import jax
import jax.numpy as jnp
from jax import lax
from jax.experimental import pallas as pl
from jax.experimental.pallas import tpu as pltpu

F32 = jnp.float32
BF16 = jnp.bfloat16

D_MODEL = 1024
D_S5 = 512
S5_GROUP = 16
S5_GROUPS = 32
S5_STATE = 64
S5_CHUNK = 16
D_GLA = 512
GLA_HEADS = 4
GLA_DK = 64
GLA_DV = 128
GLA_QK = 256
GLA_LOWRANK = 16
GLA_TAU = 16.0
GLA_CHUNK = 64
LANE = 128
N_EXPERTS = 256
TOP_K = 8
N_GROUPS = 8
TOPK_GROUPS = 4
D_EXPERT = 256
ROUTED_SCALE = 2.5
EPS = 1e-6
MOE_BLOCK = 256
ROW_CHUNKS = D_MODEL // 2 // LANE
VMEM_LIMIT = 48 * 1024 * 1024


def _silu(x):
    return x * jax.nn.sigmoid(x)


def _params(*sem):
    return pltpu.CompilerParams(dimension_semantics=sem, vmem_limit_bytes=VMEM_LIMIT)


def _ada_kernel(c_ref, w_ref, b_ref, o_ref):
    s = _silu(c_ref[...]).astype(BF16)
    o_ref[...] = jnp.dot(s, w_ref[...].astype(BF16), preferred_element_type=F32) + b_ref[...]


def _ada(c, w, b):
    bsz, d = c.shape
    n = w.shape[1]
    tn = 1024
    return pl.pallas_call(
        _ada_kernel,
        grid=(n // tn,),
        in_specs=[pl.BlockSpec((bsz, d), lambda j: (0, 0)),
                  pl.BlockSpec((d, tn), lambda j: (0, j)),
                  pl.BlockSpec((1, tn), lambda j: (0, j))],
        out_specs=pl.BlockSpec((bsz, tn), lambda j: (0, j)),
        out_shape=jax.ShapeDtypeStruct((bsz, n), F32),
        compiler_params=_params("arbitrary"),
        name="ada",
    )(c, w, b.reshape(1, n))


def _inproj_kernel(x_ref, g_ref, sh_ref, sc_ref, w_ref,
                   u_ref, q_ref, k_ref, v_ref, r_ref, glr_ref):
    x = x_ref[0]
    ms = jnp.mean(x * x, axis=-1, keepdims=True)
    h = (x * lax.rsqrt(ms + EPS)) * g_ref[...]
    h = h * (1.0 + sc_ref[0]) + sh_ref[0]
    hb = h.astype(BF16)
    col = 0
    for ref in (u_ref, q_ref, k_ref, v_ref, r_ref, glr_ref):
        n = ref.shape[-1]
        ref[0] = jnp.dot(hb, w_ref[:, col:col + n], preferred_element_type=F32).astype(ref.dtype)
        col += n


def _inproj(x, g, sh, sc, w):
    bsz, L, d = x.shape
    tt = 512
    widths = (D_S5, GLA_QK, GLA_QK, D_GLA, D_GLA, LANE)
    tok = lambda n: pl.BlockSpec((1, tt, n), lambda b, i: (b, i, 0))
    vec = pl.BlockSpec((1, 1, d), lambda b, i: (b, 0, 0))
    return pl.pallas_call(
        _inproj_kernel,
        grid=(bsz, L // tt),
        in_specs=[tok(d), pl.BlockSpec((1, d), lambda b, i: (0, 0)), vec, vec,
                  pl.BlockSpec(w.shape, lambda b, i: (0, 0))],
        out_specs=[tok(n) for n in widths],
        out_shape=[jax.ShapeDtypeStruct((bsz, L, n), BF16) for n in widths],
        compiler_params=_params("arbitrary", "arbitrary"),
        name="inproj",
    )(x, g.reshape(1, d), sh, sc, w)


def _s5_prep(lam_re, lam_im, log_dt, b_re, b_im, c_re, c_im, d_skip):
    G, N, C, TC = S5_GROUPS, S5_STATE, S5_GROUP, S5_CHUNK
    hp = lax.Precision.HIGHEST
    dt = jnp.exp(log_dt)[:, None]
    lr, li = lam_re, lam_im
    mag = jnp.exp(lr * dt)
    ab_re, ab_im = mag * jnp.cos(li * dt), mag * jnp.sin(li * dt)
    den = lr * lr + li * li
    nr = ab_re - 1.0
    coef_re = ((nr * lr + ab_im * li) / den)[..., None]
    coef_im = ((ab_im * lr - nr * li) / den)[..., None]
    bb_re = coef_re * b_re - coef_im * b_im
    bb_im = coef_re * b_im + coef_im * b_re
    p = jnp.arange(TC + 1, dtype=F32)[:, None, None]
    pm = jnp.exp(lr * dt * p)
    pr, pi = pm * jnp.cos(li * dt * p), pm * jnp.sin(li * dt * p)
    ca_re = c_re[None] * pr[:, :, None, :] - c_im[None] * pi[:, :, None, :]
    ca_im = c_re[None] * pi[:, :, None, :] + c_im[None] * pr[:, :, None, :]
    kern = (jnp.einsum('tgon,gni->tgoi', ca_re, bb_re, precision=hp)
            - jnp.einsum('tgon,gni->tgoi', ca_im, bb_im, precision=hp))
    kern = kern.at[0].add(jnp.eye(C, dtype=F32)[None] * d_skip[:, :, None])
    s_idx = jnp.arange(TC)[:, None]
    t_idx = jnp.arange(TC)[None, :]
    lag = t_idx - s_idx
    toep = jnp.where((lag >= 0)[:, :, None, None, None], kern[jnp.maximum(lag, 0)], 0.0)
    toep = toep.transpose(2, 0, 4, 1, 3).reshape(G, TC * C, TC * C)
    rr, ri = pr[TC - 1 - jnp.arange(TC)], pi[TC - 1 - jnp.arange(TC)]
    binc_re = rr[..., None] * bb_re[None] - ri[..., None] * bb_im[None]
    binc_im = rr[..., None] * bb_im[None] + ri[..., None] * bb_re[None]
    binc_re = binc_re.transpose(1, 0, 3, 2).reshape(G, TC * C, N)
    binc_im = binc_im.transpose(1, 0, 3, 2).reshape(G, TC * C, N)
    cm_re = ca_re[1:].transpose(1, 3, 0, 2).reshape(G, N, TC * C)
    cm_im = (-ca_im[1:]).transpose(1, 3, 0, 2).reshape(G, N, TC * C)
    return (toep.astype(BF16), binc_re.astype(BF16), binc_im.astype(BF16),
            cm_re.astype(BF16), cm_im.astype(BF16), pr[TC], pi[TC])


def _s5_kernel(z_ref, toep_ref, bre_ref, bim_ref, cre_ref, cim_ref, ar_ref, ai_ref,
               y_ref, incr_ref, inci_ref, hsr_ref, hsi_ref, *, bsz, n_chunks):
    z = z_ref[0]
    incr_ref[...] = jnp.dot(z, bre_ref[0], preferred_element_type=F32)
    inci_ref[...] = jnp.dot(z, bim_ref[0], preferred_element_type=F32)
    ar = ar_ref[0]
    ai = ai_ref[0]

    def step(j, carry):
        hr, hi = carry
        rows = pl.ds(pl.multiple_of(j * bsz, bsz), bsz)
        hsr_ref[rows, :] = hr
        hsi_ref[rows, :] = hi
        return (ar * hr - ai * hi + incr_ref[rows, :], ar * hi + ai * hr + inci_ref[rows, :])

    zero = jnp.zeros((bsz, S5_STATE), F32)
    lax.fori_loop(0, n_chunks, step, (zero, zero), unroll=4)
    y = jnp.dot(z, toep_ref[0], preferred_element_type=F32)
    y += jnp.dot(hsr_ref[...].astype(BF16), cre_ref[0], preferred_element_type=F32)
    y += jnp.dot(hsi_ref[...].astype(BF16), cim_ref[0], preferred_element_type=F32)
    y_ref[0] = y.astype(y_ref.dtype)


def _s5_core(z, mats, bsz):
    toep, bre, bim, cre, cim, ar, ai = mats
    G, rows, w = z.shape
    n_chunks = rows // bsz
    N = S5_STATE
    arb = jnp.broadcast_to(ar[:, None, :], (G, bsz, N))
    aib = jnp.broadcast_to(ai[:, None, :], (G, bsz, N))
    grp = lambda a, b: pl.BlockSpec((1, a, b), lambda g: (g, 0, 0))
    return pl.pallas_call(
        functools.partial(_s5_kernel, bsz=bsz, n_chunks=n_chunks),
        grid=(G,),
        in_specs=[grp(rows, w), grp(w, w), grp(w, N), grp(w, N), grp(N, w), grp(N, w),
                  grp(bsz, N), grp(bsz, N)],
        out_specs=grp(rows, w),
        out_shape=jax.ShapeDtypeStruct((G, rows, w), BF16),
        scratch_shapes=[pltpu.VMEM((rows, N), F32)] * 4,
        compiler_params=_params("arbitrary"),
        name="s5_core",
    )(z, toep, bre, bim, cre, cim, arb, aib)


def _gla_kernel(q_ref, k_ref, v_ref, glr_ref, r_ref, wg_ref, bg_ref, ng_ref, o_ref, st_ref):
    lt = q_ref.shape[1]
    C = GLA_CHUNK

    @pl.when(pl.program_id(1) == 0)
    def _():
        st_ref[...] = jnp.zeros_like(st_ref)

    z = jnp.dot(glr_ref[0], wg_ref[...], preferred_element_type=F32) + bg_ref[...]
    log_a = (jnp.minimum(z, 0.0) - jnp.log(1.0 + jnp.exp(-jnp.abs(z)))) * (1.0 / GLA_TAU)
    ri = lax.broadcasted_iota(jnp.int32, (lt, lt), 0)
    ci = lax.broadcasted_iota(jnp.int32, (lt, lt), 1)
    tril = jnp.where(((ri >> 6) == (ci >> 6)) & (ci <= ri), 1.0, 0.0).astype(BF16)
    la_hi = log_a.astype(BF16)
    la_lo = (log_a - la_hi.astype(F32)).astype(BF16)
    bcum = (jnp.dot(tril, la_hi, preferred_element_type=F32)
            + jnp.dot(tril, la_lo, preferred_element_type=F32))
    q = q_ref[0].astype(F32) * (GLA_DK ** -0.5)
    k = k_ref[0].astype(F32)
    qi = q * jnp.exp(bcum)
    ki = k * jnp.exp(-bcum)
    lane_head = lax.broadcasted_iota(jnp.int32, (1, GLA_QK), 1) >> 6
    causal = ((lax.broadcasted_iota(jnp.int32, (GLA_HEADS * C, C), 0) & (C - 1))
              >= lax.broadcasted_iota(jnp.int32, (GLA_HEADS * C, C), 1))
    same_head = ((lax.broadcasted_iota(jnp.int32, (D_GLA, GLA_QK), 0) >> 7)
                 == (lax.broadcasted_iota(jnp.int32, (D_GLA, GLA_QK), 1) >> 6))
    nt = (((1,), (1,)), ((), ()))
    ng = ng_ref[...]
    for c in range(lt // C):
        sl = slice(c * C, (c + 1) * C)
        bc = bcum[sl]
        bl = bc[C - 1:C, :]
        kd = k[sl] * jnp.exp(bl - bc)
        qic = qi[sl]
        qs = jnp.concatenate([jnp.where(lane_head == h, qic, 0.0) for h in range(GLA_HEADS)],
                             axis=0).astype(BF16)
        sc = lax.dot_general(qs, ki[sl].astype(BF16), nt, preferred_element_type=F32)
        p = jnp.where(causal, sc, 0.0).astype(BF16)
        vc = v_ref[0, sl, :]
        o_intra = jnp.concatenate(
            [jnp.dot(p[h * C:(h + 1) * C], vc[:, h * GLA_DV:(h + 1) * GLA_DV], preferred_element_type=F32)
             for h in range(GLA_HEADS)], axis=1)
        st = st_ref[...]
        o_inter = lax.dot_general(qic.astype(BF16), st.astype(BF16), nt, preferred_element_type=F32)
        v_t = vc.astype(F32).T.astype(BF16)
        kv_t = jnp.dot(v_t, kd.astype(BF16), preferred_element_type=F32)
        st_ref[...] = st * jnp.exp(bl) + jnp.where(same_head, kv_t, 0.0)
        o = o_intra + o_inter
        parts = []
        for h in range(GLA_HEADS):
            oh = o[:, h * GLA_DV:(h + 1) * GLA_DV]
            oh = oh * lax.rsqrt(jnp.mean(oh * oh, axis=-1, keepdims=True) + EPS)
            parts.append(oh * ng)
        r = r_ref[0, sl, :].astype(F32)
        o_ref[0, sl, :] = (jnp.concatenate(parts, axis=1) * _silu(r)).astype(o_ref.dtype)


def _gla(q, k, v, glr, r, wg, bg, ng):
    bsz, L, _ = q.shape
    lt = 256
    tok = lambda n: pl.BlockSpec((1, lt, n), lambda b, i: (b, i, 0))
    full = lambda a: pl.BlockSpec(a.shape, lambda b, i: (0,) * a.ndim)
    return pl.pallas_call(
        _gla_kernel,
        grid=(bsz, L // lt),
        in_specs=[tok(GLA_QK), tok(GLA_QK), tok(D_GLA), tok(LANE), tok(D_GLA), full(wg), full(bg), full(ng)],
        out_specs=tok(D_GLA),
        out_shape=jax.ShapeDtypeStruct((bsz, L, D_GLA), BF16),
        scratch_shapes=[pltpu.VMEM((D_GLA, GLA_QK), F32)],
        compiler_params=_params("arbitrary", "arbitrary"),
        name="gla",
    )(q, k, v, glr, r, wg, bg, ng)


def _pack_bf16_pairs(x):
    w = x.shape[1] // 2
    xr = x.astype(BF16).astype(F32)
    lo = lax.bitcast_convert_type(xr[:, :w], jnp.uint32) >> 16
    hi = lax.bitcast_convert_type(xr[:, w:], jnp.uint32) & jnp.uint32(0xFFFF0000)
    return lo | hi


def _unpack_bf16_pairs(p):
    lo = lax.bitcast_convert_type(p << 16, F32)
    hi = lax.bitcast_convert_type(p & jnp.uint32(0xFFFF0000), F32)
    return lo, hi


def _store_rows(ref, packed):
    n = packed.shape[0]
    for c in range(ROW_CHUNKS):
        ref[pl.ds(c, n, stride=ROW_CHUNKS), :] = packed[:, c * LANE:(c + 1) * LANE]


def _load_rows(ref, n):
    return jnp.concatenate([ref[pl.ds(c, n, stride=ROW_CHUNKS), :] for c in range(ROW_CHUNKS)], axis=1)


def _route_tile(lg_t, bias_col, tri, carry_ref):
    n_e, tt = lg_t.shape
    per_group = n_e // N_GROUPS
    scores = jax.nn.sigmoid(lg_t)
    biased = scores + bias_col
    row = lax.broadcasted_iota(jnp.int32, (n_e, tt), 0)
    neg = -jnp.inf
    group_score = []
    for g in range(N_GROUPS):
        b = biased[g * per_group:(g + 1) * per_group]
        r = lax.broadcasted_iota(jnp.int32, (per_group, tt), 0) + g * per_group
        m1 = jnp.max(b, axis=0, keepdims=True)
        i1 = jnp.min(jnp.where(b == m1, r, n_e), axis=0, keepdims=True)
        m2 = jnp.max(jnp.where(r == i1, neg, b), axis=0, keepdims=True)
        group_score.append(m1 + m2)
    parts = []
    for g in range(N_GROUPS):
        ahead = jnp.zeros((1, tt), jnp.int32)
        for g2 in range(N_GROUPS):
            if g2 != g:
                beats = (group_score[g2] >= group_score[g]) if g2 < g else (group_score[g2] > group_score[g])
                ahead = ahead + beats.astype(jnp.int32)
        parts.append(jnp.where(ahead < TOPK_GROUPS, biased[g * per_group:(g + 1) * per_group], neg))
    masked = jnp.concatenate(parts, axis=0)
    work = masked
    idxs = []
    for _ in range(TOP_K):
        m = jnp.max(work, axis=0, keepdims=True)
        ii = jnp.min(jnp.where(work == m, row, n_e), axis=0, keepdims=True)
        idxs.append(ii)
        work = jnp.where(row == ii, neg, work)
    sel = work != masked
    w = jnp.where(sel, scores, 0.0)
    gate_dense = w / jnp.sum(w, axis=0, keepdims=True) * ROUTED_SCALE
    mt = jnp.where(sel, 1.0, 0.0)
    rank_dense = jnp.dot(mt.astype(BF16), tri, preferred_element_type=F32) + carry_ref[...]
    carry_ref[...] += jnp.sum(mt, axis=1, keepdims=True)
    ranks, gts = [], []
    for ii in idxs:
        oh = row == ii
        ranks.append(jnp.sum(jnp.where(oh, rank_dense, 0.0), axis=0, keepdims=True))
        gts.append(jnp.sum(jnp.where(oh, gate_dense, 0.0), axis=0, keepdims=True))
    idx_t = jnp.concatenate(idxs, axis=0)
    rank_t = jnp.concatenate(ranks, axis=0).astype(jnp.int32)
    gate_t = jnp.concatenate(gts + [jnp.zeros((LANE - TOP_K, tt), F32)], axis=0)
    return idx_t, rank_t, gate_t.T[:, :TOP_K]


def _mix_kernel(ys_ref, yg_ref, x_ref, wglu_ref, bglu_ref, sng_ref, wo_ref, gt1_ref,
                gn_ref, sh2_ref, sc2_ref, gt2_ref, wrt_ref, rb_ref, tri_ref, wsg_ref, wsu_ref, wsd_ref,
                x1_ref, h2_ref, idx_ref, rank_ref, gate_ref, cnt_ref, carry_ref):
    @pl.when((pl.program_id(0) == 0) & (pl.program_id(1) == 0))
    def _():
        carry_ref[...] = jnp.zeros_like(carry_ref)

    y = ys_ref[0].astype(F32)
    g = y * (0.5 * (1.0 + jnp.tanh(0.7978845608028654 * (y + 0.044715 * (y * y * y)))))
    gate = jax.nn.sigmoid(jnp.dot(g.astype(BF16), wglu_ref[...], preferred_element_type=F32) + bglu_ref[...])
    out = g * gate
    out = out * lax.rsqrt(jnp.mean(out * out, axis=-1, keepdims=True) + EPS) * sng_ref[...]
    mix = (jnp.dot(out.astype(BF16), wo_ref[0:D_S5, :], preferred_element_type=F32)
           + jnp.dot(yg_ref[0], wo_ref[D_S5:, :], preferred_element_type=F32))
    x1 = x_ref[0] + gt1_ref[0] * mix
    h2 = x1 * lax.rsqrt(jnp.mean(x1 * x1, axis=-1, keepdims=True) + EPS) * gn_ref[...]
    h2 = h2 * (1.0 + sc2_ref[0]) + sh2_ref[0]
    hb = h2.astype(BF16)
    _store_rows(h2_ref, _pack_bf16_pairs(h2))
    lg_t = lax.dot_general(wrt_ref[...], hb, (((1,), (1,)), ((), ())), preferred_element_type=F32)
    idx_t, rank_t, gates = _route_tile(lg_t, rb_ref[...], tri_ref[...], carry_ref)
    idx_ref[...] = idx_t
    rank_ref[...] = rank_t
    gate_ref[...] = gates
    cnt_ref[...] = carry_ref[...]
    a = _silu(jnp.dot(hb, wsg_ref[...], preferred_element_type=F32)) * jnp.dot(
        hb, wsu_ref[...], preferred_element_type=F32)
    shared = jnp.dot(a.astype(BF16), wsd_ref[...], preferred_element_type=F32)
    x1_ref[0] = x1 + gt2_ref[0] * shared


def _mix(ys, yg, x, wglu, bglu, sng, wo, gt1, gn, sh2, sc2, gt2, wrt, rb, wsg, wsu, wsd):
    bsz, L, d = x.shape
    tt = 512
    nt = L // tt
    T = bsz * L
    tri = (jnp.arange(tt)[:, None] < jnp.arange(tt)[None, :]).astype(BF16)
    tok = lambda n: pl.BlockSpec((1, tt, n), lambda b, i: (b, i, 0))
    vec = pl.BlockSpec((1, 1, d), lambda b, i: (b, 0, 0))
    full = lambda a: pl.BlockSpec(a.shape, lambda b, i: (0,) * a.ndim)
    lanes = pl.BlockSpec((TOP_K, tt), lambda b, i: (0, b * nt + i))
    return pl.pallas_call(
        _mix_kernel,
        grid=(bsz, nt),
        in_specs=[tok(D_S5), tok(D_GLA), tok(d), full(wglu), full(bglu), full(sng), full(wo), vec,
                  full(gn), vec, vec, vec, full(wrt), full(rb), full(tri), full(wsg), full(wsu), full(wsd)],
        out_specs=[tok(d),
                   pl.BlockSpec((tt * ROW_CHUNKS, LANE), lambda b, i: (b * nt + i, 0)),
                   lanes, lanes,
                   pl.BlockSpec((tt, TOP_K), lambda b, i: (b * nt + i, 0)),
                   pl.BlockSpec((N_EXPERTS, 1), lambda b, i: (0, 0))],
        out_shape=[jax.ShapeDtypeStruct((bsz, L, d), F32),
                   jax.ShapeDtypeStruct((T * ROW_CHUNKS, LANE), jnp.uint32),
                   jax.ShapeDtypeStruct((TOP_K, T), jnp.int32),
                   jax.ShapeDtypeStruct((TOP_K, T), jnp.int32),
                   jax.ShapeDtypeStruct((T, TOP_K), F32),
                   jax.ShapeDtypeStruct((N_EXPERTS, 1), F32)],
        scratch_shapes=[pltpu.VMEM((N_EXPERTS, 1), F32)],
        compiler_params=_params("arbitrary", "arbitrary"),
        name="mix",
    )(ys, yg, x, wglu, bglu, sng, wo, gt1, gn, sh2, sc2, gt2, wrt, rb, tri, wsg, wsu, wsd)


def _pos_kernel(idx_ref, rank_ref, ps_ref, pos_ref):
    n_e = ps_ref.shape[0]
    tt = idx_ref.shape[1]
    row = lax.broadcasted_iota(jnp.int32, (n_e, tt), 0)
    ps = ps_ref[...]
    starts = [jnp.sum(jnp.where(row == idx_ref[k:k + 1, :], ps, 0.0), axis=0, keepdims=True)
              for k in range(TOP_K)]
    pos_ref[...] = jnp.concatenate(starts, axis=0).astype(jnp.int32) + rank_ref[...]


def _pos(idx_t, rank_t, pstart):
    T = idx_t.shape[1]
    tt = 2048
    blk = pl.BlockSpec((TOP_K, tt), lambda i: (0, i))
    return pl.pallas_call(
        _pos_kernel,
        grid=(T // tt,),
        in_specs=[blk, blk, pl.BlockSpec((N_EXPERTS, 1), lambda i: (0, 0))],
        out_specs=blk,
        out_shape=jax.ShapeDtypeStruct((TOP_K, T), jnp.int32),
        compiler_params=_params("arbitrary"),
        name="pos",
    )(idx_t, rank_t, pstart.astype(F32).reshape(N_EXPERTS, 1))


def _dispatch_kernel(pend_ref, padded_ref, pos_ref, h_hbm, xs_hbm, zero_ref, pos_smem, sem_pos, sem_zero, sem_row):
    tt = pos_ref.shape[1]
    step = pl.program_id(0)
    load_pos = pltpu.make_async_copy(pos_ref, pos_smem, sem_pos)
    load_pos.start()

    def zero_copy(e):
        start = pl.multiple_of(pend_ref[e] - MOE_BLOCK, MOE_BLOCK)
        return pltpu.make_async_copy(zero_ref, xs_hbm.at[pl.ds(start, MOE_BLOCK)], sem_zero)

    @pl.when(step == 0)
    def _():
        zero_ref[...] = jnp.zeros_like(zero_ref)

        def start_zero(e, c):
            @pl.when(padded_ref[e] > 0)
            def _():
                zero_copy(e).start()
            return c

        def wait_zero(e, c):
            @pl.when(padded_ref[e] > 0)
            def _():
                zero_copy(e).wait()
            return c

        lax.fori_loop(0, N_EXPERTS, start_zero, 0)
        lax.fori_loop(0, N_EXPERTS, wait_zero, 0)

    load_pos.wait()
    base = step * tt

    def issue(j, c):
        for k in range(TOP_K):
            pltpu.make_async_copy(h_hbm.at[base + j], xs_hbm.at[pos_smem[k, j]], sem_row).start()
        return c

    lax.fori_loop(0, tt, issue, 0, unroll=4)
    for k in range(TOP_K):
        pltpu.make_async_copy(h_hbm.at[pl.ds(0, tt)], xs_hbm.at[pl.ds(0, tt)], sem_row).wait()


def _dispatch(pends, padded, pos_t, h_rows, n_slots):
    T = h_rows.shape[0]
    tt = 512
    return pl.pallas_call(
        _dispatch_kernel,
        grid_spec=pltpu.PrefetchScalarGridSpec(
            num_scalar_prefetch=2,
            grid=(T // tt,),
            in_specs=[pl.BlockSpec((TOP_K, tt), lambda i, pe, pa: (0, i)),
                      pl.BlockSpec(memory_space=pl.ANY)],
            out_specs=pl.BlockSpec(memory_space=pl.ANY),
            scratch_shapes=[pltpu.VMEM((MOE_BLOCK, ROW_CHUNKS, LANE), jnp.uint32),
                            pltpu.SMEM((TOP_K, tt), jnp.int32),
                            pltpu.SemaphoreType.DMA, pltpu.SemaphoreType.DMA, pltpu.SemaphoreType.DMA]),
        out_shape=jax.ShapeDtypeStruct((n_slots, ROW_CHUNKS, LANE), jnp.uint32),
        compiler_params=_params("arbitrary"),
        name="dispatch",
    )(pends, padded, pos_t, h_rows)


def _expert_kernel(be_ref, nv_ref, x_ref, wg_ref, wu_ref, wd_ref, y_ref):
    @pl.when(pl.program_id(0) < nv_ref[0])
    def _():
        lo, hi = _unpack_bf16_pairs(_load_rows(x_ref, MOE_BLOCK))
        x = jnp.concatenate([lo, hi], axis=1).astype(BF16)
        g = jnp.dot(x, wg_ref[0].astype(BF16), preferred_element_type=F32)
        u = jnp.dot(x, wu_ref[0].astype(BF16), preferred_element_type=F32)
        h = (_silu(g) * u).astype(BF16)
        y = jnp.dot(h, wd_ref[0].astype(BF16), preferred_element_type=F32)
        _store_rows(y_ref, _pack_bf16_pairs(y))


def _experts(block_expert, n_valid, xs, wg, wu, wd):
    n_blocks = xs.shape[0] // (MOE_BLOCK * ROW_CHUNKS)
    d = wg.shape[1]
    row = lambda i, be, nv: (jnp.minimum(i, nv[0] - 1), 0)
    wsel = lambda i, be, nv: (be[jnp.minimum(i, nv[0] - 1)], 0, 0)
    return pl.pallas_call(
        _expert_kernel,
        grid_spec=pltpu.PrefetchScalarGridSpec(
            num_scalar_prefetch=2,
            grid=(n_blocks,),
            in_specs=[pl.BlockSpec((MOE_BLOCK * ROW_CHUNKS, LANE), row),
                      pl.BlockSpec((1, d, D_EXPERT), wsel),
                      pl.BlockSpec((1, d, D_EXPERT), wsel),
                      pl.BlockSpec((1, D_EXPERT, d), wsel)],
            out_specs=pl.BlockSpec((MOE_BLOCK * ROW_CHUNKS, LANE), row)),
        out_shape=jax.ShapeDtypeStruct(xs.shape, jnp.uint32),
        compiler_params=_params("arbitrary"),
        name="experts",
    )(block_expert, n_valid, xs, wg, wu, wd)


def _final_kernel(pos_ref, gate_ref, x_ref, gt_ref, g_ref, ye_hbm, o_ref, buf_ref, pos_smem, sem_pos, sem_row):
    tt = pos_ref.shape[1]
    load_pos = pltpu.make_async_copy(pos_ref, pos_smem, sem_pos)
    load_pos.start()
    load_pos.wait()

    def issue(j, c):
        for k in range(TOP_K):
            pltpu.make_async_copy(ye_hbm.at[pos_smem[k, j]], buf_ref.at[k * tt + j], sem_row).start()
        return c

    lax.fori_loop(0, tt, issue, 0, unroll=4)
    for k in range(TOP_K):
        pltpu.make_async_copy(ye_hbm.at[pl.ds(0, tt)], buf_ref.at[pl.ds(k * tt, tt)], sem_row).wait()
    gates = gate_ref[...]
    half = D_MODEL // 2
    acc_lo = jnp.zeros((tt, half), F32)
    acc_hi = jnp.zeros((tt, half), F32)
    for k in range(TOP_K):
        packed = jnp.concatenate([buf_ref[pl.ds(k * tt, tt), c, :] for c in range(ROW_CHUNKS)], axis=1)
        lo, hi = _unpack_bf16_pairs(packed)
        gk = gates[:, k:k + 1]
        acc_lo += gk * lo
        acc_hi += gk * hi
    x = x_ref[0] + gt_ref[0] * jnp.concatenate([acc_lo, acc_hi], axis=1)
    o_ref[0] = x * lax.rsqrt(jnp.mean(x * x, axis=-1, keepdims=True) + EPS) * g_ref[...]


def _final(pos_t, gates, x1, gt2, g, ye_rows):
    bsz, L, d = x1.shape
    tt = 256
    nt = L // tt
    tok = pl.BlockSpec((1, tt, d), lambda b, i: (b, i, 0))
    return pl.pallas_call(
        _final_kernel,
        grid=(bsz, nt),
        in_specs=[pl.BlockSpec((TOP_K, tt), lambda b, i: (0, b * nt + i)),
                  pl.BlockSpec((tt, TOP_K), lambda b, i: (b * nt + i, 0)),
                  tok, pl.BlockSpec((1, 1, d), lambda b, i: (b, 0, 0)),
                  pl.BlockSpec((1, d), lambda b, i: (0, 0)),
                  pl.BlockSpec(memory_space=pl.ANY)],
        out_specs=tok,
        out_shape=jax.ShapeDtypeStruct((bsz, L, d), F32),
        scratch_shapes=[pltpu.VMEM((TOP_K * tt, ROW_CHUNKS, LANE), jnp.uint32),
                        pltpu.SMEM((TOP_K, tt), jnp.int32),
                        pltpu.SemaphoreType.DMA, pltpu.SemaphoreType.DMA],
        compiler_params=_params("arbitrary", "arbitrary"),
        name="final",
    )(pos_t, gates, x1, gt2, g.reshape(1, d), ye_rows)


def _block_table(counts, n_blocks):
    counts = counts.astype(jnp.int32)
    padded = (counts + MOE_BLOCK - 1) // MOE_BLOCK * MOE_BLOCK
    pends = jnp.cumsum(padded)
    pstarts = pends - padded
    block_start = jnp.arange(n_blocks, dtype=jnp.int32) * MOE_BLOCK
    block_expert = jnp.minimum(jnp.searchsorted(pends, block_start, side='right'),
                               N_EXPERTS - 1).astype(jnp.int32)
    n_valid = (pends[-1] // MOE_BLOCK).astype(jnp.int32).reshape(1)
    return padded, pends, pstarts, block_expert, n_valid


def kernel(x, c, w_ada, b_ada, g_norm_mix, w_in, s5_lambda_re, s5_lambda_im, s5_log_dt, s5_b_re, s5_b_im, s5_c_re, s5_c_im, s5_d, s5_w_glu, s5_b_glu, s5_norm_g, gla_w_g2, gla_b_g2, gla_norm_g, w_out, g_norm_moe, w_router, router_bias, exp_w_gate, exp_w_up, exp_w_down, sh_w_gate, sh_w_up, sh_w_down, g_final):
    bsz, L, d = x.shape
    T = bsz * L
    assert w_ada.shape[0] == 1, "single-layer block"
    for l in range(1):
        mod = _ada(c, w_ada[l], b_ada[l])
        sh1, sc1, gt1, sh2, sc2, gt2 = [m.reshape(bsz, 1, d) for m in jnp.split(mod, 6, axis=-1)]

        wi = w_in[l]
        o_q, o_k, o_v, o_g, o_r = D_S5, D_S5 + GLA_QK, D_S5 + 2 * GLA_QK, D_S5 + 2 * GLA_QK + D_GLA, \
            D_S5 + 2 * GLA_QK + D_GLA + GLA_LOWRANK
        w_cat = jnp.concatenate([wi[:, :o_g], wi[:, o_r:], wi[:, o_g:o_r],
                                 jnp.zeros((d, LANE - GLA_LOWRANK), wi.dtype)], axis=1).astype(BF16)
        u, q, k, v, r, glr = _inproj(x, g_norm_mix[l], sh1, sc1, w_cat)

        n_chunks = L // S5_CHUNK
        mats = _s5_prep(s5_lambda_re[l], s5_lambda_im[l], s5_log_dt[l], s5_b_re[l], s5_b_im[l],
                        s5_c_re[l], s5_c_im[l], s5_d[l])
        z = u.reshape(bsz, n_chunks, S5_CHUNK, S5_GROUPS, S5_GROUP).transpose(3, 1, 0, 2, 4)
        z = z.reshape(S5_GROUPS, n_chunks * bsz, S5_CHUNK * S5_GROUP)
        ys = _s5_core(z, mats, bsz)
        ys = ys.reshape(S5_GROUPS, n_chunks, bsz, S5_CHUNK, S5_GROUP).transpose(2, 1, 3, 0, 4)
        ys = ys.reshape(bsz, L, D_S5)

        wg2 = jnp.concatenate([gla_w_g2[l], jnp.zeros((LANE - GLA_LOWRANK, GLA_QK), F32)], axis=0).astype(BF16)
        yg = _gla(q, k, v, glr, r, wg2, gla_b_g2[l].reshape(1, GLA_QK), gla_norm_g[l].reshape(1, GLA_DV))

        x1, h2p, idx_t, rank_t, gates, counts = _mix(
            ys, yg, x, s5_w_glu[l].astype(BF16), s5_b_glu[l].reshape(1, D_S5), s5_norm_g[l].reshape(1, D_S5),
            w_out[l].astype(BF16), gt1, g_norm_moe[l].reshape(1, d), sh2, sc2, gt2,
            w_router[l].T.astype(BF16), router_bias[l].reshape(N_EXPERTS, 1),
            sh_w_gate[l].astype(BF16), sh_w_up[l].astype(BF16), sh_w_down[l].astype(BF16))

        n_blocks = (T * TOP_K + N_EXPERTS * (MOE_BLOCK - 1) + MOE_BLOCK - 1) // MOE_BLOCK
        n_slots = n_blocks * MOE_BLOCK
        padded, pends, pstarts, block_expert, n_valid = _block_table(counts[:, 0], n_blocks)
        pos_t = _pos(idx_t, rank_t, pstarts)
        xs = _dispatch(pends, padded, pos_t, h2p.reshape(T, ROW_CHUNKS, LANE), n_slots)
        ye = _experts(block_expert, n_valid, xs.reshape(n_slots * ROW_CHUNKS, LANE),
                      exp_w_gate[l], exp_w_up[l], exp_w_down[l])
    return _final(pos_t, gates, x1, gt2, g_final, ye.reshape(n_slots, ROW_CHUNKS, LANE))
```

```python
import functools

import jax
import jax.numpy as jnp
from jax import lax
from jax.experimental import pallas as pl
from jax.experimental.pallas import tpu as pltpu

F32 = jnp.float32
BF16 = jnp.bfloat16

D_MODEL = 1024
D_S5 = 512
S5_GROUP = 16
S5_GROUPS = 32
S5_STATE = 64
S5_CHUNK = 16
D_GLA = 512
GLA_HEADS = 4
GLA_DK = 64
GLA_DV = 128
GLA_QK = 256
GLA_LOWRANK = 16
GLA_TAU = 16.0
GLA_CHUNK = 64
LANE = 128
N_EXPERTS = 256
TOP_K = 8
N_GROUPS = 8
TOPK_GROUPS = 4
D_EXPERT = 256
ROUTED_SCALE = 2.5
EPS = 1e-6
MOE_BLOCK = 256
ROW_CHUNKS = D_MODEL // 2 // LANE
VMEM_LIMIT = 48 * 1024 * 1024


def _silu(x):
    return x * jax.nn.sigmoid(x)


def _params(*sem):
    return pltpu.CompilerParams(dimension_semantics=sem, vmem_limit_bytes=VMEM_LIMIT)


def _ada_kernel(c_ref, w_ref, b_ref, o_ref):
    s = _silu(c_ref[...]).astype(BF16)
    o_ref[...] = jnp.dot(s, w_ref[...].astype(BF16), preferred_element_type=F32) + b_ref[...]


def _ada(c, w, b):
    bsz, d = c.shape
    n = w.shape[1]
    tn = 1024
    return pl.pallas_call(
        _ada_kernel,
        grid=(n // tn,),
        in_specs=[pl.BlockSpec((bsz, d), lambda j: (0, 0)),
                  pl.BlockSpec((d, tn), lambda j: (0, j)),
                  pl.BlockSpec((1, tn), lambda j: (0, j))],
        out_specs=pl.BlockSpec((bsz, tn), lambda j: (0, j)),
        out_shape=jax.ShapeDtypeStruct((bsz, n), F32),
        compiler_params=_params("arbitrary"),
        name="ada",
    )(c, w, b.reshape(1, n))


def _inproj_kernel(x_ref, g_ref, sh_ref, sc_ref, w_ref,
                   u_ref, q_ref, k_ref, v_ref, r_ref, glr_ref):
    x = x_ref[0]
    ms = jnp.mean(x * x, axis=-1, keepdims=True)
    h = (x * lax.rsqrt(ms + EPS)) * g_ref[...]
    h = h * (1.0 + sc_ref[0]) + sh_ref[0]
    hb = h.astype(BF16)
    col = 0
    for ref in (u_ref, q_ref, k_ref, v_ref, r_ref, glr_ref):
        n = ref.shape[-1]
        ref[0] = jnp.dot(hb, w_ref[:, col:col + n], preferred_element_type=F32).astype(ref.dtype)
        col += n


def _inproj(x, g, sh, sc, w):
    bsz, L, d = x.shape
    tt = 512
    widths = (D_S5, GLA_QK, GLA_QK, D_GLA, D_GLA, LANE)
    tok = lambda n: pl.BlockSpec((1, tt, n), lambda b, i: (b, i, 0))
    vec = pl.BlockSpec((1, 1, d), lambda b, i: (b, 0, 0))
    return pl.pallas_call(
        _inproj_kernel,
        grid=(bsz, L // tt),
        in_specs=[tok(d), pl.BlockSpec((1, d), lambda b, i: (0, 0)), vec, vec,
                  pl.BlockSpec(w.shape, lambda b, i: (0, 0))],
        out_specs=[tok(n) for n in widths],
        out_shape=[jax.ShapeDtypeStruct((bsz, L, n), BF16) for n in widths],
        compiler_params=_params("arbitrary", "arbitrary"),
        name="inproj",
    )(x, g.reshape(1, d), sh, sc, w)


def _s5_prep(lam_re, lam_im, log_dt, b_re, b_im, c_re, c_im, d_skip):
    G, N, C, TC = S5_GROUPS, S5_STATE, S5_GROUP, S5_CHUNK
    hp = lax.Precision.HIGHEST
    dt = jnp.exp(log_dt)[:, None]
    lr, li = lam_re, lam_im
    mag = jnp.exp(lr * dt)
    ab_re, ab_im = mag * jnp.cos(li * dt), mag * jnp.sin(li * dt)
    den = lr * lr + li * li
    nr = ab_re - 1.0
    coef_re = ((nr * lr + ab_im * li) / den)[..., None]
    coef_im = ((ab_im * lr - nr * li) / den)[..., None]
    bb_re = coef_re * b_re - coef_im * b_im
    bb_im = coef_re * b_im + coef_im * b_re
    p = jnp.arange(TC + 1, dtype=F32)[:, None, None]
    pm = jnp.exp(lr * dt * p)
    pr, pi = pm * jnp.cos(li * dt * p), pm * jnp.sin(li * dt * p)
    ca_re = c_re[None] * pr[:, :, None, :] - c_im[None] * pi[:, :, None, :]
    ca_im = c_re[None] * pi[:, :, None, :] + c_im[None] * pr[:, :, None, :]
    kern = (jnp.einsum('tgon,gni->tgoi', ca_re, bb_re, precision=hp)
            - jnp.einsum('tgon,gni->tgoi', ca_im, bb_im, precision=hp))
    kern = kern.at[0].add(jnp.eye(C, dtype=F32)[None] * d_skip[:, :, None])
    s_idx = jnp.arange(TC)[:, None]
    t_idx = jnp.arange(TC)[None, :]
    lag = t_idx - s_idx
    toep = jnp.where((lag >= 0)[:, :, None, None, None], kern[jnp.maximum(lag, 0)], 0.0)
    toep = toep.transpose(2, 0, 4, 1, 3).reshape(G, TC * C, TC * C)
    rr, ri = pr[TC - 1 - jnp.arange(TC)], pi[TC - 1 - jnp.arange(TC)]
    binc_re = rr[..., None] * bb_re[None] - ri[..., None] * bb_im[None]
    binc_im = rr[..., None] * bb_im[None] + ri[..., None] * bb_re[None]
    binc_re = binc_re.transpose(1, 0, 3, 2).reshape(G, TC * C, N)
    binc_im = binc_im.transpose(1, 0, 3, 2).reshape(G, TC * C, N)
    cm_re = ca_re[1:].transpose(1, 3, 0, 2).reshape(G, N, TC * C)
    cm_im = (-ca_im[1:]).transpose(1, 3, 0, 2).reshape(G, N, TC * C)
    return (toep.astype(BF16), binc_re.astype(BF16), binc_im.astype(BF16),
            cm_re.astype(BF16), cm_im.astype(BF16), pr[TC], pi[TC])


def _s5_kernel(z_ref, toep_ref, bre_ref, bim_ref, cre_ref, cim_ref, ar_ref, ai_ref,
               y_ref, incr_ref, inci_ref, hsr_ref, hsi_ref, *, bsz, n_chunks):
    z = z_ref[0]
    incr_ref[...] = jnp.dot(z, bre_ref[0], preferred_element_type=F32)
    inci_ref[...] = jnp.dot(z, bim_ref[0], preferred_element_type=F32)
    ar = ar_ref[0]
    ai = ai_ref[0]

    def step(j, carry):
        hr, hi = carry
        rows = pl.ds(pl.multiple_of(j * bsz, bsz), bsz)
        hsr_ref[rows, :] = hr
        hsi_ref[rows, :] = hi
        return (ar * hr - ai * hi + incr_ref[rows, :], ar * hi + ai * hr + inci_ref[rows, :])

    zero = jnp.zeros((bsz, S5_STATE), F32)
    lax.fori_loop(0, n_chunks, step, (zero, zero), unroll=4)
    y = jnp.dot(z, toep_ref[0], preferred_element_type=F32)
    y += jnp.dot(hsr_ref[...].astype(BF16), cre_ref[0], preferred_element_type=F32)
    y += jnp.dot(hsi_ref[...].astype(BF16), cim_ref[0], preferred_element_type=F32)
    y_ref[0] = y.astype(y_ref.dtype)


def _s5_core(z, mats, bsz):
    toep, bre, bim, cre, cim, ar, ai = mats
    G, rows, w = z.shape
    n_chunks = rows // bsz
    N = S5_STATE
    arb = jnp.broadcast_to(ar[:, None, :], (G, bsz, N))
    aib = jnp.broadcast_to(ai[:, None, :], (G, bsz, N))
    grp = lambda a, b: pl.BlockSpec((1, a, b), lambda g: (g, 0, 0))
    return pl.pallas_call(
        functools.partial(_s5_kernel, bsz=bsz, n_chunks=n_chunks),
        grid=(G,),
        in_specs=[grp(rows, w), grp(w, w), grp(w, N), grp(w, N), grp(N, w), grp(N, w),
                  grp(bsz, N), grp(bsz, N)],
        out_specs=grp(rows, w),
        out_shape=jax.ShapeDtypeStruct((G, rows, w), BF16),
        scratch_shapes=[pltpu.VMEM((rows, N), F32)] * 4,
        compiler_params=_params("arbitrary"),
        name="s5_core",
    )(z, toep, bre, bim, cre, cim, arb, aib)


def _gla_kernel(q_ref, k_ref, v_ref, glr_ref, r_ref, wg_ref, bg_ref, ng_ref, o_ref, st_ref):
    lt = q_ref.shape[1]
    C = GLA_CHUNK

    @pl.when(pl.program_id(1) == 0)
    def _():
        st_ref[...] = jnp.zeros_like(st_ref)

    z = jnp.dot(glr_ref[0], wg_ref[...], preferred_element_type=F32) + bg_ref[...]
    log_a = (jnp.minimum(z, 0.0) - jnp.log(1.0 + jnp.exp(-jnp.abs(z)))) * (1.0 / GLA_TAU)
    ri = lax.broadcasted_iota(jnp.int32, (lt, lt), 0)
    ci = lax.broadcasted_iota(jnp.int32, (lt, lt), 1)
    tril = jnp.where(((ri >> 6) == (ci >> 6)) & (ci <= ri), 1.0, 0.0).astype(BF16)
    la_hi = log_a.astype(BF16)
    la_lo = (log_a - la_hi.astype(F32)).astype(BF16)
    bcum = (jnp.dot(tril, la_hi, preferred_element_type=F32)
            + jnp.dot(tril, la_lo, preferred_element_type=F32))
    q = q_ref[0].astype(F32) * (GLA_DK ** -0.5)
    k = k_ref[0].astype(F32)
    qi = q * jnp.exp(bcum)
    ki = k * jnp.exp(-bcum)
    lane_head = lax.broadcasted_iota(jnp.int32, (1, GLA_QK), 1) >> 6
    causal = ((lax.broadcasted_iota(jnp.int32, (GLA_HEADS * C, C), 0) & (C - 1))
              >= lax.broadcasted_iota(jnp.int32, (GLA_HEADS * C, C), 1))
    same_head = ((lax.broadcasted_iota(jnp.int32, (D_GLA, GLA_QK), 0) >> 7)
                 == (lax.broadcasted_iota(jnp.int32, (D_GLA, GLA_QK), 1) >> 6))
    nt = (((1,), (1,)), ((), ()))
    ng = ng_ref[...]
    for c in range(lt // C):
        sl = slice(c * C, (c + 1) * C)
        bc = bcum[sl]
        bl = bc[C - 1:C, :]
        kd = k[sl] * jnp.exp(bl - bc)
        qic = qi[sl]
        qs = jnp.concatenate([jnp.where(lane_head == h, qic, 0.0) for h in range(GLA_HEADS)],
                             axis=0).astype(BF16)
        sc = lax.dot_general(qs, ki[sl].astype(BF16), nt, preferred_element_type=F32)
        p = jnp.where(causal, sc, 0.0).astype(BF16)
        vc = v_ref[0, sl, :]
        o_intra = jnp.concatenate(
            [jnp.dot(p[h * C:(h + 1) * C], vc[:, h * GLA_DV:(h + 1) * GLA_DV], preferred_element_type=F32)
             for h in range(GLA_HEADS)], axis=1)
        st = st_ref[...]
        o_inter = lax.dot_general(qic.astype(BF16), st.astype(BF16), nt, preferred_element_type=F32)
        v_t = vc.astype(F32).T.astype(BF16)
        kv_t = jnp.dot(v_t, kd.astype(BF16), preferred_element_type=F32)
        st_ref[...] = st * jnp.exp(bl) + jnp.where(same_head, kv_t, 0.0)
        o = o_intra + o_inter
        parts = []
        for h in range(GLA_HEADS):
            oh = o[:, h * GLA_DV:(h + 1) * GLA_DV]
            oh = oh * lax.rsqrt(jnp.mean(oh * oh, axis=-1, keepdims=True) + EPS)
            parts.append(oh * ng)
        r = r_ref[0, sl, :].astype(F32)
        o_ref[0, sl, :] = (jnp.concatenate(parts, axis=1) * _silu(r)).astype(o_ref.dtype)


def _gla(q, k, v, glr, r, wg, bg, ng):
    bsz, L, _ = q.shape
    lt = 256
    tok = lambda n: pl.BlockSpec((1, lt, n), lambda b, i: (b, i, 0))
    full = lambda a: pl.BlockSpec(a.shape, lambda b, i: (0,) * a.ndim)
    return pl.pallas_call(
        _gla_kernel,
        grid=(bsz, L // lt),
        in_specs=[tok(GLA_QK), tok(GLA_QK), tok(D_GLA), tok(LANE), tok(D_GLA), full(wg), full(bg), full(ng)],
        out_specs=tok(D_GLA),
        out_shape=jax.ShapeDtypeStruct((bsz, L, D_GLA), BF16),
        scratch_shapes=[pltpu.VMEM((D_GLA, GLA_QK), F32)],
        compiler_params=_params("arbitrary", "arbitrary"),
        name="gla",
    )(q, k, v, glr, r, wg, bg, ng)


def _pack_bf16_pairs(x):
    w = x.shape[1] // 2
    xr = x.astype(BF16).astype(F32)
    lo = lax.bitcast_convert_type(xr[:, :w], jnp.uint32) >> 16
    hi = lax.bitcast_convert_type(xr[:, w:], jnp.uint32) & jnp.uint32(0xFFFF0000)
    return lo | hi


def _unpack_bf16_pairs(p):
    lo = lax.bitcast_convert_type(p << 16, F32)
    hi = lax.bitcast_convert_type(p & jnp.uint32(0xFFFF0000), F32)
    return lo, hi


def _store_rows(ref, packed):
    n = packed.shape[0]
    for c in range(ROW_CHUNKS):
        ref[pl.ds(c, n, stride=ROW_CHUNKS), :] = packed[:, c * LANE:(c + 1) * LANE]


def _load_rows(ref, n):
    return jnp.concatenate([ref[pl.ds(c, n, stride=ROW_CHUNKS), :] for c in range(ROW_CHUNKS)], axis=1)


def _route_tile(lg_t, bias_col, tri, carry_ref):
    n_e, tt = lg_t.shape
    per_group = n_e // N_GROUPS
    scores = jax.nn.sigmoid(lg_t)
    biased = scores + bias_col
    row = lax.broadcasted_iota(jnp.int32, (n_e, tt), 0)
    neg = -jnp.inf
    group_score = []
    for g in range(N_GROUPS):
        b = biased[g * per_group:(g + 1) * per_group]
        r = lax.broadcasted_iota(jnp.int32, (per_group, tt), 0) + g * per_group
        m1 = jnp.max(b, axis=0, keepdims=True)
        i1 = jnp.min(jnp.where(b == m1, r, n_e), axis=0, keepdims=True)
        m2 = jnp.max(jnp.where(r == i1, neg, b), axis=0, keepdims=True)
        group_score.append(m1 + m2)
    parts = []
    for g in range(N_GROUPS):
        ahead = jnp.zeros((1, tt), jnp.int32)
        for g2 in range(N_GROUPS):
            if g2 != g:
                beats = (group_score[g2] >= group_score[g]) if g2 < g else (group_score[g2] > group_score[g])
                ahead = ahead + beats.astype(jnp.int32)
        parts.append(jnp.where(ahead < TOPK_GROUPS, biased[g * per_group:(g + 1) * per_group], neg))
    masked = jnp.concatenate(parts, axis=0)
    work = masked
    idxs = []
    for _ in range(TOP_K):
        m = jnp.max(work, axis=0, keepdims=True)
        ii = jnp.min(jnp.where(work == m, row, n_e), axis=0, keepdims=True)
        idxs.append(ii)
        work = jnp.where(row == ii, neg, work)
    sel = work != masked
    w = jnp.where(sel, scores, 0.0)
    gate_dense = w / jnp.sum(w, axis=0, keepdims=True) * ROUTED_SCALE
    mt = jnp.where(sel, 1.0, 0.0)
    rank_dense = jnp.dot(mt.astype(BF16), tri, preferred_element_type=F32) + carry_ref[...]
    carry_ref[...] += jnp.sum(mt, axis=1, keepdims=True)
    ranks, gts = [], []
    for ii in idxs:
        oh = row == ii
        ranks.append(jnp.sum(jnp.where(oh, rank_dense, 0.0), axis=0, keepdims=True))
        gts.append(jnp.sum(jnp.where(oh, gate_dense, 0.0), axis=0, keepdims=True))
    idx_t = jnp.concatenate(idxs, axis=0)
    rank_t = jnp.concatenate(ranks, axis=0).astype(jnp.int32)
    gate_t = jnp.concatenate(gts + [jnp.zeros((LANE - TOP_K, tt), F32)], axis=0)
    return idx_t, rank_t, gate_t.T[:, :TOP_K]


def _mix_kernel(ys_ref, yg_ref, x_ref, wglu_ref, bglu_ref, sng_ref, wo_ref, gt1_ref,
                gn_ref, sh2_ref, sc2_ref, gt2_ref, wrt_ref, rb_ref, tri_ref, wsg_ref, wsu_ref, wsd_ref,
                x1_ref, h2_ref, idx_ref, rank_ref, gate_ref, cnt_ref, carry_ref):
    @pl.when((pl.program_id(0) == 0) & (pl.program_id(1) == 0))
    def _():
        carry_ref[...] = jnp.zeros_like(carry_ref)

    y = ys_ref[0].astype(F32)
    g = y * (0.5 * (1.0 + jnp.tanh(0.7978845608028654 * (y + 0.044715 * (y * y * y)))))
    gate = jax.nn.sigmoid(jnp.dot(g.astype(BF16), wglu_ref[...], preferred_element_type=F32) + bglu_ref[...])
    out = g * gate
    out = out * lax.rsqrt(jnp.mean(out * out, axis=-1, keepdims=True) + EPS) * sng_ref[...]
    mix = (jnp.dot(out.astype(BF16), wo_ref[0:D_S5, :], preferred_element_type=F32)
           + jnp.dot(yg_ref[0], wo_ref[D_S5:, :], preferred_element_type=F32))
    x1 = x_ref[0] + gt1_ref[0] * mix
    h2 = x1 * lax.rsqrt(jnp.mean(x1 * x1, axis=-1, keepdims=True) + EPS) * gn_ref[...]
    h2 = h2 * (1.0 + sc2_ref[0]) + sh2_ref[0]
    hb = h2.astype(BF16)
    _store_rows(h2_ref, _pack_bf16_pairs(h2))
    lg_t = lax.dot_general(wrt_ref[...], hb, (((1,), (1,)), ((), ())), preferred_element_type=F32)
    idx_t, rank_t, gates = _route_tile(lg_t, rb_ref[...], tri_ref[...], carry_ref)
    idx_ref[...] = idx_t
    rank_ref[...] = rank_t
    gate_ref[...] = gates
    cnt_ref[...] = carry_ref[...]
    a = _silu(jnp.dot(hb, wsg_ref[...], preferred_element_type=F32)) * jnp.dot(
        hb, wsu_ref[...], preferred_element_type=F32)
    shared = jnp.dot(a.astype(BF16), wsd_ref[...], preferred_element_type=F32)
    x1_ref[0] = x1 + gt2_ref[0] * shared


def _mix(ys, yg, x, wglu, bglu, sng, wo, gt1, gn, sh2, sc2, gt2, wrt, rb, wsg, wsu, wsd):
    bsz, L, d = x.shape
    tt = 512
    nt = L // tt
    T = bsz * L
    tri = (jnp.arange(tt)[:, None] < jnp.arange(tt)[None, :]).astype(BF16)
    tok = lambda n: pl.BlockSpec((1, tt, n), lambda b, i: (b, i, 0))
    vec = pl.BlockSpec((1, 1, d), lambda b, i: (b, 0, 0))
    full = lambda a: pl.BlockSpec(a.shape, lambda b, i: (0,) * a.ndim)
    lanes = pl.BlockSpec((TOP_K, tt), lambda b, i: (0, b * nt + i))
    return pl.pallas_call(
        _mix_kernel,
        grid=(bsz, nt),
        in_specs=[tok(D_S5), tok(D_GLA), tok(d), full(wglu), full(bglu), full(sng), full(wo), vec,
                  full(gn), vec, vec, vec, full(wrt), full(rb), full(tri), full(wsg), full(wsu), full(wsd)],
        out_specs=[tok(d),
                   pl.BlockSpec((tt * ROW_CHUNKS, LANE), lambda b, i: (b * nt + i, 0)),
                   lanes, lanes,
                   pl.BlockSpec((tt, TOP_K), lambda b, i: (b * nt + i, 0)),
                   pl.BlockSpec((N_EXPERTS, 1), lambda b, i: (0, 0))],
        out_shape=[jax.ShapeDtypeStruct((bsz, L, d), F32),
                   jax.ShapeDtypeStruct((T * ROW_CHUNKS, LANE), jnp.uint32),
                   jax.ShapeDtypeStruct((TOP_K, T), jnp.int32),
                   jax.ShapeDtypeStruct((TOP_K, T), jnp.int32),
                   jax.ShapeDtypeStruct((T, TOP_K), F32),
                   jax.ShapeDtypeStruct((N_EXPERTS, 1), F32)],
        scratch_shapes=[pltpu.VMEM((N_EXPERTS, 1), F32)],
        compiler_params=_params("arbitrary", "arbitrary"),
        name="mix",
    )(ys, yg, x, wglu, bglu, sng, wo, gt1, gn, sh2, sc2, gt2, wrt, rb, tri, wsg, wsu, wsd)


def _pos_kernel(idx_ref, rank_ref, ps_ref, pos_ref):
    n_e = ps_ref.shape[0]
    tt = idx_ref.shape[1]
    row = lax.broadcasted_iota(jnp.int32, (n_e, tt), 0)
    ps = ps_ref[...]
    starts = [jnp.sum(jnp.where(row == idx_ref[k:k + 1, :], ps, 0.0), axis=0, keepdims=True)
              for k in range(TOP_K)]
    pos_ref[...] = jnp.concatenate(starts, axis=0).astype(jnp.int32) + rank_ref[...]


def _pos(idx_t, rank_t, pstart):
    T = idx_t.shape[1]
    tt = 2048
    blk = pl.BlockSpec((TOP_K, tt), lambda i: (0, i))
    return pl.pallas_call(
        _pos_kernel,
        grid=(T // tt,),
        in_specs=[blk, blk, pl.BlockSpec((N_EXPERTS, 1), lambda i: (0, 0))],
        out_specs=blk,
        out_shape=jax.ShapeDtypeStruct((TOP_K, T), jnp.int32),
        compiler_params=_params("arbitrary"),
        name="pos",
    )(idx_t, rank_t, pstart.astype(F32).reshape(N_EXPERTS, 1))


def _dispatch_kernel(pend_ref, padded_ref, pos_ref, h_ref, xs_hbm, zero_ref, pos_smem, sem_pos, sem_zero, sem_row):
    tt = pos_ref.shape[1]
    step = pl.program_id(0)
    load_pos = pltpu.make_async_copy(pos_ref, pos_smem, sem_pos)
    load_pos.start()

    def zero_copy(e):
        start = pl.multiple_of(pend_ref[e] - MOE_BLOCK, MOE_BLOCK)
        return pltpu.make_async_copy(zero_ref, xs_hbm.at[pl.ds(start, MOE_BLOCK)], sem_zero)

    @pl.when(step == 0)
    def _():
        zero_ref[...] = jnp.zeros_like(zero_ref)

        def start_zero(e, c):
            @pl.when(padded_ref[e] > 0)
            def _():
                zero_copy(e).start()
            return c

        def wait_zero(e, c):
            @pl.when(padded_ref[e] > 0)
            def _():
                zero_copy(e).wait()
            return c

        lax.fori_loop(0, N_EXPERTS, start_zero, 0)
        lax.fori_loop(0, N_EXPERTS, wait_zero, 0)

    load_pos.wait()

    def issue(j, c):
        for k in range(TOP_K):
            pltpu.make_async_copy(h_ref.at[j], xs_hbm.at[pos_smem[k, j]], sem_row).start(priority=k % 2)
        return c

    lax.fori_loop(0, tt, issue, 0, unroll=4)
    for k in range(TOP_K):
        pltpu.make_async_copy(h_ref, xs_hbm.at[pl.ds(0, tt)], sem_row).wait()


def _dispatch(pends, padded, pos_t, h_rows, n_slots):
    T = h_rows.shape[0]
    tt = 512
    return pl.pallas_call(
        _dispatch_kernel,
        grid_spec=pltpu.PrefetchScalarGridSpec(
            num_scalar_prefetch=2,
            grid=(T // tt,),
            in_specs=[pl.BlockSpec((TOP_K, tt), lambda i, pe, pa: (0, i)),
                      pl.BlockSpec((tt, ROW_CHUNKS, LANE), lambda i, pe, pa: (i, 0, 0))],
            out_specs=pl.BlockSpec(memory_space=pl.ANY),
            scratch_shapes=[pltpu.VMEM((MOE_BLOCK, ROW_CHUNKS, LANE), jnp.uint32),
                            pltpu.SMEM((TOP_K, tt), jnp.int32),
                            pltpu.SemaphoreType.DMA, pltpu.SemaphoreType.DMA, pltpu.SemaphoreType.DMA]),
        out_shape=jax.ShapeDtypeStruct((n_slots, ROW_CHUNKS, LANE), jnp.uint32),
        compiler_params=_params("arbitrary"),
        name="dispatch",
    )(pends, padded, pos_t, h_rows)


def _expert_kernel(be_ref, nv_ref, x_ref, wg_ref, wu_ref, wd_ref, y_ref):
    @pl.when(pl.program_id(0) < nv_ref[0])
    def _():
        lo, hi = _unpack_bf16_pairs(_load_rows(x_ref, MOE_BLOCK))
        x = jnp.concatenate([lo, hi], axis=1).astype(BF16)
        g = jnp.dot(x, wg_ref[0].astype(BF16), preferred_element_type=F32)
        u = jnp.dot(x, wu_ref[0].astype(BF16), preferred_element_type=F32)
        h = (_silu(g) * u).astype(BF16)
        y = jnp.dot(h, wd_ref[0].astype(BF16), preferred_element_type=F32)
        _store_rows(y_ref, _pack_bf16_pairs(y))


def _experts(block_expert, n_valid, xs, wg, wu, wd):
    n_blocks = xs.shape[0] // (MOE_BLOCK * ROW_CHUNKS)
    d = wg.shape[1]
    row = lambda i, be, nv: (jnp.minimum(i, nv[0] - 1), 0)
    wsel = lambda i, be, nv: (be[jnp.minimum(i, nv[0] - 1)], 0, 0)
    return pl.pallas_call(
        _expert_kernel,
        grid_spec=pltpu.PrefetchScalarGridSpec(
            num_scalar_prefetch=2,
            grid=(n_blocks,),
            in_specs=[pl.BlockSpec((MOE_BLOCK * ROW_CHUNKS, LANE), row),
                      pl.BlockSpec((1, d, D_EXPERT), wsel),
                      pl.BlockSpec((1, d, D_EXPERT), wsel),
                      pl.BlockSpec((1, D_EXPERT, d), wsel)],
            out_specs=pl.BlockSpec((MOE_BLOCK * ROW_CHUNKS, LANE), row)),
        out_shape=jax.ShapeDtypeStruct(xs.shape, jnp.uint32),
        compiler_params=_params("arbitrary"),
        name="experts",
    )(block_expert, n_valid, xs, wg, wu, wd)


def _final_kernel(pos_ref, gate_ref, x_ref, gt_ref, g_ref, ye_hbm, o_ref, buf_ref, pos_smem, sem_pos, sem_row):
    tt = pos_ref.shape[1]
    load_pos = pltpu.make_async_copy(pos_ref, pos_smem, sem_pos)
    load_pos.start()
    load_pos.wait()

    def issue(j, c):
        for k in range(TOP_K):
            pltpu.make_async_copy(ye_hbm.at[pos_smem[k, j]], buf_ref.at[k * tt + j], sem_row).start(priority=k % 2)
        return c

    lax.fori_loop(0, tt, issue, 0, unroll=4)
    for k in range(TOP_K):
        pltpu.make_async_copy(ye_hbm.at[pl.ds(0, tt)], buf_ref.at[pl.ds(k * tt, tt)], sem_row).wait()
    gates = gate_ref[...]
    half = D_MODEL // 2
    acc_lo = jnp.zeros((tt, half), F32)
    acc_hi = jnp.zeros((tt, half), F32)
    for k in range(TOP_K):
        packed = jnp.concatenate([buf_ref[pl.ds(k * tt, tt), c, :] for c in range(ROW_CHUNKS)], axis=1)
        lo, hi = _unpack_bf16_pairs(packed)
        gk = gates[:, k:k + 1]
        acc_lo += gk * lo
        acc_hi += gk * hi
    x = x_ref[0] + gt_ref[0] * jnp.concatenate([acc_lo, acc_hi], axis=1)
    o_ref[0] = x * lax.rsqrt(jnp.mean(x * x, axis=-1, keepdims=True) + EPS) * g_ref[...]


def _final(pos_t, gates, x1, gt2, g, ye_rows):
    bsz, L, d = x1.shape
    tt = 256
    nt = L // tt
    tok = pl.BlockSpec((1, tt, d), lambda b, i: (b, i, 0))
    return pl.pallas_call(
        _final_kernel,
        grid=(bsz, nt),
        in_specs=[pl.BlockSpec((TOP_K, tt), lambda b, i: (0, b * nt + i)),
                  pl.BlockSpec((tt, TOP_K), lambda b, i: (b * nt + i, 0)),
                  tok, pl.BlockSpec((1, 1, d), lambda b, i: (b, 0, 0)),
                  pl.BlockSpec((1, d), lambda b, i: (0, 0)),
                  pl.BlockSpec(memory_space=pl.ANY)],
        out_specs=tok,
        out_shape=jax.ShapeDtypeStruct((bsz, L, d), F32),
        scratch_shapes=[pltpu.VMEM((TOP_K * tt, ROW_CHUNKS, LANE), jnp.uint32),
                        pltpu.SMEM((TOP_K, tt), jnp.int32),
                        pltpu.SemaphoreType.DMA, pltpu.SemaphoreType.DMA],
        compiler_params=_params("arbitrary", "arbitrary"),
        name="final",
    )(pos_t, gates, x1, gt2, g.reshape(1, d), ye_rows)


def _block_table(counts, n_blocks):
    counts = counts.astype(jnp.int32)
    padded = (counts + MOE_BLOCK - 1) // MOE_BLOCK * MOE_BLOCK
    pends = jnp.cumsum(padded)
    pstarts = pends - padded
    block_start = jnp.arange(n_blocks, dtype=jnp.int32) * MOE_BLOCK
    block_expert = jnp.minimum(jnp.searchsorted(pends, block_start, side='right'),
                               N_EXPERTS - 1).astype(jnp.int32)
    n_valid = (pends[-1] // MOE_BLOCK).astype(jnp.int32).reshape(1)
    return padded, pends, pstarts, block_expert, n_valid


def kernel(x, c, w_ada, b_ada, g_norm_mix, w_in, s5_lambda_re, s5_lambda_im, s5_log_dt, s5_b_re, s5_b_im, s5_c_re, s5_c_im, s5_d, s5_w_glu, s5_b_glu, s5_norm_g, gla_w_g2, gla_b_g2, gla_norm_g, w_out, g_norm_moe, w_router, router_bias, exp_w_gate, exp_w_up, exp_w_down, sh_w_gate, sh_w_up, sh_w_down, g_final):
    bsz, L, d = x.shape
    T = bsz * L
    assert w_ada.shape[0] == 1, "single-layer block"
    for l in range(1):
        mod = _ada(c, w_ada[l], b_ada[l])
        sh1, sc1, gt1, sh2, sc2, gt2 = [m.reshape(bsz, 1, d) for m in jnp.split(mod, 6, axis=-1)]

        wi = w_in[l]
        o_q, o_k, o_v, o_g, o_r = D_S5, D_S5 + GLA_QK, D_S5 + 2 * GLA_QK, D_S5 + 2 * GLA_QK + D_GLA, \
            D_S5 + 2 * GLA_QK + D_GLA + GLA_LOWRANK
        w_cat = jnp.concatenate([wi[:, :o_g], wi[:, o_r:], wi[:, o_g:o_r],
                                 jnp.zeros((d, LANE - GLA_LOWRANK), wi.dtype)], axis=1).astype(BF16)
        u, q, k, v, r, glr = _inproj(x, g_norm_mix[l], sh1, sc1, w_cat)

        n_chunks = L // S5_CHUNK
        mats = _s5_prep(s5_lambda_re[l], s5_lambda_im[l], s5_log_dt[l], s5_b_re[l], s5_b_im[l],
                        s5_c_re[l], s5_c_im[l], s5_d[l])
        z = u.reshape(bsz, n_chunks, S5_CHUNK, S5_GROUPS, S5_GROUP).transpose(3, 1, 0, 2, 4)
        z = z.reshape(S5_GROUPS, n_chunks * bsz, S5_CHUNK * S5_GROUP)
        ys = _s5_core(z, mats, bsz)
        ys = ys.reshape(S5_GROUPS, n_chunks, bsz, S5_CHUNK, S5_GROUP).transpose(2, 1, 3, 0, 4)
        ys = ys.reshape(bsz, L, D_S5)

        wg2 = jnp.concatenate([gla_w_g2[l], jnp.zeros((LANE - GLA_LOWRANK, GLA_QK), F32)], axis=0).astype(BF16)
        yg = _gla(q, k, v, glr, r, wg2, gla_b_g2[l].reshape(1, GLA_QK), gla_norm_g[l].reshape(1, GLA_DV))

        x1, h2p, idx_t, rank_t, gates, counts = _mix(
            ys, yg, x, s5_w_glu[l].astype(BF16), s5_b_glu[l].reshape(1, D_S5), s5_norm_g[l].reshape(1, D_S5),
            w_out[l].astype(BF16), gt1, g_norm_moe[l].reshape(1, d), sh2, sc2, gt2,
            w_router[l].T.astype(BF16), router_bias[l].reshape(N_EXPERTS, 1),
            sh_w_gate[l].astype(BF16), sh_w_up[l].astype(BF16), sh_w_down[l].astype(BF16))

        n_blocks = (T * TOP_K + N_EXPERTS * (MOE_BLOCK - 1) + MOE_BLOCK - 1) // MOE_BLOCK
        n_slots = n_blocks * MOE_BLOCK
        padded, pends, pstarts, block_expert, n_valid = _block_table(counts[:, 0], n_blocks)
        pos_t = _pos(idx_t, rank_t, pstarts)
        xs = _dispatch(pends, padded, pos_t, h2p.reshape(T, ROW_CHUNKS, LANE), n_slots)
        ye = _experts(block_expert, n_valid, xs.reshape(n_slots * ROW_CHUNKS, LANE),
                      exp_w_gate[l], exp_w_up[l], exp_w_down[l])
    return _final(pos_t, gates, x1, gt2, g_final, ye.reshape(n_slots, ROW_CHUNKS, LANE))
```

```python
import functools

import jax
import jax.numpy as jnp
from jax import lax
from jax.experimental import pallas as pl
from jax.experimental.pallas import tpu as pltpu

F32 = jnp.float32
BF16 = jnp.bfloat16

D_MODEL = 1024
D_S5 = 512
S5_GROUP = 16
S5_GROUPS = 32
S5_STATE = 64
S5_CHUNK = 16
D_GLA = 512
GLA_HEADS = 4
GLA_DK = 64
GLA_DV = 128
GLA_QK = 256
GLA_LOWRANK = 16
GLA_TAU = 16.0
GLA_CHUNK = 64
LANE = 128
N_EXPERTS = 256
TOP_K = 8
N_GROUPS = 8
TOPK_GROUPS = 4
D_EXPERT = 256
ROUTED_SCALE = 2.5
EPS = 1e-6
MOE_BLOCK = 256
ROW_CHUNKS = D_MODEL // 2 // LANE
VMEM_LIMIT = 48 * 1024 * 1024


def _silu(x):
    return x * jax.nn.sigmoid(x)


def _params(*sem):
    return pltpu.CompilerParams(dimension_semantics=sem, vmem_limit_bytes=VMEM_LIMIT)


def _ada_kernel(c_ref, w_ref, b_ref, o_ref):
    s = _silu(c_ref[...]).astype(BF16)
    o_ref[...] = jnp.dot(s, w_ref[...].astype(BF16), preferred_element_type=F32) + b_ref[...]


def _ada(c, w, b):
    bsz, d = c.shape
    n = w.shape[1]
    tn = 1024
    return pl.pallas_call(
        _ada_kernel,
        grid=(n // tn,),
        in_specs=[pl.BlockSpec((bsz, d), lambda j: (0, 0)),
                  pl.BlockSpec((d, tn), lambda j: (0, j)),
                  pl.BlockSpec((1, tn), lambda j: (0, j))],
        out_specs=pl.BlockSpec((bsz, tn), lambda j: (0, j)),
        out_shape=jax.ShapeDtypeStruct((bsz, n), F32),
        compiler_params=_params("arbitrary"),
        name="ada",
    )(c, w, b.reshape(1, n))


def _inproj_kernel(x_ref, g_ref, sh_ref, sc_ref, w_ref,
                   u_ref, q_ref, k_ref, v_ref, r_ref, glr_ref):
    x = x_ref[0]
    ms = jnp.mean(x * x, axis=-1, keepdims=True)
    h = (x * lax.rsqrt(ms + EPS)) * g_ref[...]
    h = h * (1.0 + sc_ref[0]) + sh_ref[0]
    hb = h.astype(BF16)
    col = 0
    for ref in (u_ref, q_ref, k_ref, v_ref, r_ref, glr_ref):
        n = ref.shape[-1]
        ref[0] = jnp.dot(hb, w_ref[:, col:col + n], preferred_element_type=F32).astype(ref.dtype)
        col += n


def _inproj(x, g, sh, sc, w):
    bsz, L, d = x.shape
    tt = 512
    widths = (D_S5, GLA_QK, GLA_QK, D_GLA, D_GLA, LANE)
    tok = lambda n: pl.BlockSpec((1, tt, n), lambda b, i: (b, i, 0))
    vec = pl.BlockSpec((1, 1, d), lambda b, i: (b, 0, 0))
    return pl.pallas_call(
        _inproj_kernel,
        grid=(bsz, L // tt),
        in_specs=[tok(d), pl.BlockSpec((1, d), lambda b, i: (0, 0)), vec, vec,
                  pl.BlockSpec(w.shape, lambda b, i: (0, 0))],
        out_specs=[tok(n) for n in widths],
        out_shape=[jax.ShapeDtypeStruct((bsz, L, n), BF16) for n in widths],
        compiler_params=_params("arbitrary", "arbitrary"),
        name="inproj",
    )(x, g.reshape(1, d), sh, sc, w)


def _s5_prep(lam_re, lam_im, log_dt, b_re, b_im, c_re, c_im, d_skip):
    G, N, C, TC = S5_GROUPS, S5_STATE, S5_GROUP, S5_CHUNK
    hp = lax.Precision.HIGHEST
    dt = jnp.exp(log_dt)[:, None]
    lr, li = lam_re, lam_im
    mag = jnp.exp(lr * dt)
    ab_re, ab_im = mag * jnp.cos(li * dt), mag * jnp.sin(li * dt)
    den = lr * lr + li * li
    nr = ab_re - 1.0
    coef_re = ((nr * lr + ab_im * li) / den)[..., None]
    coef_im = ((ab_im * lr - nr * li) / den)[..., None]
    bb_re = coef_re * b_re - coef_im * b_im
    bb_im = coef_re * b_im + coef_im * b_re
    p = jnp.arange(TC + 1, dtype=F32)[:, None, None]
    pm = jnp.exp(lr * dt * p)
    pr, pi = pm * jnp.cos(li * dt * p), pm * jnp.sin(li * dt * p)
    ca_re = c_re[None] * pr[:, :, None, :] - c_im[None] * pi[:, :, None, :]
    ca_im = c_re[None] * pi[:, :, None, :] + c_im[None] * pr[:, :, None, :]
    kern = (jnp.einsum('tgon,gni->tgoi', ca_re, bb_re, precision=hp)
            - jnp.einsum('tgon,gni->tgoi', ca_im, bb_im, precision=hp))
    kern = kern.at[0].add(jnp.eye(C, dtype=F32)[None] * d_skip[:, :, None])
    s_idx = jnp.arange(TC)[:, None]
    t_idx = jnp.arange(TC)[None, :]
    lag = t_idx - s_idx
    toep = jnp.where((lag >= 0)[:, :, None, None, None], kern[jnp.maximum(lag, 0)], 0.0)
    toep = toep.transpose(2, 0, 4, 1, 3).reshape(G, TC * C, TC * C)
    rr, ri = pr[TC - 1 - jnp.arange(TC)], pi[TC - 1 - jnp.arange(TC)]
    binc_re = rr[..., None] * bb_re[None] - ri[..., None] * bb_im[None]
    binc_im = rr[..., None] * bb_im[None] + ri[..., None] * bb_re[None]
    binc_re = binc_re.transpose(1, 0, 3, 2).reshape(G, TC * C, N)
    binc_im = binc_im.transpose(1, 0, 3, 2).reshape(G, TC * C, N)
    cm_re = ca_re[1:].transpose(1, 3, 0, 2).reshape(G, N, TC * C)
    cm_im = (-ca_im[1:]).transpose(1, 3, 0, 2).reshape(G, N, TC * C)
    return (toep.astype(BF16), binc_re.astype(BF16), binc_im.astype(BF16),
            cm_re.astype(BF16), cm_im.astype(BF16), pr[TC], pi[TC])


def _s5_kernel(z_ref, toep_ref, bre_ref, bim_ref, cre_ref, cim_ref, ar_ref, ai_ref,
               y_ref, incr_ref, inci_ref, hsr_ref, hsi_ref, *, bsz, n_chunks):
    z = z_ref[0]
    incr_ref[...] = jnp.dot(z, bre_ref[0], preferred_element_type=F32)
    inci_ref[...] = jnp.dot(z, bim_ref[0], preferred_element_type=F32)
    ar = ar_ref[0]
    ai = ai_ref[0]

    def step(j, carry):
        hr, hi = carry
        rows = pl.ds(pl.multiple_of(j * bsz, bsz), bsz)
        hsr_ref[rows, :] = hr
        hsi_ref[rows, :] = hi
        return (ar * hr - ai * hi + incr_ref[rows, :], ar * hi + ai * hr + inci_ref[rows, :])

    zero = jnp.zeros((bsz, S5_STATE), F32)
    lax.fori_loop(0, n_chunks, step, (zero, zero), unroll=4)
    y = jnp.dot(z, toep_ref[0], preferred_element_type=F32)
    y += jnp.dot(hsr_ref[...].astype(BF16), cre_ref[0], preferred_element_type=F32)
    y += jnp.dot(hsi_ref[...].astype(BF16), cim_ref[0], preferred_element_type=F32)
    y_ref[0] = y.astype(y_ref.dtype)


def _s5_core(z, mats, bsz):
    toep, bre, bim, cre, cim, ar, ai = mats
    G, rows, w = z.shape
    n_chunks = rows // bsz
    N = S5_STATE
    arb = jnp.broadcast_to(ar[:, None, :], (G, bsz, N))
    aib = jnp.broadcast_to(ai[:, None, :], (G, bsz, N))
    grp = lambda a, b: pl.BlockSpec((1, a, b), lambda g: (g, 0, 0))
    return pl.pallas_call(
        functools.partial(_s5_kernel, bsz=bsz, n_chunks=n_chunks),
        grid=(G,),
        in_specs=[grp(rows, w), grp(w, w), grp(w, N), grp(w, N), grp(N, w), grp(N, w),
                  grp(bsz, N), grp(bsz, N)],
        out_specs=grp(rows, w),
        out_shape=jax.ShapeDtypeStruct((G, rows, w), BF16),
        scratch_shapes=[pltpu.VMEM((rows, N), F32)] * 4,
        compiler_params=_params("arbitrary"),
        name="s5_core",
    )(z, toep, bre, bim, cre, cim, arb, aib)


def _gla_kernel(q_ref, k_ref, v_ref, glr_ref, r_ref, wg_ref, bg_ref, ng_ref, o_ref, st_ref):
    lt = q_ref.shape[1]
    C = GLA_CHUNK

    @pl.when(pl.program_id(1) == 0)
    def _():
        st_ref[...] = jnp.zeros_like(st_ref)

    z = jnp.dot(glr_ref[0], wg_ref[...], preferred_element_type=F32) + bg_ref[...]
    log_a = (jnp.minimum(z, 0.0) - jnp.log(1.0 + jnp.exp(-jnp.abs(z)))) * (1.0 / GLA_TAU)
    ri = lax.broadcasted_iota(jnp.int32, (lt, lt), 0)
    ci = lax.broadcasted_iota(jnp.int32, (lt, lt), 1)
    tril = jnp.where(((ri >> 6) == (ci >> 6)) & (ci <= ri), 1.0, 0.0).astype(BF16)
    la_hi = log_a.astype(BF16)
    la_lo = (log_a - la_hi.astype(F32)).astype(BF16)
    bcum = (jnp.dot(tril, la_hi, preferred_element_type=F32)
            + jnp.dot(tril, la_lo, preferred_element_type=F32))
    q = q_ref[0].astype(F32) * (GLA_DK ** -0.5)
    k = k_ref[0].astype(F32)
    qi = q * jnp.exp(bcum)
    ki = k * jnp.exp(-bcum)
    lane_head = lax.broadcasted_iota(jnp.int32, (1, GLA_QK), 1) >> 6
    causal = ((lax.broadcasted_iota(jnp.int32, (GLA_HEADS * C, C), 0) & (C - 1))
              >= lax.broadcasted_iota(jnp.int32, (GLA_HEADS * C, C), 1))
    same_head = ((lax.broadcasted_iota(jnp.int32, (D_GLA, GLA_QK), 0) >> 7)
                 == (lax.broadcasted_iota(jnp.int32, (D_GLA, GLA_QK), 1) >> 6))
    nt = (((1,), (1,)), ((), ()))
    ng = ng_ref[...]
    for c in range(lt // C):
        sl = slice(c * C, (c + 1) * C)
        bc = bcum[sl]
        bl = bc[C - 1:C, :]
        kd = k[sl] * jnp.exp(bl - bc)
        qic = qi[sl]
        qs = jnp.concatenate([jnp.where(lane_head == h, qic, 0.0) for h in range(GLA_HEADS)],
                             axis=0).astype(BF16)
        sc = lax.dot_general(qs, ki[sl].astype(BF16), nt, preferred_element_type=F32)
        p = jnp.where(causal, sc, 0.0).astype(BF16)
        vc = v_ref[0, sl, :]
        o_intra = jnp.concatenate(
            [jnp.dot(p[h * C:(h + 1) * C], vc[:, h * GLA_DV:(h + 1) * GLA_DV], preferred_element_type=F32)
             for h in range(GLA_HEADS)], axis=1)
        st = st_ref[...]
        o_inter = lax.dot_general(qic.astype(BF16), st.astype(BF16), nt, preferred_element_type=F32)
        v_t = vc.astype(F32).T.astype(BF16)
        kv_t = jnp.dot(v_t, kd.astype(BF16), preferred_element_type=F32)
        st_ref[...] = st * jnp.exp(bl) + jnp.where(same_head, kv_t, 0.0)
        o = o_intra + o_inter
        parts = []
        for h in range(GLA_HEADS):
            oh = o[:, h * GLA_DV:(h + 1) * GLA_DV]
            oh = oh * lax.rsqrt(jnp.mean(oh * oh, axis=-1, keepdims=True) + EPS)
            parts.append(oh * ng)
        r = r_ref[0, sl, :].astype(F32)
        o_ref[0, sl, :] = (jnp.concatenate(parts, axis=1) * _silu(r)).astype(o_ref.dtype)


def _gla(q, k, v, glr, r, wg, bg, ng):
    bsz, L, _ = q.shape
    lt = 256
    tok = lambda n: pl.BlockSpec((1, lt, n), lambda b, i: (b, i, 0))
    full = lambda a: pl.BlockSpec(a.shape, lambda b, i: (0,) * a.ndim)
    return pl.pallas_call(
        _gla_kernel,
        grid=(bsz, L // lt),
        in_specs=[tok(GLA_QK), tok(GLA_QK), tok(D_GLA), tok(LANE), tok(D_GLA), full(wg), full(bg), full(ng)],
        out_specs=tok(D_GLA),
        out_shape=jax.ShapeDtypeStruct((bsz, L, D_GLA), BF16),
        scratch_shapes=[pltpu.VMEM((D_GLA, GLA_QK), F32)],
        compiler_params=_params("arbitrary", "arbitrary"),
        name="gla",
    )(q, k, v, glr, r, wg, bg, ng)


def _pack_bf16_pairs(x):
    w = x.shape[1] // 2
    xr = x.astype(BF16).astype(F32)
    lo = lax.bitcast_convert_type(xr[:, :w], jnp.uint32) >> 16
    hi = lax.bitcast_convert_type(xr[:, w:], jnp.uint32) & jnp.uint32(0xFFFF0000)
    return lo | hi


def _unpack_bf16_pairs(p):
    lo = lax.bitcast_convert_type(p << 16, F32)
    hi = lax.bitcast_convert_type(p & jnp.uint32(0xFFFF0000), F32)
    return lo, hi


def _store_rows(ref, packed):
    n = packed.shape[0]
    for c in range(ROW_CHUNKS):
        ref[pl.ds(c, n, stride=ROW_CHUNKS), :] = packed[:, c * LANE:(c + 1) * LANE]


def _load_rows(ref, n):
    return jnp.concatenate([ref[pl.ds(c, n, stride=ROW_CHUNKS), :] for c in range(ROW_CHUNKS)], axis=1)


def _route_tile(lg_t, bias_col, tri, carry_ref):
    n_e, tt = lg_t.shape
    per_group = n_e // N_GROUPS
    scores = jax.nn.sigmoid(lg_t)
    biased = scores + bias_col
    row = lax.broadcasted_iota(jnp.int32, (n_e, tt), 0)
    neg = -jnp.inf
    group_score = []
    for g in range(N_GROUPS):
        b = biased[g * per_group:(g + 1) * per_group]
        r = lax.broadcasted_iota(jnp.int32, (per_group, tt), 0) + g * per_group
        m1 = jnp.max(b, axis=0, keepdims=True)
        i1 = jnp.min(jnp.where(b == m1, r, n_e), axis=0, keepdims=True)
        m2 = jnp.max(jnp.where(r == i1, neg, b), axis=0, keepdims=True)
        group_score.append(m1 + m2)
    parts = []
    for g in range(N_GROUPS):
        ahead = jnp.zeros((1, tt), jnp.int32)
        for g2 in range(N_GROUPS):
            if g2 != g:
                beats = (group_score[g2] >= group_score[g]) if g2 < g else (group_score[g2] > group_score[g])
                ahead = ahead + beats.astype(jnp.int32)
        parts.append(jnp.where(ahead < TOPK_GROUPS, biased[g * per_group:(g + 1) * per_group], neg))
    masked = jnp.concatenate(parts, axis=0)
    work = masked
    idxs = []
    for _ in range(TOP_K):
        m = jnp.max(work, axis=0, keepdims=True)
        ii = jnp.min(jnp.where(work == m, row, n_e), axis=0, keepdims=True)
        idxs.append(ii)
        work = jnp.where(row == ii, neg, work)
    sel = work != masked
    w = jnp.where(sel, scores, 0.0)
    gate_dense = w / jnp.sum(w, axis=0, keepdims=True) * ROUTED_SCALE
    mt = jnp.where(sel, 1.0, 0.0)
    rank_dense = jnp.dot(mt.astype(BF16), tri, preferred_element_type=F32) + carry_ref[...]
    carry_ref[...] += jnp.sum(mt, axis=1, keepdims=True)
    ranks, gts = [], []
    for ii in idxs:
        oh = row == ii
        ranks.append(jnp.sum(jnp.where(oh, rank_dense, 0.0), axis=0, keepdims=True))
        gts.append(jnp.sum(jnp.where(oh, gate_dense, 0.0), axis=0, keepdims=True))
    idx_t = jnp.concatenate(idxs, axis=0)
    rank_t = jnp.concatenate(ranks, axis=0).astype(jnp.int32)
    gate_t = jnp.concatenate(gts + [jnp.zeros((LANE - TOP_K, tt), F32)], axis=0)
    return idx_t, rank_t, gate_t.T[:, :TOP_K]


def _mix_kernel(ys_ref, yg_ref, x_ref, wglu_ref, bglu_ref, sng_ref, wo_ref, gt1_ref,
                gn_ref, sh2_ref, sc2_ref, gt2_ref, wrt_ref, rb_ref, tri_ref, wsg_ref, wsu_ref, wsd_ref,
                x1_ref, h2_ref, idx_ref, rank_ref, gate_ref, cnt_ref, carry_ref):
    @pl.when((pl.program_id(0) == 0) & (pl.program_id(1) == 0))
    def _():
        carry_ref[...] = jnp.zeros_like(carry_ref)

    y = ys_ref[0].astype(F32)
    g = y * (0.5 * (1.0 + jnp.tanh(0.7978845608028654 * (y + 0.044715 * (y * y * y)))))
    gate = jax.nn.sigmoid(jnp.dot(g.astype(BF16), wglu_ref[...], preferred_element_type=F32) + bglu_ref[...])
    out = g * gate
    out = out * lax.rsqrt(jnp.mean(out * out, axis=-1, keepdims=True) + EPS) * sng_ref[...]
    mix = (jnp.dot(out.astype(BF16), wo_ref[0:D_S5, :], preferred_element_type=F32)
           + jnp.dot(yg_ref[0], wo_ref[D_S5:, :], preferred_element_type=F32))
    x1 = x_ref[0] + gt1_ref[0] * mix
    h2 = x1 * lax.rsqrt(jnp.mean(x1 * x1, axis=-1, keepdims=True) + EPS) * gn_ref[...]
    h2 = h2 * (1.0 + sc2_ref[0]) + sh2_ref[0]
    hb = h2.astype(BF16)
    _store_rows(h2_ref, _pack_bf16_pairs(h2))
    lg_t = lax.dot_general(wrt_ref[...], hb, (((1,), (1,)), ((), ())), preferred_element_type=F32)
    idx_t, rank_t, gates = _route_tile(lg_t, rb_ref[...], tri_ref[...], carry_ref)
    idx_ref[...] = idx_t
    rank_ref[...] = rank_t
    gate_ref[...] = gates
    cnt_ref[...] = carry_ref[...]
    a = _silu(jnp.dot(hb, wsg_ref[...], preferred_element_type=F32)) * jnp.dot(
        hb, wsu_ref[...], preferred_element_type=F32)
    shared = jnp.dot(a.astype(BF16), wsd_ref[...], preferred_element_type=F32)
    x1_ref[0] = x1 + gt2_ref[0] * shared


def _mix(ys, yg, x, wglu, bglu, sng, wo, gt1, gn, sh2, sc2, gt2, wrt, rb, wsg, wsu, wsd):
    bsz, L, d = x.shape
    tt = 512
    nt = L // tt
    T = bsz * L
    tri = (jnp.arange(tt)[:, None] < jnp.arange(tt)[None, :]).astype(BF16)
    tok = lambda n: pl.BlockSpec((1, tt, n), lambda b, i: (b, i, 0))
    vec = pl.BlockSpec((1, 1, d), lambda b, i: (b, 0, 0))
    full = lambda a: pl.BlockSpec(a.shape, lambda b, i: (0,) * a.ndim)
    lanes = pl.BlockSpec((TOP_K, tt), lambda b, i: (0, b * nt + i))
    return pl.pallas_call(
        _mix_kernel,
        grid=(bsz, nt),
        in_specs=[tok(D_S5), tok(D_GLA), tok(d), full(wglu), full(bglu), full(sng), full(wo), vec,
                  full(gn), vec, vec, vec, full(wrt), full(rb), full(tri), full(wsg), full(wsu), full(wsd)],
        out_specs=[tok(d),
                   pl.BlockSpec((tt * ROW_CHUNKS, LANE), lambda b, i: (b * nt + i, 0)),
                   lanes, lanes,
                   pl.BlockSpec((tt, TOP_K), lambda b, i: (b * nt + i, 0)),
                   pl.BlockSpec((N_EXPERTS, 1), lambda b, i: (0, 0))],
        out_shape=[jax.ShapeDtypeStruct((bsz, L, d), F32),
                   jax.ShapeDtypeStruct((T * ROW_CHUNKS, LANE), jnp.uint32),
                   jax.ShapeDtypeStruct((TOP_K, T), jnp.int32),
                   jax.ShapeDtypeStruct((TOP_K, T), jnp.int32),
                   jax.ShapeDtypeStruct((T, TOP_K), F32),
                   jax.ShapeDtypeStruct((N_EXPERTS, 1), F32)],
        scratch_shapes=[pltpu.VMEM((N_EXPERTS, 1), F32)],
        compiler_params=_params("arbitrary", "arbitrary"),
        name="mix",
    )(ys, yg, x, wglu, bglu, sng, wo, gt1, gn, sh2, sc2, gt2, wrt, rb, tri, wsg, wsu, wsd)


def _pos_kernel(idx_ref, rank_ref, ps_ref, pos_ref):
    n_e = ps_ref.shape[0]
    tt = idx_ref.shape[1]
    row = lax.broadcasted_iota(jnp.int32, (n_e, tt), 0)
    ps = ps_ref[...]
    starts = [jnp.sum(jnp.where(row == idx_ref[k:k + 1, :], ps, 0.0), axis=0, keepdims=True)
              for k in range(TOP_K)]
    pos_ref[...] = jnp.concatenate(starts, axis=0).astype(jnp.int32) + rank_ref[...]


def _pos(idx_t, rank_t, pstart):
    T = idx_t.shape[1]
    tt = 2048
    blk = pl.BlockSpec((TOP_K, tt), lambda i: (0, i))
    return pl.pallas_call(
        _pos_kernel,
        grid=(T // tt,),
        in_specs=[blk, blk, pl.BlockSpec((N_EXPERTS, 1), lambda i: (0, 0))],
        out_specs=blk,
        out_shape=jax.ShapeDtypeStruct((TOP_K, T), jnp.int32),
        compiler_params=_params("arbitrary"),
        name="pos",
    )(idx_t, rank_t, pstart.astype(F32).reshape(N_EXPERTS, 1))


def _dispatch_kernel(pend_ref, padded_ref, pos_ref, h_ref, xs_hbm, zero_ref, pos_smem, sem_pos, sem_zero, sem_row):
    tt = pos_ref.shape[1]
    step = pl.program_id(0)
    load_pos = pltpu.make_async_copy(pos_ref, pos_smem, sem_pos)
    load_pos.start()

    def zero_copy(e):
        start = pl.multiple_of(pend_ref[e] - MOE_BLOCK, MOE_BLOCK)
        return pltpu.make_async_copy(zero_ref, xs_hbm.at[pl.ds(start, MOE_BLOCK)], sem_zero)

    @pl.when(step == 0)
    def _():
        zero_ref[...] = jnp.zeros_like(zero_ref)

        def start_zero(e, c):
            @pl.when(padded_ref[e] > 0)
            def _():
                zero_copy(e).start()
            return c

        def wait_zero(e, c):
            @pl.when(padded_ref[e] > 0)
            def _():
                zero_copy(e).wait()
            return c

        lax.fori_loop(0, N_EXPERTS, start_zero, 0)
        lax.fori_loop(0, N_EXPERTS, wait_zero, 0)

        def fill_tail(g, c):
            start = pl.multiple_of(g * MOE_BLOCK, MOE_BLOCK)
            cp = pltpu.make_async_copy(zero_ref, xs_hbm.at[pl.ds(start, MOE_BLOCK)], sem_zero)
            cp.start()
            cp.wait()
            return c

        lax.fori_loop(pend_ref[N_EXPERTS - 1] // MOE_BLOCK, xs_hbm.shape[0] // MOE_BLOCK, fill_tail, 0)

    load_pos.wait()

    def issue(j, c):
        for k in range(TOP_K):
            pltpu.make_async_copy(h_ref.at[j], xs_hbm.at[pos_smem[k, j]], sem_row).start(priority=k % 2)
        return c

    lax.fori_loop(0, tt, issue, 0, unroll=4)
    for k in range(TOP_K):
        pltpu.make_async_copy(h_ref, xs_hbm.at[pl.ds(0, tt)], sem_row).wait()


def _dispatch(pends, padded, pos_t, h_rows, n_slots):
    T = h_rows.shape[0]
    tt = 512
    return pl.pallas_call(
        _dispatch_kernel,
        grid_spec=pltpu.PrefetchScalarGridSpec(
            num_scalar_prefetch=2,
            grid=(T // tt,),
            in_specs=[pl.BlockSpec((TOP_K, tt), lambda i, pe, pa: (0, i)),
                      pl.BlockSpec((tt, ROW_CHUNKS, LANE), lambda i, pe, pa: (i, 0, 0))],
            out_specs=pl.BlockSpec(memory_space=pl.ANY),
            scratch_shapes=[pltpu.VMEM((MOE_BLOCK, ROW_CHUNKS, LANE), jnp.uint32),
                            pltpu.SMEM((TOP_K, tt), jnp.int32),
                            pltpu.SemaphoreType.DMA, pltpu.SemaphoreType.DMA, pltpu.SemaphoreType.DMA]),
        out_shape=jax.ShapeDtypeStruct((n_slots, ROW_CHUNKS, LANE), jnp.uint32),
        compiler_params=_params("arbitrary"),
        name="dispatch",
    )(pends, padded, pos_t, h_rows)


X_BUFS = 3
Y_BUFS = 2


def _expert_kernel(b0_ref, nb_ref, nv_ref, wg_ref, wu_ref, wd_ref, xs_hbm, ye_hbm,
                   wg_s, wu_s, wd_s, xbuf, ybuf, semx, semy, *, n_blocks):
    e = pl.program_id(0)
    nb = nb_ref[e]
    b0 = b0_ref[e]
    nv = nv_ref[0]
    rows = MOE_BLOCK * ROW_CHUNKS
    ahead = X_BUFS - 1

    def block(ref, g):
        return ref.at[pl.ds(pl.multiple_of(g * rows, rows), rows)]

    def x_copy(g):
        slot = g % X_BUFS
        return pltpu.make_async_copy(block(xs_hbm, g), xbuf.at[slot], semx.at[slot])

    def y_copy(g):
        slot = g % Y_BUFS
        return pltpu.make_async_copy(ybuf.at[slot], block(ye_hbm, g), semy.at[slot])

    @pl.when(e == 0)
    def _():
        for g in range(ahead):
            x_copy(g).start()

    @pl.when(nb > 0)
    def _():
        wg_s[...] = wg_ref[0].astype(BF16)
        wu_s[...] = wu_ref[0].astype(BF16)
        wd_s[...] = wd_ref[0].astype(BF16)

    def body(i, c):
        g = b0 + i
        x_copy(g).wait()

        @pl.when(g + ahead < nv)
        def _():
            x_copy(g + ahead).start()

        lo, hi = _unpack_bf16_pairs(_load_rows(xbuf.at[g % X_BUFS], MOE_BLOCK))
        x = jnp.concatenate([lo, hi], axis=1).astype(BF16)
        a = jnp.dot(x, wg_s[...], preferred_element_type=F32)
        u = jnp.dot(x, wu_s[...], preferred_element_type=F32)
        h = (_silu(a) * u).astype(BF16)
        y = jnp.dot(h, wd_s[...], preferred_element_type=F32)

        @pl.when(g >= Y_BUFS)
        def _():
            y_copy(g - Y_BUFS).wait()

        _store_rows(ybuf.at[g % Y_BUFS], _pack_bf16_pairs(y))
        y_copy(g).start()
        return c

    lax.fori_loop(0, nb, body, 0)

    @pl.when(e == pl.num_programs(0) - 1)
    def _():
        for back in range(Y_BUFS, 0, -1):
            y_copy(nv - back).wait()
        ybuf[0] = jnp.zeros(ybuf.shape[1:], ybuf.dtype)

        def fill(g, c):
            cp = pltpu.make_async_copy(ybuf.at[0], block(ye_hbm, g), semy.at[0])
            cp.start()
            cp.wait()
            return c

        lax.fori_loop(nv, n_blocks, fill, 0)


def _experts(first_block, num_blocks, n_valid, xs, wg, wu, wd):
    rows = MOE_BLOCK * ROW_CHUNKS
    n_blocks = xs.shape[0] // rows
    n_e, d, _ = wg.shape
    wsel = lambda e, b0, nb, nv: (e, 0, 0)
    return pl.pallas_call(
        functools.partial(_expert_kernel, n_blocks=n_blocks),
        grid_spec=pltpu.PrefetchScalarGridSpec(
            num_scalar_prefetch=3,
            grid=(n_e,),
            in_specs=[pl.BlockSpec((1, d, D_EXPERT), wsel),
                      pl.BlockSpec((1, d, D_EXPERT), wsel),
                      pl.BlockSpec((1, D_EXPERT, d), wsel),
                      pl.BlockSpec(memory_space=pl.ANY)],
            out_specs=pl.BlockSpec(memory_space=pl.ANY),
            scratch_shapes=[pltpu.VMEM((d, D_EXPERT), BF16), pltpu.VMEM((d, D_EXPERT), BF16),
                            pltpu.VMEM((D_EXPERT, d), BF16),
                            pltpu.VMEM((X_BUFS, rows, LANE), jnp.uint32),
                            pltpu.VMEM((Y_BUFS, rows, LANE), jnp.uint32),
                            pltpu.SemaphoreType.DMA((X_BUFS,)), pltpu.SemaphoreType.DMA((Y_BUFS,))]),
        out_shape=jax.ShapeDtypeStruct(xs.shape, jnp.uint32),
        compiler_params=_params("arbitrary"),
        name="experts",
    )(first_block, num_blocks, n_valid, wg, wu, wd, xs)


def _final_kernel(pos0_ref, posn_ref, gate_ref, x_ref, gt_ref, g_ref, ye_hbm, o_ref,
                  buf_ref, pos_smem, sem_pos, sem_row):
    tt = pos0_ref.shape[1]
    i = pl.program_id(0)

    def gather_tile(pos_ref, tile):
        base = (tile % 2) * (TOP_K * tt)
        sem = sem_row.at[tile % 2]
        load_pos = pltpu.make_async_copy(pos_ref, pos_smem, sem_pos)
        load_pos.start()
        load_pos.wait()

        def issue(j, c):
            for k in range(TOP_K):
                pltpu.make_async_copy(ye_hbm.at[pos_smem[k, j]], buf_ref.at[base + k * tt + j],
                                      sem).start(priority=k % 2)
            return c

        lax.fori_loop(0, tt, issue, 0, unroll=4)

    @pl.when(i == 0)
    def _():
        gather_tile(pos0_ref, i)

    @pl.when(i + 1 < pl.num_programs(0))
    def _():
        gather_tile(posn_ref, i + 1)

    base = pl.multiple_of((i % 2) * (TOP_K * tt), TOP_K * tt)
    for k in range(TOP_K):
        pltpu.make_async_copy(ye_hbm.at[pl.ds(0, tt)], buf_ref.at[pl.ds(base + k * tt, tt)],
                              sem_row.at[i % 2]).wait()
    gates = gate_ref[...]
    half = D_MODEL // 2
    acc_lo = jnp.zeros((tt, half), F32)
    acc_hi = jnp.zeros((tt, half), F32)
    for k in range(TOP_K):
        packed = jnp.concatenate([buf_ref[pl.ds(base + k * tt, tt), c, :] for c in range(ROW_CHUNKS)], axis=1)
        lo, hi = _unpack_bf16_pairs(packed)
        gk = gates[:, k:k + 1]
        acc_lo += gk * lo
        acc_hi += gk * hi
    x = x_ref[0] + gt_ref[0] * jnp.concatenate([acc_lo, acc_hi], axis=1)
    o_ref[0] = x * lax.rsqrt(jnp.mean(x * x, axis=-1, keepdims=True) + EPS) * g_ref[...]


def _final(pos_t, gates, x1, gt2, g, ye_rows):
    bsz, L, d = x1.shape
    tt = 256
    nt = L // tt
    n = bsz * nt
    tok = pl.BlockSpec((1, tt, d), lambda i: (i // nt, i % nt, 0))
    return pl.pallas_call(
        _final_kernel,
        grid=(n,),
        in_specs=[pl.BlockSpec((TOP_K, tt), lambda i: (0, 0)),
                  pl.BlockSpec((TOP_K, tt), lambda i: (0, jnp.minimum(i + 1, n - 1))),
                  pl.BlockSpec((tt, TOP_K), lambda i: (i, 0)),
                  tok, pl.BlockSpec((1, 1, d), lambda i: (i // nt, 0, 0)),
                  pl.BlockSpec((1, d), lambda i: (0, 0)),
                  pl.BlockSpec(memory_space=pl.ANY)],
        out_specs=tok,
        out_shape=jax.ShapeDtypeStruct((bsz, L, d), F32),
        scratch_shapes=[pltpu.VMEM((2 * TOP_K * tt, ROW_CHUNKS, LANE), jnp.uint32),
                        pltpu.SMEM((TOP_K, tt), jnp.int32),
                        pltpu.SemaphoreType.DMA, pltpu.SemaphoreType.DMA((2,))],
        compiler_params=_params("arbitrary"),
        name="final",
    )(pos_t, pos_t, gates, x1, gt2, g.reshape(1, d), ye_rows)


def _block_table(counts):
    counts = counts.astype(jnp.int32)
    padded = (counts + MOE_BLOCK - 1) // MOE_BLOCK * MOE_BLOCK
    pends = jnp.cumsum(padded)
    pstarts = pends - padded
    n_valid = (pends[-1] // MOE_BLOCK).reshape(1)
    return padded, pends, pstarts, n_valid


def kernel(x, c, w_ada, b_ada, g_norm_mix, w_in, s5_lambda_re, s5_lambda_im, s5_log_dt, s5_b_re, s5_b_im, s5_c_re, s5_c_im, s5_d, s5_w_glu, s5_b_glu, s5_norm_g, gla_w_g2, gla_b_g2, gla_norm_g, w_out, g_norm_moe, w_router, router_bias, exp_w_gate, exp_w_up, exp_w_down, sh_w_gate, sh_w_up, sh_w_down, g_final):
    bsz, L, d = x.shape
    T = bsz * L
    assert w_ada.shape[0] == 1, "single-layer block"
    for l in range(1):
        mod = _ada(c, w_ada[l], b_ada[l])
        sh1, sc1, gt1, sh2, sc2, gt2 = [m.reshape(bsz, 1, d) for m in jnp.split(mod, 6, axis=-1)]

        wi = w_in[l]
        o_q, o_k, o_v, o_g, o_r = D_S5, D_S5 + GLA_QK, D_S5 + 2 * GLA_QK, D_S5 + 2 * GLA_QK + D_GLA, \
            D_S5 + 2 * GLA_QK + D_GLA + GLA_LOWRANK
        w_cat = jnp.concatenate([wi[:, :o_g], wi[:, o_r:], wi[:, o_g:o_r],
                                 jnp.zeros((d, LANE - GLA_LOWRANK), wi.dtype)], axis=1).astype(BF16)
        u, q, k, v, r, glr = _inproj(x, g_norm_mix[l], sh1, sc1, w_cat)

        n_chunks = L // S5_CHUNK
        mats = _s5_prep(s5_lambda_re[l], s5_lambda_im[l], s5_log_dt[l], s5_b_re[l], s5_b_im[l],
                        s5_c_re[l], s5_c_im[l], s5_d[l])
        z = u.reshape(bsz, n_chunks, S5_CHUNK, S5_GROUPS, S5_GROUP).transpose(3, 1, 0, 2, 4)
        z = z.reshape(S5_GROUPS, n_chunks * bsz, S5_CHUNK * S5_GROUP)
        ys = _s5_core(z, mats, bsz)
        ys = ys.reshape(S5_GROUPS, n_chunks, bsz, S5_CHUNK, S5_GROUP).transpose(2, 1, 3, 0, 4)
        ys = ys.reshape(bsz, L, D_S5)

        wg2 = jnp.concatenate([gla_w_g2[l], jnp.zeros((LANE - GLA_LOWRANK, GLA_QK), F32)], axis=0).astype(BF16)
        yg = _gla(q, k, v, glr, r, wg2, gla_b_g2[l].reshape(1, GLA_QK), gla_norm_g[l].reshape(1, GLA_DV))

        x1, h2p, idx_t, rank_t, gates, counts = _mix(
            ys, yg, x, s5_w_glu[l].astype(BF16), s5_b_glu[l].reshape(1, D_S5), s5_norm_g[l].reshape(1, D_S5),
            w_out[l].astype(BF16), gt1, g_norm_moe[l].reshape(1, d), sh2, sc2, gt2,
            w_router[l].T.astype(BF16), router_bias[l].reshape(N_EXPERTS, 1),
            sh_w_gate[l].astype(BF16), sh_w_up[l].astype(BF16), sh_w_down[l].astype(BF16))

        n_blocks = (T * TOP_K + N_EXPERTS * (MOE_BLOCK - 1) + MOE_BLOCK - 1) // MOE_BLOCK
        n_slots = n_blocks * MOE_BLOCK
        padded, pends, pstarts, n_valid = _block_table(counts[:, 0])
        pos_t = _pos(idx_t, rank_t, pstarts)
        xs = _dispatch(pends, padded, pos_t, h2p.reshape(T, ROW_CHUNKS, LANE), n_slots)
        ye = _experts(pstarts // MOE_BLOCK, padded // MOE_BLOCK, n_valid,
                      xs.reshape(n_slots * ROW_CHUNKS, LANE), exp_w_gate[l], exp_w_up[l], exp_w_down[l])
    return _final(pos_t, gates, x1, gt2, g_final, ye.reshape(n_slots, ROW_CHUNKS, LANE))
```

```python
import functools

import jax
import jax.numpy as jnp
from jax import lax
from jax.experimental import pallas as pl
from jax.experimental.pallas import tpu as pltpu

F32 = jnp.float32
BF16 = jnp.bfloat16

D_MODEL = 1024
D_S5 = 512
S5_GROUP = 16
S5_GROUPS = 32
S5_STATE = 64
S5_CHUNK = 16
D_GLA = 512
GLA_HEADS = 4
GLA_DK = 64
GLA_DV = 128
GLA_QK = 256
GLA_LOWRANK = 16
GLA_TAU = 16.0
GLA_CHUNK = 64
LANE = 128
N_EXPERTS = 256
TOP_K = 8
N_GROUPS = 8
TOPK_GROUPS = 4
D_EXPERT = 256
ROUTED_SCALE = 2.5
EPS = 1e-6
MOE_BLOCK = 256
ROW_CHUNKS = D_MODEL // 2 // LANE
VMEM_LIMIT = 48 * 1024 * 1024


def _silu(x):
    return x * jax.nn.sigmoid(x)


def _params(*sem):
    return pltpu.CompilerParams(dimension_semantics=sem, vmem_limit_bytes=VMEM_LIMIT)


def _ada_kernel(c_ref, w_ref, b_ref, o_ref):
    s = _silu(c_ref[...]).astype(BF16)
    o_ref[...] = jnp.dot(s, w_ref[...].astype(BF16), preferred_element_type=F32) + b_ref[...]


def _ada(c, w, b):
    bsz, d = c.shape
    n = w.shape[1]
    tn = 1024
    return pl.pallas_call(
        _ada_kernel,
        grid=(n // tn,),
        in_specs=[pl.BlockSpec((bsz, d), lambda j: (0, 0)),
                  pl.BlockSpec((d, tn), lambda j: (0, j)),
                  pl.BlockSpec((1, tn), lambda j: (0, j))],
        out_specs=pl.BlockSpec((bsz, tn), lambda j: (0, j)),
        out_shape=jax.ShapeDtypeStruct((bsz, n), F32),
        compiler_params=_params("arbitrary"),
        name="ada",
    )(c, w, b.reshape(1, n))


def _inproj_kernel(x_ref, g_ref, sh_ref, sc_ref, w_ref,
                   u_ref, q_ref, k_ref, v_ref, r_ref, glr_ref):
    x = x_ref[0]
    ms = jnp.mean(x * x, axis=-1, keepdims=True)
    h = (x * lax.rsqrt(ms + EPS)) * g_ref[...]
    h = h * (1.0 + sc_ref[0]) + sh_ref[0]
    hb = h.astype(BF16)
    col = 0
    for ref in (u_ref, q_ref, k_ref, v_ref, r_ref, glr_ref):
        n = ref.shape[-1]
        ref[0] = jnp.dot(hb, w_ref[:, col:col + n], preferred_element_type=F32).astype(ref.dtype)
        col += n


def _inproj(x, g, sh, sc, w):
    bsz, L, d = x.shape
    tt = 512
    widths = (D_S5, GLA_QK, GLA_QK, D_GLA, D_GLA, LANE)
    tok = lambda n: pl.BlockSpec((1, tt, n), lambda b, i: (b, i, 0))
    vec = pl.BlockSpec((1, 1, d), lambda b, i: (b, 0, 0))
    return pl.pallas_call(
        _inproj_kernel,
        grid=(bsz, L // tt),
        in_specs=[tok(d), pl.BlockSpec((1, d), lambda b, i: (0, 0)), vec, vec,
                  pl.BlockSpec(w.shape, lambda b, i: (0, 0))],
        out_specs=[tok(n) for n in widths],
        out_shape=[jax.ShapeDtypeStruct((bsz, L, n), BF16) for n in widths],
        compiler_params=_params("arbitrary", "arbitrary"),
        name="inproj",
    )(x, g.reshape(1, d), sh, sc, w)


def _s5_prep(lam_re, lam_im, log_dt, b_re, b_im, c_re, c_im, d_skip):
    G, N, C, TC = S5_GROUPS, S5_STATE, S5_GROUP, S5_CHUNK
    hp = lax.Precision.HIGHEST
    dt = jnp.exp(log_dt)[:, None]
    lr, li = lam_re, lam_im
    mag = jnp.exp(lr * dt)
    ab_re, ab_im = mag * jnp.cos(li * dt), mag * jnp.sin(li * dt)
    den = lr * lr + li * li
    nr = ab_re - 1.0
    coef_re = ((nr * lr + ab_im * li) / den)[..., None]
    coef_im = ((ab_im * lr - nr * li) / den)[..., None]
    bb_re = coef_re * b_re - coef_im * b_im
    bb_im = coef_re * b_im + coef_im * b_re
    p = jnp.arange(TC + 1, dtype=F32)[:, None, None]
    pm = jnp.exp(lr * dt * p)
    pr, pi = pm * jnp.cos(li * dt * p), pm * jnp.sin(li * dt * p)
    ca_re = c_re[None] * pr[:, :, None, :] - c_im[None] * pi[:, :, None, :]
    ca_im = c_re[None] * pi[:, :, None, :] + c_im[None] * pr[:, :, None, :]
    kern = (jnp.einsum('tgon,gni->tgoi', ca_re, bb_re, precision=hp)
            - jnp.einsum('tgon,gni->tgoi', ca_im, bb_im, precision=hp))
    kern = kern.at[0].add(jnp.eye(C, dtype=F32)[None] * d_skip[:, :, None])
    s_idx = jnp.arange(TC)[:, None]
    t_idx = jnp.arange(TC)[None, :]
    lag = t_idx - s_idx
    toep = jnp.where((lag >= 0)[:, :, None, None, None], kern[jnp.maximum(lag, 0)], 0.0)
    toep = toep.transpose(2, 0, 4, 1, 3).reshape(G, TC * C, TC * C)
    rr, ri = pr[TC - 1 - jnp.arange(TC)], pi[TC - 1 - jnp.arange(TC)]
    binc_re = rr[..., None] * bb_re[None] - ri[..., None] * bb_im[None]
    binc_im = rr[..., None] * bb_im[None] + ri[..., None] * bb_re[None]
    binc_re = binc_re.transpose(1, 0, 3, 2).reshape(G, TC * C, N)
    binc_im = binc_im.transpose(1, 0, 3, 2).reshape(G, TC * C, N)
    cm_re = ca_re[1:].transpose(1, 3, 0, 2).reshape(G, N, TC * C)
    cm_im = (-ca_im[1:]).transpose(1, 3, 0, 2).reshape(G, N, TC * C)
    return (toep.astype(BF16), binc_re.astype(BF16), binc_im.astype(BF16),
            cm_re.astype(BF16), cm_im.astype(BF16), pr[TC], pi[TC])


def _s5_kernel(z_ref, toep_ref, bre_ref, bim_ref, cre_ref, cim_ref, ar_ref, ai_ref,
               y_ref, incr_ref, inci_ref, hsr_ref, hsi_ref, *, bsz, n_chunks):
    z = z_ref[0]
    incr_ref[...] = jnp.dot(z, bre_ref[0], preferred_element_type=F32)
    inci_ref[...] = jnp.dot(z, bim_ref[0], preferred_element_type=F32)
    ar = ar_ref[0]
    ai = ai_ref[0]

    def step(j, carry):
        hr, hi = carry
        rows = pl.ds(pl.multiple_of(j * bsz, bsz), bsz)
        hsr_ref[rows, :] = hr
        hsi_ref[rows, :] = hi
        return (ar * hr - ai * hi + incr_ref[rows, :], ar * hi + ai * hr + inci_ref[rows, :])

    zero = jnp.zeros((bsz, S5_STATE), F32)
    lax.fori_loop(0, n_chunks, step, (zero, zero), unroll=4)
    y = jnp.dot(z, toep_ref[0], preferred_element_type=F32)
    y += jnp.dot(hsr_ref[...].astype(BF16), cre_ref[0], preferred_element_type=F32)
    y += jnp.dot(hsi_ref[...].astype(BF16), cim_ref[0], preferred_element_type=F32)
    y_ref[0] = y.astype(y_ref.dtype)


def _s5_core(z, mats, bsz):
    toep, bre, bim, cre, cim, ar, ai = mats
    G, rows, w = z.shape
    n_chunks = rows // bsz
    N = S5_STATE
    arb = jnp.broadcast_to(ar[:, None, :], (G, bsz, N))
    aib = jnp.broadcast_to(ai[:, None, :], (G, bsz, N))
    grp = lambda a, b: pl.BlockSpec((1, a, b), lambda g: (g, 0, 0))
    return pl.pallas_call(
        functools.partial(_s5_kernel, bsz=bsz, n_chunks=n_chunks),
        grid=(G,),
        in_specs=[grp(rows, w), grp(w, w), grp(w, N), grp(w, N), grp(N, w), grp(N, w),
                  grp(bsz, N), grp(bsz, N)],
        out_specs=grp(rows, w),
        out_shape=jax.ShapeDtypeStruct((G, rows, w), BF16),
        scratch_shapes=[pltpu.VMEM((rows, N), F32)] * 4,
        compiler_params=_params("arbitrary"),
        name="s5_core",
    )(z, toep, bre, bim, cre, cim, arb, aib)


def _gla_kernel(q_ref, k_ref, v_ref, glr_ref, r_ref, wg_ref, bg_ref, ng_ref, o_ref, st_ref):
    lt = q_ref.shape[1]
    C = GLA_CHUNK

    @pl.when(pl.program_id(1) == 0)
    def _():
        st_ref[...] = jnp.zeros_like(st_ref)

    z = jnp.dot(glr_ref[0], wg_ref[...], preferred_element_type=F32) + bg_ref[...]
    log_a = (jnp.minimum(z, 0.0) - jnp.log(1.0 + jnp.exp(-jnp.abs(z)))) * (1.0 / GLA_TAU)
    ri = lax.broadcasted_iota(jnp.int32, (lt, lt), 0)
    ci = lax.broadcasted_iota(jnp.int32, (lt, lt), 1)
    tril = jnp.where(((ri >> 6) == (ci >> 6)) & (ci <= ri), 1.0, 0.0).astype(BF16)
    la_hi = log_a.astype(BF16)
    la_lo = (log_a - la_hi.astype(F32)).astype(BF16)
    bcum = (jnp.dot(tril, la_hi, preferred_element_type=F32)
            + jnp.dot(tril, la_lo, preferred_element_type=F32))
    q = q_ref[0].astype(F32) * (GLA_DK ** -0.5)
    k = k_ref[0].astype(F32)
    qi = q * jnp.exp(bcum)
    ki = k * jnp.exp(-bcum)
    lane_head = lax.broadcasted_iota(jnp.int32, (1, GLA_QK), 1) >> 6
    causal = ((lax.broadcasted_iota(jnp.int32, (GLA_HEADS * C, C), 0) & (C - 1))
              >= lax.broadcasted_iota(jnp.int32, (GLA_HEADS * C, C), 1))
    same_head = ((lax.broadcasted_iota(jnp.int32, (D_GLA, GLA_QK), 0) >> 7)
                 == (lax.broadcasted_iota(jnp.int32, (D_GLA, GLA_QK), 1) >> 6))
    nt = (((1,), (1,)), ((), ()))
    ng = ng_ref[...]
    for c in range(lt // C):
        sl = slice(c * C, (c + 1) * C)
        bc = bcum[sl]
        bl = bc[C - 1:C, :]
        kd = k[sl] * jnp.exp(bl - bc)
        qic = qi[sl]
        qs = jnp.concatenate([jnp.where(lane_head == h, qic, 0.0) for h in range(GLA_HEADS)],
                             axis=0).astype(BF16)
        sc = lax.dot_general(qs, ki[sl].astype(BF16), nt, preferred_element_type=F32)
        p = jnp.where(causal, sc, 0.0).astype(BF16)
        vc = v_ref[0, sl, :]
        o_intra = jnp.concatenate(
            [jnp.dot(p[h * C:(h + 1) * C], vc[:, h * GLA_DV:(h + 1) * GLA_DV], preferred_element_type=F32)
             for h in range(GLA_HEADS)], axis=1)
        st = st_ref[...]
        o_inter = lax.dot_general(qic.astype(BF16), st.astype(BF16), nt, preferred_element_type=F32)
        v_t = vc.astype(F32).T.astype(BF16)
        kv_t = jnp.dot(v_t, kd.astype(BF16), preferred_element_type=F32)
        st_ref[...] = st * jnp.exp(bl) + jnp.where(same_head, kv_t, 0.0)
        o = o_intra + o_inter
        parts = []
        for h in range(GLA_HEADS):
            oh = o[:, h * GLA_DV:(h + 1) * GLA_DV]
            oh = oh * lax.rsqrt(jnp.mean(oh * oh, axis=-1, keepdims=True) + EPS)
            parts.append(oh * ng)
        r = r_ref[0, sl, :].astype(F32)
        o_ref[0, sl, :] = (jnp.concatenate(parts, axis=1) * _silu(r)).astype(o_ref.dtype)


def _gla(q, k, v, glr, r, wg, bg, ng):
    bsz, L, _ = q.shape
    lt = 256
    tok = lambda n: pl.BlockSpec((1, lt, n), lambda b, i: (b, i, 0))
    full = lambda a: pl.BlockSpec(a.shape, lambda b, i: (0,) * a.ndim)
    return pl.pallas_call(
        _gla_kernel,
        grid=(bsz, L // lt),
        in_specs=[tok(GLA_QK), tok(GLA_QK), tok(D_GLA), tok(LANE), tok(D_GLA), full(wg), full(bg), full(ng)],
        out_specs=tok(D_GLA),
        out_shape=jax.ShapeDtypeStruct((bsz, L, D_GLA), BF16),
        scratch_shapes=[pltpu.VMEM((D_GLA, GLA_QK), F32)],
        compiler_params=_params("arbitrary", "arbitrary"),
        name="gla",
    )(q, k, v, glr, r, wg, bg, ng)


def _pack_bf16_pairs(x):
    w = x.shape[1] // 2
    xr = x.astype(BF16).astype(F32)
    lo = lax.bitcast_convert_type(xr[:, :w], jnp.uint32) >> 16
    hi = lax.bitcast_convert_type(xr[:, w:], jnp.uint32) & jnp.uint32(0xFFFF0000)
    return lo | hi


def _unpack_bf16_pairs(p):
    lo = lax.bitcast_convert_type(p << 16, F32)
    hi = lax.bitcast_convert_type(p & jnp.uint32(0xFFFF0000), F32)
    return lo, hi


def _store_rows(ref, packed):
    n = packed.shape[0]
    for c in range(ROW_CHUNKS):
        ref[pl.ds(c, n, stride=ROW_CHUNKS), :] = packed[:, c * LANE:(c + 1) * LANE]


def _load_rows(ref, n):
    return jnp.concatenate([ref[pl.ds(c, n, stride=ROW_CHUNKS), :] for c in range(ROW_CHUNKS)], axis=1)


def _route_tile(lg_t, bias_col, tri, carry_ref):
    n_e, tt = lg_t.shape
    per_group = n_e // N_GROUPS
    scores = jax.nn.sigmoid(lg_t)
    biased = scores + bias_col
    row = lax.broadcasted_iota(jnp.int32, (n_e, tt), 0)
    neg = -jnp.inf
    group_score = []
    for g in range(N_GROUPS):
        b = biased[g * per_group:(g + 1) * per_group]
        r = lax.broadcasted_iota(jnp.int32, (per_group, tt), 0) + g * per_group
        m1 = jnp.max(b, axis=0, keepdims=True)
        i1 = jnp.min(jnp.where(b == m1, r, n_e), axis=0, keepdims=True)
        m2 = jnp.max(jnp.where(r == i1, neg, b), axis=0, keepdims=True)
        group_score.append(m1 + m2)
    parts = []
    for g in range(N_GROUPS):
        ahead = jnp.zeros((1, tt), jnp.int32)
        for g2 in range(N_GROUPS):
            if g2 != g:
                beats = (group_score[g2] >= group_score[g]) if g2 < g else (group_score[g2] > group_score[g])
                ahead = ahead + beats.astype(jnp.int32)
        parts.append(jnp.where(ahead < TOPK_GROUPS, biased[g * per_group:(g + 1) * per_group], neg))
    masked = jnp.concatenate(parts, axis=0)
    work = masked
    idxs = []
    for _ in range(TOP_K):
        m = jnp.max(work, axis=0, keepdims=True)
        ii = jnp.min(jnp.where(work == m, row, n_e), axis=0, keepdims=True)
        idxs.append(ii)
        work = jnp.where(row == ii, neg, work)
    sel = work != masked
    w = jnp.where(sel, scores, 0.0)
    gate_dense = w / jnp.sum(w, axis=0, keepdims=True) * ROUTED_SCALE
    mt = jnp.where(sel, 1.0, 0.0)
    rank_dense = jnp.dot(mt.astype(BF16), tri, preferred_element_type=F32) + carry_ref[...]
    carry_ref[...] += jnp.sum(mt, axis=1, keepdims=True)
    ranks, gts = [], []
    for ii in idxs:
        oh = row == ii
        ranks.append(jnp.sum(jnp.where(oh, rank_dense, 0.0), axis=0, keepdims=True))
        gts.append(jnp.sum(jnp.where(oh, gate_dense, 0.0), axis=0, keepdims=True))
    idx_t = jnp.concatenate(idxs, axis=0)
    rank_t = jnp.concatenate(ranks, axis=0).astype(jnp.int32)
    gate_t = jnp.concatenate(gts + [jnp.zeros((LANE - TOP_K, tt), F32)], axis=0)
    return idx_t, rank_t, gate_t.T[:, :TOP_K]


def _mix_kernel(ys_ref, yg_ref, x_ref, wglu_ref, bglu_ref, sng_ref, wo_ref, gt1_ref,
                gn_ref, sh2_ref, sc2_ref, gt2_ref, wrt_ref, rb_ref, tri_ref, wsg_ref, wsu_ref, wsd_ref,
                x1_ref, h2_ref, idx_ref, rank_ref, gate_ref, cnt_ref, carry_ref):
    @pl.when((pl.program_id(0) == 0) & (pl.program_id(1) == 0))
    def _():
        carry_ref[...] = jnp.zeros_like(carry_ref)

    y = ys_ref[0].astype(F32)
    g = y * (0.5 * (1.0 + jnp.tanh(0.7978845608028654 * (y + 0.044715 * (y * y * y)))))
    gate = jax.nn.sigmoid(jnp.dot(g.astype(BF16), wglu_ref[...], preferred_element_type=F32) + bglu_ref[...])
    out = g * gate
    out = out * lax.rsqrt(jnp.mean(out * out, axis=-1, keepdims=True) + EPS) * sng_ref[...]
    mix = (jnp.dot(out.astype(BF16), wo_ref[0:D_S5, :], preferred_element_type=F32)
           + jnp.dot(yg_ref[0], wo_ref[D_S5:, :], preferred_element_type=F32))
    x1 = x_ref[0] + gt1_ref[0] * mix
    h2 = x1 * lax.rsqrt(jnp.mean(x1 * x1, axis=-1, keepdims=True) + EPS) * gn_ref[...]
    h2 = h2 * (1.0 + sc2_ref[0]) + sh2_ref[0]
    hb = h2.astype(BF16)
    _store_rows(h2_ref, _pack_bf16_pairs(h2))
    lg_t = lax.dot_general(wrt_ref[...], hb, (((1,), (1,)), ((), ())), preferred_element_type=F32)
    idx_t, rank_t, gates = _route_tile(lg_t, rb_ref[...], tri_ref[...], carry_ref)
    idx_ref[...] = idx_t
    rank_ref[...] = rank_t
    gate_ref[...] = gates
    cnt_ref[...] = carry_ref[...]
    a = _silu(jnp.dot(hb, wsg_ref[...], preferred_element_type=F32)) * jnp.dot(
        hb, wsu_ref[...], preferred_element_type=F32)
    shared = jnp.dot(a.astype(BF16), wsd_ref[...], preferred_element_type=F32)
    x1_ref[0] = x1 + gt2_ref[0] * shared


def _mix(ys, yg, x, wglu, bglu, sng, wo, gt1, gn, sh2, sc2, gt2, wrt, rb, wsg, wsu, wsd):
    bsz, L, d = x.shape
    tt = 512
    nt = L // tt
    T = bsz * L
    tri = (jnp.arange(tt)[:, None] < jnp.arange(tt)[None, :]).astype(BF16)
    tok = lambda n: pl.BlockSpec((1, tt, n), lambda b, i: (b, i, 0))
    vec = pl.BlockSpec((1, 1, d), lambda b, i: (b, 0, 0))
    full = lambda a: pl.BlockSpec(a.shape, lambda b, i: (0,) * a.ndim)
    lanes = pl.BlockSpec((TOP_K, tt), lambda b, i: (0, b * nt + i))
    return pl.pallas_call(
        _mix_kernel,
        grid=(bsz, nt),
        in_specs=[tok(D_S5), tok(D_GLA), tok(d), full(wglu), full(bglu), full(sng), full(wo), vec,
                  full(gn), vec, vec, vec, full(wrt), full(rb), full(tri), full(wsg), full(wsu), full(wsd)],
        out_specs=[tok(d),
                   pl.BlockSpec((tt * ROW_CHUNKS, LANE), lambda b, i: (b * nt + i, 0)),
                   lanes, lanes,
                   pl.BlockSpec((tt, TOP_K), lambda b, i: (b * nt + i, 0)),
                   pl.BlockSpec((N_EXPERTS, 1), lambda b, i: (0, 0))],
        out_shape=[jax.ShapeDtypeStruct((bsz, L, d), F32),
                   jax.ShapeDtypeStruct((T * ROW_CHUNKS, LANE), jnp.uint32),
                   jax.ShapeDtypeStruct((TOP_K, T), jnp.int32),
                   jax.ShapeDtypeStruct((TOP_K, T), jnp.int32),
                   jax.ShapeDtypeStruct((T, TOP_K), F32),
                   jax.ShapeDtypeStruct((N_EXPERTS, 1), F32)],
        scratch_shapes=[pltpu.VMEM((N_EXPERTS, 1), F32)],
        compiler_params=_params("arbitrary", "arbitrary"),
        name="mix",
    )(ys, yg, x, wglu, bglu, sng, wo, gt1, gn, sh2, sc2, gt2, wrt, rb, tri, wsg, wsu, wsd)


def _pos_kernel(idx_ref, rank_ref, ps_ref, pos_ref):
    n_e = ps_ref.shape[0]
    tt = idx_ref.shape[1]
    row = lax.broadcasted_iota(jnp.int32, (n_e, tt), 0)
    ps = ps_ref[...]
    starts = [jnp.sum(jnp.where(row == idx_ref[k:k + 1, :], ps, 0.0), axis=0, keepdims=True)
              for k in range(TOP_K)]
    pos = jnp.concatenate(starts, axis=0).astype(jnp.int32) + rank_ref[...]
    for jh in range(tt // LANE):
        pos_ref[jh * TOP_K:(jh + 1) * TOP_K, :] = pos[:, jh * LANE:(jh + 1) * LANE]


def _pos(idx_t, rank_t, pstart):
    T = idx_t.shape[1]
    tt = 2048
    blk = pl.BlockSpec((TOP_K, tt), lambda i: (0, i))
    return pl.pallas_call(
        _pos_kernel,
        grid=(T // tt,),
        in_specs=[blk, blk, pl.BlockSpec((N_EXPERTS, 1), lambda i: (0, 0))],
        out_specs=pl.BlockSpec((tt // LANE * TOP_K, LANE), lambda i: (i, 0)),
        out_shape=jax.ShapeDtypeStruct((T // LANE * TOP_K, LANE), jnp.int32),
        compiler_params=_params("arbitrary"),
        name="pos",
    )(idx_t, rank_t, pstart.astype(F32).reshape(N_EXPERTS, 1))


def _dispatch_kernel(pend_ref, padded_ref, pos_ref, h_ref, xs_hbm, zero_ref, pos_smem, sem_pos, sem_zero, sem_row):
    tt = h_ref.shape[0]
    step = pl.program_id(0)
    load_pos = pltpu.make_async_copy(pos_ref, pos_smem, sem_pos)
    load_pos.start()

    def zero_block(start):
        return pltpu.make_async_copy(zero_ref, xs_hbm.at[pl.ds(pl.multiple_of(start, MOE_BLOCK), MOE_BLOCK)],
                                     sem_zero)

    def for_each_zero_block(fn):
        def per_expert(e, c):
            @pl.when(padded_ref[e] > 0)
            def _():
                fn(zero_block(pend_ref[e] - MOE_BLOCK))
            return c

        def per_tail_block(g, c):
            fn(zero_block(g * MOE_BLOCK))
            return c

        lax.fori_loop(0, N_EXPERTS, per_expert, 0)
        lax.fori_loop(pend_ref[N_EXPERTS - 1] // MOE_BLOCK, xs_hbm.shape[0] // MOE_BLOCK, per_tail_block, 0)

    @pl.when(step == 0)
    def _():
        zero_ref[...] = jnp.zeros_like(zero_ref)
        for_each_zero_block(lambda cp: cp.start())
        for_each_zero_block(lambda cp: cp.wait())

    load_pos.wait()
    for jh in range(tt // LANE):
        def issue(jl, c, jh=jh):
            for k in range(TOP_K):
                slot = pos_smem[(jh * TOP_K + k) * LANE + jl]
                pltpu.make_async_copy(h_ref.at[jh * LANE + jl], xs_hbm.at[slot], sem_row).start(priority=k % 2)
            return c

        lax.fori_loop(0, LANE, issue, 0, unroll=4)
    for k in range(TOP_K):
        pltpu.make_async_copy(h_ref, xs_hbm.at[pl.ds(0, tt)], sem_row).wait()


def _dispatch(pends, padded, pos_t, h_rows, n_slots):
    T = h_rows.shape[0]
    tt = 512
    return pl.pallas_call(
        _dispatch_kernel,
        grid_spec=pltpu.PrefetchScalarGridSpec(
            num_scalar_prefetch=2,
            grid=(T // tt,),
            in_specs=[pl.BlockSpec((tt * TOP_K,), lambda i, pe, pa: (i,)),
                      pl.BlockSpec((tt, ROW_CHUNKS, LANE), lambda i, pe, pa: (i, 0, 0))],
            out_specs=pl.BlockSpec(memory_space=pl.ANY),
            scratch_shapes=[pltpu.VMEM((MOE_BLOCK, ROW_CHUNKS, LANE), jnp.uint32),
                            pltpu.SMEM((tt * TOP_K,), jnp.int32),
                            pltpu.SemaphoreType.DMA, pltpu.SemaphoreType.DMA, pltpu.SemaphoreType.DMA]),
        out_shape=jax.ShapeDtypeStruct((n_slots, ROW_CHUNKS, LANE), jnp.uint32),
        compiler_params=_params("arbitrary"),
        name="dispatch",
    )(pends, padded, pos_t, h_rows)


X_BUFS = 3
Y_BUFS = 2


def _expert_kernel(b0_ref, nb_ref, nv_ref, wg_ref, wu_ref, wd_ref, xs_hbm, ye_hbm,
                   wg_s, wu_s, wd_s, xbuf, ybuf, semx, semy, *, n_blocks):
    e = pl.program_id(0)
    nb = nb_ref[e]
    b0 = b0_ref[e]
    nv = nv_ref[0]
    rows = MOE_BLOCK * ROW_CHUNKS
    ahead = X_BUFS - 1

    def block(ref, g):
        return ref.at[pl.ds(pl.multiple_of(g * rows, rows), rows)]

    def x_copy(g):
        slot = g % X_BUFS
        return pltpu.make_async_copy(block(xs_hbm, g), xbuf.at[slot], semx.at[slot])

    def y_copy(g):
        slot = g % Y_BUFS
        return pltpu.make_async_copy(ybuf.at[slot], block(ye_hbm, g), semy.at[slot])

    @pl.when(e == 0)
    def _():
        for g in range(ahead):
            x_copy(g).start()

    @pl.when(nb > 0)
    def _():
        wg_s[...] = wg_ref[0].astype(BF16)
        wu_s[...] = wu_ref[0].astype(BF16)
        wd_s[...] = wd_ref[0].astype(BF16)

    def body(i, c):
        g = b0 + i
        x_copy(g).wait()

        @pl.when(g + ahead < nv)
        def _():
            x_copy(g + ahead).start()

        lo, hi = _unpack_bf16_pairs(_load_rows(xbuf.at[g % X_BUFS], MOE_BLOCK))
        x = jnp.concatenate([lo, hi], axis=1).astype(BF16)
        a = jnp.dot(x, wg_s[...], preferred_element_type=F32)
        u = jnp.dot(x, wu_s[...], preferred_element_type=F32)
        h = (_silu(a) * u).astype(BF16)
        y = jnp.dot(h, wd_s[...], preferred_element_type=F32)

        @pl.when(g >= Y_BUFS)
        def _():
            y_copy(g - Y_BUFS).wait()

        _store_rows(ybuf.at[g % Y_BUFS], _pack_bf16_pairs(y))
        y_copy(g).start()
        return c

    lax.fori_loop(0, nb, body, 0)

    @pl.when(e == pl.num_programs(0) - 1)
    def _():
        for back in range(Y_BUFS, 0, -1):
            y_copy(nv - back).wait()
        ybuf[0] = jnp.zeros(ybuf.shape[1:], ybuf.dtype)

        def fill(g):
            return pltpu.make_async_copy(ybuf.at[0], block(ye_hbm, g), semy.at[0])

        lax.fori_loop(nv, n_blocks, lambda g, c: (fill(g).start(), c)[1], 0)
        lax.fori_loop(nv, n_blocks, lambda g, c: (fill(g).wait(), c)[1], 0)


def _experts(first_block, num_blocks, n_valid, xs, wg, wu, wd):
    rows = MOE_BLOCK * ROW_CHUNKS
    n_blocks = xs.shape[0] // rows
    n_e, d, _ = wg.shape
    wsel = lambda e, b0, nb, nv: (e, 0, 0)
    return pl.pallas_call(
        functools.partial(_expert_kernel, n_blocks=n_blocks),
        grid_spec=pltpu.PrefetchScalarGridSpec(
            num_scalar_prefetch=3,
            grid=(n_e,),
            in_specs=[pl.BlockSpec((1, d, D_EXPERT), wsel),
                      pl.BlockSpec((1, d, D_EXPERT), wsel),
                      pl.BlockSpec((1, D_EXPERT, d), wsel),
                      pl.BlockSpec(memory_space=pl.ANY)],
            out_specs=pl.BlockSpec(memory_space=pl.ANY),
            scratch_shapes=[pltpu.VMEM((d, D_EXPERT), BF16), pltpu.VMEM((d, D_EXPERT), BF16),
                            pltpu.VMEM((D_EXPERT, d), BF16),
                            pltpu.VMEM((X_BUFS, rows, LANE), jnp.uint32),
                            pltpu.VMEM((Y_BUFS, rows, LANE), jnp.uint32),
                            pltpu.SemaphoreType.DMA((X_BUFS,)), pltpu.SemaphoreType.DMA((Y_BUFS,))]),
        out_shape=jax.ShapeDtypeStruct(xs.shape, jnp.uint32),
        compiler_params=_params("arbitrary"),
        name="experts",
    )(first_block, num_blocks, n_valid, wg, wu, wd, xs)


def _final_kernel(pos0_ref, posn_ref, gate_ref, x_ref, gt_ref, g_ref, ye_hbm, o_ref,
                  buf_ref, pos_smem, sem_pos, sem_row):
    tt = x_ref.shape[1]
    i = pl.program_id(0)

    def gather_tile(pos_ref, tile):
        base = (tile % 2) * (TOP_K * tt)
        sem = sem_row.at[tile % 2]
        load_pos = pltpu.make_async_copy(pos_ref, pos_smem, sem_pos)
        load_pos.start()
        load_pos.wait()
        for jh in range(tt // LANE):
            def issue(jl, c, jh=jh):
                for k in range(TOP_K):
                    slot = pos_smem[(jh * TOP_K + k) * LANE + jl]
                    pltpu.make_async_copy(ye_hbm.at[slot], buf_ref.at[base + k * tt + jh * LANE + jl],
                                          sem).start(priority=k % 2)
                return c

            lax.fori_loop(0, LANE, issue, 0, unroll=4)

    @pl.when(i == 0)
    def _():
        gather_tile(pos0_ref, i)

    @pl.when(i + 1 < pl.num_programs(0))
    def _():
        gather_tile(posn_ref, i + 1)

    base = pl.multiple_of((i % 2) * (TOP_K * tt), TOP_K * tt)
    for k in range(TOP_K):
        pltpu.make_async_copy(ye_hbm.at[pl.ds(0, tt)], buf_ref.at[pl.ds(base + k * tt, tt)],
                              sem_row.at[i % 2]).wait()
    gates = gate_ref[...]
    half = D_MODEL // 2
    acc_lo = jnp.zeros((tt, half), F32)
    acc_hi = jnp.zeros((tt, half), F32)
    rows_2d = buf_ref.reshape(buf_ref.shape[0] * ROW_CHUNKS, LANE)
    for k in range(TOP_K):
        lo, hi = _unpack_bf16_pairs(_load_rows(rows_2d.at[pl.ds((base + k * tt) * ROW_CHUNKS, tt * ROW_CHUNKS)], tt))
        gk = gates[:, k:k + 1]
        acc_lo += gk * lo
        acc_hi += gk * hi
    x = x_ref[0] + gt_ref[0] * jnp.concatenate([acc_lo, acc_hi], axis=1)
    o_ref[0] = x * lax.rsqrt(jnp.mean(x * x, axis=-1, keepdims=True) + EPS) * g_ref[...]


def _final(pos_t, gates, x1, gt2, g, ye_rows):
    bsz, L, d = x1.shape
    tt = 256
    nt = L // tt
    n = bsz * nt
    tok = pl.BlockSpec((1, tt, d), lambda i: (i // nt, i % nt, 0))
    return pl.pallas_call(
        _final_kernel,
        grid=(n,),
        in_specs=[pl.BlockSpec((tt * TOP_K,), lambda i: (0,)),
                  pl.BlockSpec((tt * TOP_K,), lambda i: (jnp.minimum(i + 1, n - 1),)),
                  pl.BlockSpec((tt, TOP_K), lambda i: (i, 0)),
                  tok, pl.BlockSpec((1, 1, d), lambda i: (i // nt, 0, 0)),
                  pl.BlockSpec((1, d), lambda i: (0, 0)),
                  pl.BlockSpec(memory_space=pl.ANY)],
        out_specs=tok,
        out_shape=jax.ShapeDtypeStruct((bsz, L, d), F32),
        scratch_shapes=[pltpu.VMEM((2 * TOP_K * tt, ROW_CHUNKS, LANE), jnp.uint32),
                        pltpu.SMEM((tt * TOP_K,), jnp.int32),
                        pltpu.SemaphoreType.DMA, pltpu.SemaphoreType.DMA((2,))],
        compiler_params=_params("arbitrary"),
        name="final",
    )(pos_t, pos_t, gates, x1, gt2, g.reshape(1, d), ye_rows)


def _block_table(counts):
    counts = counts.astype(jnp.int32)
    padded = (counts + MOE_BLOCK - 1) // MOE_BLOCK * MOE_BLOCK
    pends = jnp.cumsum(padded)
    pstarts = pends - padded
    n_valid = (pends[-1] // MOE_BLOCK).reshape(1)
    return padded, pends, pstarts, n_valid


def kernel(x, c, w_ada, b_ada, g_norm_mix, w_in, s5_lambda_re, s5_lambda_im, s5_log_dt, s5_b_re, s5_b_im, s5_c_re, s5_c_im, s5_d, s5_w_glu, s5_b_glu, s5_norm_g, gla_w_g2, gla_b_g2, gla_norm_g, w_out, g_norm_moe, w_router, router_bias, exp_w_gate, exp_w_up, exp_w_down, sh_w_gate, sh_w_up, sh_w_down, g_final):
    bsz, L, d = x.shape
    T = bsz * L
    assert w_ada.shape[0] == 1, "single-layer block"
    for l in range(1):
        mod = _ada(c, w_ada[l], b_ada[l])
        sh1, sc1, gt1, sh2, sc2, gt2 = [m.reshape(bsz, 1, d) for m in jnp.split(mod, 6, axis=-1)]

        wi = w_in[l]
        o_q, o_k, o_v, o_g, o_r = D_S5, D_S5 + GLA_QK, D_S5 + 2 * GLA_QK, D_S5 + 2 * GLA_QK + D_GLA, \
            D_S5 + 2 * GLA_QK + D_GLA + GLA_LOWRANK
        w_cat = jnp.concatenate([wi[:, :o_g], wi[:, o_r:], wi[:, o_g:o_r],
                                 jnp.zeros((d, LANE - GLA_LOWRANK), wi.dtype)], axis=1).astype(BF16)
        u, q, k, v, r, glr = _inproj(x, g_norm_mix[l], sh1, sc1, w_cat)

        n_chunks = L // S5_CHUNK
        mats = _s5_prep(s5_lambda_re[l], s5_lambda_im[l], s5_log_dt[l], s5_b_re[l], s5_b_im[l],
                        s5_c_re[l], s5_c_im[l], s5_d[l])
        z = u.reshape(bsz, n_chunks, S5_CHUNK, S5_GROUPS, S5_GROUP).transpose(3, 1, 0, 2, 4)
        z = z.reshape(S5_GROUPS, n_chunks * bsz, S5_CHUNK * S5_GROUP)
        ys = _s5_core(z, mats, bsz)
        ys = ys.reshape(S5_GROUPS, n_chunks, bsz, S5_CHUNK, S5_GROUP).transpose(2, 1, 3, 0, 4)
        ys = ys.reshape(bsz, L, D_S5)

        wg2 = jnp.concatenate([gla_w_g2[l], jnp.zeros((LANE - GLA_LOWRANK, GLA_QK), F32)], axis=0).astype(BF16)
        yg = _gla(q, k, v, glr, r, wg2, gla_b_g2[l].reshape(1, GLA_QK), gla_norm_g[l].reshape(1, GLA_DV))

        x1, h2p, idx_t, rank_t, gates, counts = _mix(
            ys, yg, x, s5_w_glu[l].astype(BF16), s5_b_glu[l].reshape(1, D_S5), s5_norm_g[l].reshape(1, D_S5),
            w_out[l].astype(BF16), gt1, g_norm_moe[l].reshape(1, d), sh2, sc2, gt2,
            w_router[l].T.astype(BF16), router_bias[l].reshape(N_EXPERTS, 1),
            sh_w_gate[l].astype(BF16), sh_w_up[l].astype(BF16), sh_w_down[l].astype(BF16))

        n_blocks = (T * TOP_K + N_EXPERTS * (MOE_BLOCK - 1) + MOE_BLOCK - 1) // MOE_BLOCK
        n_slots = n_blocks * MOE_BLOCK
        padded, pends, pstarts, n_valid = _block_table(counts[:, 0])
        pos_t = _pos(idx_t, rank_t, pstarts).reshape(-1)
        xs = _dispatch(pends, padded, pos_t, h2p.reshape(T, ROW_CHUNKS, LANE), n_slots)
        ye = _experts(pstarts // MOE_BLOCK, padded // MOE_BLOCK, n_valid,
                      xs.reshape(n_slots * ROW_CHUNKS, LANE), exp_w_gate[l], exp_w_up[l], exp_w_down[l])
    return _final(pos_t, gates, x1, gt2, g_final, ye.reshape(n_slots, ROW_CHUNKS, LANE))
```

```python
import functools

import jax
import jax.numpy as jnp
from jax import lax
from jax.experimental import pallas as pl
from jax.experimental.pallas import tpu as pltpu

F32 = jnp.float32
BF16 = jnp.bfloat16

D_MODEL = 1024
D_S5 = 512
S5_GROUP = 16
S5_GROUPS = 32
S5_STATE = 64
S5_CHUNK = 16
S5_STEP_GROUPS = 8
S5_LEVELS = 7
S5_APOW_COLS = 16
D_GLA = 512
GLA_HEADS = 4
GLA_DK = 64
GLA_DV = 128
GLA_QK = 256
GLA_LOWRANK = 16
GLA_TAU = 16.0
GLA_CHUNK = 64
LANE = 128
N_EXPERTS = 256
TOP_K = 8
N_GROUPS = 8
TOPK_GROUPS = 4
D_EXPERT = 256
ROUTED_SCALE = 2.5
EPS = 1e-6
MOE_BLOCK = 256
ROW_CHUNKS = D_MODEL // 2 // LANE
VMEM_LIMIT = 48 * 1024 * 1024


def _silu(x):
    return x * jax.nn.sigmoid(x)


def _params(*sem):
    return pltpu.CompilerParams(dimension_semantics=sem, vmem_limit_bytes=VMEM_LIMIT)


def _ada_kernel(c_ref, w_ref, b_ref, o_ref):
    s = _silu(c_ref[...]).astype(BF16)
    o_ref[...] = jnp.dot(s, w_ref[...].astype(BF16), preferred_element_type=F32) + b_ref[...]


def _ada(c, w, b):
    bsz, d = c.shape
    n = w.shape[1]
    tn = 1024
    return pl.pallas_call(
        _ada_kernel,
        grid=(n // tn,),
        in_specs=[pl.BlockSpec((bsz, d), lambda j: (0, 0)),
                  pl.BlockSpec((d, tn), lambda j: (0, j)),
                  pl.BlockSpec((1, tn), lambda j: (0, j))],
        out_specs=pl.BlockSpec((bsz, tn), lambda j: (0, j)),
        out_shape=jax.ShapeDtypeStruct((bsz, n), F32),
        compiler_params=_params("arbitrary"),
        name="ada",
    )(c, w, b.reshape(1, n))


def _inproj_kernel(x_ref, g_ref, sh_ref, sc_ref, w_ref,
                   u_ref, q_ref, k_ref, v_ref, r_ref, glr_ref):
    x = x_ref[0]
    ms = jnp.mean(x * x, axis=-1, keepdims=True)
    h = (x * lax.rsqrt(ms + EPS)) * g_ref[...]
    h = h * (1.0 + sc_ref[0]) + sh_ref[0]
    hb = h.astype(BF16)
    u = jnp.dot(hb, w_ref[:, 0:D_S5], preferred_element_type=F32)
    for c in range(D_S5 // LANE):
        u_ref[0, c] = u[:, c * LANE:(c + 1) * LANE]
    col = D_S5
    for ref in (q_ref, k_ref, v_ref, r_ref, glr_ref):
        n = ref.shape[-1]
        ref[0] = jnp.dot(hb, w_ref[:, col:col + n], preferred_element_type=F32).astype(ref.dtype)
        col += n


def _inproj(x, g, sh, sc, w):
    bsz, L, d = x.shape
    tt = 512
    widths = (GLA_QK, GLA_QK, D_GLA, D_GLA, LANE)
    tok = lambda n: pl.BlockSpec((1, tt, n), lambda b, i: (b, i, 0))
    vec = pl.BlockSpec((1, 1, d), lambda b, i: (b, 0, 0))
    return pl.pallas_call(
        _inproj_kernel,
        grid=(bsz, L // tt),
        in_specs=[tok(d), pl.BlockSpec((1, d), lambda b, i: (0, 0)), vec, vec,
                  pl.BlockSpec(w.shape, lambda b, i: (0, 0))],
        out_specs=[pl.BlockSpec((1, D_S5 // LANE, tt, LANE), lambda b, i: (b, 0, i, 0))] + [tok(n) for n in widths],
        out_shape=[jax.ShapeDtypeStruct((bsz, D_S5 // LANE, L, LANE), F32)]
        + [jax.ShapeDtypeStruct((bsz, L, n), BF16) for n in widths],
        compiler_params=_params("arbitrary", "arbitrary"),
        name="inproj",
    )(x, g.reshape(1, d), sh, sc, w)


def _s5_prep(lam_re, lam_im, log_dt, b_re, b_im, c_re, c_im, d_skip):
    G, N, C, TC = S5_GROUPS, S5_STATE, S5_GROUP, S5_CHUNK
    hp = lax.Precision.HIGHEST
    dt = jnp.exp(log_dt)[:, None]
    lr, li = lam_re, lam_im
    mag = jnp.exp(lr * dt)
    ab_re, ab_im = mag * jnp.cos(li * dt), mag * jnp.sin(li * dt)
    den = lr * lr + li * li
    nr = ab_re - 1.0
    coef_re = ((nr * lr + ab_im * li) / den)[..., None]
    coef_im = ((ab_im * lr - nr * li) / den)[..., None]
    bb_re = coef_re * b_re - coef_im * b_im
    bb_im = coef_re * b_im + coef_im * b_re
    p = jnp.arange(TC + 1, dtype=F32)[:, None, None]
    pm = jnp.exp(lr * dt * p)
    pr, pi = pm * jnp.cos(li * dt * p), pm * jnp.sin(li * dt * p)
    ca_re = c_re[None] * pr[:, :, None, :] - c_im[None] * pi[:, :, None, :]
    ca_im = c_re[None] * pi[:, :, None, :] + c_im[None] * pr[:, :, None, :]
    kern = (jnp.einsum('tgon,gni->tgoi', ca_re, bb_re, precision=hp)
            - jnp.einsum('tgon,gni->tgoi', ca_im, bb_im, precision=hp))
    kern = kern.at[0].add(jnp.eye(C, dtype=F32)[None] * d_skip[:, :, None])
    lag = jnp.arange(TC)[None, :] - jnp.arange(TC)[:, None]
    toep = jnp.where((lag >= 0)[:, :, None, None, None], kern[jnp.maximum(lag, 0)], 0.0)
    toep_t = toep.transpose(2, 1, 3, 0, 4).reshape(G, TC * C, TC * C)
    rr, ri = pr[TC - 1 - jnp.arange(TC)], pi[TC - 1 - jnp.arange(TC)]
    binc_re = rr[..., None] * bb_re[None] - ri[..., None] * bb_im[None]
    binc_im = rr[..., None] * bb_im[None] + ri[..., None] * bb_re[None]
    binc_re_t = binc_re.transpose(1, 2, 0, 3).reshape(G, N, TC * C)
    binc_im_t = binc_im.transpose(1, 2, 0, 3).reshape(G, N, TC * C)
    cm_re_t = ca_re[1:].transpose(1, 0, 2, 3).reshape(G, TC * C, N)
    cm_im_t = (-ca_im[1:]).transpose(1, 0, 2, 3).reshape(G, TC * C, N)
    q = (TC * 2.0 ** jnp.arange(S5_LEVELS, dtype=F32))[:, None, None]
    qm = jnp.exp(lr * dt * q)
    qr, qi = qm * jnp.cos(li * dt * q), qm * jnp.sin(li * dt * q)
    apow = jnp.stack([qr, qi], axis=1).reshape(2 * S5_LEVELS, G, N).transpose(1, 2, 0)
    apow = jnp.concatenate([apow, jnp.zeros((G, N, S5_APOW_COLS - 2 * S5_LEVELS), F32)], axis=-1)
    return (toep_t.astype(BF16), binc_re_t.astype(BF16), binc_im_t.astype(BF16),
            cm_re_t.astype(BF16), cm_im_t.astype(BF16), apow)


def _s5_kernel(u_ref, eye_ref, toep_ref, bre_ref, bim_ref, cre_ref, cim_ref, apow_ref,
               wglu_ref, bglu_ref, sng_ref, wo_ref, o_ref, ut_ref, yt_ref):
    j = pl.program_id(1)
    n_chunks = u_ref.shape[2] // S5_CHUNK
    nt = (((1,), (1,)), ((), ()))

    @pl.when(j == 0)
    def _():
        for s in range(S5_CHUNK):
            us = jnp.concatenate([u_ref[0, c, pl.ds(s, n_chunks, stride=S5_CHUNK), :]
                                  for c in range(D_S5 // LANE)], axis=1).astype(BF16)
            ut_ref[:, s * n_chunks:(s + 1) * n_chunks] = lax.dot_general(
                eye_ref[...], us, nt, preferred_element_type=F32).astype(BF16)

    lane = lax.broadcasted_iota(jnp.int32, (S5_STATE, n_chunks), 1)
    groups = range(S5_STEP_GROUPS)
    rows = [pl.ds(pl.multiple_of(j * (S5_STEP_GROUPS * S5_GROUP) + gl * S5_GROUP, S5_GROUP), S5_GROUP)
            for gl in groups]
    zt = [jnp.concatenate([ut_ref[rows[gl], s * n_chunks:(s + 1) * n_chunks] for s in range(S5_CHUNK)], axis=0)
          for gl in groups]
    xr = [jnp.where(lane >= 1, pltpu.roll(jnp.dot(bre_ref[gl], zt[gl], preferred_element_type=F32), 1, axis=1), 0.0)
          for gl in groups]
    xi = [jnp.where(lane >= 1, pltpu.roll(jnp.dot(bim_ref[gl], zt[gl], preferred_element_type=F32), 1, axis=1), 0.0)
          for gl in groups]
    for lv in range(S5_LEVELS):
        d = 1 << lv
        for gl in groups:
            ar = apow_ref[gl, :, 2 * lv:2 * lv + 1]
            ai = apow_ref[gl, :, 2 * lv + 1:2 * lv + 2]
            sr = jnp.where(lane >= d, pltpu.roll(xr[gl], d, axis=1), 0.0)
            si = jnp.where(lane >= d, pltpu.roll(xi[gl], d, axis=1), 0.0)
            xr[gl], xi[gl] = xr[gl] + ar * sr - ai * si, xi[gl] + ar * si + ai * sr
    for gl in groups:
        yt = (jnp.dot(toep_ref[gl], zt[gl], preferred_element_type=F32)
              + jnp.dot(cre_ref[gl], xr[gl].astype(BF16), preferred_element_type=F32)
              + jnp.dot(cim_ref[gl], xi[gl].astype(BF16), preferred_element_type=F32))
        for t in range(S5_CHUNK):
            yt_ref[rows[gl], t * n_chunks:(t + 1) * n_chunks] = yt[t * S5_GROUP:(t + 1) * S5_GROUP, :]

    @pl.when(j == pl.num_programs(1) - 1)
    def _():
        cw = 2 * n_chunks
        for cc in range(yt_ref.shape[1] // cw):
            y = yt_ref[:, cc * cw:(cc + 1) * cw]
            g = y * (0.5 * (1.0 + jnp.tanh(0.7978845608028654 * (y + 0.044715 * (y * y * y)))))
            z = jnp.dot(wglu_ref[...], g.astype(BF16), preferred_element_type=F32) + bglu_ref[...]
            out = g * jax.nn.sigmoid(z)
            out = out * lax.rsqrt(jnp.mean(out * out, axis=0, keepdims=True) + EPS) * sng_ref[...]
            mixc = jnp.dot(out.T.astype(BF16), wo_ref[...], preferred_element_type=F32)
            for sl in range(cw // n_chunks):
                s = cc * (cw // n_chunks) + sl
                for c in range(D_MODEL // LANE):
                    o_ref[0, c, pl.ds(s, n_chunks, stride=S5_CHUNK), :] = mixc[sl * n_chunks:(sl + 1) * n_chunks,
                                                                               c * LANE:(c + 1) * LANE]


def _s5(u, mats, wglu_t, bglu, sng, wo_s5):
    bsz, _, L, _ = u.shape
    assert L // S5_CHUNK == LANE, "one lane tile of chunks per sequence"
    toep, bre, bim, cre, cim, apow = mats
    sg = S5_STEP_GROUPS
    eye = jnp.eye(D_S5, dtype=BF16)
    grp = lambda a, b: pl.BlockSpec((sg, a, b), lambda bi, j: (j, 0, 0))
    full = lambda a: pl.BlockSpec(a.shape, lambda bi, j: (0,) * a.ndim)
    w = S5_CHUNK * S5_GROUP
    return pl.pallas_call(
        _s5_kernel,
        grid=(bsz, S5_GROUPS // sg),
        in_specs=[pl.BlockSpec((1, D_S5 // LANE, L, LANE), lambda bi, j: (bi, 0, 0, 0)), full(eye),
                  grp(w, w), grp(S5_STATE, w), grp(S5_STATE, w), grp(w, S5_STATE), grp(w, S5_STATE),
                  grp(S5_STATE, S5_APOW_COLS), full(wglu_t), full(bglu), full(sng), full(wo_s5)],
        out_specs=pl.BlockSpec((1, D_MODEL // LANE, L, LANE), lambda bi, j: (bi, 0, 0, 0)),
        out_shape=jax.ShapeDtypeStruct((bsz, D_MODEL // LANE, L, LANE), F32),
        scratch_shapes=[pltpu.VMEM((D_S5, L), BF16), pltpu.VMEM((D_S5, L), F32)],
        compiler_params=_params("arbitrary", "arbitrary"),
        name="s5",
    )(u, eye, toep, bre, bim, cre, cim, apow, wglu_t, bglu, sng, wo_s5)


def _gla_kernel(q_ref, k_ref, v_ref, glr_ref, r_ref, wg_ref, bg_ref, ng_ref, o_ref, st_ref):
    lt = q_ref.shape[1]
    C = GLA_CHUNK

    @pl.when(pl.program_id(1) == 0)
    def _():
        st_ref[...] = jnp.zeros_like(st_ref)

    z = jnp.dot(glr_ref[0], wg_ref[...], preferred_element_type=F32) + bg_ref[...]
    log_a = (jnp.minimum(z, 0.0) - jnp.log(1.0 + jnp.exp(-jnp.abs(z)))) * (1.0 / GLA_TAU)
    ri = lax.broadcasted_iota(jnp.int32, (lt, lt), 0)
    ci = lax.broadcasted_iota(jnp.int32, (lt, lt), 1)
    tril = jnp.where(((ri >> 6) == (ci >> 6)) & (ci <= ri), 1.0, 0.0).astype(BF16)
    la_hi = log_a.astype(BF16)
    la_lo = (log_a - la_hi.astype(F32)).astype(BF16)
    bcum = (jnp.dot(tril, la_hi, preferred_element_type=F32)
            + jnp.dot(tril, la_lo, preferred_element_type=F32))
    q = q_ref[0].astype(F32) * (GLA_DK ** -0.5)
    k = k_ref[0].astype(F32)
    qi = q * jnp.exp(bcum)
    ki = k * jnp.exp(-bcum)
    lane_head = lax.broadcasted_iota(jnp.int32, (1, GLA_QK), 1) >> 6
    causal = ((lax.broadcasted_iota(jnp.int32, (GLA_HEADS * C, C), 0) & (C - 1))
              >= lax.broadcasted_iota(jnp.int32, (GLA_HEADS * C, C), 1))
    same_head = ((lax.broadcasted_iota(jnp.int32, (D_GLA, GLA_QK), 0) >> 7)
                 == (lax.broadcasted_iota(jnp.int32, (D_GLA, GLA_QK), 1) >> 6))
    nt = (((1,), (1,)), ((), ()))
    ng = ng_ref[...]
    for c in range(lt // C):
        sl = slice(c * C, (c + 1) * C)
        bc = bcum[sl]
        bl = bc[C - 1:C, :]
        kd = k[sl] * jnp.exp(bl - bc)
        qic = qi[sl]
        qs = jnp.concatenate([jnp.where(lane_head == h, qic, 0.0) for h in range(GLA_HEADS)],
                             axis=0).astype(BF16)
        sc = lax.dot_general(qs, ki[sl].astype(BF16), nt, preferred_element_type=F32)
        p = jnp.where(causal, sc, 0.0).astype(BF16)
        vc = v_ref[0, sl, :]
        o_intra = jnp.concatenate(
            [jnp.dot(p[h * C:(h + 1) * C], vc[:, h * GLA_DV:(h + 1) * GLA_DV], preferred_element_type=F32)
             for h in range(GLA_HEADS)], axis=1)
        st = st_ref[...]
        o_inter = lax.dot_general(qic.astype(BF16), st.astype(BF16), nt, preferred_element_type=F32)
        v_t = vc.astype(F32).T.astype(BF16)
        kv_t = jnp.dot(v_t, kd.astype(BF16), preferred_element_type=F32)
        st_ref[...] = st * jnp.exp(bl) + jnp.where(same_head, kv_t, 0.0)
        o = o_intra + o_inter
        parts = []
        for h in range(GLA_HEADS):
            oh = o[:, h * GLA_DV:(h + 1) * GLA_DV]
            oh = oh * lax.rsqrt(jnp.mean(oh * oh, axis=-1, keepdims=True) + EPS)
            parts.append(oh * ng)
        r = r_ref[0, sl, :].astype(F32)
        o_ref[0, sl, :] = (jnp.concatenate(parts, axis=1) * _silu(r)).astype(o_ref.dtype)


def _gla(q, k, v, glr, r, wg, bg, ng):
    bsz, L, _ = q.shape
    lt = 256
    tok = lambda n: pl.BlockSpec((1, lt, n), lambda b, i: (b, i, 0))
    full = lambda a: pl.BlockSpec(a.shape, lambda b, i: (0,) * a.ndim)
    return pl.pallas_call(
        _gla_kernel,
        grid=(bsz, L // lt),
        in_specs=[tok(GLA_QK), tok(GLA_QK), tok(D_GLA), tok(LANE), tok(D_GLA), full(wg), full(bg), full(ng)],
        out_specs=tok(D_GLA),
        out_shape=jax.ShapeDtypeStruct((bsz, L, D_GLA), BF16),
        scratch_shapes=[pltpu.VMEM((D_GLA, GLA_QK), F32)],
        compiler_params=_params("arbitrary", "arbitrary"),
        name="gla",
    )(q, k, v, glr, r, wg, bg, ng)


def _pack_bf16_pairs(x):
    w = x.shape[1] // 2
    xr = x.astype(BF16).astype(F32)
    lo = lax.bitcast_convert_type(xr[:, :w], jnp.uint32) >> 16
    hi = lax.bitcast_convert_type(xr[:, w:], jnp.uint32) & jnp.uint32(0xFFFF0000)
    return lo | hi


def _unpack_bf16_pairs(p):
    lo = lax.bitcast_convert_type(p << 16, F32)
    hi = lax.bitcast_convert_type(p & jnp.uint32(0xFFFF0000), F32)
    return lo, hi


def _store_rows(ref, packed):
    n = packed.shape[0]
    for c in range(ROW_CHUNKS):
        ref[pl.ds(c, n, stride=ROW_CHUNKS), :] = packed[:, c * LANE:(c + 1) * LANE]


def _load_rows(ref, n):
    return jnp.concatenate([ref[pl.ds(c, n, stride=ROW_CHUNKS), :] for c in range(ROW_CHUNKS)], axis=1)


def _route_tile(lg_t, bias_col, tri, carry_ref):
    n_e, tt = lg_t.shape
    per_group = n_e // N_GROUPS
    scores = jax.nn.sigmoid(lg_t)
    biased = scores + bias_col
    row = lax.broadcasted_iota(jnp.int32, (n_e, tt), 0)
    neg = -jnp.inf
    group_score = []
    for g in range(N_GROUPS):
        b = biased[g * per_group:(g + 1) * per_group]
        r = lax.broadcasted_iota(jnp.int32, (per_group, tt), 0) + g * per_group
        m1 = jnp.max(b, axis=0, keepdims=True)
        i1 = jnp.min(jnp.where(b == m1, r, n_e), axis=0, keepdims=True)
        m2 = jnp.max(jnp.where(r == i1, neg, b), axis=0, keepdims=True)
        group_score.append(m1 + m2)
    parts = []
    for g in range(N_GROUPS):
        ahead = jnp.zeros((1, tt), jnp.int32)
        for g2 in range(N_GROUPS):
            if g2 != g:
                beats = (group_score[g2] >= group_score[g]) if g2 < g else (group_score[g2] > group_score[g])
                ahead = ahead + beats.astype(jnp.int32)
        parts.append(jnp.where(ahead < TOPK_GROUPS, biased[g * per_group:(g + 1) * per_group], neg))
    masked = jnp.concatenate(parts, axis=0)
    work = masked
    idxs = []
    for _ in range(TOP_K):
        m = jnp.max(work, axis=0, keepdims=True)
        ii = jnp.min(jnp.where(work == m, row, n_e), axis=0, keepdims=True)
        idxs.append(ii)
        work = jnp.where(row == ii, neg, work)
    sel = work != masked
    w = jnp.where(sel, scores, 0.0)
    gate_dense = w / jnp.sum(w, axis=0, keepdims=True) * ROUTED_SCALE
    mt = jnp.where(sel, 1.0, 0.0)
    rank_dense = jnp.dot(mt.astype(BF16), tri, preferred_element_type=F32) + carry_ref[...]
    carry_ref[...] += jnp.sum(mt, axis=1, keepdims=True)
    ranks, gts = [], []
    for ii in idxs:
        oh = row == ii
        ranks.append(jnp.sum(jnp.where(oh, rank_dense, 0.0), axis=0, keepdims=True))
        gts.append(jnp.sum(jnp.where(oh, gate_dense, 0.0), axis=0, keepdims=True))
    idx_t = jnp.concatenate(idxs, axis=0)
    rank_t = jnp.concatenate(ranks, axis=0).astype(jnp.int32)
    gate_t = jnp.concatenate(gts + [jnp.zeros((LANE - TOP_K, tt), F32)], axis=0)
    return idx_t, rank_t, gate_t.T[:, :TOP_K]


def _mix_kernel(ms_ref, yg_ref, x_ref, wo_ref, gt1_ref,
                gn_ref, sh2_ref, sc2_ref, gt2_ref, wrt_ref, rb_ref, tri_ref, wsg_ref, wsu_ref, wsd_ref,
                x1_ref, h2_ref, idx_ref, rank_ref, gate_ref, cnt_ref, carry_ref):
    @pl.when((pl.program_id(0) == 0) & (pl.program_id(1) == 0))
    def _():
        carry_ref[...] = jnp.zeros_like(carry_ref)

    mix_s5 = jnp.concatenate([ms_ref[0, c] for c in range(D_MODEL // LANE)], axis=1)
    mix = mix_s5 + jnp.dot(yg_ref[0], wo_ref[...], preferred_element_type=F32)
    x1 = x_ref[0] + gt1_ref[0] * mix
    h2 = x1 * lax.rsqrt(jnp.mean(x1 * x1, axis=-1, keepdims=True) + EPS) * gn_ref[...]
    h2 = h2 * (1.0 + sc2_ref[0]) + sh2_ref[0]
    hb = h2.astype(BF16)
    _store_rows(h2_ref, _pack_bf16_pairs(h2))
    lg_t = lax.dot_general(wrt_ref[...], hb, (((1,), (1,)), ((), ())), preferred_element_type=F32)
    idx_t, rank_t, gates = _route_tile(lg_t, rb_ref[...], tri_ref[...], carry_ref)
    idx_ref[...] = idx_t
    rank_ref[...] = rank_t
    gate_ref[...] = gates
    cnt_ref[...] = carry_ref[...]
    a = _silu(jnp.dot(hb, wsg_ref[...], preferred_element_type=F32)) * jnp.dot(
        hb, wsu_ref[...], preferred_element_type=F32)
    shared = jnp.dot(a.astype(BF16), wsd_ref[...], preferred_element_type=F32)
    x1_ref[0] = x1 + gt2_ref[0] * shared


def _mix(ms5, yg, x, wo, gt1, gn, sh2, sc2, gt2, wrt, rb, wsg, wsu, wsd):
    bsz, L, d = x.shape
    tt = 512
    nt = L // tt
    T = bsz * L
    tri = (jnp.arange(tt)[:, None] < jnp.arange(tt)[None, :]).astype(BF16)
    tok = lambda n: pl.BlockSpec((1, tt, n), lambda b, i: (b, i, 0))
    vec = pl.BlockSpec((1, 1, d), lambda b, i: (b, 0, 0))
    full = lambda a: pl.BlockSpec(a.shape, lambda b, i: (0,) * a.ndim)
    lanes = pl.BlockSpec((TOP_K, tt), lambda b, i: (0, b * nt + i))
    return pl.pallas_call(
        _mix_kernel,
        grid=(bsz, nt),
        in_specs=[pl.BlockSpec((1, d // LANE, tt, LANE), lambda b, i: (b, 0, i, 0)), tok(D_GLA), tok(d), full(wo), vec,
                  full(gn), vec, vec, vec, full(wrt), full(rb), full(tri), full(wsg), full(wsu), full(wsd)],
        out_specs=[tok(d),
                   pl.BlockSpec((tt * ROW_CHUNKS, LANE), lambda b, i: (b * nt + i, 0)),
                   lanes, lanes,
                   pl.BlockSpec((tt, TOP_K), lambda b, i: (b * nt + i, 0)),
                   pl.BlockSpec((N_EXPERTS, 1), lambda b, i: (0, 0))],
        out_shape=[jax.ShapeDtypeStruct((bsz, L, d), F32),
                   jax.ShapeDtypeStruct((T * ROW_CHUNKS, LANE), jnp.uint32),
                   jax.ShapeDtypeStruct((TOP_K, T), jnp.int32),
                   jax.ShapeDtypeStruct((TOP_K, T), jnp.int32),
                   jax.ShapeDtypeStruct((T, TOP_K), F32),
                   jax.ShapeDtypeStruct((N_EXPERTS, 1), F32)],
        scratch_shapes=[pltpu.VMEM((N_EXPERTS, 1), F32)],
        compiler_params=_params("arbitrary", "arbitrary"),
        name="mix",
    )(ms5, yg, x, wo, gt1, gn, sh2, sc2, gt2, wrt, rb, tri, wsg, wsu, wsd)


def _pos_kernel(idx_ref, rank_ref, ps_ref, pos_ref):
    n_e = ps_ref.shape[0]
    tt = idx_ref.shape[1]
    row = lax.broadcasted_iota(jnp.int32, (n_e, tt), 0)
    ps = ps_ref[...]
    starts = [jnp.sum(jnp.where(row == idx_ref[k:k + 1, :], ps, 0.0), axis=0, keepdims=True)
              for k in range(TOP_K)]
    pos = jnp.concatenate(starts, axis=0).astype(jnp.int32) + rank_ref[...]
    for jh in range(tt // LANE):
        pos_ref[jh * TOP_K:(jh + 1) * TOP_K, :] = pos[:, jh * LANE:(jh + 1) * LANE]


def _pos(idx_t, rank_t, pstart):
    T = idx_t.shape[1]
    tt = 2048
    blk = pl.BlockSpec((TOP_K, tt), lambda i: (0, i))
    return pl.pallas_call(
        _pos_kernel,
        grid=(T // tt,),
        in_specs=[blk, blk, pl.BlockSpec((N_EXPERTS, 1), lambda i: (0, 0))],
        out_specs=pl.BlockSpec((tt // LANE * TOP_K, LANE), lambda i: (i, 0)),
        out_shape=jax.ShapeDtypeStruct((T // LANE * TOP_K, LANE), jnp.int32),
        compiler_params=_params("arbitrary"),
        name="pos",
    )(idx_t, rank_t, pstart.astype(F32).reshape(N_EXPERTS, 1))


def _dispatch_kernel(pend_ref, padded_ref, pos_ref, h_ref, xs_hbm, zero_ref, pos_smem, sem_pos, sem_zero, sem_row):
    tt = h_ref.shape[0]
    step = pl.program_id(0)
    load_pos = pltpu.make_async_copy(pos_ref, pos_smem, sem_pos)
    load_pos.start()

    def zero_block(start):
        return pltpu.make_async_copy(zero_ref, xs_hbm.at[pl.ds(pl.multiple_of(start, MOE_BLOCK), MOE_BLOCK)],
                                     sem_zero)

    def for_each_zero_block(fn):
        def per_expert(e, c):
            @pl.when(padded_ref[e] > 0)
            def _():
                fn(zero_block(pend_ref[e] - MOE_BLOCK))
            return c

        def per_tail_block(g, c):
            fn(zero_block(g * MOE_BLOCK))
            return c

        lax.fori_loop(0, N_EXPERTS, per_expert, 0)
        lax.fori_loop(pend_ref[N_EXPERTS - 1] // MOE_BLOCK, xs_hbm.shape[0] // MOE_BLOCK, per_tail_block, 0)

    @pl.when(step == 0)
    def _():
        zero_ref[...] = jnp.zeros_like(zero_ref)
        for_each_zero_block(lambda cp: cp.start())
        for_each_zero_block(lambda cp: cp.wait())

    load_pos.wait()
    for jh in range(tt // LANE):
        def issue(jl, c, jh=jh):
            for k in range(TOP_K):
                slot = pos_smem[(jh * TOP_K + k) * LANE + jl]
                pltpu.make_async_copy(h_ref.at[jh * LANE + jl], xs_hbm.at[slot], sem_row).start(priority=k % 2)
            return c

        lax.fori_loop(0, LANE, issue, 0, unroll=4)
    for k in range(TOP_K):
        pltpu.make_async_copy(h_ref, xs_hbm.at[pl.ds(0, tt)], sem_row).wait()


def _dispatch(pends, padded, pos_t, h_rows, n_slots):
    T = h_rows.shape[0]
    tt = 512
    return pl.pallas_call(
        _dispatch_kernel,
        grid_spec=pltpu.PrefetchScalarGridSpec(
            num_scalar_prefetch=2,
            grid=(T // tt,),
            in_specs=[pl.BlockSpec((tt * TOP_K,), lambda i, pe, pa: (i,)),
                      pl.BlockSpec((tt, ROW_CHUNKS, LANE), lambda i, pe, pa: (i, 0, 0))],
            out_specs=pl.BlockSpec(memory_space=pl.ANY),
            scratch_shapes=[pltpu.VMEM((MOE_BLOCK, ROW_CHUNKS, LANE), jnp.uint32),
                            pltpu.SMEM((tt * TOP_K,), jnp.int32),
                            pltpu.SemaphoreType.DMA, pltpu.SemaphoreType.DMA, pltpu.SemaphoreType.DMA]),
        out_shape=jax.ShapeDtypeStruct((n_slots, ROW_CHUNKS, LANE), jnp.uint32),
        compiler_params=_params("arbitrary"),
        name="dispatch",
    )(pends, padded, pos_t, h_rows)


X_BUFS = 3
Y_BUFS = 2


def _expert_kernel(b0_ref, nb_ref, nv_ref, wg_ref, wu_ref, wd_ref, xs_hbm, ye_hbm,
                   wg_s, wu_s, wd_s, xbuf, ybuf, semx, semy, *, n_blocks):
    e = pl.program_id(0)
    nb = nb_ref[e]
    b0 = b0_ref[e]
    nv = nv_ref[0]
    rows = MOE_BLOCK * ROW_CHUNKS
    ahead = X_BUFS - 1

    def block(ref, g):
        return ref.at[pl.ds(pl.multiple_of(g * rows, rows), rows)]

    def x_copy(g):
        slot = g % X_BUFS
        return pltpu.make_async_copy(block(xs_hbm, g), xbuf.at[slot], semx.at[slot])

    def y_copy(g):
        slot = g % Y_BUFS
        return pltpu.make_async_copy(ybuf.at[slot], block(ye_hbm, g), semy.at[slot])

    @pl.when(e == 0)
    def _():
        for g in range(ahead):
            x_copy(g).start()

    @pl.when(nb > 0)
    def _():
        wg_s[...] = wg_ref[0].astype(BF16)
        wu_s[...] = wu_ref[0].astype(BF16)
        wd_s[...] = wd_ref[0].astype(BF16)

    def body(i, c):
        g = b0 + i
        x_copy(g).wait()

        @pl.when(g + ahead < nv)
        def _():
            x_copy(g + ahead).start()

        lo, hi = _unpack_bf16_pairs(_load_rows(xbuf.at[g % X_BUFS], MOE_BLOCK))
        x = jnp.concatenate([lo, hi], axis=1).astype(BF16)
        a = jnp.dot(x, wg_s[...], preferred_element_type=F32)
        u = jnp.dot(x, wu_s[...], preferred_element_type=F32)
        h = (_silu(a) * u).astype(BF16)
        y = jnp.dot(h, wd_s[...], preferred_element_type=F32)

        @pl.when(g >= Y_BUFS)
        def _():
            y_copy(g - Y_BUFS).wait()

        _store_rows(ybuf.at[g % Y_BUFS], _pack_bf16_pairs(y))
        y_copy(g).start()
        return c

    lax.fori_loop(0, nb, body, 0)

    @pl.when(e == pl.num_programs(0) - 1)
    def _():
        for back in range(Y_BUFS, 0, -1):
            y_copy(nv - back).wait()
        ybuf[0] = jnp.zeros(ybuf.shape[1:], ybuf.dtype)

        def fill(g):
            return pltpu.make_async_copy(ybuf.at[0], block(ye_hbm, g), semy.at[0])

        lax.fori_loop(nv, n_blocks, lambda g, c: (fill(g).start(), c)[1], 0)
        lax.fori_loop(nv, n_blocks, lambda g, c: (fill(g).wait(), c)[1], 0)


def _experts(first_block, num_blocks, n_valid, xs, wg, wu, wd):
    rows = MOE_BLOCK * ROW_CHUNKS
    n_blocks = xs.shape[0] // rows
    n_e, d, _ = wg.shape
    wsel = lambda e, b0, nb, nv: (e, 0, 0)
    return pl.pallas_call(
        functools.partial(_expert_kernel, n_blocks=n_blocks),
        grid_spec=pltpu.PrefetchScalarGridSpec(
            num_scalar_prefetch=3,
            grid=(n_e,),
            in_specs=[pl.BlockSpec((1, d, D_EXPERT), wsel),
                      pl.BlockSpec((1, d, D_EXPERT), wsel),
                      pl.BlockSpec((1, D_EXPERT, d), wsel),
                      pl.BlockSpec(memory_space=pl.ANY)],
            out_specs=pl.BlockSpec(memory_space=pl.ANY),
            scratch_shapes=[pltpu.VMEM((d, D_EXPERT), BF16), pltpu.VMEM((d, D_EXPERT), BF16),
                            pltpu.VMEM((D_EXPERT, d), BF16),
                            pltpu.VMEM((X_BUFS, rows, LANE), jnp.uint32),
                            pltpu.VMEM((Y_BUFS, rows, LANE), jnp.uint32),
                            pltpu.SemaphoreType.DMA((X_BUFS,)), pltpu.SemaphoreType.DMA((Y_BUFS,))]),
        out_shape=jax.ShapeDtypeStruct(xs.shape, jnp.uint32),
        compiler_params=_params("arbitrary"),
        name="experts",
    )(first_block, num_blocks, n_valid, wg, wu, wd, xs)


def _final_kernel(pos0_ref, posn_ref, gate_ref, x_ref, gt_ref, g_ref, ye_hbm, o_ref,
                  buf_ref, pos_smem, sem_pos, sem_row):
    tt = x_ref.shape[1]
    i = pl.program_id(0)

    def gather_tile(pos_ref, tile):
        base = (tile % 2) * (TOP_K * tt)
        sem = sem_row.at[tile % 2]
        load_pos = pltpu.make_async_copy(pos_ref, pos_smem, sem_pos)
        load_pos.start()
        load_pos.wait()
        for jh in range(tt // LANE):
            def issue(jl, c, jh=jh):
                for k in range(TOP_K):
                    slot = pos_smem[(jh * TOP_K + k) * LANE + jl]
                    pltpu.make_async_copy(ye_hbm.at[slot], buf_ref.at[base + k * tt + jh * LANE + jl],
                                          sem).start(priority=k % 2)
                return c

            lax.fori_loop(0, LANE, issue, 0, unroll=4)

    @pl.when(i == 0)
    def _():
        gather_tile(pos0_ref, i)

    @pl.when(i + 1 < pl.num_programs(0))
    def _():
        gather_tile(posn_ref, i + 1)

    base = pl.multiple_of((i % 2) * (TOP_K * tt), TOP_K * tt)
    for k in range(TOP_K):
        pltpu.make_async_copy(ye_hbm.at[pl.ds(0, tt)], buf_ref.at[pl.ds(base + k * tt, tt)],
                              sem_row.at[i % 2]).wait()
    gates = gate_ref[...]
    half = D_MODEL // 2
    acc_lo = jnp.zeros((tt, half), F32)
    acc_hi = jnp.zeros((tt, half), F32)
    rows_2d = buf_ref.reshape(buf_ref.shape[0] * ROW_CHUNKS, LANE)
    for k in range(TOP_K):
        lo, hi = _unpack_bf16_pairs(_load_rows(rows_2d.at[pl.ds((base + k * tt) * ROW_CHUNKS, tt * ROW_CHUNKS)], tt))
        gk = gates[:, k:k + 1]
        acc_lo += gk * lo
        acc_hi += gk * hi
    x = x_ref[0] + gt_ref[0] * jnp.concatenate([acc_lo, acc_hi], axis=1)
    o_ref[0] = x * lax.rsqrt(jnp.mean(x * x, axis=-1, keepdims=True) + EPS) * g_ref[...]


def _final(pos_t, gates, x1, gt2, g, ye_rows):
    bsz, L, d = x1.shape
    tt = 256
    nt = L // tt
    n = bsz * nt
    tok = pl.BlockSpec((1, tt, d), lambda i: (i // nt, i % nt, 0))
    return pl.pallas_call(
        _final_kernel,
        grid=(n,),
        in_specs=[pl.BlockSpec((tt * TOP_K,), lambda i: (0,)),
                  pl.BlockSpec((tt * TOP_K,), lambda i: (jnp.minimum(i + 1, n - 1),)),
                  pl.BlockSpec((tt, TOP_K), lambda i: (i, 0)),
                  tok, pl.BlockSpec((1, 1, d), lambda i: (i // nt, 0, 0)),
                  pl.BlockSpec((1, d), lambda i: (0, 0)),
                  pl.BlockSpec(memory_space=pl.ANY)],
        out_specs=tok,
        out_shape=jax.ShapeDtypeStruct((bsz, L, d), F32),
        scratch_shapes=[pltpu.VMEM((2 * TOP_K * tt, ROW_CHUNKS, LANE), jnp.uint32),
                        pltpu.SMEM((tt * TOP_K,), jnp.int32),
                        pltpu.SemaphoreType.DMA, pltpu.SemaphoreType.DMA((2,))],
        compiler_params=_params("arbitrary"),
        name="final",
    )(pos_t, pos_t, gates, x1, gt2, g.reshape(1, d), ye_rows)


def _block_table(counts):
    counts = counts.astype(jnp.int32)
    padded = (counts + MOE_BLOCK - 1) // MOE_BLOCK * MOE_BLOCK
    pends = jnp.cumsum(padded)
    pstarts = pends - padded
    n_valid = (pends[-1] // MOE_BLOCK).reshape(1)
    return padded, pends, pstarts, n_valid


def kernel(x, c, w_ada, b_ada, g_norm_mix, w_in, s5_lambda_re, s5_lambda_im, s5_log_dt, s5_b_re, s5_b_im, s5_c_re, s5_c_im, s5_d, s5_w_glu, s5_b_glu, s5_norm_g, gla_w_g2, gla_b_g2, gla_norm_g, w_out, g_norm_moe, w_router, router_bias, exp_w_gate, exp_w_up, exp_w_down, sh_w_gate, sh_w_up, sh_w_down, g_final):
    bsz, L, d = x.shape
    T = bsz * L
    assert w_ada.shape[0] == 1, "single-layer block"
    for l in range(1):
        mod = _ada(c, w_ada[l], b_ada[l])
        sh1, sc1, gt1, sh2, sc2, gt2 = [m.reshape(bsz, 1, d) for m in jnp.split(mod, 6, axis=-1)]

        wi = w_in[l]
        o_q, o_k, o_v, o_g, o_r = D_S5, D_S5 + GLA_QK, D_S5 + 2 * GLA_QK, D_S5 + 2 * GLA_QK + D_GLA, \
            D_S5 + 2 * GLA_QK + D_GLA + GLA_LOWRANK
        w_cat = jnp.concatenate([wi[:, :o_g], wi[:, o_r:], wi[:, o_g:o_r],
                                 jnp.zeros((d, LANE - GLA_LOWRANK), wi.dtype)], axis=1).astype(BF16)
        u, q, k, v, r, glr = _inproj(x, g_norm_mix[l], sh1, sc1, w_cat)

        mats = _s5_prep(s5_lambda_re[l], s5_lambda_im[l], s5_log_dt[l], s5_b_re[l], s5_b_im[l],
                        s5_c_re[l], s5_c_im[l], s5_d[l])
        ms5 = _s5(u, mats, s5_w_glu[l].T.astype(BF16), s5_b_glu[l].reshape(D_S5, 1),
                  s5_norm_g[l].reshape(D_S5, 1), w_out[l][:D_S5].astype(BF16))

        wg2 = jnp.concatenate([gla_w_g2[l], jnp.zeros((LANE - GLA_LOWRANK, GLA_QK), F32)], axis=0).astype(BF16)
        yg = _gla(q, k, v, glr, r, wg2, gla_b_g2[l].reshape(1, GLA_QK), gla_norm_g[l].reshape(1, GLA_DV))

        x1, h2p, idx_t, rank_t, gates, counts = _mix(
            ms5, yg, x, w_out[l][D_S5:].astype(BF16), gt1, g_norm_moe[l].reshape(1, d), sh2, sc2, gt2,
            w_router[l].T.astype(BF16), router_bias[l].reshape(N_EXPERTS, 1),
            sh_w_gate[l].astype(BF16), sh_w_up[l].astype(BF16), sh_w_down[l].astype(BF16))

        n_blocks = (T * TOP_K + N_EXPERTS * (MOE_BLOCK - 1) + MOE_BLOCK - 1) // MOE_BLOCK
        n_slots = n_blocks * MOE_BLOCK
        padded, pends, pstarts, n_valid = _block_table(counts[:, 0])
        pos_t = _pos(idx_t, rank_t, pstarts).reshape(-1)
        xs = _dispatch(pends, padded, pos_t, h2p.reshape(T, ROW_CHUNKS, LANE), n_slots)
        ye = _experts(pstarts // MOE_BLOCK, padded // MOE_BLOCK, n_valid,
                      xs.reshape(n_slots * ROW_CHUNKS, LANE), exp_w_gate[l], exp_w_up[l], exp_w_down[l])
    return _final(pos_t, gates, x1, gt2, g_final, ye.reshape(n_slots, ROW_CHUNKS, LANE))
```

```python
import functools

import jax
import jax.numpy as jnp
from jax import lax
from jax.experimental import pallas as pl
from jax.experimental.pallas import tpu as pltpu

F32 = jnp.float32
BF16 = jnp.bfloat16

D_MODEL = 1024
D_S5 = 512
S5_GROUP = 16
S5_GROUPS = 32
S5_STATE = 64
S5_CHUNK = 16
S5_STEP_GROUPS = 8
S5_LEVELS = 7
S5_APOW_COLS = 16
D_GLA = 512
GLA_HEADS = 4
GLA_DK = 64
GLA_DV = 128
GLA_QK = 256
GLA_LOWRANK = 16
GLA_TAU = 16.0
GLA_CHUNK = 64
LANE = 128
N_EXPERTS = 256
TOP_K = 8
N_GROUPS = 8
TOPK_GROUPS = 4
D_EXPERT = 256
ROUTED_SCALE = 2.5
EPS = 1e-6
MOE_BLOCK = 256
ROW_CHUNKS = D_MODEL // 2 // LANE
VMEM_LIMIT = 48 * 1024 * 1024


def _silu(x):
    return x * jax.nn.sigmoid(x)


def _params(*sem):
    return pltpu.CompilerParams(dimension_semantics=sem, vmem_limit_bytes=VMEM_LIMIT)


def _ada_kernel(c_ref, w_ref, b_ref, o_ref):
    s = _silu(c_ref[...]).astype(BF16)
    o_ref[...] = jnp.dot(s, w_ref[...].astype(BF16), preferred_element_type=F32) + b_ref[...]


def _ada(c, w, b):
    bsz, d = c.shape
    n = w.shape[1]
    tn = 1024
    return pl.pallas_call(
        _ada_kernel,
        grid=(n // tn,),
        in_specs=[pl.BlockSpec((bsz, d), lambda j: (0, 0)),
                  pl.BlockSpec((d, tn), lambda j: (0, j)),
                  pl.BlockSpec((1, tn), lambda j: (0, j))],
        out_specs=pl.BlockSpec((bsz, tn), lambda j: (0, j)),
        out_shape=jax.ShapeDtypeStruct((bsz, n), F32),
        compiler_params=_params("arbitrary"),
        name="ada",
    )(c, w, b.reshape(1, n))


def _inproj_kernel(x_ref, g_ref, sh_ref, sc_ref, w_ref,
                   u_ref, q_ref, k_ref, v_ref, r_ref, glr_ref):
    x = x_ref[0]
    ms = jnp.mean(x * x, axis=-1, keepdims=True)
    h = (x * lax.rsqrt(ms + EPS)) * g_ref[...]
    h = h * (1.0 + sc_ref[0]) + sh_ref[0]
    hb = h.astype(BF16)
    u = jnp.dot(hb, w_ref[:, 0:D_S5], preferred_element_type=F32)
    for c in range(D_S5 // LANE):
        u_ref[0, c] = u[:, c * LANE:(c + 1) * LANE]
    col = D_S5
    for ref in (q_ref, k_ref, v_ref, r_ref, glr_ref):
        n = ref.shape[-1]
        ref[0] = jnp.dot(hb, w_ref[:, col:col + n], preferred_element_type=F32).astype(ref.dtype)
        col += n


def _inproj(x, g, sh, sc, w):
    bsz, L, d = x.shape
    tt = 512
    widths = (GLA_QK, GLA_QK, D_GLA, D_GLA, LANE)
    tok = lambda n: pl.BlockSpec((1, tt, n), lambda b, i: (b, i, 0))
    vec = pl.BlockSpec((1, 1, d), lambda b, i: (b, 0, 0))
    return pl.pallas_call(
        _inproj_kernel,
        grid=(bsz, L // tt),
        in_specs=[tok(d), pl.BlockSpec((1, d), lambda b, i: (0, 0)), vec, vec,
                  pl.BlockSpec(w.shape, lambda b, i: (0, 0))],
        out_specs=[pl.BlockSpec((1, D_S5 // LANE, tt, LANE), lambda b, i: (b, 0, i, 0))] + [tok(n) for n in widths],
        out_shape=[jax.ShapeDtypeStruct((bsz, D_S5 // LANE, L, LANE), F32)]
        + [jax.ShapeDtypeStruct((bsz, L, n), BF16) for n in widths],
        compiler_params=_params("arbitrary", "arbitrary"),
        name="inproj",
    )(x, g.reshape(1, d), sh, sc, w)


def _s5_prep(lam_re, lam_im, log_dt, b_re, b_im, c_re, c_im, d_skip):
    G, N, C, TC = S5_GROUPS, S5_STATE, S5_GROUP, S5_CHUNK
    hp = lax.Precision.HIGHEST
    dt = jnp.exp(log_dt)[:, None]
    lr, li = lam_re, lam_im
    mag = jnp.exp(lr * dt)
    ab_re, ab_im = mag * jnp.cos(li * dt), mag * jnp.sin(li * dt)
    den = lr * lr + li * li
    nr = ab_re - 1.0
    coef_re = ((nr * lr + ab_im * li) / den)[..., None]
    coef_im = ((ab_im * lr - nr * li) / den)[..., None]
    bb_re = coef_re * b_re - coef_im * b_im
    bb_im = coef_re * b_im + coef_im * b_re
    p = jnp.arange(TC + 1, dtype=F32)[:, None, None]
    pm = jnp.exp(lr * dt * p)
    pr, pi = pm * jnp.cos(li * dt * p), pm * jnp.sin(li * dt * p)
    ca_re = c_re[None] * pr[:, :, None, :] - c_im[None] * pi[:, :, None, :]
    ca_im = c_re[None] * pi[:, :, None, :] + c_im[None] * pr[:, :, None, :]
    kern = (jnp.einsum('tgon,gni->tgoi', ca_re, bb_re, precision=hp)
            - jnp.einsum('tgon,gni->tgoi', ca_im, bb_im, precision=hp))
    kern = kern.at[0].add(jnp.eye(C, dtype=F32)[None] * d_skip[:, :, None])
    lag = jnp.arange(TC)[None, :] - jnp.arange(TC)[:, None]
    toep = jnp.where((lag >= 0)[:, :, None, None, None], kern[jnp.maximum(lag, 0)], 0.0)
    toep_t = toep.transpose(2, 1, 3, 0, 4).reshape(G, TC * C, TC * C)
    rr, ri = pr[TC - 1 - jnp.arange(TC)], pi[TC - 1 - jnp.arange(TC)]
    binc_re = rr[..., None] * bb_re[None] - ri[..., None] * bb_im[None]
    binc_im = rr[..., None] * bb_im[None] + ri[..., None] * bb_re[None]
    binc_re_t = binc_re.transpose(1, 2, 0, 3).reshape(G, N, TC * C)
    binc_im_t = binc_im.transpose(1, 2, 0, 3).reshape(G, N, TC * C)
    cm_re_t = ca_re[1:].transpose(1, 0, 2, 3).reshape(G, TC * C, N)
    cm_im_t = (-ca_im[1:]).transpose(1, 0, 2, 3).reshape(G, TC * C, N)
    q = (TC * 2.0 ** jnp.arange(S5_LEVELS, dtype=F32))[:, None, None]
    qm = jnp.exp(lr * dt * q)
    qr, qi = qm * jnp.cos(li * dt * q), qm * jnp.sin(li * dt * q)
    apow = jnp.stack([qr, qi], axis=1).reshape(2 * S5_LEVELS, G, N).transpose(1, 2, 0)
    apow = jnp.concatenate([apow, jnp.zeros((G, N, S5_APOW_COLS - 2 * S5_LEVELS), F32)], axis=-1)
    return (toep_t.astype(BF16), binc_re_t.astype(BF16), binc_im_t.astype(BF16),
            cm_re_t.astype(BF16), cm_im_t.astype(BF16), apow)


def _s5_kernel(u_ref, eye_ref, toep_ref, bre_ref, bim_ref, cre_ref, cim_ref, apow_ref,
               wglu_ref, bglu_ref, sng_ref, wo_ref, o_ref, ut_ref, yt_ref):
    j = pl.program_id(1)
    n_chunks = u_ref.shape[2] // S5_CHUNK
    nt = (((1,), (1,)), ((), ()))

    @pl.when(j == 0)
    def _():
        for s in range(S5_CHUNK):
            us = jnp.concatenate([u_ref[0, c, pl.ds(s, n_chunks, stride=S5_CHUNK), :]
                                  for c in range(D_S5 // LANE)], axis=1).astype(BF16)
            ut_ref[:, s * n_chunks:(s + 1) * n_chunks] = lax.dot_general(
                eye_ref[...], us, nt, preferred_element_type=F32).astype(BF16)

    lane = lax.broadcasted_iota(jnp.int32, (S5_STATE, n_chunks), 1)
    groups = range(S5_STEP_GROUPS)
    rows = [pl.ds(pl.multiple_of(j * (S5_STEP_GROUPS * S5_GROUP) + gl * S5_GROUP, S5_GROUP), S5_GROUP)
            for gl in groups]
    zt = [jnp.concatenate([ut_ref[rows[gl], s * n_chunks:(s + 1) * n_chunks] for s in range(S5_CHUNK)], axis=0)
          for gl in groups]
    xr = [jnp.where(lane >= 1, pltpu.roll(jnp.dot(bre_ref[gl], zt[gl], preferred_element_type=F32), 1, axis=1), 0.0)
          for gl in groups]
    xi = [jnp.where(lane >= 1, pltpu.roll(jnp.dot(bim_ref[gl], zt[gl], preferred_element_type=F32), 1, axis=1), 0.0)
          for gl in groups]
    for lv in range(S5_LEVELS):
        d = 1 << lv
        for gl in groups:
            ar = apow_ref[gl, :, 2 * lv:2 * lv + 1]
            ai = apow_ref[gl, :, 2 * lv + 1:2 * lv + 2]
            sr = jnp.where(lane >= d, pltpu.roll(xr[gl], d, axis=1), 0.0)
            si = jnp.where(lane >= d, pltpu.roll(xi[gl], d, axis=1), 0.0)
            xr[gl], xi[gl] = xr[gl] + ar * sr - ai * si, xi[gl] + ar * si + ai * sr
    for gl in groups:
        yt = (jnp.dot(toep_ref[gl], zt[gl], preferred_element_type=F32)
              + jnp.dot(cre_ref[gl], xr[gl].astype(BF16), preferred_element_type=F32)
              + jnp.dot(cim_ref[gl], xi[gl].astype(BF16), preferred_element_type=F32))
        for t in range(S5_CHUNK):
            yt_ref[rows[gl], t * n_chunks:(t + 1) * n_chunks] = yt[t * S5_GROUP:(t + 1) * S5_GROUP, :]

    @pl.when(j == pl.num_programs(1) - 1)
    def _():
        cw = 2 * n_chunks
        for cc in range(yt_ref.shape[1] // cw):
            y = yt_ref[:, cc * cw:(cc + 1) * cw]
            g = y * (0.5 * (1.0 + jnp.tanh(0.7978845608028654 * (y + 0.044715 * (y * y * y)))))
            z = jnp.dot(wglu_ref[...], g.astype(BF16), preferred_element_type=F32) + bglu_ref[...]
            out = g * jax.nn.sigmoid(z)
            out = out * lax.rsqrt(jnp.mean(out * out, axis=0, keepdims=True) + EPS) * sng_ref[...]
            mixc = jnp.dot(out.T.astype(BF16), wo_ref[...], preferred_element_type=F32)
            for sl in range(cw // n_chunks):
                s = cc * (cw // n_chunks) + sl
                for c in range(D_MODEL // LANE):
                    o_ref[0, c, pl.ds(s, n_chunks, stride=S5_CHUNK), :] = mixc[sl * n_chunks:(sl + 1) * n_chunks,
                                                                               c * LANE:(c + 1) * LANE]


def _s5(u, mats, wglu_t, bglu, sng, wo_s5):
    bsz, _, L, _ = u.shape
    assert L // S5_CHUNK == LANE, "one lane tile of chunks per sequence"
    toep, bre, bim, cre, cim, apow = mats
    sg = S5_STEP_GROUPS
    eye = jnp.eye(D_S5, dtype=BF16)
    grp = lambda a, b: pl.BlockSpec((sg, a, b), lambda bi, j: (j, 0, 0))
    full = lambda a: pl.BlockSpec(a.shape, lambda bi, j: (0,) * a.ndim)
    w = S5_CHUNK * S5_GROUP
    return pl.pallas_call(
        _s5_kernel,
        grid=(bsz, S5_GROUPS // sg),
        in_specs=[pl.BlockSpec((1, D_S5 // LANE, L, LANE), lambda bi, j: (bi, 0, 0, 0)), full(eye),
                  grp(w, w), grp(S5_STATE, w), grp(S5_STATE, w), grp(w, S5_STATE), grp(w, S5_STATE),
                  grp(S5_STATE, S5_APOW_COLS), full(wglu_t), full(bglu), full(sng), full(wo_s5)],
        out_specs=pl.BlockSpec((1, D_MODEL // LANE, L, LANE), lambda bi, j: (bi, 0, 0, 0)),
        out_shape=jax.ShapeDtypeStruct((bsz, D_MODEL // LANE, L, LANE), F32),
        scratch_shapes=[pltpu.VMEM((D_S5, L), BF16), pltpu.VMEM((D_S5, L), F32)],
        compiler_params=_params("arbitrary", "arbitrary"),
        name="s5",
    )(u, eye, toep, bre, bim, cre, cim, apow, wglu_t, bglu, sng, wo_s5)


def _gla_kernel(q_ref, k_ref, v_ref, glr_ref, r_ref, wg_ref, bg_ref, ng_ref, o_ref, st_ref):
    lt = q_ref.shape[1]
    C = GLA_CHUNK

    @pl.when(pl.program_id(1) == 0)
    def _():
        st_ref[...] = jnp.zeros_like(st_ref)

    z = jnp.dot(glr_ref[0], wg_ref[...], preferred_element_type=F32) + bg_ref[...]
    log_a = (jnp.minimum(z, 0.0) - jnp.log(1.0 + jnp.exp(-jnp.abs(z)))) * (1.0 / GLA_TAU)
    ri = lax.broadcasted_iota(jnp.int32, (lt, lt), 0)
    ci = lax.broadcasted_iota(jnp.int32, (lt, lt), 1)
    tril = jnp.where(((ri >> 6) == (ci >> 6)) & (ci <= ri), 1.0, 0.0).astype(BF16)
    la_hi = log_a.astype(BF16)
    la_lo = (log_a - la_hi.astype(F32)).astype(BF16)
    bcum = (jnp.dot(tril, la_hi, preferred_element_type=F32)
            + jnp.dot(tril, la_lo, preferred_element_type=F32))
    q = q_ref[0].astype(F32) * (GLA_DK ** -0.5)
    k = k_ref[0].astype(F32)
    qi = q * jnp.exp(bcum)
    ki = k * jnp.exp(-bcum)
    lane_head = lax.broadcasted_iota(jnp.int32, (1, GLA_QK), 1) >> 6
    causal = ((lax.broadcasted_iota(jnp.int32, (GLA_HEADS * C, C), 0) & (C - 1))
              >= lax.broadcasted_iota(jnp.int32, (GLA_HEADS * C, C), 1))
    same_head = ((lax.broadcasted_iota(jnp.int32, (D_GLA, GLA_QK), 0) >> 7)
                 == (lax.broadcasted_iota(jnp.int32, (D_GLA, GLA_QK), 1) >> 6))
    nt = (((1,), (1,)), ((), ()))
    ng = ng_ref[...]
    for c in range(lt // C):
        sl = slice(c * C, (c + 1) * C)
        bc = bcum[sl]
        bl = bc[C - 1:C, :]
        kd = k[sl] * jnp.exp(bl - bc)
        qic = qi[sl]
        qs = jnp.concatenate([jnp.where(lane_head == h, qic, 0.0) for h in range(GLA_HEADS)],
                             axis=0).astype(BF16)
        sc = lax.dot_general(qs, ki[sl].astype(BF16), nt, preferred_element_type=F32)
        p = jnp.where(causal, sc, 0.0).astype(BF16)
        vc = v_ref[0, sl, :]
        o_intra = jnp.concatenate(
            [jnp.dot(p[h * C:(h + 1) * C], vc[:, h * GLA_DV:(h + 1) * GLA_DV], preferred_element_type=F32)
             for h in range(GLA_HEADS)], axis=1)
        st = st_ref[...]
        o_inter = lax.dot_general(qic.astype(BF16), st.astype(BF16), nt, preferred_element_type=F32)
        v_t = vc.astype(F32).T.astype(BF16)
        kv_t = jnp.dot(v_t, kd.astype(BF16), preferred_element_type=F32)
        st_ref[...] = st * jnp.exp(bl) + jnp.where(same_head, kv_t, 0.0)
        o = o_intra + o_inter
        parts = []
        for h in range(GLA_HEADS):
            oh = o[:, h * GLA_DV:(h + 1) * GLA_DV]
            oh = oh * lax.rsqrt(jnp.mean(oh * oh, axis=-1, keepdims=True) + EPS)
            parts.append(oh * ng)
        r = r_ref[0, sl, :].astype(F32)
        o_ref[0, sl, :] = (jnp.concatenate(parts, axis=1) * _silu(r)).astype(o_ref.dtype)


def _gla(q, k, v, glr, r, wg, bg, ng):
    bsz, L, _ = q.shape
    lt = 256
    tok = lambda n: pl.BlockSpec((1, lt, n), lambda b, i: (b, i, 0))
    full = lambda a: pl.BlockSpec(a.shape, lambda b, i: (0,) * a.ndim)
    return pl.pallas_call(
        _gla_kernel,
        grid=(bsz, L // lt),
        in_specs=[tok(GLA_QK), tok(GLA_QK), tok(D_GLA), tok(LANE), tok(D_GLA), full(wg), full(bg), full(ng)],
        out_specs=tok(D_GLA),
        out_shape=jax.ShapeDtypeStruct((bsz, L, D_GLA), BF16),
        scratch_shapes=[pltpu.VMEM((D_GLA, GLA_QK), F32)],
        compiler_params=_params("arbitrary", "arbitrary"),
        name="gla",
    )(q, k, v, glr, r, wg, bg, ng)


def _pack_bf16_pairs(x):
    w = x.shape[1] // 2
    xr = x.astype(BF16).astype(F32)
    lo = lax.bitcast_convert_type(xr[:, :w], jnp.uint32) >> 16
    hi = lax.bitcast_convert_type(xr[:, w:], jnp.uint32) & jnp.uint32(0xFFFF0000)
    return lo | hi


def _unpack_bf16_pairs(p):
    lo = lax.bitcast_convert_type(p << 16, F32)
    hi = lax.bitcast_convert_type(p & jnp.uint32(0xFFFF0000), F32)
    return lo, hi


def _store_rows(ref, packed):
    n = packed.shape[0]
    for c in range(ROW_CHUNKS):
        ref[pl.ds(c, n, stride=ROW_CHUNKS), :] = packed[:, c * LANE:(c + 1) * LANE]


def _load_rows(ref, n):
    return jnp.concatenate([ref[pl.ds(c, n, stride=ROW_CHUNKS), :] for c in range(ROW_CHUNKS)], axis=1)


def _route_tile(lg_t, bias_col, tri, carry_ref):
    n_e, tt = lg_t.shape
    per_group = n_e // N_GROUPS
    scores = jax.nn.sigmoid(lg_t)
    biased = scores + bias_col
    row = lax.broadcasted_iota(jnp.int32, (n_e, tt), 0)
    neg = -jnp.inf
    group_score = []
    for g in range(N_GROUPS):
        b = biased[g * per_group:(g + 1) * per_group]
        r = lax.broadcasted_iota(jnp.int32, (per_group, tt), 0) + g * per_group
        m1 = jnp.max(b, axis=0, keepdims=True)
        i1 = jnp.min(jnp.where(b == m1, r, n_e), axis=0, keepdims=True)
        m2 = jnp.max(jnp.where(r == i1, neg, b), axis=0, keepdims=True)
        group_score.append(m1 + m2)
    parts = []
    for g in range(N_GROUPS):
        ahead = jnp.zeros((1, tt), jnp.int32)
        for g2 in range(N_GROUPS):
            if g2 != g:
                beats = (group_score[g2] >= group_score[g]) if g2 < g else (group_score[g2] > group_score[g])
                ahead = ahead + beats.astype(jnp.int32)
        parts.append(jnp.where(ahead < TOPK_GROUPS, biased[g * per_group:(g + 1) * per_group], neg))
    masked = jnp.concatenate(parts, axis=0)
    work = masked
    idxs = []
    for _ in range(TOP_K):
        m = jnp.max(work, axis=0, keepdims=True)
        ii = jnp.min(jnp.where(work == m, row, n_e), axis=0, keepdims=True)
        idxs.append(ii)
        work = jnp.where(row == ii, neg, work)
    sel = work != masked
    w = jnp.where(sel, scores, 0.0)
    gate_dense = w / jnp.sum(w, axis=0, keepdims=True) * ROUTED_SCALE
    mt = jnp.where(sel, 1.0, 0.0)
    rank_dense = jnp.dot(mt.astype(BF16), tri, preferred_element_type=F32) + carry_ref[...]
    carry_ref[...] += jnp.sum(mt, axis=1, keepdims=True)
    ranks, gts = [], []
    for ii in idxs:
        oh = row == ii
        ranks.append(jnp.sum(jnp.where(oh, rank_dense, 0.0), axis=0, keepdims=True))
        gts.append(jnp.sum(jnp.where(oh, gate_dense, 0.0), axis=0, keepdims=True))
    idx_t = jnp.concatenate(idxs, axis=0)
    rank_t = jnp.concatenate(ranks, axis=0).astype(jnp.int32)
    gate_t = jnp.concatenate(gts + [jnp.zeros((LANE - TOP_K, tt), F32)], axis=0)
    return idx_t, rank_t, gate_t.T[:, :TOP_K]


def _mix_kernel(ms_ref, yg_ref, x_ref, wo_ref, gt1_ref,
                gn_ref, sh2_ref, sc2_ref, gt2_ref, wrt_ref, rb_ref, tri_ref, wsg_ref, wsu_ref, wsd_ref,
                x1_ref, h2_ref, idx_ref, rank_ref, gate_ref, cnt_ref, carry_ref):
    @pl.when((pl.program_id(0) == 0) & (pl.program_id(1) == 0))
    def _():
        carry_ref[...] = jnp.zeros_like(carry_ref)

    mix_s5 = jnp.concatenate([ms_ref[0, c] for c in range(D_MODEL // LANE)], axis=1)
    mix = mix_s5 + jnp.dot(yg_ref[0], wo_ref[...], preferred_element_type=F32)
    x1 = x_ref[0] + gt1_ref[0] * mix
    h2 = x1 * lax.rsqrt(jnp.mean(x1 * x1, axis=-1, keepdims=True) + EPS) * gn_ref[...]
    h2 = h2 * (1.0 + sc2_ref[0]) + sh2_ref[0]
    hb = h2.astype(BF16)
    _store_rows(h2_ref, _pack_bf16_pairs(h2))
    lg_t = lax.dot_general(wrt_ref[...], hb, (((1,), (1,)), ((), ())), preferred_element_type=F32)
    idx_t, rank_t, gates = _route_tile(lg_t, rb_ref[...], tri_ref[...], carry_ref)
    idx_ref[...] = idx_t
    rank_ref[...] = rank_t
    gate_ref[...] = gates
    cnt_ref[...] = carry_ref[...]
    a = _silu(jnp.dot(hb, wsg_ref[...], preferred_element_type=F32)) * jnp.dot(
        hb, wsu_ref[...], preferred_element_type=F32)
    shared = jnp.dot(a.astype(BF16), wsd_ref[...], preferred_element_type=F32)
    x1_ref[0] = x1 + gt2_ref[0] * shared


def _mix(ms5, yg, x, wo, gt1, gn, sh2, sc2, gt2, wrt, rb, wsg, wsu, wsd):
    bsz, L, d = x.shape
    tt = 512
    nt = L // tt
    T = bsz * L
    tri = (jnp.arange(tt)[:, None] < jnp.arange(tt)[None, :]).astype(BF16)
    tok = lambda n: pl.BlockSpec((1, tt, n), lambda b, i: (b, i, 0))
    vec = pl.BlockSpec((1, 1, d), lambda b, i: (b, 0, 0))
    full = lambda a: pl.BlockSpec(a.shape, lambda b, i: (0,) * a.ndim)
    lanes = pl.BlockSpec((TOP_K, tt), lambda b, i: (0, b * nt + i))
    return pl.pallas_call(
        _mix_kernel,
        grid=(bsz, nt),
        in_specs=[pl.BlockSpec((1, d // LANE, tt, LANE), lambda b, i: (b, 0, i, 0)), tok(D_GLA), tok(d), full(wo), vec,
                  full(gn), vec, vec, vec, full(wrt), full(rb), full(tri), full(wsg), full(wsu), full(wsd)],
        out_specs=[tok(d),
                   pl.BlockSpec((tt * ROW_CHUNKS, LANE), lambda b, i: (b * nt + i, 0)),
                   lanes, lanes,
                   pl.BlockSpec((tt, TOP_K), lambda b, i: (b * nt + i, 0)),
                   pl.BlockSpec((N_EXPERTS, 1), lambda b, i: (0, 0))],
        out_shape=[jax.ShapeDtypeStruct((bsz, L, d), F32),
                   jax.ShapeDtypeStruct((T * ROW_CHUNKS, LANE), jnp.uint32),
                   jax.ShapeDtypeStruct((TOP_K, T), jnp.int32),
                   jax.ShapeDtypeStruct((TOP_K, T), jnp.int32),
                   jax.ShapeDtypeStruct((T, TOP_K), F32),
                   jax.ShapeDtypeStruct((N_EXPERTS, 1), F32)],
        scratch_shapes=[pltpu.VMEM((N_EXPERTS, 1), F32)],
        compiler_params=_params("arbitrary", "arbitrary"),
        name="mix",
    )(ms5, yg, x, wo, gt1, gn, sh2, sc2, gt2, wrt, rb, tri, wsg, wsu, wsd)


def _pos_kernel(idx_ref, rank_ref, ps_ref, pos_ref):
    n_e = ps_ref.shape[0]
    tt = idx_ref.shape[1]
    row = lax.broadcasted_iota(jnp.int32, (n_e, tt), 0)
    ps = ps_ref[...]
    starts = [jnp.sum(jnp.where(row == idx_ref[k:k + 1, :], ps, 0.0), axis=0, keepdims=True)
              for k in range(TOP_K)]
    pos = jnp.concatenate(starts, axis=0).astype(jnp.int32) + rank_ref[...]
    for jh in range(tt // LANE):
        pos_ref[jh * TOP_K:(jh + 1) * TOP_K, :] = pos[:, jh * LANE:(jh + 1) * LANE]


def _pos(idx_t, rank_t, pstart):
    T = idx_t.shape[1]
    tt = 2048
    blk = pl.BlockSpec((TOP_K, tt), lambda i: (0, i))
    return pl.pallas_call(
        _pos_kernel,
        grid=(T // tt,),
        in_specs=[blk, blk, pl.BlockSpec((N_EXPERTS, 1), lambda i: (0, 0))],
        out_specs=pl.BlockSpec((tt // LANE * TOP_K, LANE), lambda i: (i, 0)),
        out_shape=jax.ShapeDtypeStruct((T // LANE * TOP_K, LANE), jnp.int32),
        compiler_params=_params("arbitrary"),
        name="pos",
    )(idx_t, rank_t, pstart.astype(F32).reshape(N_EXPERTS, 1))


def _dispatch_kernel(pend_ref, padded_ref, pos_ref, h_ref, xs_hbm, zero_ref, pos_smem, sem_pos, sem_zero, sem_row):
    tt = h_ref.shape[0]
    step = pl.program_id(0)
    load_pos = pltpu.make_async_copy(pos_ref, pos_smem, sem_pos)
    load_pos.start()

    def zero_block(start):
        return pltpu.make_async_copy(zero_ref, xs_hbm.at[pl.ds(pl.multiple_of(start, MOE_BLOCK), MOE_BLOCK)],
                                     sem_zero)

    def for_each_zero_block(fn):
        def per_expert(e, c):
            @pl.when(padded_ref[e] > 0)
            def _():
                fn(zero_block(pend_ref[e] - MOE_BLOCK))
            return c

        def per_tail_block(g, c):
            fn(zero_block(g * MOE_BLOCK))
            return c

        lax.fori_loop(0, N_EXPERTS, per_expert, 0)
        lax.fori_loop(pend_ref[N_EXPERTS - 1] // MOE_BLOCK, xs_hbm.shape[0] // MOE_BLOCK, per_tail_block, 0)

    @pl.when(step == 0)
    def _():
        zero_ref[...] = jnp.zeros_like(zero_ref)
        for_each_zero_block(lambda cp: cp.start())
        for_each_zero_block(lambda cp: cp.wait())

    load_pos.wait()
    for jh in range(tt // LANE):
        def issue(jl, c, jh=jh):
            for k in range(TOP_K):
                slot = pos_smem[(jh * TOP_K + k) * LANE + jl]
                pltpu.make_async_copy(h_ref.at[jh * LANE + jl], xs_hbm.at[slot], sem_row).start(priority=k % 2)
            return c

        lax.fori_loop(0, LANE, issue, 0, unroll=4)
    for k in range(TOP_K):
        pltpu.make_async_copy(h_ref, xs_hbm.at[pl.ds(0, tt)], sem_row).wait()


def _dispatch(pends, padded, pos_t, h_rows, n_slots):
    T = h_rows.shape[0]
    tt = 512
    return pl.pallas_call(
        _dispatch_kernel,
        grid_spec=pltpu.PrefetchScalarGridSpec(
            num_scalar_prefetch=2,
            grid=(T // tt,),
            in_specs=[pl.BlockSpec((tt * TOP_K,), lambda i, pe, pa: (i,)),
                      pl.BlockSpec((tt, ROW_CHUNKS, LANE), lambda i, pe, pa: (i, 0, 0))],
            out_specs=pl.BlockSpec(memory_space=pl.ANY),
            scratch_shapes=[pltpu.VMEM((MOE_BLOCK, ROW_CHUNKS, LANE), jnp.uint32),
                            pltpu.SMEM((tt * TOP_K,), jnp.int32),
                            pltpu.SemaphoreType.DMA, pltpu.SemaphoreType.DMA, pltpu.SemaphoreType.DMA]),
        out_shape=jax.ShapeDtypeStruct((n_slots, ROW_CHUNKS, LANE), jnp.uint32),
        compiler_params=_params("arbitrary"),
        name="dispatch",
    )(pends, padded, pos_t, h_rows)


X_AHEAD = 3
X_BUFS = X_AHEAD + 2
Y_BUFS = 4


def _expert_kernel(b0_ref, nb_ref, nv_ref, wg_ref, wu_ref, wd_ref, xs_hbm, ye_hbm,
                   wg_s, wu_s, wd_s, xbuf, ybuf, semx, semy, *, n_blocks):
    e = pl.program_id(0)
    nb = nb_ref[e]
    b0 = b0_ref[e]
    nv = nv_ref[0]
    rows = MOE_BLOCK * ROW_CHUNKS

    def block(ref, g):
        return ref.at[pl.ds(pl.multiple_of(g * rows, rows), rows)]

    def x_copy(g):
        slot = g % X_BUFS
        return pltpu.make_async_copy(block(xs_hbm, g), xbuf.at[slot], semx.at[slot])

    def y_copy(g):
        slot = g % Y_BUFS
        return pltpu.make_async_copy(ybuf.at[slot], block(ye_hbm, g), semy.at[slot])

    @pl.when(e == 0)
    def _():
        for g in range(X_AHEAD):
            x_copy(g).start()

    @pl.when(nb > 0)
    def _():
        wg_s[...] = wg_ref[0].astype(BF16)
        wu_s[...] = wu_ref[0].astype(BF16)
        wd_s[...] = wd_ref[0].astype(BF16)

    def fetch(g):
        x_copy(g).wait()

        @pl.when(g + X_AHEAD < nv)
        def _():
            x_copy(g + X_AHEAD).start()

    def swiglu(g):
        lo, hi = _unpack_bf16_pairs(_load_rows(xbuf.at[g % X_BUFS], MOE_BLOCK))
        x = jnp.concatenate([lo, hi], axis=1).astype(BF16)
        a = jnp.dot(x, wg_s[...], preferred_element_type=F32)
        u = jnp.dot(x, wu_s[...], preferred_element_type=F32)
        h = (_silu(a) * u).astype(BF16)
        return _pack_bf16_pairs(jnp.dot(h, wd_s[...], preferred_element_type=F32))

    def emit(g, y):
        @pl.when(g >= Y_BUFS)
        def _():
            y_copy(g - Y_BUFS).wait()

        _store_rows(ybuf.at[g % Y_BUFS], y)
        y_copy(g).start()

    def pair(i, c):
        g = b0 + 2 * i
        fetch(g)
        fetch(g + 1)
        y0 = swiglu(g)
        y1 = swiglu(g + 1)
        emit(g, y0)
        emit(g + 1, y1)
        return c

    lax.fori_loop(0, nb >> 1, pair, 0)

    @pl.when((nb & 1) == 1)
    def _():
        g = b0 + nb - 1
        fetch(g)
        emit(g, swiglu(g))

    @pl.when(e == pl.num_programs(0) - 1)
    def _():
        for back in range(Y_BUFS, 0, -1):
            y_copy(nv - back).wait()
        ybuf[0] = jnp.zeros(ybuf.shape[1:], ybuf.dtype)

        def fill(g):
            return pltpu.make_async_copy(ybuf.at[0], block(ye_hbm, g), semy.at[0])

        lax.fori_loop(nv, n_blocks, lambda g, c: (fill(g).start(), c)[1], 0)
        lax.fori_loop(nv, n_blocks, lambda g, c: (fill(g).wait(), c)[1], 0)


def _experts(first_block, num_blocks, n_valid, xs, wg, wu, wd):
    rows = MOE_BLOCK * ROW_CHUNKS
    n_blocks = xs.shape[0] // rows
    n_e, d, _ = wg.shape
    wsel = lambda e, b0, nb, nv: (e, 0, 0)
    return pl.pallas_call(
        functools.partial(_expert_kernel, n_blocks=n_blocks),
        grid_spec=pltpu.PrefetchScalarGridSpec(
            num_scalar_prefetch=3,
            grid=(n_e,),
            in_specs=[pl.BlockSpec((1, d, D_EXPERT), wsel),
                      pl.BlockSpec((1, d, D_EXPERT), wsel),
                      pl.BlockSpec((1, D_EXPERT, d), wsel),
                      pl.BlockSpec(memory_space=pl.ANY)],
            out_specs=pl.BlockSpec(memory_space=pl.ANY),
            scratch_shapes=[pltpu.VMEM((d, D_EXPERT), BF16), pltpu.VMEM((d, D_EXPERT), BF16),
                            pltpu.VMEM((D_EXPERT, d), BF16),
                            pltpu.VMEM((X_BUFS, rows, LANE), jnp.uint32),
                            pltpu.VMEM((Y_BUFS, rows, LANE), jnp.uint32),
                            pltpu.SemaphoreType.DMA((X_BUFS,)), pltpu.SemaphoreType.DMA((Y_BUFS,))]),
        out_shape=jax.ShapeDtypeStruct(xs.shape, jnp.uint32),
        compiler_params=_params("arbitrary"),
        name="experts",
    )(first_block, num_blocks, n_valid, wg, wu, wd, xs)


def _final_kernel(pos0_ref, posn_ref, gate_ref, x_ref, gt_ref, g_ref, ye_hbm, o_ref,
                  buf_ref, pos_smem, sem_pos, sem_row):
    tt = x_ref.shape[1]
    i = pl.program_id(0)

    def gather_tile(pos_ref, tile):
        base = (tile % 2) * (TOP_K * tt)
        sem = sem_row.at[tile % 2]
        load_pos = pltpu.make_async_copy(pos_ref, pos_smem, sem_pos)
        load_pos.start()
        load_pos.wait()
        for jh in range(tt // LANE):
            def issue(jl, c, jh=jh):
                for k in range(TOP_K):
                    slot = pos_smem[(jh * TOP_K + k) * LANE + jl]
                    pltpu.make_async_copy(ye_hbm.at[slot], buf_ref.at[base + k * tt + jh * LANE + jl],
                                          sem).start(priority=k % 2)
                return c

            lax.fori_loop(0, LANE, issue, 0, unroll=4)

    @pl.when(i == 0)
    def _():
        gather_tile(pos0_ref, i)

    @pl.when(i + 1 < pl.num_programs(0))
    def _():
        gather_tile(posn_ref, i + 1)

    base = pl.multiple_of((i % 2) * (TOP_K * tt), TOP_K * tt)
    for k in range(TOP_K):
        pltpu.make_async_copy(ye_hbm.at[pl.ds(0, tt)], buf_ref.at[pl.ds(base + k * tt, tt)],
                              sem_row.at[i % 2]).wait()
    gates = gate_ref[...]
    half = D_MODEL // 2
    acc_lo = jnp.zeros((tt, half), F32)
    acc_hi = jnp.zeros((tt, half), F32)
    rows_2d = buf_ref.reshape(buf_ref.shape[0] * ROW_CHUNKS, LANE)
    for k in range(TOP_K):
        lo, hi = _unpack_bf16_pairs(_load_rows(rows_2d.at[pl.ds((base + k * tt) * ROW_CHUNKS, tt * ROW_CHUNKS)], tt))
        gk = gates[:, k:k + 1]
        acc_lo += gk * lo
        acc_hi += gk * hi
    x = x_ref[0] + gt_ref[0] * jnp.concatenate([acc_lo, acc_hi], axis=1)
    o_ref[0] = x * lax.rsqrt(jnp.mean(x * x, axis=-1, keepdims=True) + EPS) * g_ref[...]


def _final(pos_t, gates, x1, gt2, g, ye_rows):
    bsz, L, d = x1.shape
    tt = 256
    nt = L // tt
    n = bsz * nt
    tok = pl.BlockSpec((1, tt, d), lambda i: (i // nt, i % nt, 0))
    return pl.pallas_call(
        _final_kernel,
        grid=(n,),
        in_specs=[pl.BlockSpec((tt * TOP_K,), lambda i: (0,)),
                  pl.BlockSpec((tt * TOP_K,), lambda i: (jnp.minimum(i + 1, n - 1),)),
                  pl.BlockSpec((tt, TOP_K), lambda i: (i, 0)),
                  tok, pl.BlockSpec((1, 1, d), lambda i: (i // nt, 0, 0)),
                  pl.BlockSpec((1, d), lambda i: (0, 0)),
                  pl.BlockSpec(memory_space=pl.ANY)],
        out_specs=tok,
        out_shape=jax.ShapeDtypeStruct((bsz, L, d), F32),
        scratch_shapes=[pltpu.VMEM((2 * TOP_K * tt, ROW_CHUNKS, LANE), jnp.uint32),
                        pltpu.SMEM((tt * TOP_K,), jnp.int32),
                        pltpu.SemaphoreType.DMA, pltpu.SemaphoreType.DMA((2,))],
        compiler_params=_params("arbitrary"),
        name="final",
    )(pos_t, pos_t, gates, x1, gt2, g.reshape(1, d), ye_rows)


def _block_table(counts):
    counts = counts.astype(jnp.int32)
    padded = (counts + MOE_BLOCK - 1) // MOE_BLOCK * MOE_BLOCK
    pends = jnp.cumsum(padded)
    pstarts = pends - padded
    n_valid = (pends[-1] // MOE_BLOCK).reshape(1)
    return padded, pends, pstarts, n_valid


def kernel(x, c, w_ada, b_ada, g_norm_mix, w_in, s5_lambda_re, s5_lambda_im, s5_log_dt, s5_b_re, s5_b_im, s5_c_re, s5_c_im, s5_d, s5_w_glu, s5_b_glu, s5_norm_g, gla_w_g2, gla_b_g2, gla_norm_g, w_out, g_norm_moe, w_router, router_bias, exp_w_gate, exp_w_up, exp_w_down, sh_w_gate, sh_w_up, sh_w_down, g_final):
    bsz, L, d = x.shape
    T = bsz * L
    assert w_ada.shape[0] == 1, "single-layer block"
    for l in range(1):
        mod = _ada(c, w_ada[l], b_ada[l])
        sh1, sc1, gt1, sh2, sc2, gt2 = [m.reshape(bsz, 1, d) for m in jnp.split(mod, 6, axis=-1)]

        wi = w_in[l]
        o_q, o_k, o_v, o_g, o_r = D_S5, D_S5 + GLA_QK, D_S5 + 2 * GLA_QK, D_S5 + 2 * GLA_QK + D_GLA, \
            D_S5 + 2 * GLA_QK + D_GLA + GLA_LOWRANK
        w_cat = jnp.concatenate([wi[:, :o_g], wi[:, o_r:], wi[:, o_g:o_r],
                                 jnp.zeros((d, LANE - GLA_LOWRANK), wi.dtype)], axis=1).astype(BF16)
        u, q, k, v, r, glr = _inproj(x, g_norm_mix[l], sh1, sc1, w_cat)

        mats = _s5_prep(s5_lambda_re[l], s5_lambda_im[l], s5_log_dt[l], s5_b_re[l], s5_b_im[l],
                        s5_c_re[l], s5_c_im[l], s5_d[l])
        ms5 = _s5(u, mats, s5_w_glu[l].T.astype(BF16), s5_b_glu[l].reshape(D_S5, 1),
                  s5_norm_g[l].reshape(D_S5, 1), w_out[l][:D_S5].astype(BF16))

        wg2 = jnp.concatenate([gla_w_g2[l], jnp.zeros((LANE - GLA_LOWRANK, GLA_QK), F32)], axis=0).astype(BF16)
        yg = _gla(q, k, v, glr, r, wg2, gla_b_g2[l].reshape(1, GLA_QK), gla_norm_g[l].reshape(1, GLA_DV))

        x1, h2p, idx_t, rank_t, gates, counts = _mix(
            ms5, yg, x, w_out[l][D_S5:].astype(BF16), gt1, g_norm_moe[l].reshape(1, d), sh2, sc2, gt2,
            w_router[l].T.astype(BF16), router_bias[l].reshape(N_EXPERTS, 1),
            sh_w_gate[l].astype(BF16), sh_w_up[l].astype(BF16), sh_w_down[l].astype(BF16))

        n_blocks = (T * TOP_K + N_EXPERTS * (MOE_BLOCK - 1) + MOE_BLOCK - 1) // MOE_BLOCK
        n_slots = n_blocks * MOE_BLOCK
        padded, pends, pstarts, n_valid = _block_table(counts[:, 0])
        pos_t = _pos(idx_t, rank_t, pstarts).reshape(-1)
        xs = _dispatch(pends, padded, pos_t, h2p.reshape(T, ROW_CHUNKS, LANE), n_slots)
        ye = _experts(pstarts // MOE_BLOCK, padded // MOE_BLOCK, n_valid,
                      xs.reshape(n_slots * ROW_CHUNKS, LANE), exp_w_gate[l], exp_w_up[l], exp_w_down[l])
    return _final(pos_t, gates, x1, gt2, g_final, ye.reshape(n_slots, ROW_CHUNKS, LANE))
```

```python
import functools

import jax
import jax.numpy as jnp
from jax import lax
from jax.experimental import pallas as pl
from jax.experimental.pallas import tpu as pltpu

F32 = jnp.float32
BF16 = jnp.bfloat16

D_MODEL = 1024
D_S5 = 512
S5_GROUP = 16
S5_GROUPS = 32
S5_STATE = 64
S5_CHUNK = 16
S5_STEP_GROUPS = 8
S5_LEVELS = 7
S5_APOW_COLS = 16
D_GLA = 512
GLA_HEADS = 4
GLA_DK = 64
GLA_DV = 128
GLA_QK = 256
GLA_LOWRANK = 16
GLA_TAU = 16.0
GLA_CHUNK = 64
LANE = 128
N_EXPERTS = 256
TOP_K = 8
N_GROUPS = 8
TOPK_GROUPS = 4
D_EXPERT = 256
ROUTED_SCALE = 2.5
EPS = 1e-6
MOE_BLOCK = 256
ROW_CHUNKS = D_MODEL // 2 // LANE
VMEM_LIMIT = 48 * 1024 * 1024


def _silu(x):
    return x * jax.nn.sigmoid(x)


def _params(*sem):
    return pltpu.CompilerParams(dimension_semantics=sem, vmem_limit_bytes=VMEM_LIMIT)


def _ada_kernel(c_ref, w_ref, b_ref, o_ref):
    s = _silu(c_ref[...]).astype(BF16)
    o_ref[...] = jnp.dot(s, w_ref[...].astype(BF16), preferred_element_type=F32) + b_ref[...]


def _ada(c, w, b):
    bsz, d = c.shape
    n = w.shape[1]
    tn = 1024
    return pl.pallas_call(
        _ada_kernel,
        grid=(n // tn,),
        in_specs=[pl.BlockSpec((bsz, d), lambda j: (0, 0)),
                  pl.BlockSpec((d, tn), lambda j: (0, j)),
                  pl.BlockSpec((1, tn), lambda j: (0, j))],
        out_specs=pl.BlockSpec((bsz, tn), lambda j: (0, j)),
        out_shape=jax.ShapeDtypeStruct((bsz, n), F32),
        compiler_params=_params("arbitrary"),
        name="ada",
    )(c, w, b.reshape(1, n))


def _inproj_kernel(x_ref, g_ref, sh_ref, sc_ref, w_ref,
                   u_ref, q_ref, k_ref, v_ref, r_ref, glr_ref):
    x = x_ref[0]
    ms = jnp.mean(x * x, axis=-1, keepdims=True)
    h = (x * lax.rsqrt(ms + EPS)) * g_ref[...]
    h = h * (1.0 + sc_ref[0]) + sh_ref[0]
    hb = h.astype(BF16)
    u = jnp.dot(hb, w_ref[:, 0:D_S5], preferred_element_type=F32)
    for c in range(D_S5 // LANE):
        u_ref[0, c] = u[:, c * LANE:(c + 1) * LANE]
    col = D_S5
    for ref in (q_ref, k_ref, v_ref, r_ref, glr_ref):
        n = ref.shape[-1]
        ref[0] = jnp.dot(hb, w_ref[:, col:col + n], preferred_element_type=F32).astype(ref.dtype)
        col += n


def _inproj(x, g, sh, sc, w):
    bsz, L, d = x.shape
    tt = 512
    widths = (GLA_QK, GLA_QK, D_GLA, D_GLA, LANE)
    tok = lambda n: pl.BlockSpec((1, tt, n), lambda b, i: (b, i, 0))
    vec = pl.BlockSpec((1, 1, d), lambda b, i: (b, 0, 0))
    return pl.pallas_call(
        _inproj_kernel,
        grid=(bsz, L // tt),
        in_specs=[tok(d), pl.BlockSpec((1, d), lambda b, i: (0, 0)), vec, vec,
                  pl.BlockSpec(w.shape, lambda b, i: (0, 0))],
        out_specs=[pl.BlockSpec((1, D_S5 // LANE, tt, LANE), lambda b, i: (b, 0, i, 0))] + [tok(n) for n in widths],
        out_shape=[jax.ShapeDtypeStruct((bsz, D_S5 // LANE, L, LANE), F32)]
        + [jax.ShapeDtypeStruct((bsz, L, n), BF16) for n in widths],
        compiler_params=_params("arbitrary", "arbitrary"),
        name="inproj",
    )(x, g.reshape(1, d), sh, sc, w)


def _s5_prep(lam_re, lam_im, log_dt, b_re, b_im, c_re, c_im, d_skip):
    G, N, C, TC = S5_GROUPS, S5_STATE, S5_GROUP, S5_CHUNK
    hp = lax.Precision.HIGHEST
    dt = jnp.exp(log_dt)[:, None]
    lr, li = lam_re, lam_im
    mag = jnp.exp(lr * dt)
    ab_re, ab_im = mag * jnp.cos(li * dt), mag * jnp.sin(li * dt)
    den = lr * lr + li * li
    nr = ab_re - 1.0
    coef_re = ((nr * lr + ab_im * li) / den)[..., None]
    coef_im = ((ab_im * lr - nr * li) / den)[..., None]
    bb_re = coef_re * b_re - coef_im * b_im
    bb_im = coef_re * b_im + coef_im * b_re
    p = jnp.arange(TC + 1, dtype=F32)[:, None, None]
    pm = jnp.exp(lr * dt * p)
    pr, pi = pm * jnp.cos(li * dt * p), pm * jnp.sin(li * dt * p)
    ca_re = c_re[None] * pr[:, :, None, :] - c_im[None] * pi[:, :, None, :]
    ca_im = c_re[None] * pi[:, :, None, :] + c_im[None] * pr[:, :, None, :]
    kern = (jnp.einsum('tgon,gni->tgoi', ca_re, bb_re, precision=hp)
            - jnp.einsum('tgon,gni->tgoi', ca_im, bb_im, precision=hp))
    kern = kern.at[0].add(jnp.eye(C, dtype=F32)[None] * d_skip[:, :, None])
    lag = jnp.arange(TC)[None, :] - jnp.arange(TC)[:, None]
    toep = jnp.where((lag >= 0)[:, :, None, None, None], kern[jnp.maximum(lag, 0)], 0.0)
    toep_t = toep.transpose(2, 1, 3, 0, 4).reshape(G, TC * C, TC * C)
    rr, ri = pr[TC - 1 - jnp.arange(TC)], pi[TC - 1 - jnp.arange(TC)]
    binc_re = rr[..., None] * bb_re[None] - ri[..., None] * bb_im[None]
    binc_im = rr[..., None] * bb_im[None] + ri[..., None] * bb_re[None]
    binc_re_t = binc_re.transpose(1, 2, 0, 3).reshape(G, N, TC * C)
    binc_im_t = binc_im.transpose(1, 2, 0, 3).reshape(G, N, TC * C)
    cm_re_t = ca_re[1:].transpose(1, 0, 2, 3).reshape(G, TC * C, N)
    cm_im_t = (-ca_im[1:]).transpose(1, 0, 2, 3).reshape(G, TC * C, N)
    q = (TC * 2.0 ** jnp.arange(S5_LEVELS, dtype=F32))[:, None, None]
    qm = jnp.exp(lr * dt * q)
    qr, qi = qm * jnp.cos(li * dt * q), qm * jnp.sin(li * dt * q)
    apow = jnp.stack([qr, qi], axis=1).reshape(2 * S5_LEVELS, G, N).transpose(1, 2, 0)
    apow = jnp.concatenate([apow, jnp.zeros((G, N, S5_APOW_COLS - 2 * S5_LEVELS), F32)], axis=-1)
    return (toep_t.astype(BF16), binc_re_t.astype(BF16), binc_im_t.astype(BF16),
            cm_re_t.astype(BF16), cm_im_t.astype(BF16), apow)


def _s5_kernel(u_ref, eye_ref, toep_ref, bre_ref, bim_ref, cre_ref, cim_ref, apow_ref,
               wglu_ref, bglu_ref, sng_ref, wo_ref, o_ref, ut_ref, yt_ref):
    j = pl.program_id(1)
    n_chunks = u_ref.shape[2] // S5_CHUNK
    nt = (((1,), (1,)), ((), ()))

    @pl.when(j == 0)
    def _():
        for s in range(S5_CHUNK):
            us = jnp.concatenate([u_ref[0, c, pl.ds(s, n_chunks, stride=S5_CHUNK), :]
                                  for c in range(D_S5 // LANE)], axis=1).astype(BF16)
            ut_ref[:, s * n_chunks:(s + 1) * n_chunks] = lax.dot_general(
                eye_ref[...], us, nt, preferred_element_type=F32).astype(BF16)

    lane = lax.broadcasted_iota(jnp.int32, (S5_STATE, n_chunks), 1)
    groups = range(S5_STEP_GROUPS)
    rows = [pl.ds(pl.multiple_of(j * (S5_STEP_GROUPS * S5_GROUP) + gl * S5_GROUP, S5_GROUP), S5_GROUP)
            for gl in groups]
    zt = [jnp.concatenate([ut_ref[rows[gl], s * n_chunks:(s + 1) * n_chunks] for s in range(S5_CHUNK)], axis=0)
          for gl in groups]
    xr = [jnp.where(lane >= 1, pltpu.roll(jnp.dot(bre_ref[gl], zt[gl], preferred_element_type=F32), 1, axis=1), 0.0)
          for gl in groups]
    xi = [jnp.where(lane >= 1, pltpu.roll(jnp.dot(bim_ref[gl], zt[gl], preferred_element_type=F32), 1, axis=1), 0.0)
          for gl in groups]
    for lv in range(S5_LEVELS):
        d = 1 << lv
        for gl in groups:
            ar = apow_ref[gl, :, 2 * lv:2 * lv + 1]
            ai = apow_ref[gl, :, 2 * lv + 1:2 * lv + 2]
            sr = jnp.where(lane >= d, pltpu.roll(xr[gl], d, axis=1), 0.0)
            si = jnp.where(lane >= d, pltpu.roll(xi[gl], d, axis=1), 0.0)
            xr[gl], xi[gl] = xr[gl] + ar * sr - ai * si, xi[gl] + ar * si + ai * sr
    for gl in groups:
        yt = (jnp.dot(toep_ref[gl], zt[gl], preferred_element_type=F32)
              + jnp.dot(cre_ref[gl], xr[gl].astype(BF16), preferred_element_type=F32)
              + jnp.dot(cim_ref[gl], xi[gl].astype(BF16), preferred_element_type=F32))
        for t in range(S5_CHUNK):
            yt_ref[rows[gl], t * n_chunks:(t + 1) * n_chunks] = yt[t * S5_GROUP:(t + 1) * S5_GROUP, :]

    @pl.when(j == pl.num_programs(1) - 1)
    def _():
        cw = 2 * n_chunks
        for cc in range(yt_ref.shape[1] // cw):
            y = yt_ref[:, cc * cw:(cc + 1) * cw]
            g = y * (0.5 * (1.0 + jnp.tanh(0.7978845608028654 * (y + 0.044715 * (y * y * y)))))
            z = jnp.dot(wglu_ref[...], g.astype(BF16), preferred_element_type=F32) + bglu_ref[...]
            out = g * jax.nn.sigmoid(z)
            out = out * lax.rsqrt(jnp.mean(out * out, axis=0, keepdims=True) + EPS) * sng_ref[...]
            mixc = jnp.dot(out.T.astype(BF16), wo_ref[...], preferred_element_type=F32)
            for sl in range(cw // n_chunks):
                s = cc * (cw // n_chunks) + sl
                for c in range(D_MODEL // LANE):
                    o_ref[0, c, pl.ds(s, n_chunks, stride=S5_CHUNK), :] = mixc[sl * n_chunks:(sl + 1) * n_chunks,
                                                                               c * LANE:(c + 1) * LANE]


def _s5(u, mats, wglu_t, bglu, sng, wo_s5):
    bsz, _, L, _ = u.shape
    assert L // S5_CHUNK == LANE, "one lane tile of chunks per sequence"
    toep, bre, bim, cre, cim, apow = mats
    sg = S5_STEP_GROUPS
    eye = jnp.eye(D_S5, dtype=BF16)
    grp = lambda a, b: pl.BlockSpec((sg, a, b), lambda bi, j: (j, 0, 0))
    full = lambda a: pl.BlockSpec(a.shape, lambda bi, j: (0,) * a.ndim)
    w = S5_CHUNK * S5_GROUP
    return pl.pallas_call(
        _s5_kernel,
        grid=(bsz, S5_GROUPS // sg),
        in_specs=[pl.BlockSpec((1, D_S5 // LANE, L, LANE), lambda bi, j: (bi, 0, 0, 0)), full(eye),
                  grp(w, w), grp(S5_STATE, w), grp(S5_STATE, w), grp(w, S5_STATE), grp(w, S5_STATE),
                  grp(S5_STATE, S5_APOW_COLS), full(wglu_t), full(bglu), full(sng), full(wo_s5)],
        out_specs=pl.BlockSpec((1, D_MODEL // LANE, L, LANE), lambda bi, j: (bi, 0, 0, 0)),
        out_shape=jax.ShapeDtypeStruct((bsz, D_MODEL // LANE, L, LANE), F32),
        scratch_shapes=[pltpu.VMEM((D_S5, L), BF16), pltpu.VMEM((D_S5, L), F32)],
        compiler_params=_params("arbitrary", "arbitrary"),
        name="s5",
    )(u, eye, toep, bre, bim, cre, cim, apow, wglu_t, bglu, sng, wo_s5)


def _gla_kernel(q_ref, k_ref, v_ref, glr_ref, r_ref, wg_ref, bg_ref, ng_ref, o_ref, st_ref):
    lt = q_ref.shape[1]
    C = GLA_CHUNK

    @pl.when(pl.program_id(1) == 0)
    def _():
        st_ref[...] = jnp.zeros_like(st_ref)

    z = jnp.dot(glr_ref[0], wg_ref[...], preferred_element_type=F32) + bg_ref[...]
    log_a = (jnp.minimum(z, 0.0) - jnp.log(1.0 + jnp.exp(-jnp.abs(z)))) * (1.0 / GLA_TAU)
    ri = lax.broadcasted_iota(jnp.int32, (lt, lt), 0)
    ci = lax.broadcasted_iota(jnp.int32, (lt, lt), 1)
    tril = jnp.where(((ri >> 6) == (ci >> 6)) & (ci <= ri), 1.0, 0.0).astype(BF16)
    la_hi = log_a.astype(BF16)
    la_lo = (log_a - la_hi.astype(F32)).astype(BF16)
    bcum = (jnp.dot(tril, la_hi, preferred_element_type=F32)
            + jnp.dot(tril, la_lo, preferred_element_type=F32))
    q = q_ref[0].astype(F32) * (GLA_DK ** -0.5)
    k = k_ref[0].astype(F32)
    qi = q * jnp.exp(bcum)
    ki = k * jnp.exp(-bcum)
    lane_head = lax.broadcasted_iota(jnp.int32, (1, GLA_QK), 1) >> 6
    causal = ((lax.broadcasted_iota(jnp.int32, (GLA_HEADS * C, C), 0) & (C - 1))
              >= lax.broadcasted_iota(jnp.int32, (GLA_HEADS * C, C), 1))
    same_head = ((lax.broadcasted_iota(jnp.int32, (D_GLA, GLA_QK), 0) >> 7)
                 == (lax.broadcasted_iota(jnp.int32, (D_GLA, GLA_QK), 1) >> 6))
    nt = (((1,), (1,)), ((), ()))
    ng = ng_ref[...]
    for c in range(lt // C):
        sl = slice(c * C, (c + 1) * C)
        bc = bcum[sl]
        bl = bc[C - 1:C, :]
        kd = k[sl] * jnp.exp(bl - bc)
        qic = qi[sl]
        qs = jnp.concatenate([jnp.where(lane_head == h, qic, 0.0) for h in range(GLA_HEADS)],
                             axis=0).astype(BF16)
        sc = lax.dot_general(qs, ki[sl].astype(BF16), nt, preferred_element_type=F32)
        p = jnp.where(causal, sc, 0.0).astype(BF16)
        vc = v_ref[0, sl, :]
        o_intra = jnp.concatenate(
            [jnp.dot(p[h * C:(h + 1) * C], vc[:, h * GLA_DV:(h + 1) * GLA_DV], preferred_element_type=F32)
             for h in range(GLA_HEADS)], axis=1)
        st = st_ref[...]
        o_inter = lax.dot_general(qic.astype(BF16), st.astype(BF16), nt, preferred_element_type=F32)
        v_t = vc.astype(F32).T.astype(BF16)
        kv_t = jnp.dot(v_t, kd.astype(BF16), preferred_element_type=F32)
        st_ref[...] = st * jnp.exp(bl) + jnp.where(same_head, kv_t, 0.0)
        o = o_intra + o_inter
        parts = []
        for h in range(GLA_HEADS):
            oh = o[:, h * GLA_DV:(h + 1) * GLA_DV]
            oh = oh * lax.rsqrt(jnp.mean(oh * oh, axis=-1, keepdims=True) + EPS)
            parts.append(oh * ng)
        r = r_ref[0, sl, :].astype(F32)
        o_ref[0, sl, :] = (jnp.concatenate(parts, axis=1) * _silu(r)).astype(o_ref.dtype)


def _gla(q, k, v, glr, r, wg, bg, ng):
    bsz, L, _ = q.shape
    lt = 256
    tok = lambda n: pl.BlockSpec((1, lt, n), lambda b, i: (b, i, 0))
    full = lambda a: pl.BlockSpec(a.shape, lambda b, i: (0,) * a.ndim)
    return pl.pallas_call(
        _gla_kernel,
        grid=(bsz, L // lt),
        in_specs=[tok(GLA_QK), tok(GLA_QK), tok(D_GLA), tok(LANE), tok(D_GLA), full(wg), full(bg), full(ng)],
        out_specs=tok(D_GLA),
        out_shape=jax.ShapeDtypeStruct((bsz, L, D_GLA), BF16),
        scratch_shapes=[pltpu.VMEM((D_GLA, GLA_QK), F32)],
        compiler_params=_params("arbitrary", "arbitrary"),
        name="gla",
    )(q, k, v, glr, r, wg, bg, ng)


def _pack_bf16_pairs(x):
    w = x.shape[1] // 2
    xr = x.astype(BF16).astype(F32)
    lo = lax.bitcast_convert_type(xr[:, :w], jnp.uint32) >> 16
    hi = lax.bitcast_convert_type(xr[:, w:], jnp.uint32) & jnp.uint32(0xFFFF0000)
    return lo | hi


def _unpack_bf16_pairs(p):
    lo = lax.bitcast_convert_type(p << 16, F32)
    hi = lax.bitcast_convert_type(p & jnp.uint32(0xFFFF0000), F32)
    return lo, hi


def _store_rows(ref, packed):
    n = packed.shape[0]
    for c in range(ROW_CHUNKS):
        ref[pl.ds(c, n, stride=ROW_CHUNKS), :] = packed[:, c * LANE:(c + 1) * LANE]


def _load_rows(ref, n):
    return jnp.concatenate([ref[pl.ds(c, n, stride=ROW_CHUNKS), :] for c in range(ROW_CHUNKS)], axis=1)


def _route_tile(lg_t, bias_col, tri, carry_ref):
    n_e, tt = lg_t.shape
    per_group = n_e // N_GROUPS
    scores = jax.nn.sigmoid(lg_t)
    biased = scores + bias_col
    row = lax.broadcasted_iota(jnp.int32, (n_e, tt), 0)
    neg = -jnp.inf
    group_score = []
    for g in range(N_GROUPS):
        b = biased[g * per_group:(g + 1) * per_group]
        r = lax.broadcasted_iota(jnp.int32, (per_group, tt), 0) + g * per_group
        m1 = jnp.max(b, axis=0, keepdims=True)
        i1 = jnp.min(jnp.where(b == m1, r, n_e), axis=0, keepdims=True)
        m2 = jnp.max(jnp.where(r == i1, neg, b), axis=0, keepdims=True)
        group_score.append(m1 + m2)
    parts = []
    for g in range(N_GROUPS):
        ahead = jnp.zeros((1, tt), jnp.int32)
        for g2 in range(N_GROUPS):
            if g2 != g:
                beats = (group_score[g2] >= group_score[g]) if g2 < g else (group_score[g2] > group_score[g])
                ahead = ahead + beats.astype(jnp.int32)
        parts.append(jnp.where(ahead < TOPK_GROUPS, biased[g * per_group:(g + 1) * per_group], neg))
    masked = jnp.concatenate(parts, axis=0)
    work = masked
    idxs = []
    for _ in range(TOP_K):
        m = jnp.max(work, axis=0, keepdims=True)
        ii = jnp.min(jnp.where(work == m, row, n_e), axis=0, keepdims=True)
        idxs.append(ii)
        work = jnp.where(row == ii, neg, work)
    sel = work != masked
    w = jnp.where(sel, scores, 0.0)
    gate_dense = w / jnp.sum(w, axis=0, keepdims=True) * ROUTED_SCALE
    mt = jnp.where(sel, 1.0, 0.0)
    rank_dense = jnp.dot(mt.astype(BF16), tri, preferred_element_type=F32) + carry_ref[...]
    carry_ref[...] += jnp.sum(mt, axis=1, keepdims=True)
    ranks, gts = [], []
    for ii in idxs:
        oh = row == ii
        ranks.append(jnp.sum(jnp.where(oh, rank_dense, 0.0), axis=0, keepdims=True))
        gts.append(jnp.sum(jnp.where(oh, gate_dense, 0.0), axis=0, keepdims=True))
    idx_t = jnp.concatenate(idxs, axis=0)
    rank_t = jnp.concatenate(ranks, axis=0).astype(jnp.int32)
    gate_t = jnp.concatenate(gts + [jnp.zeros((LANE - TOP_K, tt), F32)], axis=0)
    return idx_t, rank_t, gate_t.T[:, :TOP_K]


def _mix_kernel(ms_ref, yg_ref, x_ref, wo_ref, gt1_ref,
                gn_ref, sh2_ref, sc2_ref, gt2_ref, wrt_ref, rb_ref, tri_ref, wsg_ref, wsu_ref, wsd_ref,
                x1_ref, h2_ref, idx_ref, rank_ref, gate_ref, cnt_ref, carry_ref):
    @pl.when((pl.program_id(0) == 0) & (pl.program_id(1) == 0))
    def _():
        carry_ref[...] = jnp.zeros_like(carry_ref)

    mix_s5 = jnp.concatenate([ms_ref[0, c] for c in range(D_MODEL // LANE)], axis=1)
    mix = mix_s5 + jnp.dot(yg_ref[0], wo_ref[...], preferred_element_type=F32)
    x1 = x_ref[0] + gt1_ref[0] * mix
    h2 = x1 * lax.rsqrt(jnp.mean(x1 * x1, axis=-1, keepdims=True) + EPS) * gn_ref[...]
    h2 = h2 * (1.0 + sc2_ref[0]) + sh2_ref[0]
    hb = h2.astype(BF16)
    _store_rows(h2_ref, _pack_bf16_pairs(h2))
    lg_t = lax.dot_general(wrt_ref[...], hb, (((1,), (1,)), ((), ())), preferred_element_type=F32)
    idx_t, rank_t, gates = _route_tile(lg_t, rb_ref[...], tri_ref[...], carry_ref)
    idx_ref[...] = idx_t
    rank_ref[...] = rank_t
    gate_ref[...] = gates
    cnt_ref[...] = carry_ref[...]
    a = _silu(jnp.dot(hb, wsg_ref[...], preferred_element_type=F32)) * jnp.dot(
        hb, wsu_ref[...], preferred_element_type=F32)
    shared = jnp.dot(a.astype(BF16), wsd_ref[...], preferred_element_type=F32)
    x1_ref[0] = x1 + gt2_ref[0] * shared


def _mix(ms5, yg, x, wo, gt1, gn, sh2, sc2, gt2, wrt, rb, wsg, wsu, wsd):
    bsz, L, d = x.shape
    tt = 512
    nt = L // tt
    T = bsz * L
    tri = (jnp.arange(tt)[:, None] < jnp.arange(tt)[None, :]).astype(BF16)
    tok = lambda n: pl.BlockSpec((1, tt, n), lambda b, i: (b, i, 0))
    vec = pl.BlockSpec((1, 1, d), lambda b, i: (b, 0, 0))
    full = lambda a: pl.BlockSpec(a.shape, lambda b, i: (0,) * a.ndim)
    lanes = pl.BlockSpec((TOP_K, tt), lambda b, i: (0, b * nt + i))
    return pl.pallas_call(
        _mix_kernel,
        grid=(bsz, nt),
        in_specs=[pl.BlockSpec((1, d // LANE, tt, LANE), lambda b, i: (b, 0, i, 0)), tok(D_GLA), tok(d), full(wo), vec,
                  full(gn), vec, vec, vec, full(wrt), full(rb), full(tri), full(wsg), full(wsu), full(wsd)],
        out_specs=[tok(d),
                   pl.BlockSpec((tt * ROW_CHUNKS, LANE), lambda b, i: (b * nt + i, 0)),
                   lanes, lanes,
                   pl.BlockSpec((tt, TOP_K), lambda b, i: (b * nt + i, 0)),
                   pl.BlockSpec((N_EXPERTS, 1), lambda b, i: (0, 0))],
        out_shape=[jax.ShapeDtypeStruct((bsz, L, d), F32),
                   jax.ShapeDtypeStruct((T * ROW_CHUNKS, LANE), jnp.uint32),
                   jax.ShapeDtypeStruct((TOP_K, T), jnp.int32),
                   jax.ShapeDtypeStruct((TOP_K, T), jnp.int32),
                   jax.ShapeDtypeStruct((T, TOP_K), F32),
                   jax.ShapeDtypeStruct((N_EXPERTS, 1), F32)],
        scratch_shapes=[pltpu.VMEM((N_EXPERTS, 1), F32)],
        compiler_params=_params("arbitrary", "arbitrary"),
        name="mix",
    )(ms5, yg, x, wo, gt1, gn, sh2, sc2, gt2, wrt, rb, tri, wsg, wsu, wsd)


def _pos_kernel(idx_ref, rank_ref, ps_ref, pos_ref):
    n_e = ps_ref.shape[0]
    tt = idx_ref.shape[1]
    row = lax.broadcasted_iota(jnp.int32, (n_e, tt), 0)
    ps = ps_ref[...]
    starts = [jnp.sum(jnp.where(row == idx_ref[k:k + 1, :], ps, 0.0), axis=0, keepdims=True)
              for k in range(TOP_K)]
    pos = jnp.concatenate(starts, axis=0).astype(jnp.int32) + rank_ref[...]
    for jh in range(tt // LANE):
        pos_ref[jh * TOP_K:(jh + 1) * TOP_K, :] = pos[:, jh * LANE:(jh + 1) * LANE]


def _pos(idx_t, rank_t, pstart):
    T = idx_t.shape[1]
    tt = 2048
    blk = pl.BlockSpec((TOP_K, tt), lambda i: (0, i))
    return pl.pallas_call(
        _pos_kernel,
        grid=(T // tt,),
        in_specs=[blk, blk, pl.BlockSpec((N_EXPERTS, 1), lambda i: (0, 0))],
        out_specs=pl.BlockSpec((tt // LANE * TOP_K, LANE), lambda i: (i, 0)),
        out_shape=jax.ShapeDtypeStruct((T // LANE * TOP_K, LANE), jnp.int32),
        compiler_params=_params("arbitrary"),
        name="pos",
    )(idx_t, rank_t, pstart.astype(F32).reshape(N_EXPERTS, 1))


def _dispatch_kernel(pend_ref, padded_ref, pos_ref, h_ref, xs_hbm, zero_ref, pos_smem, sem_pos, sem_zero, sem_row):
    tt = h_ref.shape[0]
    step = pl.program_id(0)
    load_pos = pltpu.make_async_copy(pos_ref, pos_smem, sem_pos)
    load_pos.start()

    def zero_block(start):
        return pltpu.make_async_copy(zero_ref, xs_hbm.at[pl.ds(pl.multiple_of(start, MOE_BLOCK), MOE_BLOCK)],
                                     sem_zero)

    def for_each_zero_block(fn):
        def per_expert(e, c):
            @pl.when(padded_ref[e] > 0)
            def _():
                fn(zero_block(pend_ref[e] - MOE_BLOCK))
            return c

        def per_tail_block(g, c):
            fn(zero_block(g * MOE_BLOCK))
            return c

        lax.fori_loop(0, N_EXPERTS, per_expert, 0)
        lax.fori_loop(pend_ref[N_EXPERTS - 1] // MOE_BLOCK, xs_hbm.shape[0] // MOE_BLOCK, per_tail_block, 0)

    @pl.when(step == 0)
    def _():
        zero_ref[...] = jnp.zeros_like(zero_ref)
        for_each_zero_block(lambda cp: cp.start())
        for_each_zero_block(lambda cp: cp.wait())

    load_pos.wait()
    for jh in range(tt // LANE):
        def issue(jl, c, jh=jh):
            for k in range(TOP_K):
                slot = pos_smem[(jh * TOP_K + k) * LANE + jl]
                pltpu.make_async_copy(h_ref.at[jh * LANE + jl], xs_hbm.at[slot], sem_row).start(priority=k % 2)
            return c

        lax.fori_loop(0, LANE, issue, 0, unroll=4)
    for k in range(TOP_K):
        pltpu.make_async_copy(h_ref, xs_hbm.at[pl.ds(0, tt)], sem_row).wait()


def _dispatch(pends, padded, pos_t, h_rows, n_slots):
    T = h_rows.shape[0]
    tt = 512
    return pl.pallas_call(
        _dispatch_kernel,
        grid_spec=pltpu.PrefetchScalarGridSpec(
            num_scalar_prefetch=2,
            grid=(T // tt,),
            in_specs=[pl.BlockSpec((tt * TOP_K,), lambda i, pe, pa: (i,)),
                      pl.BlockSpec((tt, ROW_CHUNKS, LANE), lambda i, pe, pa: (i, 0, 0))],
            out_specs=pl.BlockSpec(memory_space=pl.ANY),
            scratch_shapes=[pltpu.VMEM((MOE_BLOCK, ROW_CHUNKS, LANE), jnp.uint32),
                            pltpu.SMEM((tt * TOP_K,), jnp.int32),
                            pltpu.SemaphoreType.DMA, pltpu.SemaphoreType.DMA, pltpu.SemaphoreType.DMA]),
        out_shape=jax.ShapeDtypeStruct((n_slots, ROW_CHUNKS, LANE), jnp.uint32),
        compiler_params=_params("arbitrary"),
        name="dispatch",
    )(pends, padded, pos_t, h_rows)


X_GROUP = 4
X_AHEAD = 4
X_BUFS = X_AHEAD + X_GROUP
Y_BUFS = 4


def _expert_kernel(b0_ref, nb_ref, nv_ref, wg_ref, wu_ref, wd_ref, xs_hbm, ye_hbm,
                   wg_s, wu_s, wd_s, xbuf, ybuf, semx, semy, *, n_blocks):
    e = pl.program_id(0)
    nb = nb_ref[e]
    b0 = b0_ref[e]
    nv = nv_ref[0]
    rows = MOE_BLOCK * ROW_CHUNKS

    def block(ref, g):
        return ref.at[pl.ds(pl.multiple_of(g * rows, rows), rows)]

    def x_copy(g):
        slot = g % X_BUFS
        return pltpu.make_async_copy(block(xs_hbm, g), xbuf.at[slot], semx.at[slot])

    def y_copy(g):
        slot = g % Y_BUFS
        return pltpu.make_async_copy(ybuf.at[slot], block(ye_hbm, g), semy.at[slot])

    @pl.when(e == 0)
    def _():
        for g in range(X_AHEAD):
            x_copy(g).start()

    @pl.when(nb > 0)
    def _():
        wg_s[...] = wg_ref[0].astype(BF16)
        wu_s[...] = wu_ref[0].astype(BF16)
        wd_s[...] = wd_ref[0].astype(BF16)

    def fetch(g):
        x_copy(g).wait()

        @pl.when(g + X_AHEAD < nv)
        def _():
            x_copy(g + X_AHEAD).start()

    def load_x(g):
        lo, hi = _unpack_bf16_pairs(_load_rows(xbuf.at[g % X_BUFS], MOE_BLOCK))
        return jnp.concatenate([lo, hi], axis=1).astype(BF16)

    def emit(g, y):
        @pl.when(g >= Y_BUFS)
        def _():
            y_copy(g - Y_BUFS).wait()

        _store_rows(ybuf.at[g % Y_BUFS], y)
        y_copy(g).start()

    def run(g, n):
        for b in range(n):
            fetch(g + b)
        x = jnp.concatenate([load_x(g + b) for b in range(n)], axis=0) if n > 1 else load_x(g)
        a = jnp.dot(x, wg_s[...], preferred_element_type=F32)
        u = jnp.dot(x, wu_s[...], preferred_element_type=F32)
        h = (_silu(a) * u).astype(BF16)
        y = _pack_bf16_pairs(jnp.dot(h, wd_s[...], preferred_element_type=F32))
        for b in range(n):
            emit(g + b, y[b * MOE_BLOCK:(b + 1) * MOE_BLOCK])

    def group(i, c):
        run(b0 + X_GROUP * i, X_GROUP)
        return c

    lax.fori_loop(0, nb // X_GROUP, group, 0)
    done = nb // X_GROUP * X_GROUP
    n = X_GROUP // 2
    while n >= 1:
        @pl.when((nb & n) != 0)
        def _(n=n, done=done):
            run(b0 + done, n)

        done = done + (nb & n)
        n //= 2

    @pl.when(e == pl.num_programs(0) - 1)
    def _():
        for back in range(Y_BUFS, 0, -1):
            y_copy(nv - back).wait()
        ybuf[0] = jnp.zeros(ybuf.shape[1:], ybuf.dtype)

        def fill(g):
            return pltpu.make_async_copy(ybuf.at[0], block(ye_hbm, g), semy.at[0])

        lax.fori_loop(nv, n_blocks, lambda g, c: (fill(g).start(), c)[1], 0)
        lax.fori_loop(nv, n_blocks, lambda g, c: (fill(g).wait(), c)[1], 0)


def _experts(first_block, num_blocks, n_valid, xs, wg, wu, wd):
    rows = MOE_BLOCK * ROW_CHUNKS
    n_blocks = xs.shape[0] // rows
    n_e, d, _ = wg.shape
    wsel = lambda e, b0, nb, nv: (e, 0, 0)
    return pl.pallas_call(
        functools.partial(_expert_kernel, n_blocks=n_blocks),
        grid_spec=pltpu.PrefetchScalarGridSpec(
            num_scalar_prefetch=3,
            grid=(n_e,),
            in_specs=[pl.BlockSpec((1, d, D_EXPERT), wsel),
                      pl.BlockSpec((1, d, D_EXPERT), wsel),
                      pl.BlockSpec((1, D_EXPERT, d), wsel),
                      pl.BlockSpec(memory_space=pl.ANY)],
            out_specs=pl.BlockSpec(memory_space=pl.ANY),
            scratch_shapes=[pltpu.VMEM((d, D_EXPERT), BF16), pltpu.VMEM((d, D_EXPERT), BF16),
                            pltpu.VMEM((D_EXPERT, d), BF16),
                            pltpu.VMEM((X_BUFS, rows, LANE), jnp.uint32),
                            pltpu.VMEM((Y_BUFS, rows, LANE), jnp.uint32),
                            pltpu.SemaphoreType.DMA((X_BUFS,)), pltpu.SemaphoreType.DMA((Y_BUFS,))]),
        out_shape=jax.ShapeDtypeStruct(xs.shape, jnp.uint32),
        compiler_params=_params("arbitrary"),
        name="experts",
    )(first_block, num_blocks, n_valid, wg, wu, wd, xs)


def _final_kernel(pos0_ref, posn_ref, gate_ref, x_ref, gt_ref, g_ref, ye_hbm, o_ref,
                  buf_ref, pos_smem, sem_pos, sem_row):
    tt = x_ref.shape[1]
    i = pl.program_id(0)

    def gather_tile(pos_ref, tile):
        base = (tile % 2) * (TOP_K * tt)
        sem = sem_row.at[tile % 2]
        load_pos = pltpu.make_async_copy(pos_ref, pos_smem, sem_pos)
        load_pos.start()
        load_pos.wait()
        for jh in range(tt // LANE):
            def issue(jl, c, jh=jh):
                for k in range(TOP_K):
                    slot = pos_smem[(jh * TOP_K + k) * LANE + jl]
                    pltpu.make_async_copy(ye_hbm.at[slot], buf_ref.at[base + k * tt + jh * LANE + jl],
                                          sem).start(priority=k % 2)
                return c

            lax.fori_loop(0, LANE, issue, 0, unroll=4)

    @pl.when(i == 0)
    def _():
        gather_tile(pos0_ref, i)

    @pl.when(i + 1 < pl.num_programs(0))
    def _():
        gather_tile(posn_ref, i + 1)

    base = pl.multiple_of((i % 2) * (TOP_K * tt), TOP_K * tt)
    for k in range(TOP_K):
        pltpu.make_async_copy(ye_hbm.at[pl.ds(0, tt)], buf_ref.at[pl.ds(base + k * tt, tt)],
                              sem_row.at[i % 2]).wait()
    gates = gate_ref[...]
    half = D_MODEL // 2
    acc_lo = jnp.zeros((tt, half), F32)
    acc_hi = jnp.zeros((tt, half), F32)
    rows_2d = buf_ref.reshape(buf_ref.shape[0] * ROW_CHUNKS, LANE)
    for k in range(TOP_K):
        lo, hi = _unpack_bf16_pairs(_load_rows(rows_2d.at[pl.ds((base + k * tt) * ROW_CHUNKS, tt * ROW_CHUNKS)], tt))
        gk = gates[:, k:k + 1]
        acc_lo += gk * lo
        acc_hi += gk * hi
    x = x_ref[0] + gt_ref[0] * jnp.concatenate([acc_lo, acc_hi], axis=1)
    o_ref[0] = x * lax.rsqrt(jnp.mean(x * x, axis=-1, keepdims=True) + EPS) * g_ref[...]


def _final(pos_t, gates, x1, gt2, g, ye_rows):
    bsz, L, d = x1.shape
    tt = 256
    nt = L // tt
    n = bsz * nt
    tok = pl.BlockSpec((1, tt, d), lambda i: (i // nt, i % nt, 0))
    return pl.pallas_call(
        _final_kernel,
        grid=(n,),
        in_specs=[pl.BlockSpec((tt * TOP_K,), lambda i: (0,)),
                  pl.BlockSpec((tt * TOP_K,), lambda i: (jnp.minimum(i + 1, n - 1),)),
                  pl.BlockSpec((tt, TOP_K), lambda i: (i, 0)),
                  tok, pl.BlockSpec((1, 1, d), lambda i: (i // nt, 0, 0)),
                  pl.BlockSpec((1, d), lambda i: (0, 0)),
                  pl.BlockSpec(memory_space=pl.ANY)],
        out_specs=tok,
        out_shape=jax.ShapeDtypeStruct((bsz, L, d), F32),
        scratch_shapes=[pltpu.VMEM((2 * TOP_K * tt, ROW_CHUNKS, LANE), jnp.uint32),
                        pltpu.SMEM((tt * TOP_K,), jnp.int32),
                        pltpu.SemaphoreType.DMA, pltpu.SemaphoreType.DMA((2,))],
        compiler_params=_params("arbitrary"),
        name="final",
    )(pos_t, pos_t, gates, x1, gt2, g.reshape(1, d), ye_rows)


def _block_table(counts):
    counts = counts.astype(jnp.int32)
    padded = (counts + MOE_BLOCK - 1) // MOE_BLOCK * MOE_BLOCK
    pends = jnp.cumsum(padded)
    pstarts = pends - padded
    n_valid = (pends[-1] // MOE_BLOCK).reshape(1)
    return padded, pends, pstarts, n_valid


def kernel(x, c, w_ada, b_ada, g_norm_mix, w_in, s5_lambda_re, s5_lambda_im, s5_log_dt, s5_b_re, s5_b_im, s5_c_re, s5_c_im, s5_d, s5_w_glu, s5_b_glu, s5_norm_g, gla_w_g2, gla_b_g2, gla_norm_g, w_out, g_norm_moe, w_router, router_bias, exp_w_gate, exp_w_up, exp_w_down, sh_w_gate, sh_w_up, sh_w_down, g_final):
    bsz, L, d = x.shape
    T = bsz * L
    assert w_ada.shape[0] == 1, "single-layer block"
    for l in range(1):
        mod = _ada(c, w_ada[l], b_ada[l])
        sh1, sc1, gt1, sh2, sc2, gt2 = [m.reshape(bsz, 1, d) for m in jnp.split(mod, 6, axis=-1)]

        wi = w_in[l]
        o_q, o_k, o_v, o_g, o_r = D_S5, D_S5 + GLA_QK, D_S5 + 2 * GLA_QK, D_S5 + 2 * GLA_QK + D_GLA, \
            D_S5 + 2 * GLA_QK + D_GLA + GLA_LOWRANK
        w_cat = jnp.concatenate([wi[:, :o_g], wi[:, o_r:], wi[:, o_g:o_r],
                                 jnp.zeros((d, LANE - GLA_LOWRANK), wi.dtype)], axis=1).astype(BF16)
        u, q, k, v, r, glr = _inproj(x, g_norm_mix[l], sh1, sc1, w_cat)

        mats = _s5_prep(s5_lambda_re[l], s5_lambda_im[l], s5_log_dt[l], s5_b_re[l], s5_b_im[l],
                        s5_c_re[l], s5_c_im[l], s5_d[l])
        ms5 = _s5(u, mats, s5_w_glu[l].T.astype(BF16), s5_b_glu[l].reshape(D_S5, 1),
                  s5_norm_g[l].reshape(D_S5, 1), w_out[l][:D_S5].astype(BF16))

        wg2 = jnp.concatenate([gla_w_g2[l], jnp.zeros((LANE - GLA_LOWRANK, GLA_QK), F32)], axis=0).astype(BF16)
        yg = _gla(q, k, v, glr, r, wg2, gla_b_g2[l].reshape(1, GLA_QK), gla_norm_g[l].reshape(1, GLA_DV))

        x1, h2p, idx_t, rank_t, gates, counts = _mix(
            ms5, yg, x, w_out[l][D_S5:].astype(BF16), gt1, g_norm_moe[l].reshape(1, d), sh2, sc2, gt2,
            w_router[l].T.astype(BF16), router_bias[l].reshape(N_EXPERTS, 1),
            sh_w_gate[l].astype(BF16), sh_w_up[l].astype(BF16), sh_w_down[l].astype(BF16))

        n_blocks = (T * TOP_K + N_EXPERTS * (MOE_BLOCK - 1) + MOE_BLOCK - 1) // MOE_BLOCK
        n_slots = n_blocks * MOE_BLOCK
        padded, pends, pstarts, n_valid = _block_table(counts[:, 0])
        pos_t = _pos(idx_t, rank_t, pstarts).reshape(-1)
        xs = _dispatch(pends, padded, pos_t, h2p.reshape(T, ROW_CHUNKS, LANE), n_slots)
        ye = _experts(pstarts // MOE_BLOCK, padded // MOE_BLOCK, n_valid,
                      xs.reshape(n_slots * ROW_CHUNKS, LANE), exp_w_gate[l], exp_w_up[l], exp_w_down[l])
    return _final(pos_t, gates, x1, gt2, g_final, ye.reshape(n_slots, ROW_CHUNKS, LANE))
```

```python
import functools

import jax
import jax.numpy as jnp
from jax import lax
from jax.experimental import pallas as pl
from jax.experimental.pallas import tpu as pltpu

F32 = jnp.float32
BF16 = jnp.bfloat16

D_MODEL = 1024
D_S5 = 512
S5_GROUP = 16
S5_GROUPS = 32
S5_STATE = 64
S5_CHUNK = 16
S5_STEP_GROUPS = 8
S5_LEVELS = 7
S5_APOW_COLS = 16
D_GLA = 512
GLA_HEADS = 4
GLA_DK = 64
GLA_DV = 128
GLA_QK = 256
GLA_LOWRANK = 16
GLA_TAU = 16.0
GLA_CHUNK = 64
LANE = 128
N_EXPERTS = 256
TOP_K = 8
N_GROUPS = 8
TOPK_GROUPS = 4
D_EXPERT = 256
ROUTED_SCALE = 2.5
EPS = 1e-6
MOE_BLOCK = 128
ROW_CHUNKS = D_MODEL // 2 // LANE
VMEM_LIMIT = 48 * 1024 * 1024


def _silu(x):
    return x * jax.nn.sigmoid(x)


def _params(*sem):
    return pltpu.CompilerParams(dimension_semantics=sem, vmem_limit_bytes=VMEM_LIMIT)


def _ada_kernel(c_ref, w_ref, b_ref, o_ref):
    s = _silu(c_ref[...]).astype(BF16)
    o_ref[...] = jnp.dot(s, w_ref[...].astype(BF16), preferred_element_type=F32) + b_ref[...]


def _ada(c, w, b):
    bsz, d = c.shape
    n = w.shape[1]
    tn = 1024
    return pl.pallas_call(
        _ada_kernel,
        grid=(n // tn,),
        in_specs=[pl.BlockSpec((bsz, d), lambda j: (0, 0)),
                  pl.BlockSpec((d, tn), lambda j: (0, j)),
                  pl.BlockSpec((1, tn), lambda j: (0, j))],
        out_specs=pl.BlockSpec((bsz, tn), lambda j: (0, j)),
        out_shape=jax.ShapeDtypeStruct((bsz, n), F32),
        compiler_params=_params("arbitrary"),
        name="ada",
    )(c, w, b.reshape(1, n))


def _inproj_kernel(x_ref, g_ref, sh_ref, sc_ref, w_ref,
                   u_ref, q_ref, k_ref, v_ref, r_ref, glr_ref):
    x = x_ref[0]
    ms = jnp.mean(x * x, axis=-1, keepdims=True)
    h = (x * lax.rsqrt(ms + EPS)) * g_ref[...]
    h = h * (1.0 + sc_ref[0]) + sh_ref[0]
    hb = h.astype(BF16)
    u = jnp.dot(hb, w_ref[:, 0:D_S5], preferred_element_type=F32)
    for c in range(D_S5 // LANE):
        u_ref[0, c] = u[:, c * LANE:(c + 1) * LANE]
    col = D_S5
    for ref in (q_ref, k_ref, v_ref, r_ref, glr_ref):
        n = ref.shape[-1]
        ref[0] = jnp.dot(hb, w_ref[:, col:col + n], preferred_element_type=F32).astype(ref.dtype)
        col += n


def _inproj(x, g, sh, sc, w):
    bsz, L, d = x.shape
    tt = 512
    widths = (GLA_QK, GLA_QK, D_GLA, D_GLA, LANE)
    tok = lambda n: pl.BlockSpec((1, tt, n), lambda b, i: (b, i, 0))
    vec = pl.BlockSpec((1, 1, d), lambda b, i: (b, 0, 0))
    return pl.pallas_call(
        _inproj_kernel,
        grid=(bsz, L // tt),
        in_specs=[tok(d), pl.BlockSpec((1, d), lambda b, i: (0, 0)), vec, vec,
                  pl.BlockSpec(w.shape, lambda b, i: (0, 0))],
        out_specs=[pl.BlockSpec((1, D_S5 // LANE, tt, LANE), lambda b, i: (b, 0, i, 0))] + [tok(n) for n in widths],
        out_shape=[jax.ShapeDtypeStruct((bsz, D_S5 // LANE, L, LANE), F32)]
        + [jax.ShapeDtypeStruct((bsz, L, n), BF16) for n in widths],
        compiler_params=_params("arbitrary", "arbitrary"),
        name="inproj",
    )(x, g.reshape(1, d), sh, sc, w)


def _s5_prep(lam_re, lam_im, log_dt, b_re, b_im, c_re, c_im, d_skip):
    G, N, C, TC = S5_GROUPS, S5_STATE, S5_GROUP, S5_CHUNK
    hp = lax.Precision.HIGHEST
    dt = jnp.exp(log_dt)[:, None]
    lr, li = lam_re, lam_im
    mag = jnp.exp(lr * dt)
    ab_re, ab_im = mag * jnp.cos(li * dt), mag * jnp.sin(li * dt)
    den = lr * lr + li * li
    nr = ab_re - 1.0
    coef_re = ((nr * lr + ab_im * li) / den)[..., None]
    coef_im = ((ab_im * lr - nr * li) / den)[..., None]
    bb_re = coef_re * b_re - coef_im * b_im
    bb_im = coef_re * b_im + coef_im * b_re
    p = jnp.arange(TC + 1, dtype=F32)[:, None, None]
    pm = jnp.exp(lr * dt * p)
    pr, pi = pm * jnp.cos(li * dt * p), pm * jnp.sin(li * dt * p)
    ca_re = c_re[None] * pr[:, :, None, :] - c_im[None] * pi[:, :, None, :]
    ca_im = c_re[None] * pi[:, :, None, :] + c_im[None] * pr[:, :, None, :]
    ca = jnp.concatenate([ca_re[:TC], -ca_im[:TC]], axis=-1).transpose(1, 0, 2, 3).reshape(G, TC * C, 2 * N)
    kern = jnp.einsum('gxk,gki->gxi', ca, jnp.concatenate([bb_re, bb_im], axis=1),
                      precision=hp).reshape(G, TC, C, C)
    skip = jnp.eye(C, dtype=F32)[None] * d_skip[:, :, None]
    kern = kern + jnp.concatenate([skip[:, None], jnp.zeros((G, TC - 1, C, C), F32)], axis=1)
    place = (jnp.arange(TC)[None, :, None] - jnp.arange(TC)[None, None, :]
             == jnp.arange(TC)[:, None, None]).astype(F32)
    toep_t = jnp.einsum('xts,gxoi->gtosi', place, kern, precision=hp).reshape(G, TC * C, TC * C)
    rr, ri = pr[TC - 1 - jnp.arange(TC)], pi[TC - 1 - jnp.arange(TC)]
    binc_re = rr[..., None] * bb_re[None] - ri[..., None] * bb_im[None]
    binc_im = rr[..., None] * bb_im[None] + ri[..., None] * bb_re[None]
    binc_re_t = binc_re.transpose(1, 2, 0, 3).reshape(G, N, TC * C)
    binc_im_t = binc_im.transpose(1, 2, 0, 3).reshape(G, N, TC * C)
    cm_re_t = ca_re[1:].transpose(1, 0, 2, 3).reshape(G, TC * C, N)
    cm_im_t = (-ca_im[1:]).transpose(1, 0, 2, 3).reshape(G, TC * C, N)
    q = (TC * 2.0 ** jnp.arange(S5_LEVELS, dtype=F32))[:, None, None]
    qm = jnp.exp(lr * dt * q)
    qr, qi = qm * jnp.cos(li * dt * q), qm * jnp.sin(li * dt * q)
    apow = jnp.stack([qr, qi], axis=1).reshape(2 * S5_LEVELS, G, N).transpose(1, 2, 0)
    apow = jnp.concatenate([apow, jnp.zeros((G, N, S5_APOW_COLS - 2 * S5_LEVELS), F32)], axis=-1)
    return (toep_t.astype(BF16), binc_re_t.astype(BF16), binc_im_t.astype(BF16),
            cm_re_t.astype(BF16), cm_im_t.astype(BF16), apow)


def _s5_kernel(u_ref, eye_ref, toep_ref, bre_ref, bim_ref, cre_ref, cim_ref, apow_ref,
               wglu_ref, bglu_ref, sng_ref, wo_ref, o_ref, ut_ref, yt_ref):
    j = pl.program_id(1)
    n_chunks = u_ref.shape[2] // S5_CHUNK
    nt = (((1,), (1,)), ((), ()))

    @pl.when(j == 0)
    def _():
        for s in range(S5_CHUNK):
            us = jnp.concatenate([u_ref[0, c, pl.ds(s, n_chunks, stride=S5_CHUNK), :]
                                  for c in range(D_S5 // LANE)], axis=1).astype(BF16)
            ut_ref[:, s * n_chunks:(s + 1) * n_chunks] = lax.dot_general(
                eye_ref[...], us, nt, preferred_element_type=F32).astype(BF16)

    lane = lax.broadcasted_iota(jnp.int32, (S5_STATE, n_chunks), 1)
    groups = range(S5_STEP_GROUPS)
    rows = [pl.ds(pl.multiple_of(j * (S5_STEP_GROUPS * S5_GROUP) + gl * S5_GROUP, S5_GROUP), S5_GROUP)
            for gl in groups]
    zt = [jnp.concatenate([ut_ref[rows[gl], s * n_chunks:(s + 1) * n_chunks] for s in range(S5_CHUNK)], axis=0)
          for gl in groups]
    xr = [jnp.where(lane >= 1, pltpu.roll(jnp.dot(bre_ref[gl], zt[gl], preferred_element_type=F32), 1, axis=1), 0.0)
          for gl in groups]
    xi = [jnp.where(lane >= 1, pltpu.roll(jnp.dot(bim_ref[gl], zt[gl], preferred_element_type=F32), 1, axis=1), 0.0)
          for gl in groups]
    for lv in range(S5_LEVELS):
        d = 1 << lv
        for gl in groups:
            ar = apow_ref[gl, :, 2 * lv:2 * lv + 1]
            ai = apow_ref[gl, :, 2 * lv + 1:2 * lv + 2]
            sr = jnp.where(lane >= d, pltpu.roll(xr[gl], d, axis=1), 0.0)
            si = jnp.where(lane >= d, pltpu.roll(xi[gl], d, axis=1), 0.0)
            xr[gl], xi[gl] = xr[gl] + ar * sr - ai * si, xi[gl] + ar * si + ai * sr
    for gl in groups:
        yt = (jnp.dot(toep_ref[gl], zt[gl], preferred_element_type=F32)
              + jnp.dot(cre_ref[gl], xr[gl].astype(BF16), preferred_element_type=F32)
              + jnp.dot(cim_ref[gl], xi[gl].astype(BF16), preferred_element_type=F32))
        for t in range(S5_CHUNK):
            yt_ref[rows[gl], t * n_chunks:(t + 1) * n_chunks] = yt[t * S5_GROUP:(t + 1) * S5_GROUP, :]

    @pl.when(j == pl.num_programs(1) - 1)
    def _():
        cw = 2 * n_chunks
        for cc in range(yt_ref.shape[1] // cw):
            y = yt_ref[:, cc * cw:(cc + 1) * cw]
            g = y * (0.5 * (1.0 + jnp.tanh(0.7978845608028654 * (y + 0.044715 * (y * y * y)))))
            z = jnp.dot(wglu_ref[...], g.astype(BF16), preferred_element_type=F32) + bglu_ref[...]
            out = g * jax.nn.sigmoid(z)
            out = out * lax.rsqrt(jnp.mean(out * out, axis=0, keepdims=True) + EPS) * sng_ref[...]
            mixc = jnp.dot(out.T.astype(BF16), wo_ref[...], preferred_element_type=F32)
            for sl in range(cw // n_chunks):
                s = cc * (cw // n_chunks) + sl
                for c in range(D_MODEL // LANE):
                    o_ref[0, c, pl.ds(s, n_chunks, stride=S5_CHUNK), :] = mixc[sl * n_chunks:(sl + 1) * n_chunks,
                                                                               c * LANE:(c + 1) * LANE]


def _s5(u, mats, wglu_t, bglu, sng, wo_s5):
    bsz, _, L, _ = u.shape
    assert L // S5_CHUNK == LANE, "one lane tile of chunks per sequence"
    toep, bre, bim, cre, cim, apow = mats
    sg = S5_STEP_GROUPS
    eye = jnp.eye(D_S5, dtype=BF16)
    grp = lambda a, b: pl.BlockSpec((sg, a, b), lambda bi, j: (j, 0, 0))
    full = lambda a: pl.BlockSpec(a.shape, lambda bi, j: (0,) * a.ndim)
    w = S5_CHUNK * S5_GROUP
    return pl.pallas_call(
        _s5_kernel,
        grid=(bsz, S5_GROUPS // sg),
        in_specs=[pl.BlockSpec((1, D_S5 // LANE, L, LANE), lambda bi, j: (bi, 0, 0, 0)), full(eye),
                  grp(w, w), grp(S5_STATE, w), grp(S5_STATE, w), grp(w, S5_STATE), grp(w, S5_STATE),
                  grp(S5_STATE, S5_APOW_COLS), full(wglu_t), full(bglu), full(sng), full(wo_s5)],
        out_specs=pl.BlockSpec((1, D_MODEL // LANE, L, LANE), lambda bi, j: (bi, 0, 0, 0)),
        out_shape=jax.ShapeDtypeStruct((bsz, D_MODEL // LANE, L, LANE), F32),
        scratch_shapes=[pltpu.VMEM((D_S5, L), BF16), pltpu.VMEM((D_S5, L), F32)],
        compiler_params=_params("arbitrary", "arbitrary"),
        name="s5",
    )(u, eye, toep, bre, bim, cre, cim, apow, wglu_t, bglu, sng, wo_s5)


def _gla_kernel(q_ref, k_ref, v_ref, glr_ref, r_ref, wg_ref, bg_ref, ng_ref, o_ref, st_ref):
    lt = q_ref.shape[1]
    C = GLA_CHUNK

    @pl.when(pl.program_id(1) == 0)
    def _():
        st_ref[...] = jnp.zeros_like(st_ref)

    z = jnp.dot(glr_ref[0], wg_ref[...], preferred_element_type=F32) + bg_ref[...]
    log_a = (jnp.minimum(z, 0.0) - jnp.log(1.0 + jnp.exp(-jnp.abs(z)))) * (1.0 / GLA_TAU)
    ri = lax.broadcasted_iota(jnp.int32, (lt, lt), 0)
    ci = lax.broadcasted_iota(jnp.int32, (lt, lt), 1)
    tril = jnp.where(((ri >> 6) == (ci >> 6)) & (ci <= ri), 1.0, 0.0).astype(BF16)
    la_hi = log_a.astype(BF16)
    la_lo = (log_a - la_hi.astype(F32)).astype(BF16)
    bcum = (jnp.dot(tril, la_hi, preferred_element_type=F32)
            + jnp.dot(tril, la_lo, preferred_element_type=F32))
    q = q_ref[0].astype(F32) * (GLA_DK ** -0.5)
    k = k_ref[0].astype(F32)
    qi = q * jnp.exp(bcum)
    ki = k * jnp.exp(-bcum)
    lane_head = lax.broadcasted_iota(jnp.int32, (1, GLA_QK), 1) >> 6
    causal = ((lax.broadcasted_iota(jnp.int32, (GLA_HEADS * C, C), 0) & (C - 1))
              >= lax.broadcasted_iota(jnp.int32, (GLA_HEADS * C, C), 1))
    same_head = ((lax.broadcasted_iota(jnp.int32, (D_GLA, GLA_QK), 0) >> 7)
                 == (lax.broadcasted_iota(jnp.int32, (D_GLA, GLA_QK), 1) >> 6))
    nt = (((1,), (1,)), ((), ()))
    ng = ng_ref[...]
    for c in range(lt // C):
        sl = slice(c * C, (c + 1) * C)
        bc = bcum[sl]
        bl = bc[C - 1:C, :]
        kd = k[sl] * jnp.exp(bl - bc)
        qic = qi[sl]
        qs = jnp.concatenate([jnp.where(lane_head == h, qic, 0.0) for h in range(GLA_HEADS)],
                             axis=0).astype(BF16)
        sc = lax.dot_general(qs, ki[sl].astype(BF16), nt, preferred_element_type=F32)
        p = jnp.where(causal, sc, 0.0).astype(BF16)
        vc = v_ref[0, sl, :]
        o_intra = jnp.concatenate(
            [jnp.dot(p[h * C:(h + 1) * C], vc[:, h * GLA_DV:(h + 1) * GLA_DV], preferred_element_type=F32)
             for h in range(GLA_HEADS)], axis=1)
        st = st_ref[...]
        o_inter = lax.dot_general(qic.astype(BF16), st.astype(BF16), nt, preferred_element_type=F32)
        v_t = vc.astype(F32).T.astype(BF16)
        kv_t = jnp.dot(v_t, kd.astype(BF16), preferred_element_type=F32)
        st_ref[...] = st * jnp.exp(bl) + jnp.where(same_head, kv_t, 0.0)
        o = o_intra + o_inter
        parts = []
        for h in range(GLA_HEADS):
            oh = o[:, h * GLA_DV:(h + 1) * GLA_DV]
            oh = oh * lax.rsqrt(jnp.mean(oh * oh, axis=-1, keepdims=True) + EPS)
            parts.append(oh * ng)
        r = r_ref[0, sl, :].astype(F32)
        o_ref[0, sl, :] = (jnp.concatenate(parts, axis=1) * _silu(r)).astype(o_ref.dtype)


def _gla(q, k, v, glr, r, wg, bg, ng):
    bsz, L, _ = q.shape
    lt = 256
    tok = lambda n: pl.BlockSpec((1, lt, n), lambda b, i: (b, i, 0))
    full = lambda a: pl.BlockSpec(a.shape, lambda b, i: (0,) * a.ndim)
    return pl.pallas_call(
        _gla_kernel,
        grid=(bsz, L // lt),
        in_specs=[tok(GLA_QK), tok(GLA_QK), tok(D_GLA), tok(LANE), tok(D_GLA), full(wg), full(bg), full(ng)],
        out_specs=tok(D_GLA),
        out_shape=jax.ShapeDtypeStruct((bsz, L, D_GLA), BF16),
        scratch_shapes=[pltpu.VMEM((D_GLA, GLA_QK), F32)],
        compiler_params=_params("arbitrary", "arbitrary"),
        name="gla",
    )(q, k, v, glr, r, wg, bg, ng)


def _pack_bf16_pairs(x):
    w = x.shape[1] // 2
    xr = x.astype(BF16).astype(F32)
    lo = lax.bitcast_convert_type(xr[:, :w], jnp.uint32) >> 16
    hi = lax.bitcast_convert_type(xr[:, w:], jnp.uint32) & jnp.uint32(0xFFFF0000)
    return lo | hi


def _unpack_bf16_pairs(p):
    lo = lax.bitcast_convert_type(p << 16, F32)
    hi = lax.bitcast_convert_type(p & jnp.uint32(0xFFFF0000), F32)
    return lo, hi


def _store_rows(ref, packed):
    n = packed.shape[0]
    for c in range(ROW_CHUNKS):
        ref[pl.ds(c, n, stride=ROW_CHUNKS), :] = packed[:, c * LANE:(c + 1) * LANE]


def _load_rows(ref, n):
    return jnp.concatenate([ref[pl.ds(c, n, stride=ROW_CHUNKS), :] for c in range(ROW_CHUNKS)], axis=1)


def _route_tile(lg_t, bias_col, tri, carry_ref):
    n_e, tt = lg_t.shape
    per_group = n_e // N_GROUPS
    scores = jax.nn.sigmoid(lg_t)
    biased = scores + bias_col
    row = lax.broadcasted_iota(jnp.int32, (n_e, tt), 0)
    neg = -jnp.inf
    group_score = []
    for g in range(N_GROUPS):
        b = biased[g * per_group:(g + 1) * per_group]
        r = lax.broadcasted_iota(jnp.int32, (per_group, tt), 0) + g * per_group
        m1 = jnp.max(b, axis=0, keepdims=True)
        i1 = jnp.min(jnp.where(b == m1, r, n_e), axis=0, keepdims=True)
        m2 = jnp.max(jnp.where(r == i1, neg, b), axis=0, keepdims=True)
        group_score.append(m1 + m2)
    parts = []
    for g in range(N_GROUPS):
        ahead = jnp.zeros((1, tt), jnp.int32)
        for g2 in range(N_GROUPS):
            if g2 != g:
                beats = (group_score[g2] >= group_score[g]) if g2 < g else (group_score[g2] > group_score[g])
                ahead = ahead + beats.astype(jnp.int32)
        parts.append(jnp.where(ahead < TOPK_GROUPS, biased[g * per_group:(g + 1) * per_group], neg))
    masked = jnp.concatenate(parts, axis=0)
    work = masked
    idxs = []
    for _ in range(TOP_K):
        m = jnp.max(work, axis=0, keepdims=True)
        ii = jnp.min(jnp.where(work == m, row, n_e), axis=0, keepdims=True)
        idxs.append(ii)
        work = jnp.where(row == ii, neg, work)
    sel = work != masked
    w = jnp.where(sel, scores, 0.0)
    gate_dense = w / jnp.sum(w, axis=0, keepdims=True) * ROUTED_SCALE
    mt = jnp.where(sel, 1.0, 0.0)
    rank_dense = jnp.dot(mt.astype(BF16), tri, preferred_element_type=F32) + carry_ref[...]
    carry_ref[...] += jnp.sum(mt, axis=1, keepdims=True)
    ranks, gts = [], []
    for ii in idxs:
        oh = row == ii
        ranks.append(jnp.sum(jnp.where(oh, rank_dense, 0.0), axis=0, keepdims=True))
        gts.append(jnp.sum(jnp.where(oh, gate_dense, 0.0), axis=0, keepdims=True))
    idx_t = jnp.concatenate(idxs, axis=0)
    rank_t = jnp.concatenate(ranks, axis=0).astype(jnp.int32)
    gate_t = jnp.concatenate(gts + [jnp.zeros((LANE - TOP_K, tt), F32)], axis=0)
    return idx_t, rank_t, gate_t.T[:, :TOP_K]


def _mix_kernel(ms_ref, yg_ref, x_ref, wo_ref, gt1_ref,
                gn_ref, sh2_ref, sc2_ref, gt2_ref, wrt_ref, rb_ref, tri_ref, wsg_ref, wsu_ref, wsd_ref,
                x1_ref, h2_ref, idx_ref, rank_ref, gate_ref, cnt_ref, carry_ref):
    @pl.when((pl.program_id(0) == 0) & (pl.program_id(1) == 0))
    def _():
        carry_ref[...] = jnp.zeros_like(carry_ref)

    mix_s5 = jnp.concatenate([ms_ref[0, c] for c in range(D_MODEL // LANE)], axis=1)
    mix = mix_s5 + jnp.dot(yg_ref[0], wo_ref[...], preferred_element_type=F32)
    x1 = x_ref[0] + gt1_ref[0] * mix
    h2 = x1 * lax.rsqrt(jnp.mean(x1 * x1, axis=-1, keepdims=True) + EPS) * gn_ref[...]
    h2 = h2 * (1.0 + sc2_ref[0]) + sh2_ref[0]
    hb = h2.astype(BF16)
    _store_rows(h2_ref, _pack_bf16_pairs(h2))
    lg_t = lax.dot_general(wrt_ref[...], hb, (((1,), (1,)), ((), ())), preferred_element_type=F32)
    idx_t, rank_t, gates = _route_tile(lg_t, rb_ref[...], tri_ref[...], carry_ref)
    idx_ref[...] = idx_t
    rank_ref[...] = rank_t
    gate_ref[...] = gates
    cnt_ref[...] = carry_ref[...]
    a = _silu(jnp.dot(hb, wsg_ref[...], preferred_element_type=F32)) * jnp.dot(
        hb, wsu_ref[...], preferred_element_type=F32)
    shared = jnp.dot(a.astype(BF16), wsd_ref[...], preferred_element_type=F32)
    x1_ref[0] = x1 + gt2_ref[0] * shared


def _mix(ms5, yg, x, wo, gt1, gn, sh2, sc2, gt2, wrt, rb, wsg, wsu, wsd):
    bsz, L, d = x.shape
    tt = 512
    nt = L // tt
    T = bsz * L
    tri = (jnp.arange(tt)[:, None] < jnp.arange(tt)[None, :]).astype(BF16)
    tok = lambda n: pl.BlockSpec((1, tt, n), lambda b, i: (b, i, 0))
    vec = pl.BlockSpec((1, 1, d), lambda b, i: (b, 0, 0))
    full = lambda a: pl.BlockSpec(a.shape, lambda b, i: (0,) * a.ndim)
    lanes = pl.BlockSpec((TOP_K, tt), lambda b, i: (0, b * nt + i))
    return pl.pallas_call(
        _mix_kernel,
        grid=(bsz, nt),
        in_specs=[pl.BlockSpec((1, d // LANE, tt, LANE), lambda b, i: (b, 0, i, 0)), tok(D_GLA), tok(d), full(wo), vec,
                  full(gn), vec, vec, vec, full(wrt), full(rb), full(tri), full(wsg), full(wsu), full(wsd)],
        out_specs=[tok(d),
                   pl.BlockSpec((tt * ROW_CHUNKS, LANE), lambda b, i: (b * nt + i, 0)),
                   lanes, lanes,
                   pl.BlockSpec((tt, TOP_K), lambda b, i: (b * nt + i, 0)),
                   pl.BlockSpec((N_EXPERTS, 1), lambda b, i: (0, 0))],
        out_shape=[jax.ShapeDtypeStruct((bsz, L, d), F32),
                   jax.ShapeDtypeStruct((T * ROW_CHUNKS, LANE), jnp.uint32),
                   jax.ShapeDtypeStruct((TOP_K, T), jnp.int32),
                   jax.ShapeDtypeStruct((TOP_K, T), jnp.int32),
                   jax.ShapeDtypeStruct((T, TOP_K), F32),
                   jax.ShapeDtypeStruct((N_EXPERTS, 1), F32)],
        scratch_shapes=[pltpu.VMEM((N_EXPERTS, 1), F32)],
        compiler_params=_params("arbitrary", "arbitrary"),
        name="mix",
    )(ms5, yg, x, wo, gt1, gn, sh2, sc2, gt2, wrt, rb, tri, wsg, wsu, wsd)


def _pos_kernel(idx_ref, rank_ref, ps_ref, pos_ref):
    n_e = ps_ref.shape[0]
    tt = idx_ref.shape[1]
    row = lax.broadcasted_iota(jnp.int32, (n_e, tt), 0)
    ps = ps_ref[...]
    starts = [jnp.sum(jnp.where(row == idx_ref[k:k + 1, :], ps, 0.0), axis=0, keepdims=True)
              for k in range(TOP_K)]
    pos = jnp.concatenate(starts, axis=0).astype(jnp.int32) + rank_ref[...]
    for jh in range(tt // LANE):
        pos_ref[jh * TOP_K:(jh + 1) * TOP_K, :] = pos[:, jh * LANE:(jh + 1) * LANE]


def _pos(idx_t, rank_t, pstart):
    T = idx_t.shape[1]
    tt = 2048
    blk = pl.BlockSpec((TOP_K, tt), lambda i: (0, i))
    return pl.pallas_call(
        _pos_kernel,
        grid=(T // tt,),
        in_specs=[blk, blk, pl.BlockSpec((N_EXPERTS, 1), lambda i: (0, 0))],
        out_specs=pl.BlockSpec((tt // LANE * TOP_K, LANE), lambda i: (i, 0)),
        out_shape=jax.ShapeDtypeStruct((T // LANE * TOP_K, LANE), jnp.int32),
        compiler_params=_params("arbitrary"),
        name="pos",
    )(idx_t, rank_t, pstart.astype(F32).reshape(N_EXPERTS, 1))


def _dispatch_kernel(pend_ref, padded_ref, pos_ref, h_ref, xs_hbm, zero_ref, pos_smem, sem_pos, sem_zero, sem_row):
    tt = h_ref.shape[0]
    step = pl.program_id(0)
    load_pos = pltpu.make_async_copy(pos_ref, pos_smem, sem_pos)
    load_pos.start()

    def zero_block(start):
        return pltpu.make_async_copy(zero_ref, xs_hbm.at[pl.ds(pl.multiple_of(start, MOE_BLOCK), MOE_BLOCK)],
                                     sem_zero)

    def for_each_zero_block(fn):
        def per_expert(e, c):
            @pl.when(padded_ref[e] > 0)
            def _():
                fn(zero_block(pend_ref[e] - MOE_BLOCK))
            return c

        def per_tail_block(g, c):
            fn(zero_block(g * MOE_BLOCK))
            return c

        lax.fori_loop(0, N_EXPERTS, per_expert, 0)
        lax.fori_loop(pend_ref[N_EXPERTS - 1] // MOE_BLOCK, xs_hbm.shape[0] // MOE_BLOCK, per_tail_block, 0)

    @pl.when(step == 0)
    def _():
        zero_ref[...] = jnp.zeros_like(zero_ref)
        for_each_zero_block(lambda cp: cp.start())
        for_each_zero_block(lambda cp: cp.wait())

    load_pos.wait()
    for jh in range(tt // LANE):
        def issue(jl, c, jh=jh):
            for k in range(TOP_K):
                slot = pos_smem[(jh * TOP_K + k) * LANE + jl]
                pltpu.make_async_copy(h_ref.at[jh * LANE + jl], xs_hbm.at[slot], sem_row).start(priority=k % 2)
            return c

        lax.fori_loop(0, LANE, issue, 0, unroll=4)
    for k in range(TOP_K):
        pltpu.make_async_copy(h_ref, xs_hbm.at[pl.ds(0, tt)], sem_row).wait()


def _dispatch(pends, padded, pos_t, h_rows, n_slots):
    T = h_rows.shape[0]
    tt = 512
    return pl.pallas_call(
        _dispatch_kernel,
        grid_spec=pltpu.PrefetchScalarGridSpec(
            num_scalar_prefetch=2,
            grid=(T // tt,),
            in_specs=[pl.BlockSpec((tt * TOP_K,), lambda i, pe, pa: (i,)),
                      pl.BlockSpec((tt, ROW_CHUNKS, LANE), lambda i, pe, pa: (i, 0, 0))],
            out_specs=pl.BlockSpec(memory_space=pl.ANY),
            scratch_shapes=[pltpu.VMEM((MOE_BLOCK, ROW_CHUNKS, LANE), jnp.uint32),
                            pltpu.SMEM((tt * TOP_K,), jnp.int32),
                            pltpu.SemaphoreType.DMA, pltpu.SemaphoreType.DMA, pltpu.SemaphoreType.DMA]),
        out_shape=jax.ShapeDtypeStruct((n_slots, ROW_CHUNKS, LANE), jnp.uint32),
        compiler_params=_params("arbitrary"),
        name="dispatch",
    )(pends, padded, pos_t, h_rows)


X_GROUP = 8
X_AHEAD = 8
X_BUFS = X_AHEAD + X_GROUP
Y_BUFS = X_GROUP


def _expert_kernel(b0_ref, nb_ref, nv_ref, wg_ref, wu_ref, wd_ref, xs_hbm, ye_hbm,
                   wg_s, wu_s, wd_s, xbuf, ybuf, semx, semy, *, n_blocks):
    e = pl.program_id(0)
    nb = nb_ref[e]
    b0 = b0_ref[e]
    nv = nv_ref[0]
    rows = MOE_BLOCK * ROW_CHUNKS

    def block(ref, g):
        return ref.at[pl.ds(pl.multiple_of(g * rows, rows), rows)]

    def x_copy(g):
        slot = g % X_BUFS
        return pltpu.make_async_copy(block(xs_hbm, g), xbuf.at[slot], semx.at[slot])

    def y_copy(g):
        slot = g % Y_BUFS
        return pltpu.make_async_copy(ybuf.at[slot], block(ye_hbm, g), semy.at[slot])

    @pl.when(e == 0)
    def _():
        for g in range(X_AHEAD):
            x_copy(g).start()

    @pl.when(nb > 0)
    def _():
        wg_s[...] = wg_ref[0].astype(BF16)
        wu_s[...] = wu_ref[0].astype(BF16)
        wd_s[...] = wd_ref[0].astype(BF16)

    def fetch(g):
        x_copy(g).wait()

        @pl.when(g + X_AHEAD < nv)
        def _():
            x_copy(g + X_AHEAD).start()

    def load_x(g):
        lo, hi = _unpack_bf16_pairs(_load_rows(xbuf.at[g % X_BUFS], MOE_BLOCK))
        return jnp.concatenate([lo, hi], axis=1).astype(BF16)

    def emit(g, y):
        @pl.when(g >= Y_BUFS)
        def _():
            y_copy(g - Y_BUFS).wait()

        _store_rows(ybuf.at[g % Y_BUFS], y)
        y_copy(g).start()

    def run(g, n):
        for b in range(n):
            fetch(g + b)
        x = jnp.concatenate([load_x(g + b) for b in range(n)], axis=0) if n > 1 else load_x(g)
        a = jnp.dot(x, wg_s[...], preferred_element_type=F32)
        u = jnp.dot(x, wu_s[...], preferred_element_type=F32)
        h = (_silu(a) * u).astype(BF16)
        y = _pack_bf16_pairs(jnp.dot(h, wd_s[...], preferred_element_type=F32))
        for b in range(n):
            emit(g + b, y[b * MOE_BLOCK:(b + 1) * MOE_BLOCK])

    def group(i, c):
        run(b0 + X_GROUP * i, X_GROUP)
        return c

    lax.fori_loop(0, nb // X_GROUP, group, 0)
    done = nb // X_GROUP * X_GROUP
    n = X_GROUP // 2
    while n >= 1:
        @pl.when((nb & n) != 0)
        def _(n=n, done=done):
            run(b0 + done, n)

        done = done + (nb & n)
        n //= 2

    @pl.when(e == pl.num_programs(0) - 1)
    def _():
        for back in range(Y_BUFS, 0, -1):
            y_copy(nv - back).wait()
        ybuf[0] = jnp.zeros(ybuf.shape[1:], ybuf.dtype)

        def fill(g):
            return pltpu.make_async_copy(ybuf.at[0], block(ye_hbm, g), semy.at[0])

        lax.fori_loop(nv, n_blocks, lambda g, c: (fill(g).start(), c)[1], 0)
        lax.fori_loop(nv, n_blocks, lambda g, c: (fill(g).wait(), c)[1], 0)


def _experts(first_block, num_blocks, n_valid, xs, wg, wu, wd):
    rows = MOE_BLOCK * ROW_CHUNKS
    n_blocks = xs.shape[0] // rows
    n_e, d, _ = wg.shape
    wsel = lambda e, b0, nb, nv: (e, 0, 0)
    return pl.pallas_call(
        functools.partial(_expert_kernel, n_blocks=n_blocks),
        grid_spec=pltpu.PrefetchScalarGridSpec(
            num_scalar_prefetch=3,
            grid=(n_e,),
            in_specs=[pl.BlockSpec((1, d, D_EXPERT), wsel),
                      pl.BlockSpec((1, d, D_EXPERT), wsel),
                      pl.BlockSpec((1, D_EXPERT, d), wsel),
                      pl.BlockSpec(memory_space=pl.ANY)],
            out_specs=pl.BlockSpec(memory_space=pl.ANY),
            scratch_shapes=[pltpu.VMEM((d, D_EXPERT), BF16), pltpu.VMEM((d, D_EXPERT), BF16),
                            pltpu.VMEM((D_EXPERT, d), BF16),
                            pltpu.VMEM((X_BUFS, rows, LANE), jnp.uint32),
                            pltpu.VMEM((Y_BUFS, rows, LANE), jnp.uint32),
                            pltpu.SemaphoreType.DMA((X_BUFS,)), pltpu.SemaphoreType.DMA((Y_BUFS,))]),
        out_shape=jax.ShapeDtypeStruct(xs.shape, jnp.uint32),
        compiler_params=_params("arbitrary"),
        name="experts",
    )(first_block, num_blocks, n_valid, wg, wu, wd, xs)


def _final_kernel(pos0_ref, posn_ref, gate_ref, x_ref, gt_ref, g_ref, ye_hbm, o_ref,
                  buf_ref, pos_smem, sem_pos, sem_row):
    tt = x_ref.shape[1]
    i = pl.program_id(0)

    def gather_tile(pos_ref, tile):
        base = (tile % 2) * (TOP_K * tt)
        sem = sem_row.at[tile % 2]
        load_pos = pltpu.make_async_copy(pos_ref, pos_smem, sem_pos)
        load_pos.start()
        load_pos.wait()
        for jh in range(tt // LANE):
            def issue(jl, c, jh=jh):
                for k in range(TOP_K):
                    slot = pos_smem[(jh * TOP_K + k) * LANE + jl]
                    pltpu.make_async_copy(ye_hbm.at[slot], buf_ref.at[base + k * tt + jh * LANE + jl],
                                          sem).start(priority=k % 2)
                return c

            lax.fori_loop(0, LANE, issue, 0, unroll=4)

    @pl.when(i == 0)
    def _():
        gather_tile(pos0_ref, i)

    @pl.when(i + 1 < pl.num_programs(0))
    def _():
        gather_tile(posn_ref, i + 1)

    base = pl.multiple_of((i % 2) * (TOP_K * tt), TOP_K * tt)
    for k in range(TOP_K):
        pltpu.make_async_copy(ye_hbm.at[pl.ds(0, tt)], buf_ref.at[pl.ds(base + k * tt, tt)],
                              sem_row.at[i % 2]).wait()
    gates = gate_ref[...]
    half = D_MODEL // 2
    acc_lo = jnp.zeros((tt, half), F32)
    acc_hi = jnp.zeros((tt, half), F32)
    rows_2d = buf_ref.reshape(buf_ref.shape[0] * ROW_CHUNKS, LANE)
    for k in range(TOP_K):
        lo, hi = _unpack_bf16_pairs(_load_rows(rows_2d.at[pl.ds((base + k * tt) * ROW_CHUNKS, tt * ROW_CHUNKS)], tt))
        gk = gates[:, k:k + 1]
        acc_lo += gk * lo
        acc_hi += gk * hi
    x = x_ref[0] + gt_ref[0] * jnp.concatenate([acc_lo, acc_hi], axis=1)
    o_ref[0] = x * lax.rsqrt(jnp.mean(x * x, axis=-1, keepdims=True) + EPS) * g_ref[...]


def _final(pos_t, gates, x1, gt2, g, ye_rows):
    bsz, L, d = x1.shape
    tt = 256
    nt = L // tt
    n = bsz * nt
    tok = pl.BlockSpec((1, tt, d), lambda i: (i // nt, i % nt, 0))
    return pl.pallas_call(
        _final_kernel,
        grid=(n,),
        in_specs=[pl.BlockSpec((tt * TOP_K,), lambda i: (0,)),
                  pl.BlockSpec((tt * TOP_K,), lambda i: (jnp.minimum(i + 1, n - 1),)),
                  pl.BlockSpec((tt, TOP_K), lambda i: (i, 0)),
                  tok, pl.BlockSpec((1, 1, d), lambda i: (i // nt, 0, 0)),
                  pl.BlockSpec((1, d), lambda i: (0, 0)),
                  pl.BlockSpec(memory_space=pl.ANY)],
        out_specs=tok,
        out_shape=jax.ShapeDtypeStruct((bsz, L, d), F32),
        scratch_shapes=[pltpu.VMEM((2 * TOP_K * tt, ROW_CHUNKS, LANE), jnp.uint32),
                        pltpu.SMEM((tt * TOP_K,), jnp.int32),
                        pltpu.SemaphoreType.DMA, pltpu.SemaphoreType.DMA((2,))],
        compiler_params=_params("arbitrary"),
        name="final",
    )(pos_t, pos_t, gates, x1, gt2, g.reshape(1, d), ye_rows)


def _block_table(counts):
    counts = counts.astype(jnp.int32)
    padded = (counts + MOE_BLOCK - 1) // MOE_BLOCK * MOE_BLOCK
    pends = jnp.cumsum(padded)
    pstarts = pends - padded
    n_valid = (pends[-1] // MOE_BLOCK).reshape(1)
    return padded, pends, pstarts, n_valid


def kernel(x, c, w_ada, b_ada, g_norm_mix, w_in, s5_lambda_re, s5_lambda_im, s5_log_dt, s5_b_re, s5_b_im, s5_c_re, s5_c_im, s5_d, s5_w_glu, s5_b_glu, s5_norm_g, gla_w_g2, gla_b_g2, gla_norm_g, w_out, g_norm_moe, w_router, router_bias, exp_w_gate, exp_w_up, exp_w_down, sh_w_gate, sh_w_up, sh_w_down, g_final):
    bsz, L, d = x.shape
    T = bsz * L
    assert w_ada.shape[0] == 1, "single-layer block"
    for l in range(1):
        mod = _ada(c, w_ada[l], b_ada[l])
        sh1, sc1, gt1, sh2, sc2, gt2 = [m.reshape(bsz, 1, d) for m in jnp.split(mod, 6, axis=-1)]

        wi = w_in[l]
        o_q, o_k, o_v, o_g, o_r = D_S5, D_S5 + GLA_QK, D_S5 + 2 * GLA_QK, D_S5 + 2 * GLA_QK + D_GLA, \
            D_S5 + 2 * GLA_QK + D_GLA + GLA_LOWRANK
        w_cat = jnp.concatenate([wi[:, :o_g], wi[:, o_r:], wi[:, o_g:o_r],
                                 jnp.zeros((d, LANE - GLA_LOWRANK), wi.dtype)], axis=1).astype(BF16)
        u, q, k, v, r, glr = _inproj(x, g_norm_mix[l], sh1, sc1, w_cat)

        mats = _s5_prep(s5_lambda_re[l], s5_lambda_im[l], s5_log_dt[l], s5_b_re[l], s5_b_im[l],
                        s5_c_re[l], s5_c_im[l], s5_d[l])
        ms5 = _s5(u, mats, s5_w_glu[l].T.astype(BF16), s5_b_glu[l].reshape(D_S5, 1),
                  s5_norm_g[l].reshape(D_S5, 1), w_out[l][:D_S5].astype(BF16))

        wg2 = jnp.concatenate([gla_w_g2[l], jnp.zeros((LANE - GLA_LOWRANK, GLA_QK), F32)], axis=0).astype(BF16)
        yg = _gla(q, k, v, glr, r, wg2, gla_b_g2[l].reshape(1, GLA_QK), gla_norm_g[l].reshape(1, GLA_DV))

        x1, h2p, idx_t, rank_t, gates, counts = _mix(
            ms5, yg, x, w_out[l][D_S5:].astype(BF16), gt1, g_norm_moe[l].reshape(1, d), sh2, sc2, gt2,
            w_router[l].T.astype(BF16), router_bias[l].reshape(N_EXPERTS, 1),
            sh_w_gate[l].astype(BF16), sh_w_up[l].astype(BF16), sh_w_down[l].astype(BF16))

        n_blocks = (T * TOP_K + N_EXPERTS * (MOE_BLOCK - 1) + MOE_BLOCK - 1) // MOE_BLOCK
        n_slots = n_blocks * MOE_BLOCK
        padded, pends, pstarts, n_valid = _block_table(counts[:, 0])
        pos_t = _pos(idx_t, rank_t, pstarts).reshape(-1)
        xs = _dispatch(pends, padded, pos_t, h2p.reshape(T, ROW_CHUNKS, LANE), n_slots)
        ye = _experts(pstarts // MOE_BLOCK, padded // MOE_BLOCK, n_valid,
                      xs.reshape(n_slots * ROW_CHUNKS, LANE), exp_w_gate[l], exp_w_up[l], exp_w_down[l])
    return _final(pos_t, gates, x1, gt2, g_final, ye.reshape(n_slots, ROW_CHUNKS, LANE))
```

```python
import functools

import jax
import jax.numpy as jnp
from jax import lax
from jax.experimental import pallas as pl
from jax.experimental.pallas import tpu as pltpu

F32 = jnp.float32
BF16 = jnp.bfloat16

D_MODEL = 1024
D_S5 = 512
S5_GROUP = 16
S5_GROUPS = 32
S5_STATE = 64
S5_CHUNK = 16
S5_STEP_GROUPS = 8
S5_LEVELS = 7
S5_APOW_COLS = 16
D_GLA = 512
GLA_HEADS = 4
GLA_DK = 64
GLA_DV = 128
GLA_QK = 256
GLA_LOWRANK = 16
GLA_TAU = 16.0
GLA_CHUNK = 64
LANE = 128
N_EXPERTS = 256
TOP_K = 8
N_GROUPS = 8
TOPK_GROUPS = 4
D_EXPERT = 256
ROUTED_SCALE = 2.5
EPS = 1e-6
MOE_BLOCK = 128
ROW_CHUNKS = D_MODEL // 2 // LANE
VMEM_LIMIT = 48 * 1024 * 1024


def _silu(x):
    return x * jax.nn.sigmoid(x)


def _params(*sem):
    return pltpu.CompilerParams(dimension_semantics=sem, vmem_limit_bytes=VMEM_LIMIT)


def _ada_kernel(c_ref, w_ref, b_ref, o_ref):
    s = _silu(c_ref[...]).astype(BF16)
    o_ref[...] = jnp.dot(s, w_ref[...].astype(BF16), preferred_element_type=F32) + b_ref[...]


def _ada(c, w, b):
    bsz, d = c.shape
    n = w.shape[1]
    tn = 1024
    return pl.pallas_call(
        _ada_kernel,
        grid=(n // tn,),
        in_specs=[pl.BlockSpec((bsz, d), lambda j: (0, 0)),
                  pl.BlockSpec((d, tn), lambda j: (0, j)),
                  pl.BlockSpec((1, tn), lambda j: (0, j))],
        out_specs=pl.BlockSpec((bsz, tn), lambda j: (0, j)),
        out_shape=jax.ShapeDtypeStruct((bsz, n), F32),
        compiler_params=_params("arbitrary"),
        name="ada",
    )(c, w, b.reshape(1, n))


def _inproj_kernel(x_ref, g_ref, sh_ref, sc_ref, w_ref,
                   u_ref, q_ref, k_ref, v_ref, r_ref, glr_ref):
    x = x_ref[0]
    ms = jnp.mean(x * x, axis=-1, keepdims=True)
    h = (x * lax.rsqrt(ms + EPS)) * g_ref[...]
    h = h * (1.0 + sc_ref[0]) + sh_ref[0]
    hb = h.astype(BF16)
    u = jnp.dot(hb, w_ref[:, 0:D_S5], preferred_element_type=F32)
    for c in range(D_S5 // LANE):
        u_ref[0, c] = u[:, c * LANE:(c + 1) * LANE]
    col = D_S5
    for ref in (q_ref, k_ref, v_ref, r_ref, glr_ref):
        n = ref.shape[-1]
        ref[0] = jnp.dot(hb, w_ref[:, col:col + n], preferred_element_type=F32).astype(ref.dtype)
        col += n


def _inproj(x, g, sh, sc, w):
    bsz, L, d = x.shape
    tt = 512
    widths = (GLA_QK, GLA_QK, D_GLA, D_GLA, LANE)
    tok = lambda n: pl.BlockSpec((1, tt, n), lambda b, i: (b, i, 0))
    vec = pl.BlockSpec((1, 1, d), lambda b, i: (b, 0, 0))
    return pl.pallas_call(
        _inproj_kernel,
        grid=(bsz, L // tt),
        in_specs=[tok(d), pl.BlockSpec((1, d), lambda b, i: (0, 0)), vec, vec,
                  pl.BlockSpec(w.shape, lambda b, i: (0, 0))],
        out_specs=[pl.BlockSpec((1, D_S5 // LANE, tt, LANE), lambda b, i: (b, 0, i, 0))] + [tok(n) for n in widths],
        out_shape=[jax.ShapeDtypeStruct((bsz, D_S5 // LANE, L, LANE), F32)]
        + [jax.ShapeDtypeStruct((bsz, L, n), BF16) for n in widths],
        compiler_params=_params("arbitrary", "arbitrary"),
        name="inproj",
    )(x, g.reshape(1, d), sh, sc, w)


def _s5_prep(lam_re, lam_im, log_dt, b_re, b_im, c_re, c_im, d_skip):
    G, N, C, TC = S5_GROUPS, S5_STATE, S5_GROUP, S5_CHUNK
    hp = lax.Precision.HIGHEST
    dt = jnp.exp(log_dt)[:, None]
    lr, li = lam_re, lam_im
    mag = jnp.exp(lr * dt)
    ab_re, ab_im = mag * jnp.cos(li * dt), mag * jnp.sin(li * dt)
    den = lr * lr + li * li
    nr = ab_re - 1.0
    coef_re = ((nr * lr + ab_im * li) / den)[..., None]
    coef_im = ((ab_im * lr - nr * li) / den)[..., None]
    bb_re = coef_re * b_re - coef_im * b_im
    bb_im = coef_re * b_im + coef_im * b_re
    p = jnp.arange(TC + 1, dtype=F32)[:, None, None]
    pm = jnp.exp(lr * dt * p)
    pr, pi = pm * jnp.cos(li * dt * p), pm * jnp.sin(li * dt * p)
    ca_re = c_re[None] * pr[:, :, None, :] - c_im[None] * pi[:, :, None, :]
    ca_im = c_re[None] * pi[:, :, None, :] + c_im[None] * pr[:, :, None, :]
    ca = jnp.concatenate([ca_re[:TC], -ca_im[:TC]], axis=-1).transpose(1, 0, 2, 3).reshape(G, TC * C, 2 * N)
    kern = jnp.einsum('gxk,gki->gxi', ca, jnp.concatenate([bb_re, bb_im], axis=1),
                      precision=hp).reshape(G, TC, C, C)
    skip = jnp.eye(C, dtype=F32)[None] * d_skip[:, :, None]
    kern = kern + jnp.concatenate([skip[:, None], jnp.zeros((G, TC - 1, C, C), F32)], axis=1)
    place = (jnp.arange(TC)[None, :, None] - jnp.arange(TC)[None, None, :]
             == jnp.arange(TC)[:, None, None]).astype(F32)
    toep_t = jnp.einsum('xts,gxoi->gtosi', place, kern, precision=hp).reshape(G, TC * C, TC * C)
    rr, ri = pr[TC - 1 - jnp.arange(TC)], pi[TC - 1 - jnp.arange(TC)]
    binc_re = rr[..., None] * bb_re[None] - ri[..., None] * bb_im[None]
    binc_im = rr[..., None] * bb_im[None] + ri[..., None] * bb_re[None]
    binc_re_t = binc_re.transpose(1, 2, 0, 3).reshape(G, N, TC * C)
    binc_im_t = binc_im.transpose(1, 2, 0, 3).reshape(G, N, TC * C)
    cm_re_t = ca_re[1:].transpose(1, 0, 2, 3).reshape(G, TC * C, N)
    cm_im_t = (-ca_im[1:]).transpose(1, 0, 2, 3).reshape(G, TC * C, N)
    q = (TC * 2.0 ** jnp.arange(S5_LEVELS, dtype=F32))[:, None, None]
    qm = jnp.exp(lr * dt * q)
    qr, qi = qm * jnp.cos(li * dt * q), qm * jnp.sin(li * dt * q)
    apow = jnp.stack([qr, qi], axis=1).reshape(2 * S5_LEVELS, G, N).transpose(1, 2, 0)
    apow = jnp.concatenate([apow, jnp.zeros((G, N, S5_APOW_COLS - 2 * S5_LEVELS), F32)], axis=-1)
    return (toep_t.astype(BF16), binc_re_t.astype(BF16), binc_im_t.astype(BF16),
            cm_re_t.astype(BF16), cm_im_t.astype(BF16), apow)


def _s5_kernel(u_ref, eye_ref, toep_ref, bre_ref, bim_ref, cre_ref, cim_ref, apow_ref,
               wglu_ref, bglu_ref, sng_ref, wo_ref, o_ref, ut_ref, yt_ref):
    j = pl.program_id(1)
    n_chunks = u_ref.shape[2] // S5_CHUNK
    nt = (((1,), (1,)), ((), ()))

    @pl.when(j == 0)
    def _():
        for s in range(S5_CHUNK):
            us = jnp.concatenate([u_ref[0, c, pl.ds(s, n_chunks, stride=S5_CHUNK), :]
                                  for c in range(D_S5 // LANE)], axis=1).astype(BF16)
            ut_ref[:, s * n_chunks:(s + 1) * n_chunks] = lax.dot_general(
                eye_ref[...], us, nt, preferred_element_type=F32).astype(BF16)

    lane = lax.broadcasted_iota(jnp.int32, (S5_STATE, n_chunks), 1)
    groups = range(S5_STEP_GROUPS)
    rows = [pl.ds(pl.multiple_of(j * (S5_STEP_GROUPS * S5_GROUP) + gl * S5_GROUP, S5_GROUP), S5_GROUP)
            for gl in groups]
    zt = [jnp.concatenate([ut_ref[rows[gl], s * n_chunks:(s + 1) * n_chunks] for s in range(S5_CHUNK)], axis=0)
          for gl in groups]
    xr = [jnp.where(lane >= 1, pltpu.roll(jnp.dot(bre_ref[gl], zt[gl], preferred_element_type=F32), 1, axis=1), 0.0)
          for gl in groups]
    xi = [jnp.where(lane >= 1, pltpu.roll(jnp.dot(bim_ref[gl], zt[gl], preferred_element_type=F32), 1, axis=1), 0.0)
          for gl in groups]
    for lv in range(S5_LEVELS):
        d = 1 << lv
        for gl in groups:
            ar = apow_ref[gl, :, 2 * lv:2 * lv + 1]
            ai = apow_ref[gl, :, 2 * lv + 1:2 * lv + 2]
            sr = jnp.where(lane >= d, pltpu.roll(xr[gl], d, axis=1), 0.0)
            si = jnp.where(lane >= d, pltpu.roll(xi[gl], d, axis=1), 0.0)
            xr[gl], xi[gl] = xr[gl] + ar * sr - ai * si, xi[gl] + ar * si + ai * sr
    for gl in groups:
        yt = (jnp.dot(toep_ref[gl], zt[gl], preferred_element_type=F32)
              + jnp.dot(cre_ref[gl], xr[gl].astype(BF16), preferred_element_type=F32)
              + jnp.dot(cim_ref[gl], xi[gl].astype(BF16), preferred_element_type=F32))
        for t in range(S5_CHUNK):
            yt_ref[rows[gl], t * n_chunks:(t + 1) * n_chunks] = yt[t * S5_GROUP:(t + 1) * S5_GROUP, :]

    @pl.when(j == pl.num_programs(1) - 1)
    def _():
        cw = 2 * n_chunks
        for cc in range(yt_ref.shape[1] // cw):
            y = yt_ref[:, cc * cw:(cc + 1) * cw]
            g = y * (0.5 * (1.0 + jnp.tanh(0.7978845608028654 * (y + 0.044715 * (y * y * y)))))
            z = jnp.dot(wglu_ref[...], g.astype(BF16), preferred_element_type=F32) + bglu_ref[...]
            out = g * jax.nn.sigmoid(z)
            out = out * lax.rsqrt(jnp.mean(out * out, axis=0, keepdims=True) + EPS) * sng_ref[...]
            mixc = jnp.dot(out.T.astype(BF16), wo_ref[...], preferred_element_type=F32)
            for sl in range(cw // n_chunks):
                s = cc * (cw // n_chunks) + sl
                for c in range(D_MODEL // LANE):
                    o_ref[0, c, pl.ds(s, n_chunks, stride=S5_CHUNK), :] = mixc[sl * n_chunks:(sl + 1) * n_chunks,
                                                                               c * LANE:(c + 1) * LANE]


def _s5(u, mats, wglu_t, bglu, sng, wo_s5):
    bsz, _, L, _ = u.shape
    assert L // S5_CHUNK == LANE, "one lane tile of chunks per sequence"
    toep, bre, bim, cre, cim, apow = mats
    sg = S5_STEP_GROUPS
    eye = jnp.eye(D_S5, dtype=BF16)
    grp = lambda a, b: pl.BlockSpec((sg, a, b), lambda bi, j: (j, 0, 0))
    full = lambda a: pl.BlockSpec(a.shape, lambda bi, j: (0,) * a.ndim)
    w = S5_CHUNK * S5_GROUP
    return pl.pallas_call(
        _s5_kernel,
        grid=(bsz, S5_GROUPS // sg),
        in_specs=[pl.BlockSpec((1, D_S5 // LANE, L, LANE), lambda bi, j: (bi, 0, 0, 0)), full(eye),
                  grp(w, w), grp(S5_STATE, w), grp(S5_STATE, w), grp(w, S5_STATE), grp(w, S5_STATE),
                  grp(S5_STATE, S5_APOW_COLS), full(wglu_t), full(bglu), full(sng), full(wo_s5)],
        out_specs=pl.BlockSpec((1, D_MODEL // LANE, L, LANE), lambda bi, j: (bi, 0, 0, 0)),
        out_shape=jax.ShapeDtypeStruct((bsz, D_MODEL // LANE, L, LANE), F32),
        scratch_shapes=[pltpu.VMEM((D_S5, L), BF16), pltpu.VMEM((D_S5, L), F32)],
        compiler_params=_params("arbitrary", "arbitrary"),
        name="s5",
    )(u, eye, toep, bre, bim, cre, cim, apow, wglu_t, bglu, sng, wo_s5)


def _gla_kernel(q_ref, k_ref, v_ref, glr_ref, r_ref, wg_ref, bg_ref, ng_ref, o_ref, st_ref):
    lt = q_ref.shape[1]
    C = GLA_CHUNK

    @pl.when(pl.program_id(1) == 0)
    def _():
        st_ref[...] = jnp.zeros_like(st_ref)

    z = jnp.dot(glr_ref[0], wg_ref[...], preferred_element_type=F32) + bg_ref[...]
    log_a = (jnp.minimum(z, 0.0) - jnp.log(1.0 + jnp.exp(-jnp.abs(z)))) * (1.0 / GLA_TAU)
    ri = lax.broadcasted_iota(jnp.int32, (lt, lt), 0)
    ci = lax.broadcasted_iota(jnp.int32, (lt, lt), 1)
    tril = jnp.where(((ri >> 6) == (ci >> 6)) & (ci <= ri), 1.0, 0.0).astype(BF16)
    la_hi = log_a.astype(BF16)
    la_lo = (log_a - la_hi.astype(F32)).astype(BF16)
    bcum = (jnp.dot(tril, la_hi, preferred_element_type=F32)
            + jnp.dot(tril, la_lo, preferred_element_type=F32))
    q = q_ref[0].astype(F32) * (GLA_DK ** -0.5)
    k = k_ref[0].astype(F32)
    qi = q * jnp.exp(bcum)
    ki = k * jnp.exp(-bcum)
    lane_head = lax.broadcasted_iota(jnp.int32, (1, GLA_QK), 1) >> 6
    causal = ((lax.broadcasted_iota(jnp.int32, (GLA_HEADS * C, C), 0) & (C - 1))
              >= lax.broadcasted_iota(jnp.int32, (GLA_HEADS * C, C), 1))
    same_head = ((lax.broadcasted_iota(jnp.int32, (D_GLA, GLA_QK), 0) >> 7)
                 == (lax.broadcasted_iota(jnp.int32, (D_GLA, GLA_QK), 1) >> 6))
    nt = (((1,), (1,)), ((), ()))
    ng = ng_ref[...]
    st = st_ref[...]
    for c in range(lt // C):
        sl = slice(c * C, (c + 1) * C)
        bc = bcum[sl]
        bl = bc[C - 1:C, :]
        kd = k[sl] * jnp.exp(bl - bc)
        qic = qi[sl]
        qs = jnp.concatenate([jnp.where(lane_head == h, qic, 0.0) for h in range(GLA_HEADS)],
                             axis=0).astype(BF16)
        sc = lax.dot_general(qs, ki[sl].astype(BF16), nt, preferred_element_type=F32)
        p = jnp.where(causal, sc, 0.0).astype(BF16)
        vc = v_ref[0, sl, :]
        o_intra = jnp.concatenate(
            [jnp.dot(p[h * C:(h + 1) * C], vc[:, h * GLA_DV:(h + 1) * GLA_DV], preferred_element_type=F32)
             for h in range(GLA_HEADS)], axis=1)
        o_inter = lax.dot_general(qic.astype(BF16), st.astype(BF16), nt, preferred_element_type=F32)
        v_t = vc.astype(F32).T.astype(BF16)
        kv_t = jnp.dot(v_t, kd.astype(BF16), preferred_element_type=F32)
        st = st * jnp.exp(bl) + jnp.where(same_head, kv_t, 0.0)
        o = o_intra + o_inter
        parts = []
        for h in range(GLA_HEADS):
            oh = o[:, h * GLA_DV:(h + 1) * GLA_DV]
            oh = oh * lax.rsqrt(jnp.mean(oh * oh, axis=-1, keepdims=True) + EPS)
            parts.append(oh * ng)
        r = r_ref[0, sl, :].astype(F32)
        o_ref[0, sl, :] = (jnp.concatenate(parts, axis=1) * _silu(r)).astype(o_ref.dtype)
    st_ref[...] = st


def _gla(q, k, v, glr, r, wg, bg, ng):
    bsz, L, _ = q.shape
    lt = 256
    tok = lambda n: pl.BlockSpec((1, lt, n), lambda b, i: (b, i, 0))
    full = lambda a: pl.BlockSpec(a.shape, lambda b, i: (0,) * a.ndim)
    return pl.pallas_call(
        _gla_kernel,
        grid=(bsz, L // lt),
        in_specs=[tok(GLA_QK), tok(GLA_QK), tok(D_GLA), tok(LANE), tok(D_GLA), full(wg), full(bg), full(ng)],
        out_specs=tok(D_GLA),
        out_shape=jax.ShapeDtypeStruct((bsz, L, D_GLA), BF16),
        scratch_shapes=[pltpu.VMEM((D_GLA, GLA_QK), F32)],
        compiler_params=_params("arbitrary", "arbitrary"),
        name="gla",
    )(q, k, v, glr, r, wg, bg, ng)


def _pack_bf16_pairs(x):
    w = x.shape[1] // 2
    xr = x.astype(BF16).astype(F32)
    lo = lax.bitcast_convert_type(xr[:, :w], jnp.uint32) >> 16
    hi = lax.bitcast_convert_type(xr[:, w:], jnp.uint32) & jnp.uint32(0xFFFF0000)
    return lo | hi


def _unpack_bf16_pairs(p):
    lo = lax.bitcast_convert_type(p << 16, F32)
    hi = lax.bitcast_convert_type(p & jnp.uint32(0xFFFF0000), F32)
    return lo, hi


def _store_rows(ref, packed):
    n = packed.shape[0]
    for c in range(ROW_CHUNKS):
        ref[pl.ds(c, n, stride=ROW_CHUNKS), :] = packed[:, c * LANE:(c + 1) * LANE]


def _load_rows(ref, n):
    return jnp.concatenate([ref[pl.ds(c, n, stride=ROW_CHUNKS), :] for c in range(ROW_CHUNKS)], axis=1)


def _route_tile(lg_t, bias_col, tri, carry_ref):
    n_e, tt = lg_t.shape
    per_group = n_e // N_GROUPS
    scores = jax.nn.sigmoid(lg_t)
    biased = scores + bias_col
    row = lax.broadcasted_iota(jnp.int32, (n_e, tt), 0)
    neg = -jnp.inf
    group_score = []
    for g in range(N_GROUPS):
        b = biased[g * per_group:(g + 1) * per_group]
        r = lax.broadcasted_iota(jnp.int32, (per_group, tt), 0) + g * per_group
        m1 = jnp.max(b, axis=0, keepdims=True)
        i1 = jnp.min(jnp.where(b == m1, r, n_e), axis=0, keepdims=True)
        m2 = jnp.max(jnp.where(r == i1, neg, b), axis=0, keepdims=True)
        group_score.append(m1 + m2)
    parts = []
    for g in range(N_GROUPS):
        ahead = jnp.zeros((1, tt), jnp.int32)
        for g2 in range(N_GROUPS):
            if g2 != g:
                beats = (group_score[g2] >= group_score[g]) if g2 < g else (group_score[g2] > group_score[g])
                ahead = ahead + beats.astype(jnp.int32)
        parts.append(jnp.where(ahead < TOPK_GROUPS, biased[g * per_group:(g + 1) * per_group], neg))
    masked = jnp.concatenate(parts, axis=0)
    work = masked
    idxs = []
    for _ in range(TOP_K):
        m = jnp.max(work, axis=0, keepdims=True)
        ii = jnp.min(jnp.where(work == m, row, n_e), axis=0, keepdims=True)
        idxs.append(ii)
        work = jnp.where(row == ii, neg, work)
    sel = work != masked
    w = jnp.where(sel, scores, 0.0)
    gate_dense = w / jnp.sum(w, axis=0, keepdims=True) * ROUTED_SCALE
    mt = jnp.where(sel, 1.0, 0.0)
    rank_dense = jnp.dot(mt.astype(BF16), tri, preferred_element_type=F32) + carry_ref[...]
    carry_ref[...] += jnp.sum(mt, axis=1, keepdims=True)
    ranks, gts = [], []
    for ii in idxs:
        oh = row == ii
        ranks.append(jnp.sum(jnp.where(oh, rank_dense, 0.0), axis=0, keepdims=True))
        gts.append(jnp.sum(jnp.where(oh, gate_dense, 0.0), axis=0, keepdims=True))
    idx_t = jnp.concatenate(idxs, axis=0)
    rank_t = jnp.concatenate(ranks, axis=0).astype(jnp.int32)
    gate_t = jnp.concatenate(gts + [jnp.zeros((LANE - TOP_K, tt), F32)], axis=0)
    return idx_t, rank_t, gate_t.T[:, :TOP_K]


def _mix_kernel(ms_ref, yg_ref, x_ref, wo_ref, gt1_ref,
                gn_ref, sh2_ref, sc2_ref, gt2_ref, wrt_ref, rb_ref, tri_ref, wsg_ref, wsu_ref, wsd_ref,
                x1_ref, h2_ref, idx_ref, rank_ref, gate_ref, cnt_ref, carry_ref):
    @pl.when((pl.program_id(0) == 0) & (pl.program_id(1) == 0))
    def _():
        carry_ref[...] = jnp.zeros_like(carry_ref)

    mix_s5 = jnp.concatenate([ms_ref[0, c] for c in range(D_MODEL // LANE)], axis=1)
    mix = mix_s5 + jnp.dot(yg_ref[0], wo_ref[...], preferred_element_type=F32)
    x1 = x_ref[0] + gt1_ref[0] * mix
    h2 = x1 * lax.rsqrt(jnp.mean(x1 * x1, axis=-1, keepdims=True) + EPS) * gn_ref[...]
    h2 = h2 * (1.0 + sc2_ref[0]) + sh2_ref[0]
    hb = h2.astype(BF16)
    _store_rows(h2_ref, _pack_bf16_pairs(h2))
    lg_t = lax.dot_general(wrt_ref[...], hb, (((1,), (1,)), ((), ())), preferred_element_type=F32)
    idx_t, rank_t, gates = _route_tile(lg_t, rb_ref[...], tri_ref[...], carry_ref)
    idx_ref[...] = idx_t
    rank_ref[...] = rank_t
    gate_ref[...] = gates
    cnt_ref[...] = carry_ref[...]
    a = _silu(jnp.dot(hb, wsg_ref[...], preferred_element_type=F32)) * jnp.dot(
        hb, wsu_ref[...], preferred_element_type=F32)
    shared = jnp.dot(a.astype(BF16), wsd_ref[...], preferred_element_type=F32)
    x1_ref[0] = x1 + gt2_ref[0] * shared


def _mix(ms5, yg, x, wo, gt1, gn, sh2, sc2, gt2, wrt, rb, wsg, wsu, wsd):
    bsz, L, d = x.shape
    tt = 512
    nt = L // tt
    T = bsz * L
    tri = (jnp.arange(tt)[:, None] < jnp.arange(tt)[None, :]).astype(BF16)
    tok = lambda n: pl.BlockSpec((1, tt, n), lambda b, i: (b, i, 0))
    vec = pl.BlockSpec((1, 1, d), lambda b, i: (b, 0, 0))
    full = lambda a: pl.BlockSpec(a.shape, lambda b, i: (0,) * a.ndim)
    lanes = pl.BlockSpec((TOP_K, tt), lambda b, i: (0, b * nt + i))
    return pl.pallas_call(
        _mix_kernel,
        grid=(bsz, nt),
        in_specs=[pl.BlockSpec((1, d // LANE, tt, LANE), lambda b, i: (b, 0, i, 0)), tok(D_GLA), tok(d), full(wo), vec,
                  full(gn), vec, vec, vec, full(wrt), full(rb), full(tri), full(wsg), full(wsu), full(wsd)],
        out_specs=[tok(d),
                   pl.BlockSpec((tt * ROW_CHUNKS, LANE), lambda b, i: (b * nt + i, 0)),
                   lanes, lanes,
                   pl.BlockSpec((tt, TOP_K), lambda b, i: (b * nt + i, 0)),
                   pl.BlockSpec((N_EXPERTS, 1), lambda b, i: (0, 0))],
        out_shape=[jax.ShapeDtypeStruct((bsz, L, d), F32),
                   jax.ShapeDtypeStruct((T * ROW_CHUNKS, LANE), jnp.uint32),
                   jax.ShapeDtypeStruct((TOP_K, T), jnp.int32),
                   jax.ShapeDtypeStruct((TOP_K, T), jnp.int32),
                   jax.ShapeDtypeStruct((T, TOP_K), F32),
                   jax.ShapeDtypeStruct((N_EXPERTS, 1), F32)],
        scratch_shapes=[pltpu.VMEM((N_EXPERTS, 1), F32)],
        compiler_params=_params("arbitrary", "arbitrary"),
        name="mix",
    )(ms5, yg, x, wo, gt1, gn, sh2, sc2, gt2, wrt, rb, tri, wsg, wsu, wsd)


def _pos_kernel(idx_ref, rank_ref, ps_ref, pos_ref):
    n_e = ps_ref.shape[0]
    tt = idx_ref.shape[1]
    row = lax.broadcasted_iota(jnp.int32, (n_e, tt), 0)
    ps = ps_ref[...]
    starts = [jnp.sum(jnp.where(row == idx_ref[k:k + 1, :], ps, 0.0), axis=0, keepdims=True)
              for k in range(TOP_K)]
    pos = jnp.concatenate(starts, axis=0).astype(jnp.int32) + rank_ref[...]
    for jh in range(tt // LANE):
        pos_ref[jh * TOP_K:(jh + 1) * TOP_K, :] = pos[:, jh * LANE:(jh + 1) * LANE]


def _pos(idx_t, rank_t, pstart):
    T = idx_t.shape[1]
    tt = 2048
    blk = pl.BlockSpec((TOP_K, tt), lambda i: (0, i))
    return pl.pallas_call(
        _pos_kernel,
        grid=(T // tt,),
        in_specs=[blk, blk, pl.BlockSpec((N_EXPERTS, 1), lambda i: (0, 0))],
        out_specs=pl.BlockSpec((tt // LANE * TOP_K, LANE), lambda i: (i, 0)),
        out_shape=jax.ShapeDtypeStruct((T // LANE * TOP_K, LANE), jnp.int32),
        compiler_params=_params("arbitrary"),
        name="pos",
    )(idx_t, rank_t, pstart.astype(F32).reshape(N_EXPERTS, 1))


def _dispatch_kernel(pend_ref, padded_ref, pos_ref, h_ref, xs_hbm, zero_ref, pos_smem, sem_pos, sem_zero, sem_row):
    tt = h_ref.shape[0]
    step = pl.program_id(0)
    load_pos = pltpu.make_async_copy(pos_ref, pos_smem, sem_pos)
    load_pos.start()

    def zero_block(start):
        return pltpu.make_async_copy(zero_ref, xs_hbm.at[pl.ds(pl.multiple_of(start, MOE_BLOCK), MOE_BLOCK)],
                                     sem_zero)

    def for_each_zero_block(fn):
        def per_expert(e, c):
            @pl.when(padded_ref[e] > 0)
            def _():
                fn(zero_block(pend_ref[e] - MOE_BLOCK))
            return c

        def per_tail_block(g, c):
            fn(zero_block(g * MOE_BLOCK))
            return c

        lax.fori_loop(0, N_EXPERTS, per_expert, 0)
        lax.fori_loop(pend_ref[N_EXPERTS - 1] // MOE_BLOCK, xs_hbm.shape[0] // MOE_BLOCK, per_tail_block, 0)

    @pl.when(step == 0)
    def _():
        zero_ref[...] = jnp.zeros_like(zero_ref)
        for_each_zero_block(lambda cp: cp.start())
        for_each_zero_block(lambda cp: cp.wait())

    load_pos.wait()
    for jh in range(tt // LANE):
        def issue(jl, c, jh=jh):
            for k in range(TOP_K):
                slot = pos_smem[(jh * TOP_K + k) * LANE + jl]
                pltpu.make_async_copy(h_ref.at[jh * LANE + jl], xs_hbm.at[slot], sem_row).start(priority=k % 2)
            return c

        lax.fori_loop(0, LANE, issue, 0, unroll=4)
    for k in range(TOP_K):
        pltpu.make_async_copy(h_ref, xs_hbm.at[pl.ds(0, tt)], sem_row).wait()


def _dispatch(pends, padded, pos_t, h_rows, n_slots):
    T = h_rows.shape[0]
    tt = 512
    return pl.pallas_call(
        _dispatch_kernel,
        grid_spec=pltpu.PrefetchScalarGridSpec(
            num_scalar_prefetch=2,
            grid=(T // tt,),
            in_specs=[pl.BlockSpec((tt * TOP_K,), lambda i, pe, pa: (i,)),
                      pl.BlockSpec((tt, ROW_CHUNKS, LANE), lambda i, pe, pa: (i, 0, 0))],
            out_specs=pl.BlockSpec(memory_space=pl.ANY),
            scratch_shapes=[pltpu.VMEM((MOE_BLOCK, ROW_CHUNKS, LANE), jnp.uint32),
                            pltpu.SMEM((tt * TOP_K,), jnp.int32),
                            pltpu.SemaphoreType.DMA, pltpu.SemaphoreType.DMA, pltpu.SemaphoreType.DMA]),
        out_shape=jax.ShapeDtypeStruct((n_slots, ROW_CHUNKS, LANE), jnp.uint32),
        compiler_params=_params("arbitrary"),
        name="dispatch",
    )(pends, padded, pos_t, h_rows)


X_GROUP = 8
X_AHEAD = 8
X_BUFS = X_AHEAD + X_GROUP
Y_BUFS = X_GROUP


def _expert_kernel(b0_ref, nb_ref, nv_ref, wg_ref, wu_ref, wd_ref, xs_hbm, ye_hbm,
                   wg_s, wu_s, wd_s, xbuf, ybuf, semx, semy, *, n_blocks):
    e = pl.program_id(0)
    nb = nb_ref[e]
    b0 = b0_ref[e]
    nv = nv_ref[0]
    rows = MOE_BLOCK * ROW_CHUNKS

    def block(ref, g):
        return ref.at[pl.ds(pl.multiple_of(g * rows, rows), rows)]

    def x_copy(g):
        slot = g % X_BUFS
        return pltpu.make_async_copy(block(xs_hbm, g), xbuf.at[slot], semx.at[slot])

    def y_copy(g):
        slot = g % Y_BUFS
        return pltpu.make_async_copy(ybuf.at[slot], block(ye_hbm, g), semy.at[slot])

    @pl.when(e == 0)
    def _():
        for g in range(X_AHEAD):
            x_copy(g).start()

    @pl.when(nb > 0)
    def _():
        wg_s[...] = wg_ref[0].astype(BF16)
        wu_s[...] = wu_ref[0].astype(BF16)
        wd_s[...] = wd_ref[0].astype(BF16)

    def fetch(g):
        x_copy(g).wait()

        @pl.when(g + X_AHEAD < nv)
        def _():
            x_copy(g + X_AHEAD).start()

    def load_x(g):
        lo, hi = _unpack_bf16_pairs(_load_rows(xbuf.at[g % X_BUFS], MOE_BLOCK))
        return jnp.concatenate([lo, hi], axis=1).astype(BF16)

    def emit(g, y):
        @pl.when(g >= Y_BUFS)
        def _():
            y_copy(g - Y_BUFS).wait()

        _store_rows(ybuf.at[g % Y_BUFS], y)
        y_copy(g).start()

    def run(g, n):
        for b in range(n):
            fetch(g + b)
        x = jnp.concatenate([load_x(g + b) for b in range(n)], axis=0) if n > 1 else load_x(g)
        a = jnp.dot(x, wg_s[...], preferred_element_type=F32)
        u = jnp.dot(x, wu_s[...], preferred_element_type=F32)
        h = (_silu(a) * u).astype(BF16)
        y = _pack_bf16_pairs(jnp.dot(h, wd_s[...], preferred_element_type=F32))
        for b in range(n):
            emit(g + b, y[b * MOE_BLOCK:(b + 1) * MOE_BLOCK])

    def group(i, c):
        run(b0 + X_GROUP * i, X_GROUP)
        return c

    lax.fori_loop(0, nb // X_GROUP, group, 0)
    done = nb // X_GROUP * X_GROUP
    n = X_GROUP // 2
    while n >= 1:
        @pl.when((nb & n) != 0)
        def _(n=n, done=done):
            run(b0 + done, n)

        done = done + (nb & n)
        n //= 2

    @pl.when(e == pl.num_programs(0) - 1)
    def _():
        for back in range(Y_BUFS, 0, -1):
            y_copy(nv - back).wait()
        ybuf[0] = jnp.zeros(ybuf.shape[1:], ybuf.dtype)

        def fill(g):
            return pltpu.make_async_copy(ybuf.at[0], block(ye_hbm, g), semy.at[0])

        lax.fori_loop(nv, n_blocks, lambda g, c: (fill(g).start(), c)[1], 0)
        lax.fori_loop(nv, n_blocks, lambda g, c: (fill(g).wait(), c)[1], 0)


def _experts(first_block, num_blocks, n_valid, xs, wg, wu, wd):
    rows = MOE_BLOCK * ROW_CHUNKS
    n_blocks = xs.shape[0] // rows
    n_e, d, _ = wg.shape
    wsel = lambda e, b0, nb, nv: (e, 0, 0)
    return pl.pallas_call(
        functools.partial(_expert_kernel, n_blocks=n_blocks),
        grid_spec=pltpu.PrefetchScalarGridSpec(
            num_scalar_prefetch=3,
            grid=(n_e,),
            in_specs=[pl.BlockSpec((1, d, D_EXPERT), wsel),
                      pl.BlockSpec((1, d, D_EXPERT), wsel),
                      pl.BlockSpec((1, D_EXPERT, d), wsel),
                      pl.BlockSpec(memory_space=pl.ANY)],
            out_specs=pl.BlockSpec(memory_space=pl.ANY),
            scratch_shapes=[pltpu.VMEM((d, D_EXPERT), BF16), pltpu.VMEM((d, D_EXPERT), BF16),
                            pltpu.VMEM((D_EXPERT, d), BF16),
                            pltpu.VMEM((X_BUFS, rows, LANE), jnp.uint32),
                            pltpu.VMEM((Y_BUFS, rows, LANE), jnp.uint32),
                            pltpu.SemaphoreType.DMA((X_BUFS,)), pltpu.SemaphoreType.DMA((Y_BUFS,))]),
        out_shape=jax.ShapeDtypeStruct(xs.shape, jnp.uint32),
        compiler_params=_params("arbitrary"),
        name="experts",
    )(first_block, num_blocks, n_valid, wg, wu, wd, xs)


def _final_kernel(pos0_ref, posn_ref, gate_ref, x_ref, gt_ref, g_ref, ye_hbm, o_ref,
                  buf_ref, pos_smem, sem_pos, sem_row):
    tt = x_ref.shape[1]
    i = pl.program_id(0)

    def gather_tile(pos_ref, tile):
        base = (tile % 2) * (TOP_K * tt)
        sem = sem_row.at[tile % 2]
        load_pos = pltpu.make_async_copy(pos_ref, pos_smem, sem_pos)
        load_pos.start()
        load_pos.wait()
        for jh in range(tt // LANE):
            def issue(jl, c, jh=jh):
                for k in range(TOP_K):
                    slot = pos_smem[(jh * TOP_K + k) * LANE + jl]
                    pltpu.make_async_copy(ye_hbm.at[slot], buf_ref.at[base + k * tt + jh * LANE + jl],
                                          sem).start(priority=k % 2)
                return c

            lax.fori_loop(0, LANE, issue, 0, unroll=4)

    @pl.when(i == 0)
    def _():
        gather_tile(pos0_ref, i)

    @pl.when(i + 1 < pl.num_programs(0))
    def _():
        gather_tile(posn_ref, i + 1)

    base = pl.multiple_of((i % 2) * (TOP_K * tt), TOP_K * tt)
    for k in range(TOP_K):
        pltpu.make_async_copy(ye_hbm.at[pl.ds(0, tt)], buf_ref.at[pl.ds(base + k * tt, tt)],
                              sem_row.at[i % 2]).wait()
    gates = gate_ref[...]
    half = D_MODEL // 2
    acc_lo = jnp.zeros((tt, half), F32)
    acc_hi = jnp.zeros((tt, half), F32)
    rows_2d = buf_ref.reshape(buf_ref.shape[0] * ROW_CHUNKS, LANE)
    for k in range(TOP_K):
        lo, hi = _unpack_bf16_pairs(_load_rows(rows_2d.at[pl.ds((base + k * tt) * ROW_CHUNKS, tt * ROW_CHUNKS)], tt))
        gk = gates[:, k:k + 1]
        acc_lo += gk * lo
        acc_hi += gk * hi
    x = x_ref[0] + gt_ref[0] * jnp.concatenate([acc_lo, acc_hi], axis=1)
    o_ref[0] = x * lax.rsqrt(jnp.mean(x * x, axis=-1, keepdims=True) + EPS) * g_ref[...]


def _final(pos_t, gates, x1, gt2, g, ye_rows):
    bsz, L, d = x1.shape
    tt = 256
    nt = L // tt
    n = bsz * nt
    tok = pl.BlockSpec((1, tt, d), lambda i: (i // nt, i % nt, 0))
    return pl.pallas_call(
        _final_kernel,
        grid=(n,),
        in_specs=[pl.BlockSpec((tt * TOP_K,), lambda i: (0,)),
                  pl.BlockSpec((tt * TOP_K,), lambda i: (jnp.minimum(i + 1, n - 1),)),
                  pl.BlockSpec((tt, TOP_K), lambda i: (i, 0)),
                  tok, pl.BlockSpec((1, 1, d), lambda i: (i // nt, 0, 0)),
                  pl.BlockSpec((1, d), lambda i: (0, 0)),
                  pl.BlockSpec(memory_space=pl.ANY)],
        out_specs=tok,
        out_shape=jax.ShapeDtypeStruct((bsz, L, d), F32),
        scratch_shapes=[pltpu.VMEM((2 * TOP_K * tt, ROW_CHUNKS, LANE), jnp.uint32),
                        pltpu.SMEM((tt * TOP_K,), jnp.int32),
                        pltpu.SemaphoreType.DMA, pltpu.SemaphoreType.DMA((2,))],
        compiler_params=_params("arbitrary"),
        name="final",
    )(pos_t, pos_t, gates, x1, gt2, g.reshape(1, d), ye_rows)


def _block_table(counts):
    counts = counts.astype(jnp.int32)
    padded = (counts + MOE_BLOCK - 1) // MOE_BLOCK * MOE_BLOCK
    pends = jnp.cumsum(padded)
    pstarts = pends - padded
    n_valid = (pends[-1] // MOE_BLOCK).reshape(1)
    return padded, pends, pstarts, n_valid


def kernel(x, c, w_ada, b_ada, g_norm_mix, w_in, s5_lambda_re, s5_lambda_im, s5_log_dt, s5_b_re, s5_b_im, s5_c_re, s5_c_im, s5_d, s5_w_glu, s5_b_glu, s5_norm_g, gla_w_g2, gla_b_g2, gla_norm_g, w_out, g_norm_moe, w_router, router_bias, exp_w_gate, exp_w_up, exp_w_down, sh_w_gate, sh_w_up, sh_w_down, g_final):
    bsz, L, d = x.shape
    T = bsz * L
    assert w_ada.shape[0] == 1, "single-layer block"
    for l in range(1):
        mod = _ada(c, w_ada[l], b_ada[l])
        sh1, sc1, gt1, sh2, sc2, gt2 = [m.reshape(bsz, 1, d) for m in jnp.split(mod, 6, axis=-1)]

        wi = w_in[l]
        o_q, o_k, o_v, o_g, o_r = D_S5, D_S5 + GLA_QK, D_S5 + 2 * GLA_QK, D_S5 + 2 * GLA_QK + D_GLA, \
            D_S5 + 2 * GLA_QK + D_GLA + GLA_LOWRANK
        w_cat = jnp.concatenate([wi[:, :o_g], wi[:, o_r:], wi[:, o_g:o_r],
                                 jnp.zeros((d, LANE - GLA_LOWRANK), wi.dtype)], axis=1).astype(BF16)
        u, q, k, v, r, glr = _inproj(x, g_norm_mix[l], sh1, sc1, w_cat)

        mats = _s5_prep(s5_lambda_re[l], s5_lambda_im[l], s5_log_dt[l], s5_b_re[l], s5_b_im[l],
                        s5_c_re[l], s5_c_im[l], s5_d[l])
        ms5 = _s5(u, mats, s5_w_glu[l].T.astype(BF16), s5_b_glu[l].reshape(D_S5, 1),
                  s5_norm_g[l].reshape(D_S5, 1), w_out[l][:D_S5].astype(BF16))

        wg2 = jnp.concatenate([gla_w_g2[l], jnp.zeros((LANE - GLA_LOWRANK, GLA_QK), F32)], axis=0).astype(BF16)
        yg = _gla(q, k, v, glr, r, wg2, gla_b_g2[l].reshape(1, GLA_QK), gla_norm_g[l].reshape(1, GLA_DV))

        x1, h2p, idx_t, rank_t, gates, counts = _mix(
            ms5, yg, x, w_out[l][D_S5:].astype(BF16), gt1, g_norm_moe[l].reshape(1, d), sh2, sc2, gt2,
            w_router[l].T.astype(BF16), router_bias[l].reshape(N_EXPERTS, 1),
            sh_w_gate[l].astype(BF16), sh_w_up[l].astype(BF16), sh_w_down[l].astype(BF16))

        n_blocks = (T * TOP_K + N_EXPERTS * (MOE_BLOCK - 1) + MOE_BLOCK - 1) // MOE_BLOCK
        n_slots = n_blocks * MOE_BLOCK
        padded, pends, pstarts, n_valid = _block_table(counts[:, 0])
        pos_t = _pos(idx_t, rank_t, pstarts).reshape(-1)
        xs = _dispatch(pends, padded, pos_t, h2p.reshape(T, ROW_CHUNKS, LANE), n_slots)
        ye = _experts(pstarts // MOE_BLOCK, padded // MOE_BLOCK, n_valid,
                      xs.reshape(n_slots * ROW_CHUNKS, LANE), exp_w_gate[l], exp_w_up[l], exp_w_down[l])
    return _final(pos_t, gates, x1, gt2, g_final, ye.reshape(n_slots, ROW_CHUNKS, LANE))
```

```python
import functools

import jax
import jax.numpy as jnp
from jax import lax
from jax.experimental import pallas as pl
from jax.experimental.pallas import tpu as pltpu

F32 = jnp.float32
BF16 = jnp.bfloat16

D_MODEL = 1024
D_S5 = 512
S5_GROUP = 16
S5_GROUPS = 32
S5_STATE = 64
S5_CHUNK = 16
S5_STEP_GROUPS = 8
S5_LEVELS = 7
S5_APOW_COLS = 16
D_GLA = 512
GLA_HEADS = 4
GLA_DK = 64
GLA_DV = 128
GLA_QK = 256
GLA_LOWRANK = 16
GLA_TAU = 16.0
GLA_CHUNK = 64
LANE = 128
N_EXPERTS = 256
TOP_K = 8
N_GROUPS = 8
TOPK_GROUPS = 4
D_EXPERT = 256
ROUTED_SCALE = 2.5
EPS = 1e-6
MOE_BLOCK = 128
ROW_CHUNKS = D_MODEL // 2 // LANE
VMEM_LIMIT = 48 * 1024 * 1024


def _silu(x):
    return x * jax.nn.sigmoid(x)


def _params(*sem):
    return pltpu.CompilerParams(dimension_semantics=sem, vmem_limit_bytes=VMEM_LIMIT)


def _ada_kernel(c_ref, w_ref, b_ref, o_ref):
    s = _silu(c_ref[...]).astype(BF16)
    o_ref[...] = jnp.dot(s, w_ref[...].astype(BF16), preferred_element_type=F32) + b_ref[...]


def _ada(c, w, b):
    bsz, d = c.shape
    n = w.shape[1]
    tn = 1024
    return pl.pallas_call(
        _ada_kernel,
        grid=(n // tn,),
        in_specs=[pl.BlockSpec((bsz, d), lambda j: (0, 0)),
                  pl.BlockSpec((d, tn), lambda j: (0, j)),
                  pl.BlockSpec((1, tn), lambda j: (0, j))],
        out_specs=pl.BlockSpec((bsz, tn), lambda j: (0, j)),
        out_shape=jax.ShapeDtypeStruct((bsz, n), F32),
        compiler_params=_params("arbitrary"),
        name="ada",
    )(c, w, b.reshape(1, n))


def _inproj_kernel(x_ref, g_ref, sh_ref, sc_ref, w_ref,
                   u_ref, q_ref, k_ref, v_ref, r_ref, glr_ref):
    x = x_ref[0]
    ms = jnp.mean(x * x, axis=-1, keepdims=True)
    h = (x * lax.rsqrt(ms + EPS)) * g_ref[...]
    h = h * (1.0 + sc_ref[0]) + sh_ref[0]
    hb = h.astype(BF16)
    u = jnp.dot(hb, w_ref[:, 0:D_S5], preferred_element_type=F32)
    for c in range(D_S5 // LANE):
        u_ref[0, c] = u[:, c * LANE:(c + 1) * LANE]
    col = D_S5
    for ref in (q_ref, k_ref, v_ref, r_ref, glr_ref):
        n = ref.shape[-1]
        ref[0] = jnp.dot(hb, w_ref[:, col:col + n], preferred_element_type=F32).astype(ref.dtype)
        col += n


def _inproj(x, g, sh, sc, w):
    bsz, L, d = x.shape
    tt = 512
    widths = (GLA_QK, GLA_QK, D_GLA, D_GLA, LANE)
    tok = lambda n: pl.BlockSpec((1, tt, n), lambda b, i: (b, i, 0))
    vec = pl.BlockSpec((1, 1, d), lambda b, i: (b, 0, 0))
    return pl.pallas_call(
        _inproj_kernel,
        grid=(bsz, L // tt),
        in_specs=[tok(d), pl.BlockSpec((1, d), lambda b, i: (0, 0)), vec, vec,
                  pl.BlockSpec(w.shape, lambda b, i: (0, 0))],
        out_specs=[pl.BlockSpec((1, D_S5 // LANE, tt, LANE), lambda b, i: (b, 0, i, 0))] + [tok(n) for n in widths],
        out_shape=[jax.ShapeDtypeStruct((bsz, D_S5 // LANE, L, LANE), F32)]
        + [jax.ShapeDtypeStruct((bsz, L, n), BF16) for n in widths],
        compiler_params=_params("arbitrary", "arbitrary"),
        name="inproj",
    )(x, g.reshape(1, d), sh, sc, w)


def _s5_prep(lam_re, lam_im, log_dt, b_re, b_im, c_re, c_im, d_skip):
    G, N, C, TC = S5_GROUPS, S5_STATE, S5_GROUP, S5_CHUNK
    hp = lax.Precision.HIGHEST
    dt = jnp.exp(log_dt)[:, None]
    lr, li = lam_re, lam_im
    mag = jnp.exp(lr * dt)
    ab_re, ab_im = mag * jnp.cos(li * dt), mag * jnp.sin(li * dt)
    den = lr * lr + li * li
    nr = ab_re - 1.0
    coef_re = ((nr * lr + ab_im * li) / den)[..., None]
    coef_im = ((ab_im * lr - nr * li) / den)[..., None]
    bb_re = coef_re * b_re - coef_im * b_im
    bb_im = coef_re * b_im + coef_im * b_re
    p = jnp.arange(TC + 1, dtype=F32)[:, None, None]
    pm = jnp.exp(lr * dt * p)
    pr, pi = pm * jnp.cos(li * dt * p), pm * jnp.sin(li * dt * p)
    ca_re = c_re[None] * pr[:, :, None, :] - c_im[None] * pi[:, :, None, :]
    ca_im = c_re[None] * pi[:, :, None, :] + c_im[None] * pr[:, :, None, :]
    ca = jnp.concatenate([ca_re[:TC], -ca_im[:TC]], axis=-1).transpose(1, 0, 2, 3).reshape(G, TC * C, 2 * N)
    kern = jnp.einsum('gxk,gki->gxi', ca, jnp.concatenate([bb_re, bb_im], axis=1),
                      precision=hp).reshape(G, TC, C, C)
    skip = jnp.eye(C, dtype=F32)[None] * d_skip[:, :, None]
    kern = kern + jnp.concatenate([skip[:, None], jnp.zeros((G, TC - 1, C, C), F32)], axis=1)
    place = (jnp.arange(TC)[None, :, None] - jnp.arange(TC)[None, None, :]
             == jnp.arange(TC)[:, None, None]).astype(F32)
    toep_t = jnp.einsum('xts,gxoi->gtosi', place, kern, precision=hp).reshape(G, TC * C, TC * C)
    rr, ri = pr[TC - 1 - jnp.arange(TC)], pi[TC - 1 - jnp.arange(TC)]
    binc_re = rr[..., None] * bb_re[None] - ri[..., None] * bb_im[None]
    binc_im = rr[..., None] * bb_im[None] + ri[..., None] * bb_re[None]
    binc_re_t = binc_re.transpose(1, 2, 0, 3).reshape(G, N, TC * C)
    binc_im_t = binc_im.transpose(1, 2, 0, 3).reshape(G, N, TC * C)
    cm_re_t = ca_re[1:].transpose(1, 0, 2, 3).reshape(G, TC * C, N)
    cm_im_t = (-ca_im[1:]).transpose(1, 0, 2, 3).reshape(G, TC * C, N)
    q = (TC * 2.0 ** jnp.arange(S5_LEVELS, dtype=F32))[:, None, None]
    qm = jnp.exp(lr * dt * q)
    qr, qi = qm * jnp.cos(li * dt * q), qm * jnp.sin(li * dt * q)
    apow = jnp.stack([qr, qi], axis=1).reshape(2 * S5_LEVELS, G, N).transpose(1, 2, 0)
    apow = jnp.concatenate([apow, jnp.zeros((G, N, S5_APOW_COLS - 2 * S5_LEVELS), F32)], axis=-1)
    from_z = jnp.concatenate([toep_t, binc_re_t, binc_im_t], axis=1).astype(BF16)
    from_h = jnp.concatenate([cm_re_t, cm_im_t], axis=2).astype(BF16)
    return from_z, from_h, apow


def _s5_kernel(u_ref, eye_ref, fz_ref, fh_ref, apow_ref,
               wglu_ref, bglu_ref, sng_ref, wo_ref, o_ref, ut_ref, yt_ref):
    j = pl.program_id(1)
    n_chunks = u_ref.shape[2] // S5_CHUNK
    nt = (((1,), (1,)), ((), ()))

    @pl.when(j == 0)
    def _():
        for s in range(S5_CHUNK):
            us = jnp.concatenate([u_ref[0, c, pl.ds(s, n_chunks, stride=S5_CHUNK), :]
                                  for c in range(D_S5 // LANE)], axis=1).astype(BF16)
            ut_ref[:, s * n_chunks:(s + 1) * n_chunks] = lax.dot_general(
                eye_ref[...], us, nt, preferred_element_type=F32).astype(BF16)

    lane = lax.broadcasted_iota(jnp.int32, (S5_STATE, n_chunks), 1)
    groups = range(S5_STEP_GROUPS)
    rows = [pl.ds(pl.multiple_of(j * (S5_STEP_GROUPS * S5_GROUP) + gl * S5_GROUP, S5_GROUP), S5_GROUP)
            for gl in groups]
    zt = [jnp.concatenate([ut_ref[rows[gl], s * n_chunks:(s + 1) * n_chunks] for s in range(S5_CHUNK)], axis=0)
          for gl in groups]
    w = S5_CHUNK * S5_GROUP
    zprod = [jnp.dot(fz_ref[gl], zt[gl], preferred_element_type=F32) for gl in groups]
    xr = [jnp.where(lane >= 1, pltpu.roll(zprod[gl][w:w + S5_STATE], 1, axis=1), 0.0) for gl in groups]
    xi = [jnp.where(lane >= 1, pltpu.roll(zprod[gl][w + S5_STATE:], 1, axis=1), 0.0) for gl in groups]
    for lv in range(S5_LEVELS):
        d = 1 << lv
        for gl in groups:
            ar = apow_ref[gl, :, 2 * lv:2 * lv + 1]
            ai = apow_ref[gl, :, 2 * lv + 1:2 * lv + 2]
            sr = jnp.where(lane >= d, pltpu.roll(xr[gl], d, axis=1), 0.0)
            si = jnp.where(lane >= d, pltpu.roll(xi[gl], d, axis=1), 0.0)
            xr[gl], xi[gl] = xr[gl] + ar * sr - ai * si, xi[gl] + ar * si + ai * sr
    for gl in groups:
        state = jnp.concatenate([xr[gl], xi[gl]], axis=0).astype(BF16)
        yt = zprod[gl][:w] + jnp.dot(fh_ref[gl], state, preferred_element_type=F32)
        for t in range(S5_CHUNK):
            yt_ref[rows[gl], t * n_chunks:(t + 1) * n_chunks] = yt[t * S5_GROUP:(t + 1) * S5_GROUP, :]

    @pl.when(j == pl.num_programs(1) - 1)
    def _():
        cw = 2 * n_chunks
        for cc in range(yt_ref.shape[1] // cw):
            y = yt_ref[:, cc * cw:(cc + 1) * cw]
            g = y * (0.5 * (1.0 + jnp.tanh(0.7978845608028654 * (y + 0.044715 * (y * y * y)))))
            z = jnp.dot(wglu_ref[...], g.astype(BF16), preferred_element_type=F32) + bglu_ref[...]
            out = g * jax.nn.sigmoid(z)
            out = out * lax.rsqrt(jnp.mean(out * out, axis=0, keepdims=True) + EPS) * sng_ref[...]
            mixc = jnp.dot(out.T.astype(BF16), wo_ref[...], preferred_element_type=F32)
            for sl in range(cw // n_chunks):
                s = cc * (cw // n_chunks) + sl
                for c in range(D_MODEL // LANE):
                    o_ref[0, c, pl.ds(s, n_chunks, stride=S5_CHUNK), :] = mixc[sl * n_chunks:(sl + 1) * n_chunks,
                                                                               c * LANE:(c + 1) * LANE]


def _s5(u, mats, wglu_t, bglu, sng, wo_s5):
    bsz, _, L, _ = u.shape
    assert L // S5_CHUNK == LANE, "one lane tile of chunks per sequence"
    from_z, from_h, apow = mats
    sg = S5_STEP_GROUPS
    eye = jnp.eye(D_S5, dtype=BF16)
    grp = lambda a: pl.BlockSpec((sg,) + a.shape[1:], lambda bi, j: (j, 0, 0))
    full = lambda a: pl.BlockSpec(a.shape, lambda bi, j: (0,) * a.ndim)
    return pl.pallas_call(
        _s5_kernel,
        grid=(bsz, S5_GROUPS // sg),
        in_specs=[pl.BlockSpec((1, D_S5 // LANE, L, LANE), lambda bi, j: (bi, 0, 0, 0)), full(eye),
                  grp(from_z), grp(from_h), grp(apow), full(wglu_t), full(bglu), full(sng), full(wo_s5)],
        out_specs=pl.BlockSpec((1, D_MODEL // LANE, L, LANE), lambda bi, j: (bi, 0, 0, 0)),
        out_shape=jax.ShapeDtypeStruct((bsz, D_MODEL // LANE, L, LANE), F32),
        scratch_shapes=[pltpu.VMEM((D_S5, L), BF16), pltpu.VMEM((D_S5, L), F32)],
        compiler_params=_params("arbitrary", "arbitrary"),
        name="s5",
    )(u, eye, from_z, from_h, apow, wglu_t, bglu, sng, wo_s5)


def _gla_kernel(q_ref, k_ref, v_ref, glr_ref, r_ref, wg_ref, bg_ref, ng_ref, o_ref, st_ref):
    lt = q_ref.shape[1]
    C = GLA_CHUNK

    @pl.when(pl.program_id(1) == 0)
    def _():
        st_ref[...] = jnp.zeros_like(st_ref)

    z = jnp.dot(glr_ref[0], wg_ref[...], preferred_element_type=F32) + bg_ref[...]
    log_a = (jnp.minimum(z, 0.0) - jnp.log(1.0 + jnp.exp(-jnp.abs(z)))) * (1.0 / GLA_TAU)
    ri = lax.broadcasted_iota(jnp.int32, (lt, lt), 0)
    ci = lax.broadcasted_iota(jnp.int32, (lt, lt), 1)
    tril = jnp.where(((ri >> 6) == (ci >> 6)) & (ci <= ri), 1.0, 0.0).astype(BF16)
    la_hi = log_a.astype(BF16)
    la_lo = (log_a - la_hi.astype(F32)).astype(BF16)
    bcum = (jnp.dot(tril, la_hi, preferred_element_type=F32)
            + jnp.dot(tril, la_lo, preferred_element_type=F32))
    q = q_ref[0].astype(F32) * (GLA_DK ** -0.5)
    k = k_ref[0].astype(F32)
    qi = q * jnp.exp(bcum)
    ki = k * jnp.exp(-bcum)
    lane_head = lax.broadcasted_iota(jnp.int32, (1, GLA_QK), 1) >> 6
    causal = ((lax.broadcasted_iota(jnp.int32, (GLA_HEADS * C, C), 0) & (C - 1))
              >= lax.broadcasted_iota(jnp.int32, (GLA_HEADS * C, C), 1))
    same_head = ((lax.broadcasted_iota(jnp.int32, (D_GLA, GLA_QK), 0) >> 7)
                 == (lax.broadcasted_iota(jnp.int32, (D_GLA, GLA_QK), 1) >> 6))
    nt = (((1,), (1,)), ((), ()))
    ng = ng_ref[...]
    st = st_ref[...]
    for c in range(lt // C):
        sl = slice(c * C, (c + 1) * C)
        bc = bcum[sl]
        bl = bc[C - 1:C, :]
        kd = k[sl] * jnp.exp(bl - bc)
        qic = qi[sl]
        qs = jnp.concatenate([jnp.where(lane_head == h, qic, 0.0) for h in range(GLA_HEADS)],
                             axis=0).astype(BF16)
        sc = lax.dot_general(qs, ki[sl].astype(BF16), nt, preferred_element_type=F32)
        p = jnp.where(causal, sc, 0.0).astype(BF16)
        vc = v_ref[0, sl, :]
        o_intra = jnp.concatenate(
            [jnp.dot(p[h * C:(h + 1) * C], vc[:, h * GLA_DV:(h + 1) * GLA_DV], preferred_element_type=F32)
             for h in range(GLA_HEADS)], axis=1)
        o_inter = lax.dot_general(qic.astype(BF16), st.astype(BF16), nt, preferred_element_type=F32)
        v_t = vc.astype(F32).T.astype(BF16)
        kv_t = jnp.dot(v_t, kd.astype(BF16), preferred_element_type=F32)
        st = st * jnp.exp(bl) + jnp.where(same_head, kv_t, 0.0)
        o = o_intra + o_inter
        parts = []
        for h in range(GLA_HEADS):
            oh = o[:, h * GLA_DV:(h + 1) * GLA_DV]
            oh = oh * lax.rsqrt(jnp.mean(oh * oh, axis=-1, keepdims=True) + EPS)
            parts.append(oh * ng)
        r = r_ref[0, sl, :].astype(F32)
        o_ref[0, sl, :] = (jnp.concatenate(parts, axis=1) * _silu(r)).astype(o_ref.dtype)
    st_ref[...] = st


def _gla(q, k, v, glr, r, wg, bg, ng):
    bsz, L, _ = q.shape
    lt = 256
    tok = lambda n: pl.BlockSpec((1, lt, n), lambda b, i: (b, i, 0))
    full = lambda a: pl.BlockSpec(a.shape, lambda b, i: (0,) * a.ndim)
    return pl.pallas_call(
        _gla_kernel,
        grid=(bsz, L // lt),
        in_specs=[tok(GLA_QK), tok(GLA_QK), tok(D_GLA), tok(LANE), tok(D_GLA), full(wg), full(bg), full(ng)],
        out_specs=tok(D_GLA),
        out_shape=jax.ShapeDtypeStruct((bsz, L, D_GLA), BF16),
        scratch_shapes=[pltpu.VMEM((D_GLA, GLA_QK), F32)],
        compiler_params=_params("arbitrary", "arbitrary"),
        name="gla",
    )(q, k, v, glr, r, wg, bg, ng)


def _pack_bf16_pairs(x):
    w = x.shape[1] // 2
    xr = x.astype(BF16).astype(F32)
    lo = lax.bitcast_convert_type(xr[:, :w], jnp.uint32) >> 16
    hi = lax.bitcast_convert_type(xr[:, w:], jnp.uint32) & jnp.uint32(0xFFFF0000)
    return lo | hi


def _unpack_bf16_pairs(p):
    lo = lax.bitcast_convert_type(p << 16, F32)
    hi = lax.bitcast_convert_type(p & jnp.uint32(0xFFFF0000), F32)
    return lo, hi


def _store_rows(ref, packed):
    n = packed.shape[0]
    for c in range(ROW_CHUNKS):
        ref[pl.ds(c, n, stride=ROW_CHUNKS), :] = packed[:, c * LANE:(c + 1) * LANE]


def _load_rows(ref, n):
    return jnp.concatenate([ref[pl.ds(c, n, stride=ROW_CHUNKS), :] for c in range(ROW_CHUNKS)], axis=1)


def _route_tile(lg_t, bias_col, tri, carry_ref):
    n_e, tt = lg_t.shape
    per_group = n_e // N_GROUPS
    scores = jax.nn.sigmoid(lg_t)
    biased = scores + bias_col
    row = lax.broadcasted_iota(jnp.int32, (n_e, tt), 0)
    neg = -jnp.inf
    group_score = []
    for g in range(N_GROUPS):
        b = biased[g * per_group:(g + 1) * per_group]
        r = lax.broadcasted_iota(jnp.int32, (per_group, tt), 0) + g * per_group
        m1 = jnp.max(b, axis=0, keepdims=True)
        i1 = jnp.min(jnp.where(b == m1, r, n_e), axis=0, keepdims=True)
        m2 = jnp.max(jnp.where(r == i1, neg, b), axis=0, keepdims=True)
        group_score.append(m1 + m2)
    parts = []
    for g in range(N_GROUPS):
        ahead = jnp.zeros((1, tt), jnp.int32)
        for g2 in range(N_GROUPS):
            if g2 != g:
                beats = (group_score[g2] >= group_score[g]) if g2 < g else (group_score[g2] > group_score[g])
                ahead = ahead + beats.astype(jnp.int32)
        parts.append(jnp.where(ahead < TOPK_GROUPS, biased[g * per_group:(g + 1) * per_group], neg))
    masked = jnp.concatenate(parts, axis=0)
    work = masked
    idxs = []
    for _ in range(TOP_K):
        m = jnp.max(work, axis=0, keepdims=True)
        ii = jnp.min(jnp.where(work == m, row, n_e), axis=0, keepdims=True)
        idxs.append(ii)
        work = jnp.where(row == ii, neg, work)
    sel = work != masked
    w = jnp.where(sel, scores, 0.0)
    gate_dense = w / jnp.sum(w, axis=0, keepdims=True) * ROUTED_SCALE
    mt = jnp.where(sel, 1.0, 0.0)
    rank_dense = jnp.dot(mt.astype(BF16), tri, preferred_element_type=F32) + carry_ref[...]
    carry_ref[...] += jnp.sum(mt, axis=1, keepdims=True)
    ranks, gts = [], []
    for ii in idxs:
        oh = row == ii
        ranks.append(jnp.sum(jnp.where(oh, rank_dense, 0.0), axis=0, keepdims=True))
        gts.append(jnp.sum(jnp.where(oh, gate_dense, 0.0), axis=0, keepdims=True))
    idx_t = jnp.concatenate(idxs, axis=0)
    rank_t = jnp.concatenate(ranks, axis=0).astype(jnp.int32)
    gate_t = jnp.concatenate(gts + [jnp.zeros((LANE - TOP_K, tt), F32)], axis=0)
    return idx_t, rank_t, gate_t.T[:, :TOP_K]


def _mix_kernel(ms_ref, yg_ref, x_ref, wo_ref, gt1_ref,
                gn_ref, sh2_ref, sc2_ref, gt2_ref, wrt_ref, rb_ref, tri_ref, wsg_ref, wsu_ref, wsd_ref,
                x1_ref, h2_ref, idx_ref, rank_ref, gate_ref, cnt_ref, carry_ref):
    @pl.when((pl.program_id(0) == 0) & (pl.program_id(1) == 0))
    def _():
        carry_ref[...] = jnp.zeros_like(carry_ref)

    mix_s5 = jnp.concatenate([ms_ref[0, c] for c in range(D_MODEL // LANE)], axis=1)
    mix = mix_s5 + jnp.dot(yg_ref[0], wo_ref[...], preferred_element_type=F32)
    x1 = x_ref[0] + gt1_ref[0] * mix
    h2 = x1 * lax.rsqrt(jnp.mean(x1 * x1, axis=-1, keepdims=True) + EPS) * gn_ref[...]
    h2 = h2 * (1.0 + sc2_ref[0]) + sh2_ref[0]
    hb = h2.astype(BF16)
    _store_rows(h2_ref, _pack_bf16_pairs(h2))
    lg_t = lax.dot_general(wrt_ref[...], hb, (((1,), (1,)), ((), ())), preferred_element_type=F32)
    idx_t, rank_t, gates = _route_tile(lg_t, rb_ref[...], tri_ref[...], carry_ref)
    idx_ref[...] = idx_t
    rank_ref[...] = rank_t
    gate_ref[...] = gates
    cnt_ref[...] = carry_ref[...]
    a = _silu(jnp.dot(hb, wsg_ref[...], preferred_element_type=F32)) * jnp.dot(
        hb, wsu_ref[...], preferred_element_type=F32)
    shared = jnp.dot(a.astype(BF16), wsd_ref[...], preferred_element_type=F32)
    x1_ref[0] = x1 + gt2_ref[0] * shared


def _mix(ms5, yg, x, wo, gt1, gn, sh2, sc2, gt2, wrt, rb, wsg, wsu, wsd):
    bsz, L, d = x.shape
    tt = 512
    nt = L // tt
    T = bsz * L
    tri = (jnp.arange(tt)[:, None] < jnp.arange(tt)[None, :]).astype(BF16)
    tok = lambda n: pl.BlockSpec((1, tt, n), lambda b, i: (b, i, 0))
    vec = pl.BlockSpec((1, 1, d), lambda b, i: (b, 0, 0))
    full = lambda a: pl.BlockSpec(a.shape, lambda b, i: (0,) * a.ndim)
    lanes = pl.BlockSpec((TOP_K, tt), lambda b, i: (0, b * nt + i))
    return pl.pallas_call(
        _mix_kernel,
        grid=(bsz, nt),
        in_specs=[pl.BlockSpec((1, d // LANE, tt, LANE), lambda b, i: (b, 0, i, 0)), tok(D_GLA), tok(d), full(wo), vec,
                  full(gn), vec, vec, vec, full(wrt), full(rb), full(tri), full(wsg), full(wsu), full(wsd)],
        out_specs=[tok(d),
                   pl.BlockSpec((tt * ROW_CHUNKS, LANE), lambda b, i: (b * nt + i, 0)),
                   lanes, lanes,
                   pl.BlockSpec((tt, TOP_K), lambda b, i: (b * nt + i, 0)),
                   pl.BlockSpec((N_EXPERTS, 1), lambda b, i: (0, 0))],
        out_shape=[jax.ShapeDtypeStruct((bsz, L, d), F32),
                   jax.ShapeDtypeStruct((T * ROW_CHUNKS, LANE), jnp.uint32),
                   jax.ShapeDtypeStruct((TOP_K, T), jnp.int32),
                   jax.ShapeDtypeStruct((TOP_K, T), jnp.int32),
                   jax.ShapeDtypeStruct((T, TOP_K), F32),
                   jax.ShapeDtypeStruct((N_EXPERTS, 1), F32)],
        scratch_shapes=[pltpu.VMEM((N_EXPERTS, 1), F32)],
        compiler_params=_params("arbitrary", "arbitrary"),
        name="mix",
    )(ms5, yg, x, wo, gt1, gn, sh2, sc2, gt2, wrt, rb, tri, wsg, wsu, wsd)


def _pos_kernel(idx_ref, rank_ref, ps_ref, pos_ref):
    n_e = ps_ref.shape[0]
    tt = idx_ref.shape[1]
    row = lax.broadcasted_iota(jnp.int32, (n_e, tt), 0)
    ps = ps_ref[...]
    starts = [jnp.sum(jnp.where(row == idx_ref[k:k + 1, :], ps, 0.0), axis=0, keepdims=True)
              for k in range(TOP_K)]
    pos = jnp.concatenate(starts, axis=0).astype(jnp.int32) + rank_ref[...]
    for jh in range(tt // LANE):
        pos_ref[jh * TOP_K:(jh + 1) * TOP_K, :] = pos[:, jh * LANE:(jh + 1) * LANE]


def _pos(idx_t, rank_t, pstart):
    T = idx_t.shape[1]
    tt = 2048
    blk = pl.BlockSpec((TOP_K, tt), lambda i: (0, i))
    return pl.pallas_call(
        _pos_kernel,
        grid=(T // tt,),
        in_specs=[blk, blk, pl.BlockSpec((N_EXPERTS, 1), lambda i: (0, 0))],
        out_specs=pl.BlockSpec((tt // LANE * TOP_K, LANE), lambda i: (i, 0)),
        out_shape=jax.ShapeDtypeStruct((T // LANE * TOP_K, LANE), jnp.int32),
        compiler_params=_params("arbitrary"),
        name="pos",
    )(idx_t, rank_t, pstart.astype(F32).reshape(N_EXPERTS, 1))


def _dispatch_kernel(pend_ref, padded_ref, pos_ref, h_ref, xs_hbm, zero_ref, pos_smem, sem_pos, sem_zero, sem_row):
    tt = h_ref.shape[0]
    step = pl.program_id(0)
    load_pos = pltpu.make_async_copy(pos_ref, pos_smem, sem_pos)
    load_pos.start()

    def zero_block(start):
        return pltpu.make_async_copy(zero_ref, xs_hbm.at[pl.ds(pl.multiple_of(start, MOE_BLOCK), MOE_BLOCK)],
                                     sem_zero)

    def for_each_zero_block(fn):
        def per_expert(e, c):
            @pl.when(padded_ref[e] > 0)
            def _():
                fn(zero_block(pend_ref[e] - MOE_BLOCK))
            return c

        def per_tail_block(g, c):
            fn(zero_block(g * MOE_BLOCK))
            return c

        lax.fori_loop(0, N_EXPERTS, per_expert, 0)
        lax.fori_loop(pend_ref[N_EXPERTS - 1] // MOE_BLOCK, xs_hbm.shape[0] // MOE_BLOCK, per_tail_block, 0)

    @pl.when(step == 0)
    def _():
        zero_ref[...] = jnp.zeros_like(zero_ref)
        for_each_zero_block(lambda cp: cp.start())
        for_each_zero_block(lambda cp: cp.wait())

    load_pos.wait()
    for jh in range(tt // LANE):
        def issue(jl, c, jh=jh):
            for k in range(TOP_K):
                slot = pos_smem[(jh * TOP_K + k) * LANE + jl]
                pltpu.make_async_copy(h_ref.at[jh * LANE + jl], xs_hbm.at[slot], sem_row).start(priority=k % 2)
            return c

        lax.fori_loop(0, LANE, issue, 0, unroll=4)
    for k in range(TOP_K):
        pltpu.make_async_copy(h_ref, xs_hbm.at[pl.ds(0, tt)], sem_row).wait()


def _dispatch(pends, padded, pos_t, h_rows, n_slots):
    T = h_rows.shape[0]
    tt = 512
    return pl.pallas_call(
        _dispatch_kernel,
        grid_spec=pltpu.PrefetchScalarGridSpec(
            num_scalar_prefetch=2,
            grid=(T // tt,),
            in_specs=[pl.BlockSpec((tt * TOP_K,), lambda i, pe, pa: (i,)),
                      pl.BlockSpec((tt, ROW_CHUNKS, LANE), lambda i, pe, pa: (i, 0, 0))],
            out_specs=pl.BlockSpec(memory_space=pl.ANY),
            scratch_shapes=[pltpu.VMEM((MOE_BLOCK, ROW_CHUNKS, LANE), jnp.uint32),
                            pltpu.SMEM((tt * TOP_K,), jnp.int32),
                            pltpu.SemaphoreType.DMA, pltpu.SemaphoreType.DMA, pltpu.SemaphoreType.DMA]),
        out_shape=jax.ShapeDtypeStruct((n_slots, ROW_CHUNKS, LANE), jnp.uint32),
        compiler_params=_params("arbitrary"),
        name="dispatch",
    )(pends, padded, pos_t, h_rows)


X_GROUP = 8
X_AHEAD = 8
X_BUFS = X_AHEAD + X_GROUP
Y_BUFS = X_GROUP


def _expert_kernel(b0_ref, nb_ref, nv_ref, wg_ref, wu_ref, wd_ref, xs_hbm, ye_hbm,
                   wg_s, wu_s, wd_s, xbuf, ybuf, semx, semy, *, n_blocks):
    e = pl.program_id(0)
    nb = nb_ref[e]
    b0 = b0_ref[e]
    nv = nv_ref[0]
    rows = MOE_BLOCK * ROW_CHUNKS

    def block(ref, g):
        return ref.at[pl.ds(pl.multiple_of(g * rows, rows), rows)]

    def x_copy(g):
        slot = g % X_BUFS
        return pltpu.make_async_copy(block(xs_hbm, g), xbuf.at[slot], semx.at[slot])

    def y_copy(g):
        slot = g % Y_BUFS
        return pltpu.make_async_copy(ybuf.at[slot], block(ye_hbm, g), semy.at[slot])

    @pl.when(e == 0)
    def _():
        for g in range(X_AHEAD):
            x_copy(g).start()

    @pl.when(nb > 0)
    def _():
        wg_s[...] = wg_ref[0].astype(BF16)
        wu_s[...] = wu_ref[0].astype(BF16)
        wd_s[...] = wd_ref[0].astype(BF16)

    def fetch(g):
        x_copy(g).wait()

        @pl.when(g + X_AHEAD < nv)
        def _():
            x_copy(g + X_AHEAD).start()

    def load_x(g):
        lo, hi = _unpack_bf16_pairs(_load_rows(xbuf.at[g % X_BUFS], MOE_BLOCK))
        return jnp.concatenate([lo, hi], axis=1).astype(BF16)

    def emit(g, y):
        @pl.when(g >= Y_BUFS)
        def _():
            y_copy(g - Y_BUFS).wait()

        _store_rows(ybuf.at[g % Y_BUFS], y)
        y_copy(g).start()

    def run(g, n):
        for b in range(n):
            fetch(g + b)
        x = jnp.concatenate([load_x(g + b) for b in range(n)], axis=0) if n > 1 else load_x(g)
        a = jnp.dot(x, wg_s[...], preferred_element_type=F32)
        u = jnp.dot(x, wu_s[...], preferred_element_type=F32)
        h = (_silu(a) * u).astype(BF16)
        y = _pack_bf16_pairs(jnp.dot(h, wd_s[...], preferred_element_type=F32))
        for b in range(n):
            emit(g + b, y[b * MOE_BLOCK:(b + 1) * MOE_BLOCK])

    def group(i, c):
        run(b0 + X_GROUP * i, X_GROUP)
        return c

    lax.fori_loop(0, nb // X_GROUP, group, 0)
    done = nb // X_GROUP * X_GROUP
    n = X_GROUP // 2
    while n >= 1:
        @pl.when((nb & n) != 0)
        def _(n=n, done=done):
            run(b0 + done, n)

        done = done + (nb & n)
        n //= 2

    @pl.when(e == pl.num_programs(0) - 1)
    def _():
        for back in range(Y_BUFS, 0, -1):
            y_copy(nv - back).wait()
        ybuf[0] = jnp.zeros(ybuf.shape[1:], ybuf.dtype)

        def fill(g):
            return pltpu.make_async_copy(ybuf.at[0], block(ye_hbm, g), semy.at[0])

        lax.fori_loop(nv, n_blocks, lambda g, c: (fill(g).start(), c)[1], 0)
        lax.fori_loop(nv, n_blocks, lambda g, c: (fill(g).wait(), c)[1], 0)


def _experts(first_block, num_blocks, n_valid, xs, wg, wu, wd):
    rows = MOE_BLOCK * ROW_CHUNKS
    n_blocks = xs.shape[0] // rows
    n_e, d, _ = wg.shape
    wsel = lambda e, b0, nb, nv: (e, 0, 0)
    return pl.pallas_call(
        functools.partial(_expert_kernel, n_blocks=n_blocks),
        grid_spec=pltpu.PrefetchScalarGridSpec(
            num_scalar_prefetch=3,
            grid=(n_e,),
            in_specs=[pl.BlockSpec((1, d, D_EXPERT), wsel),
                      pl.BlockSpec((1, d, D_EXPERT), wsel),
                      pl.BlockSpec((1, D_EXPERT, d), wsel),
                      pl.BlockSpec(memory_space=pl.ANY)],
            out_specs=pl.BlockSpec(memory_space=pl.ANY),
            scratch_shapes=[pltpu.VMEM((d, D_EXPERT), BF16), pltpu.VMEM((d, D_EXPERT), BF16),
                            pltpu.VMEM((D_EXPERT, d), BF16),
                            pltpu.VMEM((X_BUFS, rows, LANE), jnp.uint32),
                            pltpu.VMEM((Y_BUFS, rows, LANE), jnp.uint32),
                            pltpu.SemaphoreType.DMA((X_BUFS,)), pltpu.SemaphoreType.DMA((Y_BUFS,))]),
        out_shape=jax.ShapeDtypeStruct(xs.shape, jnp.uint32),
        compiler_params=_params("arbitrary"),
        name="experts",
    )(first_block, num_blocks, n_valid, wg, wu, wd, xs)


def _final_kernel(pos0_ref, posn_ref, gate_ref, x_ref, gt_ref, g_ref, ye_hbm, o_ref,
                  buf_ref, pos_smem, sem_pos, sem_row):
    tt = x_ref.shape[1]
    i = pl.program_id(0)

    def gather_tile(pos_ref, tile):
        base = (tile % 2) * (TOP_K * tt)
        sem = sem_row.at[tile % 2]
        load_pos = pltpu.make_async_copy(pos_ref, pos_smem, sem_pos)
        load_pos.start()
        load_pos.wait()
        for jh in range(tt // LANE):
            def issue(jl, c, jh=jh):
                for k in range(TOP_K):
                    slot = pos_smem[(jh * TOP_K + k) * LANE + jl]
                    pltpu.make_async_copy(ye_hbm.at[slot], buf_ref.at[base + k * tt + jh * LANE + jl],
                                          sem).start(priority=k % 2)
                return c

            lax.fori_loop(0, LANE, issue, 0, unroll=4)

    @pl.when(i == 0)
    def _():
        gather_tile(pos0_ref, i)

    @pl.when(i + 1 < pl.num_programs(0))
    def _():
        gather_tile(posn_ref, i + 1)

    base = pl.multiple_of((i % 2) * (TOP_K * tt), TOP_K * tt)
    for k in range(TOP_K):
        pltpu.make_async_copy(ye_hbm.at[pl.ds(0, tt)], buf_ref.at[pl.ds(base + k * tt, tt)],
                              sem_row.at[i % 2]).wait()
    gates = gate_ref[...]
    half = D_MODEL // 2
    acc_lo = jnp.zeros((tt, half), F32)
    acc_hi = jnp.zeros((tt, half), F32)
    rows_2d = buf_ref.reshape(buf_ref.shape[0] * ROW_CHUNKS, LANE)
    for k in range(TOP_K):
        lo, hi = _unpack_bf16_pairs(_load_rows(rows_2d.at[pl.ds((base + k * tt) * ROW_CHUNKS, tt * ROW_CHUNKS)], tt))
        gk = gates[:, k:k + 1]
        acc_lo += gk * lo
        acc_hi += gk * hi
    x = x_ref[0] + gt_ref[0] * jnp.concatenate([acc_lo, acc_hi], axis=1)
    o_ref[0] = x * lax.rsqrt(jnp.mean(x * x, axis=-1, keepdims=True) + EPS) * g_ref[...]


def _final(pos_t, gates, x1, gt2, g, ye_rows):
    bsz, L, d = x1.shape
    tt = 256
    nt = L // tt
    n = bsz * nt
    tok = pl.BlockSpec((1, tt, d), lambda i: (i // nt, i % nt, 0))
    return pl.pallas_call(
        _final_kernel,
        grid=(n,),
        in_specs=[pl.BlockSpec((tt * TOP_K,), lambda i: (0,)),
                  pl.BlockSpec((tt * TOP_K,), lambda i: (jnp.minimum(i + 1, n - 1),)),
                  pl.BlockSpec((tt, TOP_K), lambda i: (i, 0)),
                  tok, pl.BlockSpec((1, 1, d), lambda i: (i // nt, 0, 0)),
                  pl.BlockSpec((1, d), lambda i: (0, 0)),
                  pl.BlockSpec(memory_space=pl.ANY)],
        out_specs=tok,
        out_shape=jax.ShapeDtypeStruct((bsz, L, d), F32),
        scratch_shapes=[pltpu.VMEM((2 * TOP_K * tt, ROW_CHUNKS, LANE), jnp.uint32),
                        pltpu.SMEM((tt * TOP_K,), jnp.int32),
                        pltpu.SemaphoreType.DMA, pltpu.SemaphoreType.DMA((2,))],
        compiler_params=_params("arbitrary"),
        name="final",
    )(pos_t, pos_t, gates, x1, gt2, g.reshape(1, d), ye_rows)


def _block_table(counts):
    counts = counts.astype(jnp.int32)
    padded = (counts + MOE_BLOCK - 1) // MOE_BLOCK * MOE_BLOCK
    pends = jnp.cumsum(padded)
    pstarts = pends - padded
    n_valid = (pends[-1] // MOE_BLOCK).reshape(1)
    return padded, pends, pstarts, n_valid


def kernel(x, c, w_ada, b_ada, g_norm_mix, w_in, s5_lambda_re, s5_lambda_im, s5_log_dt, s5_b_re, s5_b_im, s5_c_re, s5_c_im, s5_d, s5_w_glu, s5_b_glu, s5_norm_g, gla_w_g2, gla_b_g2, gla_norm_g, w_out, g_norm_moe, w_router, router_bias, exp_w_gate, exp_w_up, exp_w_down, sh_w_gate, sh_w_up, sh_w_down, g_final):
    bsz, L, d = x.shape
    T = bsz * L
    assert w_ada.shape[0] == 1, "single-layer block"
    for l in range(1):
        mod = _ada(c, w_ada[l], b_ada[l])
        sh1, sc1, gt1, sh2, sc2, gt2 = [m.reshape(bsz, 1, d) for m in jnp.split(mod, 6, axis=-1)]

        wi = w_in[l]
        o_q, o_k, o_v, o_g, o_r = D_S5, D_S5 + GLA_QK, D_S5 + 2 * GLA_QK, D_S5 + 2 * GLA_QK + D_GLA, \
            D_S5 + 2 * GLA_QK + D_GLA + GLA_LOWRANK
        w_cat = jnp.concatenate([wi[:, :o_g], wi[:, o_r:], wi[:, o_g:o_r],
                                 jnp.zeros((d, LANE - GLA_LOWRANK), wi.dtype)], axis=1).astype(BF16)
        u, q, k, v, r, glr = _inproj(x, g_norm_mix[l], sh1, sc1, w_cat)

        mats = _s5_prep(s5_lambda_re[l], s5_lambda_im[l], s5_log_dt[l], s5_b_re[l], s5_b_im[l],
                        s5_c_re[l], s5_c_im[l], s5_d[l])
        ms5 = _s5(u, mats, s5_w_glu[l].T.astype(BF16), s5_b_glu[l].reshape(D_S5, 1),
                  s5_norm_g[l].reshape(D_S5, 1), w_out[l][:D_S5].astype(BF16))

        wg2 = jnp.concatenate([gla_w_g2[l], jnp.zeros((LANE - GLA_LOWRANK, GLA_QK), F32)], axis=0).astype(BF16)
        yg = _gla(q, k, v, glr, r, wg2, gla_b_g2[l].reshape(1, GLA_QK), gla_norm_g[l].reshape(1, GLA_DV))

        x1, h2p, idx_t, rank_t, gates, counts = _mix(
            ms5, yg, x, w_out[l][D_S5:].astype(BF16), gt1, g_norm_moe[l].reshape(1, d), sh2, sc2, gt2,
            w_router[l].T.astype(BF16), router_bias[l].reshape(N_EXPERTS, 1),
            sh_w_gate[l].astype(BF16), sh_w_up[l].astype(BF16), sh_w_down[l].astype(BF16))

        n_blocks = (T * TOP_K + N_EXPERTS * (MOE_BLOCK - 1) + MOE_BLOCK - 1) // MOE_BLOCK
        n_slots = n_blocks * MOE_BLOCK
        padded, pends, pstarts, n_valid = _block_table(counts[:, 0])
        pos_t = _pos(idx_t, rank_t, pstarts).reshape(-1)
        xs = _dispatch(pends, padded, pos_t, h2p.reshape(T, ROW_CHUNKS, LANE), n_slots)
        ye = _experts(pstarts // MOE_BLOCK, padded // MOE_BLOCK, n_valid,
                      xs.reshape(n_slots * ROW_CHUNKS, LANE), exp_w_gate[l], exp_w_up[l], exp_w_down[l])
    return _final(pos_t, gates, x1, gt2, g_final, ye.reshape(n_slots, ROW_CHUNKS, LANE))
```

```python
import functools

import jax
import jax.numpy as jnp
from jax import lax
from jax.experimental import pallas as pl
from jax.experimental.pallas import tpu as pltpu

F32 = jnp.float32
BF16 = jnp.bfloat16

D_MODEL = 1024
D_S5 = 512
S5_GROUP = 16
S5_GROUPS = 32
S5_STATE = 64
S5_CHUNK = 16
S5_STEP_GROUPS = 8
S5_LEVELS = 7
S5_APOW_COLS = 16
D_GLA = 512
GLA_HEADS = 4
GLA_DK = 64
GLA_DV = 128
GLA_QK = 256
GLA_LOWRANK = 16
GLA_TAU = 16.0
GLA_CHUNK = 64
LANE = 128
N_EXPERTS = 256
TOP_K = 8
N_GROUPS = 8
TOPK_GROUPS = 4
D_EXPERT = 256
ROUTED_SCALE = 2.5
EPS = 1e-6
MOE_BLOCK = 128
ROW_CHUNKS = D_MODEL // 2 // LANE
VMEM_LIMIT = 48 * 1024 * 1024


def _silu(x):
    return x * jax.nn.sigmoid(x)


def _params(*sem):
    return pltpu.CompilerParams(dimension_semantics=sem, vmem_limit_bytes=VMEM_LIMIT)


def _ada_kernel(c_ref, w_ref, b_ref, o_ref):
    s = _silu(c_ref[...]).astype(BF16)
    o_ref[...] = jnp.dot(s, w_ref[...].astype(BF16), preferred_element_type=F32) + b_ref[...]


def _ada(c, w, b):
    bsz, d = c.shape
    n = w.shape[1]
    tn = 1024
    return pl.pallas_call(
        _ada_kernel,
        grid=(n // tn,),
        in_specs=[pl.BlockSpec((bsz, d), lambda j: (0, 0)),
                  pl.BlockSpec((d, tn), lambda j: (0, j)),
                  pl.BlockSpec((1, tn), lambda j: (0, j))],
        out_specs=pl.BlockSpec((bsz, tn), lambda j: (0, j)),
        out_shape=jax.ShapeDtypeStruct((bsz, n), F32),
        compiler_params=_params("arbitrary"),
        name="ada",
    )(c, w, b.reshape(1, n))


def _inproj_kernel(x_ref, g_ref, sh_ref, sc_ref, w_ref,
                   u_ref, q_ref, k_ref, v_ref, r_ref, glr_ref):
    x = x_ref[0]
    ms = jnp.mean(x * x, axis=-1, keepdims=True)
    h = (x * lax.rsqrt(ms + EPS)) * g_ref[...]
    h = h * (1.0 + sc_ref[0]) + sh_ref[0]
    hb = h.astype(BF16)
    u = jnp.dot(hb, w_ref[:, 0:D_S5], preferred_element_type=F32)
    for c in range(D_S5 // LANE):
        u_ref[0, c] = u[:, c * LANE:(c + 1) * LANE]
    col = D_S5
    for ref in (q_ref, k_ref, v_ref, r_ref, glr_ref):
        n = ref.shape[-1]
        ref[0] = jnp.dot(hb, w_ref[:, col:col + n], preferred_element_type=F32).astype(ref.dtype)
        col += n


def _inproj(x, g, sh, sc, w):
    bsz, L, d = x.shape
    tt = 512
    widths = (GLA_QK, GLA_QK, D_GLA, D_GLA, LANE)
    tok = lambda n: pl.BlockSpec((1, tt, n), lambda b, i: (b, i, 0))
    vec = pl.BlockSpec((1, 1, d), lambda b, i: (b, 0, 0))
    return pl.pallas_call(
        _inproj_kernel,
        grid=(bsz, L // tt),
        in_specs=[tok(d), pl.BlockSpec((1, d), lambda b, i: (0, 0)), vec, vec,
                  pl.BlockSpec(w.shape, lambda b, i: (0, 0))],
        out_specs=[pl.BlockSpec((1, D_S5 // LANE, tt, LANE), lambda b, i: (b, 0, i, 0))] + [tok(n) for n in widths],
        out_shape=[jax.ShapeDtypeStruct((bsz, D_S5 // LANE, L, LANE), F32)]
        + [jax.ShapeDtypeStruct((bsz, L, n), BF16) for n in widths],
        compiler_params=_params("arbitrary", "arbitrary"),
        name="inproj",
    )(x, g.reshape(1, d), sh, sc, w)


def _s5_prep(lam_re, lam_im, log_dt, b_re, b_im, c_re, c_im, d_skip):
    G, N, C, TC = S5_GROUPS, S5_STATE, S5_GROUP, S5_CHUNK
    hp = lax.Precision.HIGHEST
    dt = jnp.exp(log_dt)[:, None]
    lr, li = lam_re, lam_im
    mag = jnp.exp(lr * dt)
    ab_re, ab_im = mag * jnp.cos(li * dt), mag * jnp.sin(li * dt)
    den = lr * lr + li * li
    nr = ab_re - 1.0
    coef_re = ((nr * lr + ab_im * li) / den)[..., None]
    coef_im = ((ab_im * lr - nr * li) / den)[..., None]
    bb_re = coef_re * b_re - coef_im * b_im
    bb_im = coef_re * b_im + coef_im * b_re
    p = jnp.arange(TC + 1, dtype=F32)[:, None, None]
    pm = jnp.exp(lr * dt * p)
    pr, pi = pm * jnp.cos(li * dt * p), pm * jnp.sin(li * dt * p)
    ca_re = c_re[None] * pr[:, :, None, :] - c_im[None] * pi[:, :, None, :]
    ca_im = c_re[None] * pi[:, :, None, :] + c_im[None] * pr[:, :, None, :]
    ca = jnp.concatenate([ca_re[:TC], -ca_im[:TC]], axis=-1).transpose(1, 0, 2, 3).reshape(G, TC * C, 2 * N)
    kern = jnp.einsum('gxk,gki->gxi', ca, jnp.concatenate([bb_re, bb_im], axis=1),
                      precision=hp).reshape(G, TC, C, C)
    skip = jnp.eye(C, dtype=F32)[None] * d_skip[:, :, None]
    kern = kern + jnp.concatenate([skip[:, None], jnp.zeros((G, TC - 1, C, C), F32)], axis=1)
    rev = kern[:, ::-1].transpose(0, 2, 1, 3).reshape(G, C, TC * C)
    rev = jnp.concatenate([rev, jnp.zeros((G, C, (TC - 1) * C), F32)], axis=-1)
    toep_t = jnp.stack([rev[:, :, (TC - 1 - t) * C:(TC - 1 - t) * C + TC * C] for t in range(TC)],
                       axis=1).reshape(G, TC * C, TC * C)
    rr, ri = pr[TC - 1 - jnp.arange(TC)], pi[TC - 1 - jnp.arange(TC)]
    binc_re = rr[..., None] * bb_re[None] - ri[..., None] * bb_im[None]
    binc_im = rr[..., None] * bb_im[None] + ri[..., None] * bb_re[None]
    binc_re_t = binc_re.transpose(1, 2, 0, 3).reshape(G, N, TC * C)
    binc_im_t = binc_im.transpose(1, 2, 0, 3).reshape(G, N, TC * C)
    cm_re_t = ca_re[1:].transpose(1, 0, 2, 3).reshape(G, TC * C, N)
    cm_im_t = (-ca_im[1:]).transpose(1, 0, 2, 3).reshape(G, TC * C, N)
    q = (TC * 2.0 ** jnp.arange(S5_LEVELS, dtype=F32))[:, None, None]
    qm = jnp.exp(lr * dt * q)
    qr, qi = qm * jnp.cos(li * dt * q), qm * jnp.sin(li * dt * q)
    apow = jnp.stack([qr, qi], axis=1).reshape(2 * S5_LEVELS, G, N).transpose(1, 2, 0)
    apow = jnp.concatenate([apow, jnp.zeros((G, N, S5_APOW_COLS - 2 * S5_LEVELS), F32)], axis=-1)
    from_z = jnp.concatenate([toep_t, binc_re_t, binc_im_t], axis=1).astype(BF16)
    from_h = jnp.concatenate([cm_re_t, cm_im_t], axis=2).astype(BF16)
    return from_z, from_h, apow


def _s5_kernel(u_ref, eye_ref, fz_ref, fh_ref, apow_ref,
               wglu_ref, bglu_ref, sng_ref, wo_ref, o_ref, ut_ref, yt_ref):
    j = pl.program_id(1)
    n_chunks = u_ref.shape[2] // S5_CHUNK
    nt = (((1,), (1,)), ((), ()))

    @pl.when(j == 0)
    def _():
        for s in range(S5_CHUNK):
            us = jnp.concatenate([u_ref[0, c, pl.ds(s, n_chunks, stride=S5_CHUNK), :]
                                  for c in range(D_S5 // LANE)], axis=1).astype(BF16)
            ut_ref[:, s * n_chunks:(s + 1) * n_chunks] = lax.dot_general(
                eye_ref[...], us, nt, preferred_element_type=F32).astype(BF16)

    lane = lax.broadcasted_iota(jnp.int32, (S5_STATE, n_chunks), 1)
    groups = range(S5_STEP_GROUPS)
    rows = [pl.ds(pl.multiple_of(j * (S5_STEP_GROUPS * S5_GROUP) + gl * S5_GROUP, S5_GROUP), S5_GROUP)
            for gl in groups]
    zt = [jnp.concatenate([ut_ref[rows[gl], s * n_chunks:(s + 1) * n_chunks] for s in range(S5_CHUNK)], axis=0)
          for gl in groups]
    w = S5_CHUNK * S5_GROUP
    zprod = [jnp.dot(fz_ref[gl], zt[gl], preferred_element_type=F32) for gl in groups]
    xr = [jnp.where(lane >= 1, pltpu.roll(zprod[gl][w:w + S5_STATE], 1, axis=1), 0.0) for gl in groups]
    xi = [jnp.where(lane >= 1, pltpu.roll(zprod[gl][w + S5_STATE:], 1, axis=1), 0.0) for gl in groups]
    for lv in range(S5_LEVELS):
        d = 1 << lv
        for gl in groups:
            ar = apow_ref[gl, :, 2 * lv:2 * lv + 1]
            ai = apow_ref[gl, :, 2 * lv + 1:2 * lv + 2]
            sr = jnp.where(lane >= d, pltpu.roll(xr[gl], d, axis=1), 0.0)
            si = jnp.where(lane >= d, pltpu.roll(xi[gl], d, axis=1), 0.0)
            xr[gl], xi[gl] = xr[gl] + ar * sr - ai * si, xi[gl] + ar * si + ai * sr
    for gl in groups:
        state = jnp.concatenate([xr[gl], xi[gl]], axis=0).astype(BF16)
        yt = zprod[gl][:w] + jnp.dot(fh_ref[gl], state, preferred_element_type=F32)
        for t in range(S5_CHUNK):
            yt_ref[rows[gl], t * n_chunks:(t + 1) * n_chunks] = yt[t * S5_GROUP:(t + 1) * S5_GROUP, :]

    @pl.when(j == pl.num_programs(1) - 1)
    def _():
        cw = 2 * n_chunks
        for cc in range(yt_ref.shape[1] // cw):
            y = yt_ref[:, cc * cw:(cc + 1) * cw]
            g = y * (0.5 * (1.0 + jnp.tanh(0.7978845608028654 * (y + 0.044715 * (y * y * y)))))
            z = jnp.dot(wglu_ref[...], g.astype(BF16), preferred_element_type=F32) + bglu_ref[...]
            out = g * jax.nn.sigmoid(z)
            out = out * lax.rsqrt(jnp.mean(out * out, axis=0, keepdims=True) + EPS) * sng_ref[...]
            mixc = jnp.dot(out.T.astype(BF16), wo_ref[...], preferred_element_type=F32)
            for sl in range(cw // n_chunks):
                s = cc * (cw // n_chunks) + sl
                for c in range(D_MODEL // LANE):
                    o_ref[0, c, pl.ds(s, n_chunks, stride=S5_CHUNK), :] = mixc[sl * n_chunks:(sl + 1) * n_chunks,
                                                                               c * LANE:(c + 1) * LANE]


def _s5(u, mats, wglu_t, bglu, sng, wo_s5):
    bsz, _, L, _ = u.shape
    assert L // S5_CHUNK == LANE, "one lane tile of chunks per sequence"
    from_z, from_h, apow = mats
    sg = S5_STEP_GROUPS
    eye = jnp.eye(D_S5, dtype=BF16)
    grp = lambda a: pl.BlockSpec((sg,) + a.shape[1:], lambda bi, j: (j, 0, 0))
    full = lambda a: pl.BlockSpec(a.shape, lambda bi, j: (0,) * a.ndim)
    return pl.pallas_call(
        _s5_kernel,
        grid=(bsz, S5_GROUPS // sg),
        in_specs=[pl.BlockSpec((1, D_S5 // LANE, L, LANE), lambda bi, j: (bi, 0, 0, 0)), full(eye),
                  grp(from_z), grp(from_h), grp(apow), full(wglu_t), full(bglu), full(sng), full(wo_s5)],
        out_specs=pl.BlockSpec((1, D_MODEL // LANE, L, LANE), lambda bi, j: (bi, 0, 0, 0)),
        out_shape=jax.ShapeDtypeStruct((bsz, D_MODEL // LANE, L, LANE), F32),
        scratch_shapes=[pltpu.VMEM((D_S5, L), BF16), pltpu.VMEM((D_S5, L), F32)],
        compiler_params=_params("arbitrary", "arbitrary"),
        name="s5",
    )(u, eye, from_z, from_h, apow, wglu_t, bglu, sng, wo_s5)


def _gla_kernel(q_ref, k_ref, v_ref, glr_ref, r_ref, wg_ref, bg_ref, ng_ref, o_ref, st_ref):
    lt = q_ref.shape[1]
    C = GLA_CHUNK

    @pl.when(pl.program_id(1) == 0)
    def _():
        st_ref[...] = jnp.zeros_like(st_ref)

    z = jnp.dot(glr_ref[0], wg_ref[...], preferred_element_type=F32) + bg_ref[...]
    log_a = (jnp.minimum(z, 0.0) - jnp.log(1.0 + jnp.exp(-jnp.abs(z)))) * (1.0 / GLA_TAU)
    ri = lax.broadcasted_iota(jnp.int32, (lt, lt), 0)
    ci = lax.broadcasted_iota(jnp.int32, (lt, lt), 1)
    tril = jnp.where(((ri >> 6) == (ci >> 6)) & (ci <= ri), 1.0, 0.0).astype(BF16)
    la_hi = log_a.astype(BF16)
    la_lo = (log_a - la_hi.astype(F32)).astype(BF16)
    bcum = (jnp.dot(tril, la_hi, preferred_element_type=F32)
            + jnp.dot(tril, la_lo, preferred_element_type=F32))
    q = q_ref[0].astype(F32) * (GLA_DK ** -0.5)
    k = k_ref[0].astype(F32)
    qi = q * jnp.exp(bcum)
    ki = k * jnp.exp(-bcum)
    lane_head = lax.broadcasted_iota(jnp.int32, (1, GLA_QK), 1) >> 6
    causal = ((lax.broadcasted_iota(jnp.int32, (GLA_HEADS * C, C), 0) & (C - 1))
              >= lax.broadcasted_iota(jnp.int32, (GLA_HEADS * C, C), 1))
    same_head = ((lax.broadcasted_iota(jnp.int32, (D_GLA, GLA_QK), 0) >> 7)
                 == (lax.broadcasted_iota(jnp.int32, (D_GLA, GLA_QK), 1) >> 6))
    nt = (((1,), (1,)), ((), ()))
    ng = ng_ref[...]
    st = st_ref[...]
    for c in range(lt // C):
        sl = slice(c * C, (c + 1) * C)
        bc = bcum[sl]
        bl = bc[C - 1:C, :]
        kd = k[sl] * jnp.exp(bl - bc)
        qic = qi[sl]
        qs = jnp.concatenate([jnp.where(lane_head == h, qic, 0.0) for h in range(GLA_HEADS)],
                             axis=0).astype(BF16)
        sc = lax.dot_general(qs, ki[sl].astype(BF16), nt, preferred_element_type=F32)
        p = jnp.where(causal, sc, 0.0).astype(BF16)
        vc = v_ref[0, sl, :]
        o_intra = jnp.concatenate(
            [jnp.dot(p[h * C:(h + 1) * C], vc[:, h * GLA_DV:(h + 1) * GLA_DV], preferred_element_type=F32)
             for h in range(GLA_HEADS)], axis=1)
        o_inter = lax.dot_general(qic.astype(BF16), st.astype(BF16), nt, preferred_element_type=F32)
        v_t = vc.astype(F32).T.astype(BF16)
        kv_t = jnp.dot(v_t, kd.astype(BF16), preferred_element_type=F32)
        st = st * jnp.exp(bl) + jnp.where(same_head, kv_t, 0.0)
        o = o_intra + o_inter
        parts = []
        for h in range(GLA_HEADS):
            oh = o[:, h * GLA_DV:(h + 1) * GLA_DV]
            oh = oh * lax.rsqrt(jnp.mean(oh * oh, axis=-1, keepdims=True) + EPS)
            parts.append(oh * ng)
        r = r_ref[0, sl, :].astype(F32)
        o_ref[0, sl, :] = (jnp.concatenate(parts, axis=1) * _silu(r)).astype(o_ref.dtype)
    st_ref[...] = st


def _gla(q, k, v, glr, r, wg, bg, ng):
    bsz, L, _ = q.shape
    lt = 256
    tok = lambda n: pl.BlockSpec((1, lt, n), lambda b, i: (b, i, 0))
    full = lambda a: pl.BlockSpec(a.shape, lambda b, i: (0,) * a.ndim)
    return pl.pallas_call(
        _gla_kernel,
        grid=(bsz, L // lt),
        in_specs=[tok(GLA_QK), tok(GLA_QK), tok(D_GLA), tok(LANE), tok(D_GLA), full(wg), full(bg), full(ng)],
        out_specs=tok(D_GLA),
        out_shape=jax.ShapeDtypeStruct((bsz, L, D_GLA), BF16),
        scratch_shapes=[pltpu.VMEM((D_GLA, GLA_QK), F32)],
        compiler_params=_params("arbitrary", "arbitrary"),
        name="gla",
    )(q, k, v, glr, r, wg, bg, ng)


def _pack_bf16_pairs(x):
    w = x.shape[1] // 2
    xr = x.astype(BF16).astype(F32)
    lo = lax.bitcast_convert_type(xr[:, :w], jnp.uint32) >> 16
    hi = lax.bitcast_convert_type(xr[:, w:], jnp.uint32) & jnp.uint32(0xFFFF0000)
    return lo | hi


def _unpack_bf16_pairs(p):
    lo = lax.bitcast_convert_type(p << 16, F32)
    hi = lax.bitcast_convert_type(p & jnp.uint32(0xFFFF0000), F32)
    return lo, hi


def _store_rows(ref, packed):
    n = packed.shape[0]
    for c in range(ROW_CHUNKS):
        ref[pl.ds(c, n, stride=ROW_CHUNKS), :] = packed[:, c * LANE:(c + 1) * LANE]


def _load_rows(ref, n):
    return jnp.concatenate([ref[pl.ds(c, n, stride=ROW_CHUNKS), :] for c in range(ROW_CHUNKS)], axis=1)


def _route_tile(lg_t, bias_col, tri, carry_ref):
    n_e, tt = lg_t.shape
    per_group = n_e // N_GROUPS
    scores = jax.nn.sigmoid(lg_t)
    biased = scores + bias_col
    row = lax.broadcasted_iota(jnp.int32, (n_e, tt), 0)
    neg = -jnp.inf
    group_score = []
    for g in range(N_GROUPS):
        b = biased[g * per_group:(g + 1) * per_group]
        r = lax.broadcasted_iota(jnp.int32, (per_group, tt), 0) + g * per_group
        m1 = jnp.max(b, axis=0, keepdims=True)
        i1 = jnp.min(jnp.where(b == m1, r, n_e), axis=0, keepdims=True)
        m2 = jnp.max(jnp.where(r == i1, neg, b), axis=0, keepdims=True)
        group_score.append(m1 + m2)
    parts = []
    for g in range(N_GROUPS):
        ahead = jnp.zeros((1, tt), jnp.int32)
        for g2 in range(N_GROUPS):
            if g2 != g:
                beats = (group_score[g2] >= group_score[g]) if g2 < g else (group_score[g2] > group_score[g])
                ahead = ahead + beats.astype(jnp.int32)
        parts.append(jnp.where(ahead < TOPK_GROUPS, biased[g * per_group:(g + 1) * per_group], neg))
    masked = jnp.concatenate(parts, axis=0)
    work = masked
    idxs = []
    for _ in range(TOP_K):
        m = jnp.max(work, axis=0, keepdims=True)
        ii = jnp.min(jnp.where(work == m, row, n_e), axis=0, keepdims=True)
        idxs.append(ii)
        work = jnp.where(row == ii, neg, work)
    sel = work != masked
    w = jnp.where(sel, scores, 0.0)
    gate_dense = w / jnp.sum(w, axis=0, keepdims=True) * ROUTED_SCALE
    mt = jnp.where(sel, 1.0, 0.0)
    rank_dense = jnp.dot(mt.astype(BF16), tri, preferred_element_type=F32) + carry_ref[...]
    carry_ref[...] += jnp.sum(mt, axis=1, keepdims=True)
    ranks, gts = [], []
    for ii in idxs:
        oh = row == ii
        ranks.append(jnp.sum(jnp.where(oh, rank_dense, 0.0), axis=0, keepdims=True))
        gts.append(jnp.sum(jnp.where(oh, gate_dense, 0.0), axis=0, keepdims=True))
    idx_t = jnp.concatenate(idxs, axis=0)
    rank_t = jnp.concatenate(ranks, axis=0).astype(jnp.int32)
    gate_t = jnp.concatenate(gts + [jnp.zeros((LANE - TOP_K, tt), F32)], axis=0)
    return idx_t, rank_t, gate_t.T[:, :TOP_K]


def _mix_kernel(ms_ref, yg_ref, x_ref, wo_ref, gt1_ref,
                gn_ref, sh2_ref, sc2_ref, gt2_ref, wrt_ref, rb_ref, tri_ref, wsg_ref, wsu_ref, wsd_ref,
                x1_ref, h2_ref, idx_ref, rank_ref, gate_ref, cnt_ref, carry_ref):
    @pl.when((pl.program_id(0) == 0) & (pl.program_id(1) == 0))
    def _():
        carry_ref[...] = jnp.zeros_like(carry_ref)

    mix_s5 = jnp.concatenate([ms_ref[0, c] for c in range(D_MODEL // LANE)], axis=1)
    mix = mix_s5 + jnp.dot(yg_ref[0], wo_ref[...], preferred_element_type=F32)
    x1 = x_ref[0] + gt1_ref[0] * mix
    h2 = x1 * lax.rsqrt(jnp.mean(x1 * x1, axis=-1, keepdims=True) + EPS) * gn_ref[...]
    h2 = h2 * (1.0 + sc2_ref[0]) + sh2_ref[0]
    hb = h2.astype(BF16)
    _store_rows(h2_ref, _pack_bf16_pairs(h2))
    lg_t = lax.dot_general(wrt_ref[...], hb, (((1,), (1,)), ((), ())), preferred_element_type=F32)
    idx_t, rank_t, gates = _route_tile(lg_t, rb_ref[...], tri_ref[...], carry_ref)
    idx_ref[...] = idx_t
    rank_ref[...] = rank_t
    gate_ref[...] = gates
    cnt_ref[...] = carry_ref[...]
    a = _silu(jnp.dot(hb, wsg_ref[...], preferred_element_type=F32)) * jnp.dot(
        hb, wsu_ref[...], preferred_element_type=F32)
    shared = jnp.dot(a.astype(BF16), wsd_ref[...], preferred_element_type=F32)
    x1_ref[0] = x1 + gt2_ref[0] * shared


def _mix(ms5, yg, x, wo, gt1, gn, sh2, sc2, gt2, wrt, rb, wsg, wsu, wsd):
    bsz, L, d = x.shape
    tt = 512
    nt = L // tt
    T = bsz * L
    tri = (jnp.arange(tt)[:, None] < jnp.arange(tt)[None, :]).astype(BF16)
    tok = lambda n: pl.BlockSpec((1, tt, n), lambda b, i: (b, i, 0))
    vec = pl.BlockSpec((1, 1, d), lambda b, i: (b, 0, 0))
    full = lambda a: pl.BlockSpec(a.shape, lambda b, i: (0,) * a.ndim)
    lanes = pl.BlockSpec((TOP_K, tt), lambda b, i: (0, b * nt + i))
    return pl.pallas_call(
        _mix_kernel,
        grid=(bsz, nt),
        in_specs=[pl.BlockSpec((1, d // LANE, tt, LANE), lambda b, i: (b, 0, i, 0)), tok(D_GLA), tok(d), full(wo), vec,
                  full(gn), vec, vec, vec, full(wrt), full(rb), full(tri), full(wsg), full(wsu), full(wsd)],
        out_specs=[tok(d),
                   pl.BlockSpec((tt * ROW_CHUNKS, LANE), lambda b, i: (b * nt + i, 0)),
                   lanes, lanes,
                   pl.BlockSpec((tt, TOP_K), lambda b, i: (b * nt + i, 0)),
                   pl.BlockSpec((N_EXPERTS, 1), lambda b, i: (0, 0))],
        out_shape=[jax.ShapeDtypeStruct((bsz, L, d), F32),
                   jax.ShapeDtypeStruct((T * ROW_CHUNKS, LANE), jnp.uint32),
                   jax.ShapeDtypeStruct((TOP_K, T), jnp.int32),
                   jax.ShapeDtypeStruct((TOP_K, T), jnp.int32),
                   jax.ShapeDtypeStruct((T, TOP_K), F32),
                   jax.ShapeDtypeStruct((N_EXPERTS, 1), F32)],
        scratch_shapes=[pltpu.VMEM((N_EXPERTS, 1), F32)],
        compiler_params=_params("arbitrary", "arbitrary"),
        name="mix",
    )(ms5, yg, x, wo, gt1, gn, sh2, sc2, gt2, wrt, rb, tri, wsg, wsu, wsd)


def _pos_kernel(idx_ref, rank_ref, ps_ref, pos_ref):
    n_e = ps_ref.shape[0]
    tt = idx_ref.shape[1]
    row = lax.broadcasted_iota(jnp.int32, (n_e, tt), 0)
    ps = ps_ref[...]
    starts = [jnp.sum(jnp.where(row == idx_ref[k:k + 1, :], ps, 0.0), axis=0, keepdims=True)
              for k in range(TOP_K)]
    pos = jnp.concatenate(starts, axis=0).astype(jnp.int32) + rank_ref[...]
    for jh in range(tt // LANE):
        pos_ref[jh * TOP_K:(jh + 1) * TOP_K, :] = pos[:, jh * LANE:(jh + 1) * LANE]


def _pos(idx_t, rank_t, pstart):
    T = idx_t.shape[1]
    tt = 2048
    blk = pl.BlockSpec((TOP_K, tt), lambda i: (0, i))
    return pl.pallas_call(
        _pos_kernel,
        grid=(T // tt,),
        in_specs=[blk, blk, pl.BlockSpec((N_EXPERTS, 1), lambda i: (0, 0))],
        out_specs=pl.BlockSpec((tt // LANE * TOP_K, LANE), lambda i: (i, 0)),
        out_shape=jax.ShapeDtypeStruct((T // LANE * TOP_K, LANE), jnp.int32),
        compiler_params=_params("arbitrary"),
        name="pos",
    )(idx_t, rank_t, pstart.astype(F32).reshape(N_EXPERTS, 1))


def _dispatch_kernel(pend_ref, padded_ref, pos_ref, h_ref, xs_hbm, zero_ref, pos_smem, sem_pos, sem_zero, sem_row):
    tt = h_ref.shape[0]
    step = pl.program_id(0)
    load_pos = pltpu.make_async_copy(pos_ref, pos_smem, sem_pos)
    load_pos.start()

    def zero_block(start):
        return pltpu.make_async_copy(zero_ref, xs_hbm.at[pl.ds(pl.multiple_of(start, MOE_BLOCK), MOE_BLOCK)],
                                     sem_zero)

    def for_each_zero_block(fn):
        def per_expert(e, c):
            @pl.when(padded_ref[e] > 0)
            def _():
                fn(zero_block(pend_ref[e] - MOE_BLOCK))
            return c

        def per_tail_block(g, c):
            fn(zero_block(g * MOE_BLOCK))
            return c

        lax.fori_loop(0, N_EXPERTS, per_expert, 0)
        lax.fori_loop(pend_ref[N_EXPERTS - 1] // MOE_BLOCK, xs_hbm.shape[0] // MOE_BLOCK, per_tail_block, 0)

    @pl.when(step == 0)
    def _():
        zero_ref[...] = jnp.zeros_like(zero_ref)
        for_each_zero_block(lambda cp: cp.start())
        for_each_zero_block(lambda cp: cp.wait())

    load_pos.wait()
    for jh in range(tt // LANE):
        def issue(jl, c, jh=jh):
            for k in range(TOP_K):
                slot = pos_smem[(jh * TOP_K + k) * LANE + jl]
                pltpu.make_async_copy(h_ref.at[jh * LANE + jl], xs_hbm.at[slot], sem_row).start(priority=k % 2)
            return c

        lax.fori_loop(0, LANE, issue, 0, unroll=4)
    for k in range(TOP_K):
        pltpu.make_async_copy(h_ref, xs_hbm.at[pl.ds(0, tt)], sem_row).wait()


def _dispatch(pends, padded, pos_t, h_rows, n_slots):
    T = h_rows.shape[0]
    tt = 512
    return pl.pallas_call(
        _dispatch_kernel,
        grid_spec=pltpu.PrefetchScalarGridSpec(
            num_scalar_prefetch=2,
            grid=(T // tt,),
            in_specs=[pl.BlockSpec((tt * TOP_K,), lambda i, pe, pa: (i,)),
                      pl.BlockSpec((tt, ROW_CHUNKS, LANE), lambda i, pe, pa: (i, 0, 0))],
            out_specs=pl.BlockSpec(memory_space=pl.ANY),
            scratch_shapes=[pltpu.VMEM((MOE_BLOCK, ROW_CHUNKS, LANE), jnp.uint32),
                            pltpu.SMEM((tt * TOP_K,), jnp.int32),
                            pltpu.SemaphoreType.DMA, pltpu.SemaphoreType.DMA, pltpu.SemaphoreType.DMA]),
        out_shape=jax.ShapeDtypeStruct((n_slots, ROW_CHUNKS, LANE), jnp.uint32),
        compiler_params=_params("arbitrary"),
        name="dispatch",
    )(pends, padded, pos_t, h_rows)


X_GROUP = 8
X_AHEAD = 8
X_BUFS = X_AHEAD + X_GROUP
Y_BUFS = X_GROUP


def _expert_kernel(b0_ref, nb_ref, nv_ref, wg_ref, wu_ref, wd_ref, xs_hbm, ye_hbm,
                   wg_s, wu_s, wd_s, xbuf, ybuf, semx, semy, *, n_blocks):
    e = pl.program_id(0)
    nb = nb_ref[e]
    b0 = b0_ref[e]
    nv = nv_ref[0]
    rows = MOE_BLOCK * ROW_CHUNKS

    def block(ref, g):
        return ref.at[pl.ds(pl.multiple_of(g * rows, rows), rows)]

    def x_copy(g):
        slot = g % X_BUFS
        return pltpu.make_async_copy(block(xs_hbm, g), xbuf.at[slot], semx.at[slot])

    def y_copy(g):
        slot = g % Y_BUFS
        return pltpu.make_async_copy(ybuf.at[slot], block(ye_hbm, g), semy.at[slot])

    @pl.when(e == 0)
    def _():
        for g in range(X_AHEAD):
            x_copy(g).start()

    @pl.when(nb > 0)
    def _():
        wg_s[...] = wg_ref[0].astype(BF16)
        wu_s[...] = wu_ref[0].astype(BF16)
        wd_s[...] = wd_ref[0].astype(BF16)

    def fetch(g):
        x_copy(g).wait()

        @pl.when(g + X_AHEAD < nv)
        def _():
            x_copy(g + X_AHEAD).start()

    def load_x(g):
        lo, hi = _unpack_bf16_pairs(_load_rows(xbuf.at[g % X_BUFS], MOE_BLOCK))
        return jnp.concatenate([lo, hi], axis=1).astype(BF16)

    def emit(g, y):
        @pl.when(g >= Y_BUFS)
        def _():
            y_copy(g - Y_BUFS).wait()

        _store_rows(ybuf.at[g % Y_BUFS], y)
        y_copy(g).start()

    def run(g, n):
        for b in range(n):
            fetch(g + b)
        x = jnp.concatenate([load_x(g + b) for b in range(n)], axis=0) if n > 1 else load_x(g)
        a = jnp.dot(x, wg_s[...], preferred_element_type=F32)
        u = jnp.dot(x, wu_s[...], preferred_element_type=F32)
        h = (_silu(a) * u).astype(BF16)
        y = _pack_bf16_pairs(jnp.dot(h, wd_s[...], preferred_element_type=F32))
        for b in range(n):
            emit(g + b, y[b * MOE_BLOCK:(b + 1) * MOE_BLOCK])

    def group(i, c):
        run(b0 + X_GROUP * i, X_GROUP)
        return c

    lax.fori_loop(0, nb // X_GROUP, group, 0)
    done = nb // X_GROUP * X_GROUP
    n = X_GROUP // 2
    while n >= 1:
        @pl.when((nb & n) != 0)
        def _(n=n, done=done):
            run(b0 + done, n)

        done = done + (nb & n)
        n //= 2

    @pl.when(e == pl.num_programs(0) - 1)
    def _():
        for back in range(Y_BUFS, 0, -1):
            y_copy(nv - back).wait()
        ybuf[0] = jnp.zeros(ybuf.shape[1:], ybuf.dtype)

        def fill(g):
            return pltpu.make_async_copy(ybuf.at[0], block(ye_hbm, g), semy.at[0])

        lax.fori_loop(nv, n_blocks, lambda g, c: (fill(g).start(), c)[1], 0)
        lax.fori_loop(nv, n_blocks, lambda g, c: (fill(g).wait(), c)[1], 0)


def _experts(first_block, num_blocks, n_valid, xs, wg, wu, wd):
    rows = MOE_BLOCK * ROW_CHUNKS
    n_blocks = xs.shape[0] // rows
    n_e, d, _ = wg.shape
    wsel = lambda e, b0, nb, nv: (e, 0, 0)
    return pl.pallas_call(
        functools.partial(_expert_kernel, n_blocks=n_blocks),
        grid_spec=pltpu.PrefetchScalarGridSpec(
            num_scalar_prefetch=3,
            grid=(n_e,),
            in_specs=[pl.BlockSpec((1, d, D_EXPERT), wsel),
                      pl.BlockSpec((1, d, D_EXPERT), wsel),
                      pl.BlockSpec((1, D_EXPERT, d), wsel),
                      pl.BlockSpec(memory_space=pl.ANY)],
            out_specs=pl.BlockSpec(memory_space=pl.ANY),
            scratch_shapes=[pltpu.VMEM((d, D_EXPERT), BF16), pltpu.VMEM((d, D_EXPERT), BF16),
                            pltpu.VMEM((D_EXPERT, d), BF16),
                            pltpu.VMEM((X_BUFS, rows, LANE), jnp.uint32),
                            pltpu.VMEM((Y_BUFS, rows, LANE), jnp.uint32),
                            pltpu.SemaphoreType.DMA((X_BUFS,)), pltpu.SemaphoreType.DMA((Y_BUFS,))]),
        out_shape=jax.ShapeDtypeStruct(xs.shape, jnp.uint32),
        compiler_params=_params("arbitrary"),
        name="experts",
    )(first_block, num_blocks, n_valid, wg, wu, wd, xs)


def _final_kernel(pos0_ref, posn_ref, gate_ref, x_ref, gt_ref, g_ref, ye_hbm, o_ref,
                  buf_ref, pos_smem, sem_pos, sem_row):
    tt = x_ref.shape[1]
    i = pl.program_id(0)

    def gather_tile(pos_ref, tile):
        base = (tile % 2) * (TOP_K * tt)
        sem = sem_row.at[tile % 2]
        load_pos = pltpu.make_async_copy(pos_ref, pos_smem, sem_pos)
        load_pos.start()
        load_pos.wait()
        for jh in range(tt // LANE):
            def issue(jl, c, jh=jh):
                for k in range(TOP_K):
                    slot = pos_smem[(jh * TOP_K + k) * LANE + jl]
                    pltpu.make_async_copy(ye_hbm.at[slot], buf_ref.at[base + k * tt + jh * LANE + jl],
                                          sem).start(priority=k % 2)
                return c

            lax.fori_loop(0, LANE, issue, 0, unroll=4)

    @pl.when(i == 0)
    def _():
        gather_tile(pos0_ref, i)

    @pl.when(i + 1 < pl.num_programs(0))
    def _():
        gather_tile(posn_ref, i + 1)

    base = pl.multiple_of((i % 2) * (TOP_K * tt), TOP_K * tt)
    for k in range(TOP_K):
        pltpu.make_async_copy(ye_hbm.at[pl.ds(0, tt)], buf_ref.at[pl.ds(base + k * tt, tt)],
                              sem_row.at[i % 2]).wait()
    gates = gate_ref[...]
    half = D_MODEL // 2
    acc_lo = jnp.zeros((tt, half), F32)
    acc_hi = jnp.zeros((tt, half), F32)
    rows_2d = buf_ref.reshape(buf_ref.shape[0] * ROW_CHUNKS, LANE)
    for k in range(TOP_K):
        lo, hi = _unpack_bf16_pairs(_load_rows(rows_2d.at[pl.ds((base + k * tt) * ROW_CHUNKS, tt * ROW_CHUNKS)], tt))
        gk = gates[:, k:k + 1]
        acc_lo += gk * lo
        acc_hi += gk * hi
    x = x_ref[0] + gt_ref[0] * jnp.concatenate([acc_lo, acc_hi], axis=1)
    o_ref[0] = x * lax.rsqrt(jnp.mean(x * x, axis=-1, keepdims=True) + EPS) * g_ref[...]


def _final(pos_t, gates, x1, gt2, g, ye_rows):
    bsz, L, d = x1.shape
    tt = 256
    nt = L // tt
    n = bsz * nt
    tok = pl.BlockSpec((1, tt, d), lambda i: (i // nt, i % nt, 0))
    return pl.pallas_call(
        _final_kernel,
        grid=(n,),
        in_specs=[pl.BlockSpec((tt * TOP_K,), lambda i: (0,)),
                  pl.BlockSpec((tt * TOP_K,), lambda i: (jnp.minimum(i + 1, n - 1),)),
                  pl.BlockSpec((tt, TOP_K), lambda i: (i, 0)),
                  tok, pl.BlockSpec((1, 1, d), lambda i: (i // nt, 0, 0)),
                  pl.BlockSpec((1, d), lambda i: (0, 0)),
                  pl.BlockSpec(memory_space=pl.ANY)],
        out_specs=tok,
        out_shape=jax.ShapeDtypeStruct((bsz, L, d), F32),
        scratch_shapes=[pltpu.VMEM((2 * TOP_K * tt, ROW_CHUNKS, LANE), jnp.uint32),
                        pltpu.SMEM((tt * TOP_K,), jnp.int32),
                        pltpu.SemaphoreType.DMA, pltpu.SemaphoreType.DMA((2,))],
        compiler_params=_params("arbitrary"),
        name="final",
    )(pos_t, pos_t, gates, x1, gt2, g.reshape(1, d), ye_rows)


def _block_table(counts):
    counts = counts.astype(jnp.int32)
    padded = (counts + MOE_BLOCK - 1) // MOE_BLOCK * MOE_BLOCK
    pends = jnp.cumsum(padded)
    pstarts = pends - padded
    n_valid = (pends[-1] // MOE_BLOCK).reshape(1)
    return padded, pends, pstarts, n_valid


def kernel(x, c, w_ada, b_ada, g_norm_mix, w_in, s5_lambda_re, s5_lambda_im, s5_log_dt, s5_b_re, s5_b_im, s5_c_re, s5_c_im, s5_d, s5_w_glu, s5_b_glu, s5_norm_g, gla_w_g2, gla_b_g2, gla_norm_g, w_out, g_norm_moe, w_router, router_bias, exp_w_gate, exp_w_up, exp_w_down, sh_w_gate, sh_w_up, sh_w_down, g_final):
    bsz, L, d = x.shape
    T = bsz * L
    assert w_ada.shape[0] == 1, "single-layer block"
    for l in range(1):
        mod = _ada(c, w_ada[l], b_ada[l])
        sh1, sc1, gt1, sh2, sc2, gt2 = [m.reshape(bsz, 1, d) for m in jnp.split(mod, 6, axis=-1)]

        wi = w_in[l]
        o_q, o_k, o_v, o_g, o_r = D_S5, D_S5 + GLA_QK, D_S5 + 2 * GLA_QK, D_S5 + 2 * GLA_QK + D_GLA, \
            D_S5 + 2 * GLA_QK + D_GLA + GLA_LOWRANK
        w_cat = jnp.concatenate([wi[:, :o_g], wi[:, o_r:], wi[:, o_g:o_r],
                                 jnp.zeros((d, LANE - GLA_LOWRANK), wi.dtype)], axis=1).astype(BF16)
        u, q, k, v, r, glr = _inproj(x, g_norm_mix[l], sh1, sc1, w_cat)

        mats = _s5_prep(s5_lambda_re[l], s5_lambda_im[l], s5_log_dt[l], s5_b_re[l], s5_b_im[l],
                        s5_c_re[l], s5_c_im[l], s5_d[l])
        ms5 = _s5(u, mats, s5_w_glu[l].T.astype(BF16), s5_b_glu[l].reshape(D_S5, 1),
                  s5_norm_g[l].reshape(D_S5, 1), w_out[l][:D_S5].astype(BF16))

        wg2 = jnp.concatenate([gla_w_g2[l], jnp.zeros((LANE - GLA_LOWRANK, GLA_QK), F32)], axis=0).astype(BF16)
        yg = _gla(q, k, v, glr, r, wg2, gla_b_g2[l].reshape(1, GLA_QK), gla_norm_g[l].reshape(1, GLA_DV))

        x1, h2p, idx_t, rank_t, gates, counts = _mix(
            ms5, yg, x, w_out[l][D_S5:].astype(BF16), gt1, g_norm_moe[l].reshape(1, d), sh2, sc2, gt2,
            w_router[l].T.astype(BF16), router_bias[l].reshape(N_EXPERTS, 1),
            sh_w_gate[l].astype(BF16), sh_w_up[l].astype(BF16), sh_w_down[l].astype(BF16))

        n_blocks = (T * TOP_K + N_EXPERTS * (MOE_BLOCK - 1) + MOE_BLOCK - 1) // MOE_BLOCK
        n_slots = n_blocks * MOE_BLOCK
        padded, pends, pstarts, n_valid = _block_table(counts[:, 0])
        pos_t = _pos(idx_t, rank_t, pstarts).reshape(-1)
        xs = _dispatch(pends, padded, pos_t, h2p.reshape(T, ROW_CHUNKS, LANE), n_slots)
        ye = _experts(pstarts // MOE_BLOCK, padded // MOE_BLOCK, n_valid,
                      xs.reshape(n_slots * ROW_CHUNKS, LANE), exp_w_gate[l], exp_w_up[l], exp_w_down[l])
    return _final(pos_t, gates, x1, gt2, g_final, ye.reshape(n_slots, ROW_CHUNKS, LANE))
```

```python
import functools

import jax
import jax.numpy as jnp
from jax import lax
from jax.experimental import pallas as pl
from jax.experimental.pallas import tpu as pltpu

F32 = jnp.float32
BF16 = jnp.bfloat16

D_MODEL = 1024
D_S5 = 512
S5_GROUP = 16
S5_GROUPS = 32
S5_STATE = 64
S5_CHUNK = 16
S5_STEP_GROUPS = 8
S5_LEVELS = 7
S5_APOW_COLS = 16
D_GLA = 512
GLA_HEADS = 4
GLA_DK = 64
GLA_DV = 128
GLA_QK = 256
GLA_LOWRANK = 16
GLA_TAU = 16.0
GLA_CHUNK = 64
LANE = 128
N_EXPERTS = 256
TOP_K = 8
N_GROUPS = 8
TOPK_GROUPS = 4
D_EXPERT = 256
ROUTED_SCALE = 2.5
EPS = 1e-6
MOE_BLOCK = 128
ROW_CHUNKS = D_MODEL // 2 // LANE
VMEM_LIMIT = 48 * 1024 * 1024


def _silu(x):
    return x * jax.nn.sigmoid(x)


def _params(*sem):
    return pltpu.CompilerParams(dimension_semantics=sem, vmem_limit_bytes=VMEM_LIMIT)


def _ada_kernel(c_ref, w_ref, b_ref, o_ref):
    s = _silu(c_ref[...]).astype(BF16)
    o_ref[...] = jnp.dot(s, w_ref[...].astype(BF16), preferred_element_type=F32) + b_ref[...]


def _ada(c, w, b):
    bsz, d = c.shape
    n = w.shape[1]
    tn = 1024
    return pl.pallas_call(
        _ada_kernel,
        grid=(n // tn,),
        in_specs=[pl.BlockSpec((bsz, d), lambda j: (0, 0)),
                  pl.BlockSpec((d, tn), lambda j: (0, j)),
                  pl.BlockSpec((1, tn), lambda j: (0, j))],
        out_specs=pl.BlockSpec((bsz, tn), lambda j: (0, j)),
        out_shape=jax.ShapeDtypeStruct((bsz, n), F32),
        compiler_params=_params("arbitrary"),
        name="ada",
    )(c, w, b.reshape(1, n))


def _inproj_kernel(x_ref, g_ref, sh_ref, sc_ref, w_ref,
                   u_ref, q_ref, k_ref, v_ref, r_ref, glr_ref):
    x = x_ref[0]
    ms = jnp.mean(x * x, axis=-1, keepdims=True)
    h = (x * lax.rsqrt(ms + EPS)) * g_ref[...]
    h = h * (1.0 + sc_ref[0]) + sh_ref[0]
    hb = h.astype(BF16)
    u = jnp.dot(hb, w_ref[:, 0:D_S5], preferred_element_type=F32)
    for c in range(D_S5 // LANE):
        u_ref[0, c] = u[:, c * LANE:(c + 1) * LANE]
    col = D_S5
    for ref in (q_ref, k_ref, v_ref, r_ref, glr_ref):
        n = ref.shape[-1]
        ref[0] = jnp.dot(hb, w_ref[:, col:col + n], preferred_element_type=F32).astype(ref.dtype)
        col += n


def _inproj(x, g, sh, sc, w):
    bsz, L, d = x.shape
    tt = 512
    widths = (GLA_QK, GLA_QK, D_GLA, D_GLA, LANE)
    tok = lambda n: pl.BlockSpec((1, tt, n), lambda b, i: (b, i, 0))
    vec = pl.BlockSpec((1, 1, d), lambda b, i: (b, 0, 0))
    return pl.pallas_call(
        _inproj_kernel,
        grid=(bsz, L // tt),
        in_specs=[tok(d), pl.BlockSpec((1, d), lambda b, i: (0, 0)), vec, vec,
                  pl.BlockSpec(w.shape, lambda b, i: (0, 0))],
        out_specs=[pl.BlockSpec((1, D_S5 // LANE, tt, LANE), lambda b, i: (b, 0, i, 0))] + [tok(n) for n in widths],
        out_shape=[jax.ShapeDtypeStruct((bsz, D_S5 // LANE, L, LANE), F32)]
        + [jax.ShapeDtypeStruct((bsz, L, n), BF16) for n in widths],
        compiler_params=_params("arbitrary", "arbitrary"),
        name="inproj",
    )(x, g.reshape(1, d), sh, sc, w)


def _s5_prep(lam_re, lam_im, log_dt, b_re, b_im, c_re, c_im, d_skip):
    G, N, C, TC = S5_GROUPS, S5_STATE, S5_GROUP, S5_CHUNK
    hp = lax.Precision.HIGHEST
    dt = jnp.exp(log_dt)[:, None]
    lr, li = lam_re, lam_im
    mag = jnp.exp(lr * dt)
    ab_re, ab_im = mag * jnp.cos(li * dt), mag * jnp.sin(li * dt)
    den = lr * lr + li * li
    nr = ab_re - 1.0
    coef_re = ((nr * lr + ab_im * li) / den)[..., None]
    coef_im = ((ab_im * lr - nr * li) / den)[..., None]
    bb_re = coef_re * b_re - coef_im * b_im
    bb_im = coef_re * b_im + coef_im * b_re
    p = jnp.arange(TC + 1, dtype=F32)[:, None, None]
    pm = jnp.exp(lr * dt * p)
    pr, pi = pm * jnp.cos(li * dt * p), pm * jnp.sin(li * dt * p)
    ca_re = c_re[None] * pr[:, :, None, :] - c_im[None] * pi[:, :, None, :]
    ca_im = c_re[None] * pi[:, :, None, :] + c_im[None] * pr[:, :, None, :]
    ca = jnp.concatenate([ca_re[:TC], -ca_im[:TC]], axis=-1).transpose(1, 0, 2, 3).reshape(G, TC * C, 2 * N)
    kern = jnp.einsum('gxk,gki->gxi', ca, jnp.concatenate([bb_re, bb_im], axis=1),
                      precision=hp).reshape(G, TC, C, C)
    skip = jnp.eye(C, dtype=F32)[None] * d_skip[:, :, None]
    kern = kern + jnp.concatenate([skip[:, None], jnp.zeros((G, TC - 1, C, C), F32)], axis=1)
    rev = kern[:, ::-1].transpose(0, 2, 1, 3).reshape(G, C, TC * C)
    rev = jnp.concatenate([rev, jnp.zeros((G, C, (TC - 1) * C), F32)], axis=-1)
    toep_t = jnp.stack([rev[:, :, (TC - 1 - t) * C:(TC - 1 - t) * C + TC * C] for t in range(TC)],
                       axis=1).reshape(G, TC * C, TC * C)
    rr, ri = pr[TC - 1 - jnp.arange(TC)], pi[TC - 1 - jnp.arange(TC)]
    binc_re = rr[..., None] * bb_re[None] - ri[..., None] * bb_im[None]
    binc_im = rr[..., None] * bb_im[None] + ri[..., None] * bb_re[None]
    binc_re_t = binc_re.transpose(1, 2, 0, 3).reshape(G, N, TC * C)
    binc_im_t = binc_im.transpose(1, 2, 0, 3).reshape(G, N, TC * C)
    cm_re_t = ca_re[1:].transpose(1, 0, 2, 3).reshape(G, TC * C, N)
    cm_im_t = (-ca_im[1:]).transpose(1, 0, 2, 3).reshape(G, TC * C, N)
    q = (TC * 2.0 ** jnp.arange(S5_LEVELS, dtype=F32))[:, None, None]
    qm = jnp.exp(lr * dt * q)
    qr, qi = qm * jnp.cos(li * dt * q), qm * jnp.sin(li * dt * q)
    apow = jnp.stack([qr, qi], axis=1).reshape(2 * S5_LEVELS, G, N).transpose(1, 2, 0)
    apow = jnp.concatenate([apow, jnp.zeros((G, N, S5_APOW_COLS - 2 * S5_LEVELS), F32)], axis=-1)
    from_z = jnp.concatenate([toep_t, binc_re_t, binc_im_t], axis=1).astype(BF16)
    from_h = jnp.concatenate([cm_re_t, cm_im_t], axis=2).astype(BF16)
    return from_z, from_h, apow


def _s5_kernel(u_ref, eye_ref, fz_ref, fh_ref, apow_ref,
               wglu_ref, bglu_ref, sng_ref, wo_ref, o_ref, ut_ref, yt_ref):
    j = pl.program_id(1)
    n_chunks = u_ref.shape[2] // S5_CHUNK
    nt = (((1,), (1,)), ((), ()))

    @pl.when(j == 0)
    def _():
        for s in range(S5_CHUNK):
            us = jnp.concatenate([u_ref[0, c, pl.ds(s, n_chunks, stride=S5_CHUNK), :]
                                  for c in range(D_S5 // LANE)], axis=1).astype(BF16)
            ut_ref[:, s * n_chunks:(s + 1) * n_chunks] = lax.dot_general(
                eye_ref[...], us, nt, preferred_element_type=F32).astype(BF16)

    lane = lax.broadcasted_iota(jnp.int32, (S5_STATE, n_chunks), 1)
    groups = range(S5_STEP_GROUPS)
    rows = [pl.ds(pl.multiple_of(j * (S5_STEP_GROUPS * S5_GROUP) + gl * S5_GROUP, S5_GROUP), S5_GROUP)
            for gl in groups]
    zt = [jnp.concatenate([ut_ref[rows[gl], s * n_chunks:(s + 1) * n_chunks] for s in range(S5_CHUNK)], axis=0)
          for gl in groups]
    w = S5_CHUNK * S5_GROUP
    zprod = [jnp.dot(fz_ref[gl], zt[gl], preferred_element_type=F32) for gl in groups]
    xr = [jnp.where(lane >= 1, pltpu.roll(zprod[gl][w:w + S5_STATE], 1, axis=1), 0.0) for gl in groups]
    xi = [jnp.where(lane >= 1, pltpu.roll(zprod[gl][w + S5_STATE:], 1, axis=1), 0.0) for gl in groups]
    for lv in range(S5_LEVELS):
        d = 1 << lv
        for gl in groups:
            ar = apow_ref[gl, :, 2 * lv:2 * lv + 1]
            ai = apow_ref[gl, :, 2 * lv + 1:2 * lv + 2]
            sr = jnp.where(lane >= d, pltpu.roll(xr[gl], d, axis=1), 0.0)
            si = jnp.where(lane >= d, pltpu.roll(xi[gl], d, axis=1), 0.0)
            xr[gl], xi[gl] = xr[gl] + ar * sr - ai * si, xi[gl] + ar * si + ai * sr
    for gl in groups:
        state = jnp.concatenate([xr[gl], xi[gl]], axis=0).astype(BF16)
        yt = zprod[gl][:w] + jnp.dot(fh_ref[gl], state, preferred_element_type=F32)
        for t in range(S5_CHUNK):
            yt_ref[rows[gl], t * n_chunks:(t + 1) * n_chunks] = yt[t * S5_GROUP:(t + 1) * S5_GROUP, :]

    @pl.when(j == pl.num_programs(1) - 1)
    def _():
        cw = 2 * n_chunks
        for cc in range(yt_ref.shape[1] // cw):
            y = yt_ref[:, cc * cw:(cc + 1) * cw]
            g = y * (0.5 * (1.0 + jnp.tanh(0.7978845608028654 * (y + 0.044715 * (y * y * y)))))
            z = jnp.dot(wglu_ref[...], g.astype(BF16), preferred_element_type=F32) + bglu_ref[...]
            out = g * jax.nn.sigmoid(z)
            out = out * lax.rsqrt(jnp.mean(out * out, axis=0, keepdims=True) + EPS) * sng_ref[...]
            mixc = jnp.dot(out.T.astype(BF16), wo_ref[...], preferred_element_type=F32)
            for sl in range(cw // n_chunks):
                s = cc * (cw // n_chunks) + sl
                for c in range(D_MODEL // LANE):
                    o_ref[0, c, pl.ds(s, n_chunks, stride=S5_CHUNK), :] = mixc[sl * n_chunks:(sl + 1) * n_chunks,
                                                                               c * LANE:(c + 1) * LANE]


def _s5(u, mats, wglu_t, bglu, sng, wo_s5):
    bsz, _, L, _ = u.shape
    assert L // S5_CHUNK == LANE, "one lane tile of chunks per sequence"
    from_z, from_h, apow = mats
    sg = S5_STEP_GROUPS
    eye = jnp.eye(D_S5, dtype=BF16)
    grp = lambda a: pl.BlockSpec((sg,) + a.shape[1:], lambda bi, j: (j, 0, 0))
    full = lambda a: pl.BlockSpec(a.shape, lambda bi, j: (0,) * a.ndim)
    return pl.pallas_call(
        _s5_kernel,
        grid=(bsz, S5_GROUPS // sg),
        in_specs=[pl.BlockSpec((1, D_S5 // LANE, L, LANE), lambda bi, j: (bi, 0, 0, 0)), full(eye),
                  grp(from_z), grp(from_h), grp(apow), full(wglu_t), full(bglu), full(sng), full(wo_s5)],
        out_specs=pl.BlockSpec((1, D_MODEL // LANE, L, LANE), lambda bi, j: (bi, 0, 0, 0)),
        out_shape=jax.ShapeDtypeStruct((bsz, D_MODEL // LANE, L, LANE), F32),
        scratch_shapes=[pltpu.VMEM((D_S5, L), BF16), pltpu.VMEM((D_S5, L), F32)],
        compiler_params=_params("arbitrary", "arbitrary"),
        name="s5",
    )(u, eye, from_z, from_h, apow, wglu_t, bglu, sng, wo_s5)


def _gla_kernel(q_ref, k_ref, v_ref, glr_ref, r_ref, wg_ref, bg_ref, ng_ref, o_ref, st_ref):
    lt = q_ref.shape[1]
    C = GLA_CHUNK

    @pl.when(pl.program_id(1) == 0)
    def _():
        st_ref[...] = jnp.zeros_like(st_ref)

    z = jnp.dot(glr_ref[0], wg_ref[...], preferred_element_type=F32) + bg_ref[...]
    log_a = (jnp.minimum(z, 0.0) - jnp.log(1.0 + jnp.exp(-jnp.abs(z)))) * (1.0 / GLA_TAU)
    ri = lax.broadcasted_iota(jnp.int32, (lt, lt), 0)
    ci = lax.broadcasted_iota(jnp.int32, (lt, lt), 1)
    tril = jnp.where(((ri >> 6) == (ci >> 6)) & (ci <= ri), 1.0, 0.0).astype(BF16)
    la_hi = log_a.astype(BF16)
    la_lo = (log_a - la_hi.astype(F32)).astype(BF16)
    bcum = (jnp.dot(tril, la_hi, preferred_element_type=F32)
            + jnp.dot(tril, la_lo, preferred_element_type=F32))
    q = q_ref[0].astype(F32) * (GLA_DK ** -0.5)
    k = k_ref[0].astype(F32)
    qi = q * jnp.exp(bcum)
    ki = k * jnp.exp(-bcum)
    lane_head = lax.broadcasted_iota(jnp.int32, (1, GLA_QK), 1) >> 6
    causal = ((lax.broadcasted_iota(jnp.int32, (GLA_HEADS * C, C), 0) & (C - 1))
              >= lax.broadcasted_iota(jnp.int32, (GLA_HEADS * C, C), 1))
    same_head = ((lax.broadcasted_iota(jnp.int32, (D_GLA, GLA_QK), 0) >> 7)
                 == (lax.broadcasted_iota(jnp.int32, (D_GLA, GLA_QK), 1) >> 6))
    nt = (((1,), (1,)), ((), ()))
    ng = ng_ref[...]
    st = st_ref[...]
    for c in range(lt // C):
        sl = slice(c * C, (c + 1) * C)
        bc = bcum[sl]
        bl = bc[C - 1:C, :]
        kd = k[sl] * jnp.exp(bl - bc)
        qic = qi[sl]
        qs = jnp.concatenate([jnp.where(lane_head == h, qic, 0.0) for h in range(GLA_HEADS)],
                             axis=0).astype(BF16)
        sc = lax.dot_general(qs, ki[sl].astype(BF16), nt, preferred_element_type=F32)
        p = jnp.where(causal, sc, 0.0).astype(BF16)
        vc = v_ref[0, sl, :]
        o_intra = jnp.concatenate(
            [jnp.dot(p[h * C:(h + 1) * C], vc[:, h * GLA_DV:(h + 1) * GLA_DV], preferred_element_type=F32)
             for h in range(GLA_HEADS)], axis=1)
        o_inter = lax.dot_general(qic.astype(BF16), st.astype(BF16), nt, preferred_element_type=F32)
        v_t = vc.astype(F32).T.astype(BF16)
        kv_t = jnp.dot(v_t, kd.astype(BF16), preferred_element_type=F32)
        st = st * jnp.exp(bl) + jnp.where(same_head, kv_t, 0.0)
        o = o_intra + o_inter
        parts = []
        for h in range(GLA_HEADS):
            oh = o[:, h * GLA_DV:(h + 1) * GLA_DV]
            oh = oh * lax.rsqrt(jnp.mean(oh * oh, axis=-1, keepdims=True) + EPS)
            parts.append(oh * ng)
        r = r_ref[0, sl, :].astype(F32)
        o_ref[0, sl, :] = (jnp.concatenate(parts, axis=1) * _silu(r)).astype(o_ref.dtype)
    st_ref[...] = st


def _gla(q, k, v, glr, r, wg, bg, ng):
    bsz, L, _ = q.shape
    lt = 256
    tok = lambda n: pl.BlockSpec((1, lt, n), lambda b, i: (b, i, 0))
    full = lambda a: pl.BlockSpec(a.shape, lambda b, i: (0,) * a.ndim)
    return pl.pallas_call(
        _gla_kernel,
        grid=(bsz, L // lt),
        in_specs=[tok(GLA_QK), tok(GLA_QK), tok(D_GLA), tok(LANE), tok(D_GLA), full(wg), full(bg), full(ng)],
        out_specs=tok(D_GLA),
        out_shape=jax.ShapeDtypeStruct((bsz, L, D_GLA), BF16),
        scratch_shapes=[pltpu.VMEM((D_GLA, GLA_QK), F32)],
        compiler_params=_params("arbitrary", "arbitrary"),
        name="gla",
    )(q, k, v, glr, r, wg, bg, ng)


def _pack_bf16_pairs(x):
    w = x.shape[1] // 2
    xr = x.astype(BF16).astype(F32)
    lo = lax.bitcast_convert_type(xr[:, :w], jnp.uint32) >> 16
    hi = lax.bitcast_convert_type(xr[:, w:], jnp.uint32) & jnp.uint32(0xFFFF0000)
    return lo | hi


def _unpack_bf16_pairs(p):
    lo = lax.bitcast_convert_type(p << 16, F32)
    hi = lax.bitcast_convert_type(p & jnp.uint32(0xFFFF0000), F32)
    return lo, hi


def _store_rows(ref, packed):
    n = packed.shape[0]
    for c in range(ROW_CHUNKS):
        ref[pl.ds(c, n, stride=ROW_CHUNKS), :] = packed[:, c * LANE:(c + 1) * LANE]


def _load_rows(ref, n):
    return jnp.concatenate([ref[pl.ds(c, n, stride=ROW_CHUNKS), :] for c in range(ROW_CHUNKS)], axis=1)


def _route_tile(lg_t, bias_col, tri, carry_ref):
    n_e, tt = lg_t.shape
    per_group = n_e // N_GROUPS
    scores = jax.nn.sigmoid(lg_t)
    biased = scores + bias_col
    row = lax.broadcasted_iota(jnp.int32, (n_e, tt), 0)
    neg = -jnp.inf
    group_score = []
    for g in range(N_GROUPS):
        b = biased[g * per_group:(g + 1) * per_group]
        r = lax.broadcasted_iota(jnp.int32, (per_group, tt), 0) + g * per_group
        m1 = jnp.max(b, axis=0, keepdims=True)
        i1 = jnp.min(jnp.where(b == m1, r, n_e), axis=0, keepdims=True)
        m2 = jnp.max(jnp.where(r == i1, neg, b), axis=0, keepdims=True)
        group_score.append(m1 + m2)
    parts = []
    for g in range(N_GROUPS):
        ahead = jnp.zeros((1, tt), jnp.int32)
        for g2 in range(N_GROUPS):
            if g2 != g:
                beats = (group_score[g2] >= group_score[g]) if g2 < g else (group_score[g2] > group_score[g])
                ahead = ahead + beats.astype(jnp.int32)
        parts.append(jnp.where(ahead < TOPK_GROUPS, biased[g * per_group:(g + 1) * per_group], neg))
    masked = jnp.concatenate(parts, axis=0)
    work = masked
    idxs = []
    for _ in range(TOP_K):
        m = jnp.max(work, axis=0, keepdims=True)
        ii = jnp.min(jnp.where(work == m, row, n_e), axis=0, keepdims=True)
        idxs.append(ii)
        work = jnp.where(row == ii, neg, work)
    sel = work != masked
    w = jnp.where(sel, scores, 0.0)
    gate_dense = w / jnp.sum(w, axis=0, keepdims=True) * ROUTED_SCALE
    mt = jnp.where(sel, 1.0, 0.0)
    rank_dense = jnp.dot(mt.astype(BF16), tri, preferred_element_type=F32) + carry_ref[...]
    carry_ref[...] += jnp.sum(mt, axis=1, keepdims=True)
    ranks, gts = [], []
    for ii in idxs:
        oh = row == ii
        ranks.append(jnp.sum(jnp.where(oh, rank_dense, 0.0), axis=0, keepdims=True))
        gts.append(jnp.sum(jnp.where(oh, gate_dense, 0.0), axis=0, keepdims=True))
    idx_t = jnp.concatenate(idxs, axis=0)
    rank_t = jnp.concatenate(ranks, axis=0).astype(jnp.int32)
    gate_t = jnp.concatenate(gts + [jnp.zeros((LANE - TOP_K, tt), F32)], axis=0)
    return idx_t, rank_t, gate_t.T[:, :TOP_K]


def _mix_kernel(ms_ref, yg_ref, x_ref, wo_ref, gt1_ref,
                gn_ref, sh2_ref, sc2_ref, gt2_ref, wrt_ref, rb_ref, tri_ref, wsg_ref, wsu_ref, wsd_ref,
                x1_ref, h2_ref, idx_ref, rank_ref, gate_ref, cnt_ref, carry_ref):
    @pl.when((pl.program_id(0) == 0) & (pl.program_id(1) == 0))
    def _():
        carry_ref[...] = jnp.zeros_like(carry_ref)

    mix_s5 = jnp.concatenate([ms_ref[0, c] for c in range(D_MODEL // LANE)], axis=1)
    mix = mix_s5 + jnp.dot(yg_ref[0], wo_ref[...], preferred_element_type=F32)
    x1 = x_ref[0] + gt1_ref[0] * mix
    h2 = x1 * lax.rsqrt(jnp.mean(x1 * x1, axis=-1, keepdims=True) + EPS) * gn_ref[...]
    h2 = h2 * (1.0 + sc2_ref[0]) + sh2_ref[0]
    hb = h2.astype(BF16)
    _store_rows(h2_ref, _pack_bf16_pairs(h2))
    lg_t = lax.dot_general(wrt_ref[...], hb, (((1,), (1,)), ((), ())), preferred_element_type=F32)
    idx_t, rank_t, gates = _route_tile(lg_t, rb_ref[...], tri_ref[...], carry_ref)
    idx_ref[...] = idx_t
    rank_ref[...] = rank_t
    gate_ref[...] = gates
    cnt_ref[...] = carry_ref[...]
    a = _silu(jnp.dot(hb, wsg_ref[...], preferred_element_type=F32)) * jnp.dot(
        hb, wsu_ref[...], preferred_element_type=F32)
    shared = jnp.dot(a.astype(BF16), wsd_ref[...], preferred_element_type=F32)
    x1_ref[0] = x1 + gt2_ref[0] * shared


def _mix(ms5, yg, x, wo, gt1, gn, sh2, sc2, gt2, wrt, rb, wsg, wsu, wsd):
    bsz, L, d = x.shape
    tt = 512
    nt = L // tt
    T = bsz * L
    tri = (jnp.arange(tt)[:, None] < jnp.arange(tt)[None, :]).astype(BF16)
    tok = lambda n: pl.BlockSpec((1, tt, n), lambda b, i: (b, i, 0))
    vec = pl.BlockSpec((1, 1, d), lambda b, i: (b, 0, 0))
    full = lambda a: pl.BlockSpec(a.shape, lambda b, i: (0,) * a.ndim)
    lanes = pl.BlockSpec((TOP_K, tt), lambda b, i: (0, b * nt + i))
    return pl.pallas_call(
        _mix_kernel,
        grid=(bsz, nt),
        in_specs=[pl.BlockSpec((1, d // LANE, tt, LANE), lambda b, i: (b, 0, i, 0)), tok(D_GLA), tok(d), full(wo), vec,
                  full(gn), vec, vec, vec, full(wrt), full(rb), full(tri), full(wsg), full(wsu), full(wsd)],
        out_specs=[tok(d),
                   pl.BlockSpec((tt * ROW_CHUNKS, LANE), lambda b, i: (b * nt + i, 0)),
                   lanes, lanes,
                   pl.BlockSpec((tt, TOP_K), lambda b, i: (b * nt + i, 0)),
                   pl.BlockSpec((N_EXPERTS, 1), lambda b, i: (0, 0))],
        out_shape=[jax.ShapeDtypeStruct((bsz, L, d), F32),
                   jax.ShapeDtypeStruct((T * ROW_CHUNKS, LANE), jnp.uint32),
                   jax.ShapeDtypeStruct((TOP_K, T), jnp.int32),
                   jax.ShapeDtypeStruct((TOP_K, T), jnp.int32),
                   jax.ShapeDtypeStruct((T, TOP_K), F32),
                   jax.ShapeDtypeStruct((N_EXPERTS, 1), F32)],
        scratch_shapes=[pltpu.VMEM((N_EXPERTS, 1), F32)],
        compiler_params=_params("arbitrary", "arbitrary"),
        name="mix",
    )(ms5, yg, x, wo, gt1, gn, sh2, sc2, gt2, wrt, rb, tri, wsg, wsu, wsd)


def _pos_kernel(idx_ref, rank_ref, ps_ref, pos_ref):
    n_e = ps_ref.shape[0]
    tt = idx_ref.shape[1]
    row = lax.broadcasted_iota(jnp.int32, (n_e, tt), 0)
    ps = ps_ref[...]
    starts = [jnp.sum(jnp.where(row == idx_ref[k:k + 1, :], ps, 0.0), axis=0, keepdims=True)
              for k in range(TOP_K)]
    pos = jnp.concatenate(starts, axis=0).astype(jnp.int32) + rank_ref[...]
    for jh in range(tt // LANE):
        pos_ref[jh * TOP_K:(jh + 1) * TOP_K, :] = pos[:, jh * LANE:(jh + 1) * LANE]


def _pos(idx_t, rank_t, pstart):
    T = idx_t.shape[1]
    tt = 2048
    blk = pl.BlockSpec((TOP_K, tt), lambda i: (0, i))
    return pl.pallas_call(
        _pos_kernel,
        grid=(T // tt,),
        in_specs=[blk, blk, pl.BlockSpec((N_EXPERTS, 1), lambda i: (0, 0))],
        out_specs=pl.BlockSpec((tt // LANE * TOP_K, LANE), lambda i: (i, 0)),
        out_shape=jax.ShapeDtypeStruct((T // LANE * TOP_K, LANE), jnp.int32),
        compiler_params=_params("arbitrary"),
        name="pos",
    )(idx_t, rank_t, pstart.astype(F32).reshape(N_EXPERTS, 1))


def _dispatch_kernel(pend_ref, padded_ref, pos_ref, h_ref, xs_hbm, zero_ref, pos_smem, sem_pos, sem_zero, sem_row):
    tt = h_ref.shape[0]
    step = pl.program_id(0)
    load_pos = pltpu.make_async_copy(pos_ref, pos_smem, sem_pos)
    load_pos.start()

    def zero_block(start):
        return pltpu.make_async_copy(zero_ref, xs_hbm.at[pl.ds(pl.multiple_of(start, MOE_BLOCK), MOE_BLOCK)],
                                     sem_zero)

    def for_each_zero_block(fn):
        def per_expert(e, c):
            @pl.when(padded_ref[e] > 0)
            def _():
                fn(zero_block(pend_ref[e] - MOE_BLOCK))
            return c

        def per_tail_block(g, c):
            fn(zero_block(g * MOE_BLOCK))
            return c

        lax.fori_loop(0, N_EXPERTS, per_expert, 0)
        lax.fori_loop(pend_ref[N_EXPERTS - 1] // MOE_BLOCK, xs_hbm.shape[0] // MOE_BLOCK, per_tail_block, 0)

    @pl.when(step == 0)
    def _():
        zero_ref[...] = jnp.zeros_like(zero_ref)
        for_each_zero_block(lambda cp: cp.start())
        for_each_zero_block(lambda cp: cp.wait())

    load_pos.wait()
    for jh in range(tt // LANE):
        def issue(jl, c, jh=jh):
            for k in range(TOP_K):
                slot = pos_smem[(jh * TOP_K + k) * LANE + jl]
                pltpu.make_async_copy(h_ref.at[jh * LANE + jl], xs_hbm.at[slot], sem_row).start(priority=k % 2)
            return c

        lax.fori_loop(0, LANE, issue, 0, unroll=4)
    for k in range(TOP_K):
        pltpu.make_async_copy(h_ref, xs_hbm.at[pl.ds(0, tt)], sem_row).wait()


def _dispatch(pends, padded, pos_t, h_rows, n_slots):
    T = h_rows.shape[0]
    tt = 512
    return pl.pallas_call(
        _dispatch_kernel,
        grid_spec=pltpu.PrefetchScalarGridSpec(
            num_scalar_prefetch=2,
            grid=(T // tt,),
            in_specs=[pl.BlockSpec((tt * TOP_K,), lambda i, pe, pa: (i,)),
                      pl.BlockSpec((tt, ROW_CHUNKS, LANE), lambda i, pe, pa: (i, 0, 0))],
            out_specs=pl.BlockSpec(memory_space=pl.ANY),
            scratch_shapes=[pltpu.VMEM((MOE_BLOCK, ROW_CHUNKS, LANE), jnp.uint32),
                            pltpu.SMEM((tt * TOP_K,), jnp.int32),
                            pltpu.SemaphoreType.DMA, pltpu.SemaphoreType.DMA, pltpu.SemaphoreType.DMA]),
        out_shape=jax.ShapeDtypeStruct((n_slots, ROW_CHUNKS, LANE), jnp.uint32),
        compiler_params=_params("arbitrary"),
        name="dispatch",
    )(pends, padded, pos_t, h_rows)


X_GROUP = 8
X_AHEAD = 8
X_BUFS = X_AHEAD + X_GROUP
Y_BUFS = X_GROUP


def _expert_kernel(b0_ref, nb_ref, nv_ref, wg_ref, wu_ref, wd_ref, xs_hbm, ye_hbm,
                   wg_s, wu_s, wd_s, xbuf, ybuf, semx, semy, *, n_blocks):
    e = pl.program_id(0)
    nb = nb_ref[e]
    b0 = b0_ref[e]
    nv = nv_ref[0]
    rows = MOE_BLOCK * ROW_CHUNKS

    def block(ref, g):
        return ref.at[pl.ds(pl.multiple_of(g * rows, rows), rows)]

    def x_copy(g):
        slot = g % X_BUFS
        return pltpu.make_async_copy(block(xs_hbm, g), xbuf.at[slot], semx.at[slot])

    def y_copy(g):
        slot = g % Y_BUFS
        return pltpu.make_async_copy(ybuf.at[slot], block(ye_hbm, g), semy.at[slot])

    @pl.when(e == 0)
    def _():
        for g in range(X_AHEAD):
            x_copy(g).start()

    @pl.when(nb > 0)
    def _():
        wg_s[...] = wg_ref[0].astype(BF16)
        wu_s[...] = wu_ref[0].astype(BF16)
        wd_s[...] = wd_ref[0].astype(BF16)

    def fetch(g):
        x_copy(g).wait()

        @pl.when(g + X_AHEAD < nv)
        def _():
            x_copy(g + X_AHEAD).start()

    def load_x(g):
        lo, hi = _unpack_bf16_pairs(_load_rows(xbuf.at[g % X_BUFS], MOE_BLOCK))
        return jnp.concatenate([lo, hi], axis=1).astype(BF16)

    def emit(g, y):
        @pl.when(g >= Y_BUFS)
        def _():
            y_copy(g - Y_BUFS).wait()

        _store_rows(ybuf.at[g % Y_BUFS], y)
        y_copy(g).start()

    def run(g, n):
        for b in range(n):
            fetch(g + b)
        x = jnp.concatenate([load_x(g + b) for b in range(n)], axis=0) if n > 1 else load_x(g)
        a = jnp.dot(x, wg_s[...], preferred_element_type=F32)
        u = jnp.dot(x, wu_s[...], preferred_element_type=F32)
        h = (_silu(a) * u).astype(BF16)
        y = _pack_bf16_pairs(jnp.dot(h, wd_s[...], preferred_element_type=F32))
        for b in range(n):
            emit(g + b, y[b * MOE_BLOCK:(b + 1) * MOE_BLOCK])

    def group(i, c):
        run(b0 + X_GROUP * i, X_GROUP)
        return c

    lax.fori_loop(0, nb // X_GROUP, group, 0)
    done = nb // X_GROUP * X_GROUP
    n = X_GROUP // 2
    while n >= 1:
        @pl.when((nb & n) != 0)
        def _(n=n, done=done):
            run(b0 + done, n)

        done = done + (nb & n)
        n //= 2

    @pl.when(e == pl.num_programs(0) - 1)
    def _():
        for back in range(Y_BUFS, 0, -1):
            y_copy(nv - back).wait()
        ybuf[0] = jnp.zeros(ybuf.shape[1:], ybuf.dtype)

        def fill(g):
            return pltpu.make_async_copy(ybuf.at[0], block(ye_hbm, g), semy.at[0])

        lax.fori_loop(nv, n_blocks, lambda g, c: (fill(g).start(), c)[1], 0)
        lax.fori_loop(nv, n_blocks, lambda g, c: (fill(g).wait(), c)[1], 0)


def _experts(first_block, num_blocks, n_valid, xs, wg, wu, wd):
    rows = MOE_BLOCK * ROW_CHUNKS
    n_blocks = xs.shape[0] // rows
    n_e, d, _ = wg.shape
    wsel = lambda e, b0, nb, nv: (e, 0, 0)
    return pl.pallas_call(
        functools.partial(_expert_kernel, n_blocks=n_blocks),
        grid_spec=pltpu.PrefetchScalarGridSpec(
            num_scalar_prefetch=3,
            grid=(n_e,),
            in_specs=[pl.BlockSpec((1, d, D_EXPERT), wsel),
                      pl.BlockSpec((1, d, D_EXPERT), wsel),
                      pl.BlockSpec((1, D_EXPERT, d), wsel),
                      pl.BlockSpec(memory_space=pl.ANY)],
            out_specs=pl.BlockSpec(memory_space=pl.ANY),
            scratch_shapes=[pltpu.VMEM((d, D_EXPERT), BF16), pltpu.VMEM((d, D_EXPERT), BF16),
                            pltpu.VMEM((D_EXPERT, d), BF16),
                            pltpu.VMEM((X_BUFS, rows, LANE), jnp.uint32),
                            pltpu.VMEM((Y_BUFS, rows, LANE), jnp.uint32),
                            pltpu.SemaphoreType.DMA((X_BUFS,)), pltpu.SemaphoreType.DMA((Y_BUFS,))]),
        out_shape=jax.ShapeDtypeStruct(xs.shape, jnp.uint32),
        compiler_params=_params("arbitrary"),
        name="experts",
    )(first_block, num_blocks, n_valid, wg, wu, wd, xs)


def _final_kernel(pos0_ref, posn_ref, gate_ref, x_ref, gt_ref, g_ref, ye_hbm, o_ref,
                  buf_ref, pos_smem, sem_pos, sem_row):
    tt = x_ref.shape[1]
    i = pl.program_id(0)

    def gather_tile(pos_ref, tile):
        base = (tile % 2) * (TOP_K * tt)
        sem = sem_row.at[tile % 2]
        load_pos = pltpu.make_async_copy(pos_ref, pos_smem, sem_pos)
        load_pos.start()
        load_pos.wait()
        for jh in range(tt // LANE):
            def issue(jl, c, jh=jh):
                for k in range(TOP_K):
                    slot = pos_smem[(jh * TOP_K + k) * LANE + jl]
                    pltpu.make_async_copy(ye_hbm.at[slot], buf_ref.at[base + k * tt + jh * LANE + jl],
                                          sem).start(priority=k % 2)
                return c

            lax.fori_loop(0, LANE, issue, 0, unroll=4)

    @pl.when(i == 0)
    def _():
        gather_tile(pos0_ref, i)

    @pl.when(i + 1 < pl.num_programs(0))
    def _():
        gather_tile(posn_ref, i + 1)

    base = pl.multiple_of((i % 2) * (TOP_K * tt), TOP_K * tt)
    for k in range(TOP_K):
        pltpu.make_async_copy(ye_hbm.at[pl.ds(0, tt)], buf_ref.at[pl.ds(base + k * tt, tt)],
                              sem_row.at[i % 2]).wait()
    gates = gate_ref[...]
    half = D_MODEL // 2
    acc_lo = jnp.zeros((tt, half), F32)
    acc_hi = jnp.zeros((tt, half), F32)
    rows_2d = buf_ref.reshape(buf_ref.shape[0] * ROW_CHUNKS, LANE)
    for k in range(TOP_K):
        lo, hi = _unpack_bf16_pairs(_load_rows(rows_2d.at[pl.ds((base + k * tt) * ROW_CHUNKS, tt * ROW_CHUNKS)], tt))
        gk = gates[:, k:k + 1]
        acc_lo += gk * lo
        acc_hi += gk * hi
    x = x_ref[0] + gt_ref[0] * jnp.concatenate([acc_lo, acc_hi], axis=1)
    o_ref[0] = x * lax.rsqrt(jnp.mean(x * x, axis=-1, keepdims=True) + EPS) * g_ref[...]


def _final(pos_t, gates, x1, gt2, g, ye_rows):
    bsz, L, d = x1.shape
    tt = 512
    nt = L // tt
    n = bsz * nt
    tok = pl.BlockSpec((1, tt, d), lambda i: (i // nt, i % nt, 0))
    return pl.pallas_call(
        _final_kernel,
        grid=(n,),
        in_specs=[pl.BlockSpec((tt * TOP_K,), lambda i: (0,)),
                  pl.BlockSpec((tt * TOP_K,), lambda i: (jnp.minimum(i + 1, n - 1),)),
                  pl.BlockSpec((tt, TOP_K), lambda i: (i, 0)),
                  tok, pl.BlockSpec((1, 1, d), lambda i: (i // nt, 0, 0)),
                  pl.BlockSpec((1, d), lambda i: (0, 0)),
                  pl.BlockSpec(memory_space=pl.ANY)],
        out_specs=tok,
        out_shape=jax.ShapeDtypeStruct((bsz, L, d), F32),
        scratch_shapes=[pltpu.VMEM((2 * TOP_K * tt, ROW_CHUNKS, LANE), jnp.uint32),
                        pltpu.SMEM((tt * TOP_K,), jnp.int32),
                        pltpu.SemaphoreType.DMA, pltpu.SemaphoreType.DMA((2,))],
        compiler_params=_params("arbitrary"),
        name="final",
    )(pos_t, pos_t, gates, x1, gt2, g.reshape(1, d), ye_rows)


def _block_table(counts):
    counts = counts.astype(jnp.int32)
    padded = (counts + MOE_BLOCK - 1) // MOE_BLOCK * MOE_BLOCK
    pends = jnp.cumsum(padded)
    pstarts = pends - padded
    n_valid = (pends[-1] // MOE_BLOCK).reshape(1)
    return padded, pends, pstarts, n_valid


def kernel(x, c, w_ada, b_ada, g_norm_mix, w_in, s5_lambda_re, s5_lambda_im, s5_log_dt, s5_b_re, s5_b_im, s5_c_re, s5_c_im, s5_d, s5_w_glu, s5_b_glu, s5_norm_g, gla_w_g2, gla_b_g2, gla_norm_g, w_out, g_norm_moe, w_router, router_bias, exp_w_gate, exp_w_up, exp_w_down, sh_w_gate, sh_w_up, sh_w_down, g_final):
    bsz, L, d = x.shape
    T = bsz * L
    assert w_ada.shape[0] == 1, "single-layer block"
    for l in range(1):
        mod = _ada(c, w_ada[l], b_ada[l])
        sh1, sc1, gt1, sh2, sc2, gt2 = [m.reshape(bsz, 1, d) for m in jnp.split(mod, 6, axis=-1)]

        wi = w_in[l]
        o_q, o_k, o_v, o_g, o_r = D_S5, D_S5 + GLA_QK, D_S5 + 2 * GLA_QK, D_S5 + 2 * GLA_QK + D_GLA, \
            D_S5 + 2 * GLA_QK + D_GLA + GLA_LOWRANK
        w_cat = jnp.concatenate([wi[:, :o_g], wi[:, o_r:], wi[:, o_g:o_r],
                                 jnp.zeros((d, LANE - GLA_LOWRANK), wi.dtype)], axis=1).astype(BF16)
        u, q, k, v, r, glr = _inproj(x, g_norm_mix[l], sh1, sc1, w_cat)

        mats = _s5_prep(s5_lambda_re[l], s5_lambda_im[l], s5_log_dt[l], s5_b_re[l], s5_b_im[l],
                        s5_c_re[l], s5_c_im[l], s5_d[l])
        ms5 = _s5(u, mats, s5_w_glu[l].T.astype(BF16), s5_b_glu[l].reshape(D_S5, 1),
                  s5_norm_g[l].reshape(D_S5, 1), w_out[l][:D_S5].astype(BF16))

        wg2 = jnp.concatenate([gla_w_g2[l], jnp.zeros((LANE - GLA_LOWRANK, GLA_QK), F32)], axis=0).astype(BF16)
        yg = _gla(q, k, v, glr, r, wg2, gla_b_g2[l].reshape(1, GLA_QK), gla_norm_g[l].reshape(1, GLA_DV))

        x1, h2p, idx_t, rank_t, gates, counts = _mix(
            ms5, yg, x, w_out[l][D_S5:].astype(BF16), gt1, g_norm_moe[l].reshape(1, d), sh2, sc2, gt2,
            w_router[l].T.astype(BF16), router_bias[l].reshape(N_EXPERTS, 1),
            sh_w_gate[l].astype(BF16), sh_w_up[l].astype(BF16), sh_w_down[l].astype(BF16))

        n_blocks = (T * TOP_K + N_EXPERTS * (MOE_BLOCK - 1) + MOE_BLOCK - 1) // MOE_BLOCK
        n_slots = n_blocks * MOE_BLOCK
        padded, pends, pstarts, n_valid = _block_table(counts[:, 0])
        pos_t = _pos(idx_t, rank_t, pstarts).reshape(-1)
        xs = _dispatch(pends, padded, pos_t, h2p.reshape(T, ROW_CHUNKS, LANE), n_slots)
        ye = _experts(pstarts // MOE_BLOCK, padded // MOE_BLOCK, n_valid,
                      xs.reshape(n_slots * ROW_CHUNKS, LANE), exp_w_gate[l], exp_w_up[l], exp_w_down[l])
    return _final(pos_t, gates, x1, gt2, g_final, ye.reshape(n_slots, ROW_CHUNKS, LANE))
```

```python
import functools

import jax
import jax.numpy as jnp
from jax import lax
from jax.experimental import pallas as pl
from jax.experimental.pallas import tpu as pltpu

F32 = jnp.float32
BF16 = jnp.bfloat16

D_MODEL = 1024
D_S5 = 512
S5_GROUP = 16
S5_GROUPS = 32
S5_STATE = 64
S5_CHUNK = 16
S5_STEP_GROUPS = 8
S5_LEVELS = 7
S5_APOW_COLS = 16
D_GLA = 512
GLA_HEADS = 4
GLA_DK = 64
GLA_DV = 128
GLA_QK = 256
GLA_LOWRANK = 16
GLA_TAU = 16.0
GLA_CHUNK = 64
LANE = 128
N_EXPERTS = 256
TOP_K = 8
N_GROUPS = 8
TOPK_GROUPS = 4
D_EXPERT = 256
ROUTED_SCALE = 2.5
EPS = 1e-6
MOE_BLOCK = 128
ROW_CHUNKS = D_MODEL // 2 // LANE
VMEM_LIMIT = 48 * 1024 * 1024


def _silu(x):
    return x * jax.nn.sigmoid(x)


def _params(*sem):
    return pltpu.CompilerParams(dimension_semantics=sem, vmem_limit_bytes=VMEM_LIMIT)


def _ada_kernel(c_ref, w_ref, b_ref, o_ref):
    s = _silu(c_ref[...]).astype(BF16)
    o_ref[...] = jnp.dot(s, w_ref[...].astype(BF16), preferred_element_type=F32) + b_ref[...]


def _ada(c, w, b):
    bsz, d = c.shape
    n = w.shape[1]
    tn = 1024
    return pl.pallas_call(
        _ada_kernel,
        grid=(n // tn,),
        in_specs=[pl.BlockSpec((bsz, d), lambda j: (0, 0)),
                  pl.BlockSpec((d, tn), lambda j: (0, j)),
                  pl.BlockSpec((1, tn), lambda j: (0, j))],
        out_specs=pl.BlockSpec((bsz, tn), lambda j: (0, j)),
        out_shape=jax.ShapeDtypeStruct((bsz, n), F32),
        compiler_params=_params("arbitrary"),
        name="ada",
    )(c, w, b.reshape(1, n))


def _inproj_kernel(x_ref, g_ref, sh_ref, sc_ref, w_ref,
                   u_ref, q_ref, k_ref, v_ref, r_ref, glr_ref):
    x = x_ref[0]
    ms = jnp.mean(x * x, axis=-1, keepdims=True)
    h = (x * lax.rsqrt(ms + EPS)) * g_ref[...]
    h = h * (1.0 + sc_ref[0]) + sh_ref[0]
    hb = h.astype(BF16)
    u = jnp.dot(hb, w_ref[:, 0:D_S5], preferred_element_type=F32)
    for c in range(D_S5 // LANE):
        u_ref[0, c] = u[:, c * LANE:(c + 1) * LANE]
    col = D_S5
    for ref in (q_ref, k_ref, v_ref, r_ref, glr_ref):
        n = ref.shape[-1]
        ref[0] = jnp.dot(hb, w_ref[:, col:col + n], preferred_element_type=F32).astype(ref.dtype)
        col += n


def _inproj(x, g, sh, sc, w):
    bsz, L, d = x.shape
    tt = 512
    widths = (GLA_QK, GLA_QK, D_GLA, D_GLA, LANE)
    tok = lambda n: pl.BlockSpec((1, tt, n), lambda b, i: (b, i, 0))
    vec = pl.BlockSpec((1, 1, d), lambda b, i: (b, 0, 0))
    return pl.pallas_call(
        _inproj_kernel,
        grid=(bsz, L // tt),
        in_specs=[tok(d), pl.BlockSpec((1, d), lambda b, i: (0, 0)), vec, vec,
                  pl.BlockSpec(w.shape, lambda b, i: (0, 0))],
        out_specs=[pl.BlockSpec((1, D_S5 // LANE, tt, LANE), lambda b, i: (b, 0, i, 0))] + [tok(n) for n in widths],
        out_shape=[jax.ShapeDtypeStruct((bsz, D_S5 // LANE, L, LANE), F32)]
        + [jax.ShapeDtypeStruct((bsz, L, n), BF16) for n in widths],
        compiler_params=_params("arbitrary", "arbitrary"),
        name="inproj",
    )(x, g.reshape(1, d), sh, sc, w)


def _s5_prep(lam_re, lam_im, log_dt, b_re, b_im, c_re, c_im, d_skip):
    G, N, C, TC = S5_GROUPS, S5_STATE, S5_GROUP, S5_CHUNK
    hp = lax.Precision.HIGHEST
    dt = jnp.exp(log_dt)[:, None]
    lr, li = lam_re, lam_im
    mag = jnp.exp(lr * dt)
    ab_re, ab_im = mag * jnp.cos(li * dt), mag * jnp.sin(li * dt)
    den = lr * lr + li * li
    nr = ab_re - 1.0
    coef_re = ((nr * lr + ab_im * li) / den)[..., None]
    coef_im = ((ab_im * lr - nr * li) / den)[..., None]
    bb_re = coef_re * b_re - coef_im * b_im
    bb_im = coef_re * b_im + coef_im * b_re
    p = jnp.arange(TC + 1, dtype=F32)[:, None, None]
    pm = jnp.exp(lr * dt * p)
    pr, pi = pm * jnp.cos(li * dt * p), pm * jnp.sin(li * dt * p)
    ca_re = c_re[None] * pr[:, :, None, :] - c_im[None] * pi[:, :, None, :]
    ca_im = c_re[None] * pi[:, :, None, :] + c_im[None] * pr[:, :, None, :]
    ca = jnp.concatenate([ca_re[:TC], -ca_im[:TC]], axis=-1).transpose(1, 0, 2, 3).reshape(G, TC * C, 2 * N)
    kern = jnp.einsum('gxk,gki->gxi', ca, jnp.concatenate([bb_re, bb_im], axis=1),
                      precision=hp).reshape(G, TC, C, C)
    skip = jnp.eye(C, dtype=F32)[None] * d_skip[:, :, None]
    kern = kern + jnp.concatenate([skip[:, None], jnp.zeros((G, TC - 1, C, C), F32)], axis=1)
    rev = kern[:, ::-1].transpose(0, 2, 1, 3).reshape(G, C, TC * C)
    rev = jnp.concatenate([rev, jnp.zeros((G, C, (TC - 1) * C), F32)], axis=-1)
    toep_t = jnp.stack([rev[:, :, (TC - 1 - t) * C:(TC - 1 - t) * C + TC * C] for t in range(TC)],
                       axis=1).reshape(G, TC * C, TC * C)
    rr, ri = pr[TC - 1 - jnp.arange(TC)], pi[TC - 1 - jnp.arange(TC)]
    binc_re = rr[..., None] * bb_re[None] - ri[..., None] * bb_im[None]
    binc_im = rr[..., None] * bb_im[None] + ri[..., None] * bb_re[None]
    binc_re_t = binc_re.transpose(1, 2, 0, 3).reshape(G, N, TC * C)
    binc_im_t = binc_im.transpose(1, 2, 0, 3).reshape(G, N, TC * C)
    cm_re_t = ca_re[1:].transpose(1, 0, 2, 3).reshape(G, TC * C, N)
    cm_im_t = (-ca_im[1:]).transpose(1, 0, 2, 3).reshape(G, TC * C, N)
    q = (TC * 2.0 ** jnp.arange(S5_LEVELS, dtype=F32))[:, None, None]
    qm = jnp.exp(lr * dt * q)
    qr, qi = qm * jnp.cos(li * dt * q), qm * jnp.sin(li * dt * q)
    apow = jnp.stack([qr, qi], axis=1).reshape(2 * S5_LEVELS, G, N).transpose(1, 2, 0)
    apow = jnp.concatenate([apow, jnp.zeros((G, N, S5_APOW_COLS - 2 * S5_LEVELS), F32)], axis=-1)
    from_z = jnp.concatenate([toep_t, binc_re_t, binc_im_t], axis=1).astype(BF16)
    from_h = jnp.concatenate([cm_re_t, cm_im_t], axis=2).astype(BF16)
    return from_z, from_h, apow


def _s5_kernel(u_ref, eye_ref, fz_ref, fh_ref, apow_ref,
               wglu_ref, bglu_ref, sng_ref, wo_ref, o_ref, ut_ref, yt_ref):
    j = pl.program_id(1)
    n_chunks = u_ref.shape[2] // S5_CHUNK
    nt = (((1,), (1,)), ((), ()))

    @pl.when(j == 0)
    def _():
        for s in range(S5_CHUNK):
            us = jnp.concatenate([u_ref[0, c, pl.ds(s, n_chunks, stride=S5_CHUNK), :]
                                  for c in range(D_S5 // LANE)], axis=1).astype(BF16)
            ut_ref[:, s * n_chunks:(s + 1) * n_chunks] = lax.dot_general(
                eye_ref[...], us, nt, preferred_element_type=F32).astype(BF16)

    lane = lax.broadcasted_iota(jnp.int32, (S5_STATE, n_chunks), 1)
    groups = range(S5_STEP_GROUPS)
    rows = [pl.ds(pl.multiple_of(j * (S5_STEP_GROUPS * S5_GROUP) + gl * S5_GROUP, S5_GROUP), S5_GROUP)
            for gl in groups]
    zt = [jnp.concatenate([ut_ref[rows[gl], s * n_chunks:(s + 1) * n_chunks] for s in range(S5_CHUNK)], axis=0)
          for gl in groups]
    w = S5_CHUNK * S5_GROUP
    zprod = [jnp.dot(fz_ref[gl], zt[gl], preferred_element_type=F32) for gl in groups]
    xr = [jnp.where(lane >= 1, pltpu.roll(zprod[gl][w:w + S5_STATE], 1, axis=1), 0.0) for gl in groups]
    xi = [jnp.where(lane >= 1, pltpu.roll(zprod[gl][w + S5_STATE:], 1, axis=1), 0.0) for gl in groups]
    for lv in range(S5_LEVELS):
        d = 1 << lv
        for gl in groups:
            ar = apow_ref[gl, :, 2 * lv:2 * lv + 1]
            ai = apow_ref[gl, :, 2 * lv + 1:2 * lv + 2]
            sr = jnp.where(lane >= d, pltpu.roll(xr[gl], d, axis=1), 0.0)
            si = jnp.where(lane >= d, pltpu.roll(xi[gl], d, axis=1), 0.0)
            xr[gl], xi[gl] = xr[gl] + ar * sr - ai * si, xi[gl] + ar * si + ai * sr
    for gl in groups:
        state = jnp.concatenate([xr[gl], xi[gl]], axis=0).astype(BF16)
        yt = zprod[gl][:w] + jnp.dot(fh_ref[gl], state, preferred_element_type=F32)
        for t in range(S5_CHUNK):
            yt_ref[rows[gl], t * n_chunks:(t + 1) * n_chunks] = yt[t * S5_GROUP:(t + 1) * S5_GROUP, :]

    @pl.when(j == pl.num_programs(1) - 1)
    def _():
        cw = 2 * n_chunks
        for cc in range(yt_ref.shape[1] // cw):
            y = yt_ref[:, cc * cw:(cc + 1) * cw]
            g = y * (0.5 * (1.0 + jnp.tanh(0.7978845608028654 * (y + 0.044715 * (y * y * y)))))
            z = jnp.dot(wglu_ref[...], g.astype(BF16), preferred_element_type=F32) + bglu_ref[...]
            out = g * jax.nn.sigmoid(z)
            out = out * lax.rsqrt(jnp.mean(out * out, axis=0, keepdims=True) + EPS) * sng_ref[...]
            mixc = jnp.dot(out.T.astype(BF16), wo_ref[...], preferred_element_type=F32)
            for sl in range(cw // n_chunks):
                s = cc * (cw // n_chunks) + sl
                for c in range(D_MODEL // LANE):
                    o_ref[0, c, pl.ds(s, n_chunks, stride=S5_CHUNK), :] = mixc[sl * n_chunks:(sl + 1) * n_chunks,
                                                                               c * LANE:(c + 1) * LANE]


def _s5(u, mats, wglu_t, bglu, sng, wo_s5):
    bsz, _, L, _ = u.shape
    assert L // S5_CHUNK == LANE, "one lane tile of chunks per sequence"
    from_z, from_h, apow = mats
    sg = S5_STEP_GROUPS
    eye = jnp.eye(D_S5, dtype=BF16)
    grp = lambda a: pl.BlockSpec((sg,) + a.shape[1:], lambda bi, j: (j, 0, 0))
    full = lambda a: pl.BlockSpec(a.shape, lambda bi, j: (0,) * a.ndim)
    return pl.pallas_call(
        _s5_kernel,
        grid=(bsz, S5_GROUPS // sg),
        in_specs=[pl.BlockSpec((1, D_S5 // LANE, L, LANE), lambda bi, j: (bi, 0, 0, 0)), full(eye),
                  grp(from_z), grp(from_h), grp(apow), full(wglu_t), full(bglu), full(sng), full(wo_s5)],
        out_specs=pl.BlockSpec((1, D_MODEL // LANE, L, LANE), lambda bi, j: (bi, 0, 0, 0)),
        out_shape=jax.ShapeDtypeStruct((bsz, D_MODEL // LANE, L, LANE), F32),
        scratch_shapes=[pltpu.VMEM((D_S5, L), BF16), pltpu.VMEM((D_S5, L), F32)],
        compiler_params=_params("arbitrary", "arbitrary"),
        name="s5",
    )(u, eye, from_z, from_h, apow, wglu_t, bglu, sng, wo_s5)


def _gla_kernel(q_ref, k_ref, v_ref, glr_ref, r_ref, wg_ref, bg_ref, ng_ref, o_ref, st_ref):
    lt = q_ref.shape[1]
    C = GLA_CHUNK

    @pl.when(pl.program_id(1) == 0)
    def _():
        st_ref[...] = jnp.zeros_like(st_ref)

    ri = lax.broadcasted_iota(jnp.int32, (lt, lt), 0)
    ci = lax.broadcasted_iota(jnp.int32, (lt, lt), 1)
    tril = jnp.where(((ri >> 6) == (ci >> 6)) & (ci <= ri), 1.0, 0.0).astype(BF16)
    seqs = range(q_ref.shape[0])
    bcum, k, qi, ki = [], [], [], []
    for b in seqs:
        z = jnp.dot(glr_ref[b], wg_ref[...], preferred_element_type=F32) + bg_ref[...]
        log_a = (jnp.minimum(z, 0.0) - jnp.log(1.0 + jnp.exp(-jnp.abs(z)))) * (1.0 / GLA_TAU)
        la_hi = log_a.astype(BF16)
        la_lo = (log_a - la_hi.astype(F32)).astype(BF16)
        bcum.append(jnp.dot(tril, la_hi, preferred_element_type=F32)
                    + jnp.dot(tril, la_lo, preferred_element_type=F32))
        k.append(k_ref[b].astype(F32))
        qi.append(q_ref[b].astype(F32) * (GLA_DK ** -0.5) * jnp.exp(bcum[b]))
        ki.append(k[b] * jnp.exp(-bcum[b]))
    lane_head = lax.broadcasted_iota(jnp.int32, (1, GLA_QK), 1) >> 6
    causal = ((lax.broadcasted_iota(jnp.int32, (GLA_HEADS * C, C), 0) & (C - 1))
              >= lax.broadcasted_iota(jnp.int32, (GLA_HEADS * C, C), 1))
    same_head = ((lax.broadcasted_iota(jnp.int32, (D_GLA, GLA_QK), 0) >> 7)
                 == (lax.broadcasted_iota(jnp.int32, (D_GLA, GLA_QK), 1) >> 6))
    nt = (((1,), (1,)), ((), ()))
    ng = ng_ref[...]
    st = [st_ref[b] for b in seqs]
    for c in range(lt // C):
        sl = slice(c * C, (c + 1) * C)
        for b in seqs:
            bc = bcum[b][sl]
            bl = bc[C - 1:C, :]
            kd = k[b][sl] * jnp.exp(bl - bc)
            qic = qi[b][sl]
            qs = jnp.concatenate([jnp.where(lane_head == h, qic, 0.0) for h in range(GLA_HEADS)],
                                 axis=0).astype(BF16)
            sc = lax.dot_general(qs, ki[b][sl].astype(BF16), nt, preferred_element_type=F32)
            p = jnp.where(causal, sc, 0.0).astype(BF16)
            vc = v_ref[b, sl, :]
            o_intra = jnp.concatenate(
                [jnp.dot(p[h * C:(h + 1) * C], vc[:, h * GLA_DV:(h + 1) * GLA_DV], preferred_element_type=F32)
                 for h in range(GLA_HEADS)], axis=1)
            o_inter = lax.dot_general(qic.astype(BF16), st[b].astype(BF16), nt, preferred_element_type=F32)
            v_t = vc.astype(F32).T.astype(BF16)
            kv_t = jnp.dot(v_t, kd.astype(BF16), preferred_element_type=F32)
            st[b] = st[b] * jnp.exp(bl) + jnp.where(same_head, kv_t, 0.0)
            o = o_intra + o_inter
            parts = []
            for h in range(GLA_HEADS):
                oh = o[:, h * GLA_DV:(h + 1) * GLA_DV]
                oh = oh * lax.rsqrt(jnp.mean(oh * oh, axis=-1, keepdims=True) + EPS)
                parts.append(oh * ng)
            r = r_ref[b, sl, :].astype(F32)
            o_ref[b, sl, :] = (jnp.concatenate(parts, axis=1) * _silu(r)).astype(o_ref.dtype)
    for b in seqs:
        st_ref[b] = st[b]


def _gla(q, k, v, glr, r, wg, bg, ng):
    bsz, L, _ = q.shape
    lt = 256
    nb = 2
    tok = lambda n: pl.BlockSpec((nb, lt, n), lambda b, i: (b, i, 0))
    full = lambda a: pl.BlockSpec(a.shape, lambda b, i: (0,) * a.ndim)
    return pl.pallas_call(
        _gla_kernel,
        grid=(bsz // nb, L // lt),
        in_specs=[tok(GLA_QK), tok(GLA_QK), tok(D_GLA), tok(LANE), tok(D_GLA), full(wg), full(bg), full(ng)],
        out_specs=tok(D_GLA),
        out_shape=jax.ShapeDtypeStruct((bsz, L, D_GLA), BF16),
        scratch_shapes=[pltpu.VMEM((nb, D_GLA, GLA_QK), F32)],
        compiler_params=_params("arbitrary", "arbitrary"),
        name="gla",
    )(q, k, v, glr, r, wg, bg, ng)


def _pack_bf16_pairs(x):
    w = x.shape[1] // 2
    xr = x.astype(BF16).astype(F32)
    lo = lax.bitcast_convert_type(xr[:, :w], jnp.uint32) >> 16
    hi = lax.bitcast_convert_type(xr[:, w:], jnp.uint32) & jnp.uint32(0xFFFF0000)
    return lo | hi


def _unpack_bf16_pairs(p):
    lo = lax.bitcast_convert_type(p << 16, F32)
    hi = lax.bitcast_convert_type(p & jnp.uint32(0xFFFF0000), F32)
    return lo, hi


def _store_rows(ref, packed):
    n = packed.shape[0]
    for c in range(ROW_CHUNKS):
        ref[pl.ds(c, n, stride=ROW_CHUNKS), :] = packed[:, c * LANE:(c + 1) * LANE]


def _load_rows(ref, n):
    return jnp.concatenate([ref[pl.ds(c, n, stride=ROW_CHUNKS), :] for c in range(ROW_CHUNKS)], axis=1)


def _route_tile(lg_t, bias_col, tri, carry_ref):
    n_e, tt = lg_t.shape
    per_group = n_e // N_GROUPS
    scores = jax.nn.sigmoid(lg_t)
    biased = scores + bias_col
    row = lax.broadcasted_iota(jnp.int32, (n_e, tt), 0)
    neg = -jnp.inf
    group_score = []
    for g in range(N_GROUPS):
        b = biased[g * per_group:(g + 1) * per_group]
        r = lax.broadcasted_iota(jnp.int32, (per_group, tt), 0) + g * per_group
        m1 = jnp.max(b, axis=0, keepdims=True)
        i1 = jnp.min(jnp.where(b == m1, r, n_e), axis=0, keepdims=True)
        m2 = jnp.max(jnp.where(r == i1, neg, b), axis=0, keepdims=True)
        group_score.append(m1 + m2)
    parts = []
    for g in range(N_GROUPS):
        ahead = jnp.zeros((1, tt), jnp.int32)
        for g2 in range(N_GROUPS):
            if g2 != g:
                beats = (group_score[g2] >= group_score[g]) if g2 < g else (group_score[g2] > group_score[g])
                ahead = ahead + beats.astype(jnp.int32)
        parts.append(jnp.where(ahead < TOPK_GROUPS, biased[g * per_group:(g + 1) * per_group], neg))
    masked = jnp.concatenate(parts, axis=0)
    work = masked
    idxs = []
    for _ in range(TOP_K):
        m = jnp.max(work, axis=0, keepdims=True)
        ii = jnp.min(jnp.where(work == m, row, n_e), axis=0, keepdims=True)
        idxs.append(ii)
        work = jnp.where(row == ii, neg, work)
    sel = work != masked
    w = jnp.where(sel, scores, 0.0)
    gate_dense = w / jnp.sum(w, axis=0, keepdims=True) * ROUTED_SCALE
    mt = jnp.where(sel, 1.0, 0.0)
    rank_dense = jnp.dot(mt.astype(BF16), tri, preferred_element_type=F32) + carry_ref[...]
    carry_ref[...] += jnp.sum(mt, axis=1, keepdims=True)
    ranks, gts = [], []
    for ii in idxs:
        oh = row == ii
        ranks.append(jnp.sum(jnp.where(oh, rank_dense, 0.0), axis=0, keepdims=True))
        gts.append(jnp.sum(jnp.where(oh, gate_dense, 0.0), axis=0, keepdims=True))
    idx_t = jnp.concatenate(idxs, axis=0)
    rank_t = jnp.concatenate(ranks, axis=0).astype(jnp.int32)
    gate_t = jnp.concatenate(gts + [jnp.zeros((LANE - TOP_K, tt), F32)], axis=0)
    return idx_t, rank_t, gate_t.T[:, :TOP_K]


def _mix_kernel(ms_ref, yg_ref, x_ref, wo_ref, gt1_ref,
                gn_ref, sh2_ref, sc2_ref, gt2_ref, wrt_ref, rb_ref, tri_ref, wsg_ref, wsu_ref, wsd_ref,
                x1_ref, h2_ref, idx_ref, rank_ref, gate_ref, cnt_ref, carry_ref):
    @pl.when((pl.program_id(0) == 0) & (pl.program_id(1) == 0))
    def _():
        carry_ref[...] = jnp.zeros_like(carry_ref)

    mix_s5 = jnp.concatenate([ms_ref[0, c] for c in range(D_MODEL // LANE)], axis=1)
    mix = mix_s5 + jnp.dot(yg_ref[0], wo_ref[...], preferred_element_type=F32)
    x1 = x_ref[0] + gt1_ref[0] * mix
    h2 = x1 * lax.rsqrt(jnp.mean(x1 * x1, axis=-1, keepdims=True) + EPS) * gn_ref[...]
    h2 = h2 * (1.0 + sc2_ref[0]) + sh2_ref[0]
    hb = h2.astype(BF16)
    _store_rows(h2_ref, _pack_bf16_pairs(h2))
    lg_t = lax.dot_general(wrt_ref[...], hb, (((1,), (1,)), ((), ())), preferred_element_type=F32)
    idx_t, rank_t, gates = _route_tile(lg_t, rb_ref[...], tri_ref[...], carry_ref)
    idx_ref[...] = idx_t
    rank_ref[...] = rank_t
    gate_ref[...] = gates
    cnt_ref[...] = carry_ref[...]
    a = _silu(jnp.dot(hb, wsg_ref[...], preferred_element_type=F32)) * jnp.dot(
        hb, wsu_ref[...], preferred_element_type=F32)
    shared = jnp.dot(a.astype(BF16), wsd_ref[...], preferred_element_type=F32)
    x1_ref[0] = x1 + gt2_ref[0] * shared


def _mix(ms5, yg, x, wo, gt1, gn, sh2, sc2, gt2, wrt, rb, wsg, wsu, wsd):
    bsz, L, d = x.shape
    tt = 512
    nt = L // tt
    T = bsz * L
    tri = (jnp.arange(tt)[:, None] < jnp.arange(tt)[None, :]).astype(BF16)
    tok = lambda n: pl.BlockSpec((1, tt, n), lambda b, i: (b, i, 0))
    vec = pl.BlockSpec((1, 1, d), lambda b, i: (b, 0, 0))
    full = lambda a: pl.BlockSpec(a.shape, lambda b, i: (0,) * a.ndim)
    lanes = pl.BlockSpec((TOP_K, tt), lambda b, i: (0, b * nt + i))
    return pl.pallas_call(
        _mix_kernel,
        grid=(bsz, nt),
        in_specs=[pl.BlockSpec((1, d // LANE, tt, LANE), lambda b, i: (b, 0, i, 0)), tok(D_GLA), tok(d), full(wo), vec,
                  full(gn), vec, vec, vec, full(wrt), full(rb), full(tri), full(wsg), full(wsu), full(wsd)],
        out_specs=[tok(d),
                   pl.BlockSpec((tt * ROW_CHUNKS, LANE), lambda b, i: (b * nt + i, 0)),
                   lanes, lanes,
                   pl.BlockSpec((tt, TOP_K), lambda b, i: (b * nt + i, 0)),
                   pl.BlockSpec((N_EXPERTS, 1), lambda b, i: (0, 0))],
        out_shape=[jax.ShapeDtypeStruct((bsz, L, d), F32),
                   jax.ShapeDtypeStruct((T * ROW_CHUNKS, LANE), jnp.uint32),
                   jax.ShapeDtypeStruct((TOP_K, T), jnp.int32),
                   jax.ShapeDtypeStruct((TOP_K, T), jnp.int32),
                   jax.ShapeDtypeStruct((T, TOP_K), F32),
                   jax.ShapeDtypeStruct((N_EXPERTS, 1), F32)],
        scratch_shapes=[pltpu.VMEM((N_EXPERTS, 1), F32)],
        compiler_params=_params("arbitrary", "arbitrary"),
        name="mix",
    )(ms5, yg, x, wo, gt1, gn, sh2, sc2, gt2, wrt, rb, tri, wsg, wsu, wsd)


def _pos_kernel(idx_ref, rank_ref, ps_ref, pos_ref):
    n_e = ps_ref.shape[0]
    tt = idx_ref.shape[1]
    row = lax.broadcasted_iota(jnp.int32, (n_e, tt), 0)
    ps = ps_ref[...]
    starts = [jnp.sum(jnp.where(row == idx_ref[k:k + 1, :], ps, 0.0), axis=0, keepdims=True)
              for k in range(TOP_K)]
    pos = jnp.concatenate(starts, axis=0).astype(jnp.int32) + rank_ref[...]
    for jh in range(tt // LANE):
        pos_ref[jh * TOP_K:(jh + 1) * TOP_K, :] = pos[:, jh * LANE:(jh + 1) * LANE]


def _pos(idx_t, rank_t, pstart):
    T = idx_t.shape[1]
    tt = 2048
    blk = pl.BlockSpec((TOP_K, tt), lambda i: (0, i))
    return pl.pallas_call(
        _pos_kernel,
        grid=(T // tt,),
        in_specs=[blk, blk, pl.BlockSpec((N_EXPERTS, 1), lambda i: (0, 0))],
        out_specs=pl.BlockSpec((tt // LANE * TOP_K, LANE), lambda i: (i, 0)),
        out_shape=jax.ShapeDtypeStruct((T // LANE * TOP_K, LANE), jnp.int32),
        compiler_params=_params("arbitrary"),
        name="pos",
    )(idx_t, rank_t, pstart.astype(F32).reshape(N_EXPERTS, 1))


def _dispatch_kernel(pend_ref, padded_ref, pos_ref, h_ref, xs_hbm, zero_ref, pos_smem, sem_pos, sem_zero, sem_row):
    tt = h_ref.shape[0]
    step = pl.program_id(0)
    load_pos = pltpu.make_async_copy(pos_ref, pos_smem, sem_pos)
    load_pos.start()

    def zero_block(start):
        return pltpu.make_async_copy(zero_ref, xs_hbm.at[pl.ds(pl.multiple_of(start, MOE_BLOCK), MOE_BLOCK)],
                                     sem_zero)

    def for_each_zero_block(fn):
        def per_expert(e, c):
            @pl.when(padded_ref[e] > 0)
            def _():
                fn(zero_block(pend_ref[e] - MOE_BLOCK))
            return c

        def per_tail_block(g, c):
            fn(zero_block(g * MOE_BLOCK))
            return c

        lax.fori_loop(0, N_EXPERTS, per_expert, 0)
        lax.fori_loop(pend_ref[N_EXPERTS - 1] // MOE_BLOCK, xs_hbm.shape[0] // MOE_BLOCK, per_tail_block, 0)

    @pl.when(step == 0)
    def _():
        zero_ref[...] = jnp.zeros_like(zero_ref)
        for_each_zero_block(lambda cp: cp.start())
        for_each_zero_block(lambda cp: cp.wait())

    load_pos.wait()
    for jh in range(tt // LANE):
        def issue(jl, c, jh=jh):
            for k in range(TOP_K):
                slot = pos_smem[(jh * TOP_K + k) * LANE + jl]
                pltpu.make_async_copy(h_ref.at[jh * LANE + jl], xs_hbm.at[slot], sem_row).start(priority=k % 2)
            return c

        lax.fori_loop(0, LANE, issue, 0, unroll=4)
    for k in range(TOP_K):
        pltpu.make_async_copy(h_ref, xs_hbm.at[pl.ds(0, tt)], sem_row).wait()


def _dispatch(pends, padded, pos_t, h_rows, n_slots):
    T = h_rows.shape[0]
    tt = 512
    return pl.pallas_call(
        _dispatch_kernel,
        grid_spec=pltpu.PrefetchScalarGridSpec(
            num_scalar_prefetch=2,
            grid=(T // tt,),
            in_specs=[pl.BlockSpec((tt * TOP_K,), lambda i, pe, pa: (i,)),
                      pl.BlockSpec((tt, ROW_CHUNKS, LANE), lambda i, pe, pa: (i, 0, 0))],
            out_specs=pl.BlockSpec(memory_space=pl.ANY),
            scratch_shapes=[pltpu.VMEM((MOE_BLOCK, ROW_CHUNKS, LANE), jnp.uint32),
                            pltpu.SMEM((tt * TOP_K,), jnp.int32),
                            pltpu.SemaphoreType.DMA, pltpu.SemaphoreType.DMA, pltpu.SemaphoreType.DMA]),
        out_shape=jax.ShapeDtypeStruct((n_slots, ROW_CHUNKS, LANE), jnp.uint32),
        compiler_params=_params("arbitrary"),
        name="dispatch",
    )(pends, padded, pos_t, h_rows)


X_GROUP = 8
X_AHEAD = 8
X_BUFS = X_AHEAD + X_GROUP
Y_BUFS = X_GROUP


def _expert_kernel(b0_ref, nb_ref, nv_ref, wg_ref, wu_ref, wd_ref, xs_hbm, ye_hbm,
                   wg_s, wu_s, wd_s, xbuf, ybuf, semx, semy, *, n_blocks):
    e = pl.program_id(0)
    nb = nb_ref[e]
    b0 = b0_ref[e]
    nv = nv_ref[0]
    rows = MOE_BLOCK * ROW_CHUNKS

    def block(ref, g):
        return ref.at[pl.ds(pl.multiple_of(g * rows, rows), rows)]

    def x_copy(g):
        slot = g % X_BUFS
        return pltpu.make_async_copy(block(xs_hbm, g), xbuf.at[slot], semx.at[slot])

    def y_copy(g):
        slot = g % Y_BUFS
        return pltpu.make_async_copy(ybuf.at[slot], block(ye_hbm, g), semy.at[slot])

    @pl.when(e == 0)
    def _():
        for g in range(X_AHEAD):
            x_copy(g).start()

    @pl.when(nb > 0)
    def _():
        wg_s[...] = wg_ref[0].astype(BF16)
        wu_s[...] = wu_ref[0].astype(BF16)
        wd_s[...] = wd_ref[0].astype(BF16)

    def fetch(g):
        x_copy(g).wait()

        @pl.when(g + X_AHEAD < nv)
        def _():
            x_copy(g + X_AHEAD).start()

    def load_x(g):
        lo, hi = _unpack_bf16_pairs(_load_rows(xbuf.at[g % X_BUFS], MOE_BLOCK))
        return jnp.concatenate([lo, hi], axis=1).astype(BF16)

    def emit(g, y):
        @pl.when(g >= Y_BUFS)
        def _():
            y_copy(g - Y_BUFS).wait()

        _store_rows(ybuf.at[g % Y_BUFS], y)
        y_copy(g).start()

    def run(g, n):
        for b in range(n):
            fetch(g + b)
        x = jnp.concatenate([load_x(g + b) for b in range(n)], axis=0) if n > 1 else load_x(g)
        a = jnp.dot(x, wg_s[...], preferred_element_type=F32)
        u = jnp.dot(x, wu_s[...], preferred_element_type=F32)
        h = (_silu(a) * u).astype(BF16)
        y = _pack_bf16_pairs(jnp.dot(h, wd_s[...], preferred_element_type=F32))
        for b in range(n):
            emit(g + b, y[b * MOE_BLOCK:(b + 1) * MOE_BLOCK])

    def group(i, c):
        run(b0 + X_GROUP * i, X_GROUP)
        return c

    lax.fori_loop(0, nb // X_GROUP, group, 0)
    done = nb // X_GROUP * X_GROUP
    n = X_GROUP // 2
    while n >= 1:
        @pl.when((nb & n) != 0)
        def _(n=n, done=done):
            run(b0 + done, n)

        done = done + (nb & n)
        n //= 2

    @pl.when(e == pl.num_programs(0) - 1)
    def _():
        for back in range(Y_BUFS, 0, -1):
            y_copy(nv - back).wait()
        ybuf[0] = jnp.zeros(ybuf.shape[1:], ybuf.dtype)

        def fill(g):
            return pltpu.make_async_copy(ybuf.at[0], block(ye_hbm, g), semy.at[0])

        lax.fori_loop(nv, n_blocks, lambda g, c: (fill(g).start(), c)[1], 0)
        lax.fori_loop(nv, n_blocks, lambda g, c: (fill(g).wait(), c)[1], 0)


def _experts(first_block, num_blocks, n_valid, xs, wg, wu, wd):
    rows = MOE_BLOCK * ROW_CHUNKS
    n_blocks = xs.shape[0] // rows
    n_e, d, _ = wg.shape
    wsel = lambda e, b0, nb, nv: (e, 0, 0)
    return pl.pallas_call(
        functools.partial(_expert_kernel, n_blocks=n_blocks),
        grid_spec=pltpu.PrefetchScalarGridSpec(
            num_scalar_prefetch=3,
            grid=(n_e,),
            in_specs=[pl.BlockSpec((1, d, D_EXPERT), wsel),
                      pl.BlockSpec((1, d, D_EXPERT), wsel),
                      pl.BlockSpec((1, D_EXPERT, d), wsel),
                      pl.BlockSpec(memory_space=pl.ANY)],
            out_specs=pl.BlockSpec(memory_space=pl.ANY),
            scratch_shapes=[pltpu.VMEM((d, D_EXPERT), BF16), pltpu.VMEM((d, D_EXPERT), BF16),
                            pltpu.VMEM((D_EXPERT, d), BF16),
                            pltpu.VMEM((X_BUFS, rows, LANE), jnp.uint32),
                            pltpu.VMEM((Y_BUFS, rows, LANE), jnp.uint32),
                            pltpu.SemaphoreType.DMA((X_BUFS,)), pltpu.SemaphoreType.DMA((Y_BUFS,))]),
        out_shape=jax.ShapeDtypeStruct(xs.shape, jnp.uint32),
        compiler_params=_params("arbitrary"),
        name="experts",
    )(first_block, num_blocks, n_valid, wg, wu, wd, xs)


def _final_kernel(pos0_ref, posn_ref, gate_ref, x_ref, gt_ref, g_ref, ye_hbm, o_ref,
                  buf_ref, pos_smem, sem_pos, sem_row):
    tt = x_ref.shape[1]
    i = pl.program_id(0)

    def gather_tile(pos_ref, tile):
        base = (tile % 2) * (TOP_K * tt)
        sem = sem_row.at[tile % 2]
        load_pos = pltpu.make_async_copy(pos_ref, pos_smem, sem_pos)
        load_pos.start()
        load_pos.wait()
        for jh in range(tt // LANE):
            def issue(jl, c, jh=jh):
                for k in range(TOP_K):
                    slot = pos_smem[(jh * TOP_K + k) * LANE + jl]
                    pltpu.make_async_copy(ye_hbm.at[slot], buf_ref.at[base + k * tt + jh * LANE + jl],
                                          sem).start(priority=k % 2)
                return c

            lax.fori_loop(0, LANE, issue, 0, unroll=4)

    @pl.when(i == 0)
    def _():
        gather_tile(pos0_ref, i)

    @pl.when(i + 1 < pl.num_programs(0))
    def _():
        gather_tile(posn_ref, i + 1)

    base = pl.multiple_of((i % 2) * (TOP_K * tt), TOP_K * tt)
    for k in range(TOP_K):
        pltpu.make_async_copy(ye_hbm.at[pl.ds(0, tt)], buf_ref.at[pl.ds(base + k * tt, tt)],
                              sem_row.at[i % 2]).wait()
    gates = gate_ref[...]
    half = D_MODEL // 2
    acc_lo = jnp.zeros((tt, half), F32)
    acc_hi = jnp.zeros((tt, half), F32)
    rows_2d = buf_ref.reshape(buf_ref.shape[0] * ROW_CHUNKS, LANE)
    for k in range(TOP_K):
        lo, hi = _unpack_bf16_pairs(_load_rows(rows_2d.at[pl.ds((base + k * tt) * ROW_CHUNKS, tt * ROW_CHUNKS)], tt))
        gk = gates[:, k:k + 1]
        acc_lo += gk * lo
        acc_hi += gk * hi
    x = x_ref[0] + gt_ref[0] * jnp.concatenate([acc_lo, acc_hi], axis=1)
    o_ref[0] = x * lax.rsqrt(jnp.mean(x * x, axis=-1, keepdims=True) + EPS) * g_ref[...]


def _final(pos_t, gates, x1, gt2, g, ye_rows):
    bsz, L, d = x1.shape
    tt = 512
    nt = L // tt
    n = bsz * nt
    tok = pl.BlockSpec((1, tt, d), lambda i: (i // nt, i % nt, 0))
    return pl.pallas_call(
        _final_kernel,
        grid=(n,),
        in_specs=[pl.BlockSpec((tt * TOP_K,), lambda i: (0,)),
                  pl.BlockSpec((tt * TOP_K,), lambda i: (jnp.minimum(i + 1, n - 1),)),
                  pl.BlockSpec((tt, TOP_K), lambda i: (i, 0)),
                  tok, pl.BlockSpec((1, 1, d), lambda i: (i // nt, 0, 0)),
                  pl.BlockSpec((1, d), lambda i: (0, 0)),
                  pl.BlockSpec(memory_space=pl.ANY)],
        out_specs=tok,
        out_shape=jax.ShapeDtypeStruct((bsz, L, d), F32),
        scratch_shapes=[pltpu.VMEM((2 * TOP_K * tt, ROW_CHUNKS, LANE), jnp.uint32),
                        pltpu.SMEM((tt * TOP_K,), jnp.int32),
                        pltpu.SemaphoreType.DMA, pltpu.SemaphoreType.DMA((2,))],
        compiler_params=_params("arbitrary"),
        name="final",
    )(pos_t, pos_t, gates, x1, gt2, g.reshape(1, d), ye_rows)


def _block_table(counts):
    counts = counts.astype(jnp.int32)
    padded = (counts + MOE_BLOCK - 1) // MOE_BLOCK * MOE_BLOCK
    pends = jnp.cumsum(padded)
    pstarts = pends - padded
    n_valid = (pends[-1] // MOE_BLOCK).reshape(1)
    return padded, pends, pstarts, n_valid


def kernel(x, c, w_ada, b_ada, g_norm_mix, w_in, s5_lambda_re, s5_lambda_im, s5_log_dt, s5_b_re, s5_b_im, s5_c_re, s5_c_im, s5_d, s5_w_glu, s5_b_glu, s5_norm_g, gla_w_g2, gla_b_g2, gla_norm_g, w_out, g_norm_moe, w_router, router_bias, exp_w_gate, exp_w_up, exp_w_down, sh_w_gate, sh_w_up, sh_w_down, g_final):
    bsz, L, d = x.shape
    T = bsz * L
    assert w_ada.shape[0] == 1, "single-layer block"
    for l in range(1):
        mod = _ada(c, w_ada[l], b_ada[l])
        sh1, sc1, gt1, sh2, sc2, gt2 = [m.reshape(bsz, 1, d) for m in jnp.split(mod, 6, axis=-1)]

        wi = w_in[l]
        o_q, o_k, o_v, o_g, o_r = D_S5, D_S5 + GLA_QK, D_S5 + 2 * GLA_QK, D_S5 + 2 * GLA_QK + D_GLA, \
            D_S5 + 2 * GLA_QK + D_GLA + GLA_LOWRANK
        w_cat = jnp.concatenate([wi[:, :o_g], wi[:, o_r:], wi[:, o_g:o_r],
                                 jnp.zeros((d, LANE - GLA_LOWRANK), wi.dtype)], axis=1).astype(BF16)
        u, q, k, v, r, glr = _inproj(x, g_norm_mix[l], sh1, sc1, w_cat)

        mats = _s5_prep(s5_lambda_re[l], s5_lambda_im[l], s5_log_dt[l], s5_b_re[l], s5_b_im[l],
                        s5_c_re[l], s5_c_im[l], s5_d[l])
        ms5 = _s5(u, mats, s5_w_glu[l].T.astype(BF16), s5_b_glu[l].reshape(D_S5, 1),
                  s5_norm_g[l].reshape(D_S5, 1), w_out[l][:D_S5].astype(BF16))

        wg2 = jnp.concatenate([gla_w_g2[l], jnp.zeros((LANE - GLA_LOWRANK, GLA_QK), F32)], axis=0).astype(BF16)
        yg = _gla(q, k, v, glr, r, wg2, gla_b_g2[l].reshape(1, GLA_QK), gla_norm_g[l].reshape(1, GLA_DV))

        x1, h2p, idx_t, rank_t, gates, counts = _mix(
            ms5, yg, x, w_out[l][D_S5:].astype(BF16), gt1, g_norm_moe[l].reshape(1, d), sh2, sc2, gt2,
            w_router[l].T.astype(BF16), router_bias[l].reshape(N_EXPERTS, 1),
            sh_w_gate[l].astype(BF16), sh_w_up[l].astype(BF16), sh_w_down[l].astype(BF16))

        n_blocks = (T * TOP_K + N_EXPERTS * (MOE_BLOCK - 1) + MOE_BLOCK - 1) // MOE_BLOCK
        n_slots = n_blocks * MOE_BLOCK
        padded, pends, pstarts, n_valid = _block_table(counts[:, 0])
        pos_t = _pos(idx_t, rank_t, pstarts).reshape(-1)
        xs = _dispatch(pends, padded, pos_t, h2p.reshape(T, ROW_CHUNKS, LANE), n_slots)
        ye = _experts(pstarts // MOE_BLOCK, padded // MOE_BLOCK, n_valid,
                      xs.reshape(n_slots * ROW_CHUNKS, LANE), exp_w_gate[l], exp_w_up[l], exp_w_down[l])
    return _final(pos_t, gates, x1, gt2, g_final, ye.reshape(n_slots, ROW_CHUNKS, LANE))
```

```python
import functools

import jax
import jax.numpy as jnp
from jax import lax
from jax.experimental import pallas as pl
from jax.experimental.pallas import tpu as pltpu

F32 = jnp.float32
BF16 = jnp.bfloat16

D_MODEL = 1024
D_S5 = 512
S5_GROUP = 16
S5_GROUPS = 32
S5_STATE = 64
S5_CHUNK = 16
S5_STEP_GROUPS = 8
S5_LEVELS = 7
S5_APOW_COLS = 16
D_GLA = 512
GLA_HEADS = 4
GLA_DK = 64
GLA_DV = 128
GLA_QK = 256
GLA_LOWRANK = 16
GLA_TAU = 16.0
GLA_CHUNK = 64
LANE = 128
N_EXPERTS = 256
TOP_K = 8
N_GROUPS = 8
TOPK_GROUPS = 4
D_EXPERT = 256
ROUTED_SCALE = 2.5
EPS = 1e-6
MOE_BLOCK = 128
ROW_CHUNKS = D_MODEL // 2 // LANE
VMEM_LIMIT = 48 * 1024 * 1024


def _silu(x):
    return x * jax.nn.sigmoid(x)


def _params(*sem):
    return pltpu.CompilerParams(dimension_semantics=sem, vmem_limit_bytes=VMEM_LIMIT)


def _ada_kernel(c_ref, w_ref, b_ref, o_ref):
    s = _silu(c_ref[...]).astype(BF16)
    o_ref[...] = jnp.dot(s, w_ref[...].astype(BF16), preferred_element_type=F32) + b_ref[...]


def _ada(c, w, b):
    bsz, d = c.shape
    n = w.shape[1]
    tn = 1024
    return pl.pallas_call(
        _ada_kernel,
        grid=(n // tn,),
        in_specs=[pl.BlockSpec((bsz, d), lambda j: (0, 0)),
                  pl.BlockSpec((d, tn), lambda j: (0, j)),
                  pl.BlockSpec((1, tn), lambda j: (0, j))],
        out_specs=pl.BlockSpec((bsz, tn), lambda j: (0, j)),
        out_shape=jax.ShapeDtypeStruct((bsz, n), F32),
        compiler_params=_params("arbitrary"),
        name="ada",
    )(c, w, b.reshape(1, n))


def _inproj_kernel(x_ref, g_ref, sh_ref, sc_ref, w_ref,
                   u_ref, q_ref, k_ref, v_ref, r_ref, glr_ref):
    x = x_ref[0]
    ms = jnp.mean(x * x, axis=-1, keepdims=True)
    h = (x * lax.rsqrt(ms + EPS)) * g_ref[...]
    h = h * (1.0 + sc_ref[0]) + sh_ref[0]
    hb = h.astype(BF16)
    u = jnp.dot(hb, w_ref[:, 0:D_S5], preferred_element_type=F32)
    for c in range(D_S5 // LANE):
        u_ref[0, c] = u[:, c * LANE:(c + 1) * LANE]
    col = D_S5
    for ref in (q_ref, k_ref, v_ref, r_ref, glr_ref):
        n = ref.shape[-1]
        ref[0] = jnp.dot(hb, w_ref[:, col:col + n], preferred_element_type=F32).astype(ref.dtype)
        col += n


def _inproj(x, g, sh, sc, w):
    bsz, L, d = x.shape
    tt = 1024
    widths = (GLA_QK, GLA_QK, D_GLA, D_GLA, LANE)
    tok = lambda n: pl.BlockSpec((1, tt, n), lambda b, i: (b, i, 0))
    vec = pl.BlockSpec((1, 1, d), lambda b, i: (b, 0, 0))
    return pl.pallas_call(
        _inproj_kernel,
        grid=(bsz, L // tt),
        in_specs=[tok(d), pl.BlockSpec((1, d), lambda b, i: (0, 0)), vec, vec,
                  pl.BlockSpec(w.shape, lambda b, i: (0, 0))],
        out_specs=[pl.BlockSpec((1, D_S5 // LANE, tt, LANE), lambda b, i: (b, 0, i, 0))] + [tok(n) for n in widths],
        out_shape=[jax.ShapeDtypeStruct((bsz, D_S5 // LANE, L, LANE), F32)]
        + [jax.ShapeDtypeStruct((bsz, L, n), BF16) for n in widths],
        compiler_params=_params("arbitrary", "arbitrary"),
        name="inproj",
    )(x, g.reshape(1, d), sh, sc, w)


def _s5_prep(lam_re, lam_im, log_dt, b_re, b_im, c_re, c_im, d_skip):
    G, N, C, TC = S5_GROUPS, S5_STATE, S5_GROUP, S5_CHUNK
    hp = lax.Precision.HIGHEST
    dt = jnp.exp(log_dt)[:, None]
    lr, li = lam_re, lam_im
    mag = jnp.exp(lr * dt)
    ab_re, ab_im = mag * jnp.cos(li * dt), mag * jnp.sin(li * dt)
    den = lr * lr + li * li
    nr = ab_re - 1.0
    coef_re = ((nr * lr + ab_im * li) / den)[..., None]
    coef_im = ((ab_im * lr - nr * li) / den)[..., None]
    bb_re = coef_re * b_re - coef_im * b_im
    bb_im = coef_re * b_im + coef_im * b_re
    p = jnp.arange(TC + 1, dtype=F32)[:, None, None]
    pm = jnp.exp(lr * dt * p)
    pr, pi = pm * jnp.cos(li * dt * p), pm * jnp.sin(li * dt * p)
    ca_re = c_re[None] * pr[:, :, None, :] - c_im[None] * pi[:, :, None, :]
    ca_im = c_re[None] * pi[:, :, None, :] + c_im[None] * pr[:, :, None, :]
    ca = jnp.concatenate([ca_re[:TC], -ca_im[:TC]], axis=-1).transpose(1, 0, 2, 3).reshape(G, TC * C, 2 * N)
    kern = jnp.einsum('gxk,gki->gxi', ca, jnp.concatenate([bb_re, bb_im], axis=1),
                      precision=hp).reshape(G, TC, C, C)
    skip = jnp.eye(C, dtype=F32)[None] * d_skip[:, :, None]
    kern = kern + jnp.concatenate([skip[:, None], jnp.zeros((G, TC - 1, C, C), F32)], axis=1)
    rev = kern[:, ::-1].transpose(0, 2, 1, 3).reshape(G, C, TC * C)
    rev = jnp.concatenate([rev, jnp.zeros((G, C, (TC - 1) * C), F32)], axis=-1)
    toep_t = jnp.stack([rev[:, :, (TC - 1 - t) * C:(TC - 1 - t) * C + TC * C] for t in range(TC)],
                       axis=1).reshape(G, TC * C, TC * C)
    rr, ri = pr[TC - 1 - jnp.arange(TC)], pi[TC - 1 - jnp.arange(TC)]
    binc_re = rr[..., None] * bb_re[None] - ri[..., None] * bb_im[None]
    binc_im = rr[..., None] * bb_im[None] + ri[..., None] * bb_re[None]
    binc_re_t = binc_re.transpose(1, 2, 0, 3).reshape(G, N, TC * C)
    binc_im_t = binc_im.transpose(1, 2, 0, 3).reshape(G, N, TC * C)
    cm_re_t = ca_re[1:].transpose(1, 0, 2, 3).reshape(G, TC * C, N)
    cm_im_t = (-ca_im[1:]).transpose(1, 0, 2, 3).reshape(G, TC * C, N)
    q = (TC * 2.0 ** jnp.arange(S5_LEVELS, dtype=F32))[:, None, None]
    qm = jnp.exp(lr * dt * q)
    qr, qi = qm * jnp.cos(li * dt * q), qm * jnp.sin(li * dt * q)
    apow = jnp.stack([qr, qi], axis=1).reshape(2 * S5_LEVELS, G, N).transpose(1, 2, 0)
    apow = jnp.concatenate([apow, jnp.zeros((G, N, S5_APOW_COLS - 2 * S5_LEVELS), F32)], axis=-1)
    from_z = jnp.concatenate([toep_t, binc_re_t, binc_im_t], axis=1).astype(BF16)
    from_h = jnp.concatenate([cm_re_t, cm_im_t], axis=2).astype(BF16)
    return from_z, from_h, apow


def _s5_kernel(u_ref, eye_ref, fz_ref, fh_ref, apow_ref,
               wglu_ref, bglu_ref, sng_ref, wo_ref, o_ref, ut_ref, yt_ref, ot_ref):
    j = pl.program_id(1)
    n_chunks = u_ref.shape[2] // S5_CHUNK
    nt = (((1,), (1,)), ((), ()))

    @pl.when(j == 0)
    def _():
        for s in range(S5_CHUNK):
            us = jnp.concatenate([u_ref[0, c, pl.ds(s, n_chunks, stride=S5_CHUNK), :]
                                  for c in range(D_S5 // LANE)], axis=1).astype(BF16)
            ut_ref[:, s * n_chunks:(s + 1) * n_chunks] = lax.dot_general(
                eye_ref[...], us, nt, preferred_element_type=F32).astype(BF16)

    lane = lax.broadcasted_iota(jnp.int32, (S5_STATE, n_chunks), 1)
    groups = range(S5_STEP_GROUPS)
    rows = [pl.ds(pl.multiple_of(j * (S5_STEP_GROUPS * S5_GROUP) + gl * S5_GROUP, S5_GROUP), S5_GROUP)
            for gl in groups]
    zt = [jnp.concatenate([ut_ref[rows[gl], s * n_chunks:(s + 1) * n_chunks] for s in range(S5_CHUNK)], axis=0)
          for gl in groups]
    w = S5_CHUNK * S5_GROUP
    zprod = [jnp.dot(fz_ref[gl], zt[gl], preferred_element_type=F32) for gl in groups]
    xr = [jnp.where(lane >= 1, pltpu.roll(zprod[gl][w:w + S5_STATE], 1, axis=1), 0.0) for gl in groups]
    xi = [jnp.where(lane >= 1, pltpu.roll(zprod[gl][w + S5_STATE:], 1, axis=1), 0.0) for gl in groups]
    for lv in range(S5_LEVELS):
        d = 1 << lv
        for gl in groups:
            ar = apow_ref[gl, :, 2 * lv:2 * lv + 1]
            ai = apow_ref[gl, :, 2 * lv + 1:2 * lv + 2]
            sr = jnp.where(lane >= d, pltpu.roll(xr[gl], d, axis=1), 0.0)
            si = jnp.where(lane >= d, pltpu.roll(xi[gl], d, axis=1), 0.0)
            xr[gl], xi[gl] = xr[gl] + ar * sr - ai * si, xi[gl] + ar * si + ai * sr
    for gl in groups:
        state = jnp.concatenate([xr[gl], xi[gl]], axis=0).astype(BF16)
        yt = zprod[gl][:w] + jnp.dot(fh_ref[gl], state, preferred_element_type=F32)
        for t in range(S5_CHUNK):
            yt_ref[rows[gl], t * n_chunks:(t + 1) * n_chunks] = yt[t * S5_GROUP:(t + 1) * S5_GROUP, :]

    @pl.when(j == pl.num_programs(1) - 1)
    def _():
        cw = 2 * n_chunks
        for cc in range(yt_ref.shape[1] // cw):
            y = yt_ref[:, cc * cw:(cc + 1) * cw]
            g = y * (0.5 * (1.0 + jnp.tanh(0.7978845608028654 * (y + 0.044715 * (y * y * y)))))
            z = jnp.dot(wglu_ref[...], g.astype(BF16), preferred_element_type=F32) + bglu_ref[...]
            out = g * jax.nn.sigmoid(z)
            out = out * lax.rsqrt(jnp.mean(out * out, axis=0, keepdims=True) + EPS) * sng_ref[...]
            ot_ref[cc * cw:(cc + 1) * cw, :] = out.T.astype(BF16)
        rows = ot_ref.shape[0] // 2
        for half in range(2):
            mix = jnp.dot(ot_ref[half * rows:(half + 1) * rows, :], wo_ref[...], preferred_element_type=F32)
            for sl in range(rows // n_chunks):
                s = half * (rows // n_chunks) + sl
                for c in range(D_MODEL // LANE):
                    o_ref[0, c, pl.ds(s, n_chunks, stride=S5_CHUNK), :] = mix[sl * n_chunks:(sl + 1) * n_chunks,
                                                                              c * LANE:(c + 1) * LANE]


def _s5(u, mats, wglu_t, bglu, sng, wo_s5):
    bsz, _, L, _ = u.shape
    assert L // S5_CHUNK == LANE, "one lane tile of chunks per sequence"
    from_z, from_h, apow = mats
    sg = S5_STEP_GROUPS
    eye = jnp.eye(D_S5, dtype=BF16)
    grp = lambda a: pl.BlockSpec((sg,) + a.shape[1:], lambda bi, j: (j, 0, 0))
    full = lambda a: pl.BlockSpec(a.shape, lambda bi, j: (0,) * a.ndim)
    return pl.pallas_call(
        _s5_kernel,
        grid=(bsz, S5_GROUPS // sg),
        in_specs=[pl.BlockSpec((1, D_S5 // LANE, L, LANE), lambda bi, j: (bi, 0, 0, 0)), full(eye),
                  grp(from_z), grp(from_h), grp(apow), full(wglu_t), full(bglu), full(sng), full(wo_s5)],
        out_specs=pl.BlockSpec((1, D_MODEL // LANE, L, LANE), lambda bi, j: (bi, 0, 0, 0)),
        out_shape=jax.ShapeDtypeStruct((bsz, D_MODEL // LANE, L, LANE), F32),
        scratch_shapes=[pltpu.VMEM((D_S5, L), BF16), pltpu.VMEM((D_S5, L), F32), pltpu.VMEM((L, D_S5), BF16)],
        compiler_params=_params("arbitrary", "arbitrary"),
        name="s5",
    )(u, eye, from_z, from_h, apow, wglu_t, bglu, sng, wo_s5)


def _gla_kernel(q_ref, k_ref, v_ref, glr_ref, r_ref, wg_ref, bg_ref, ng_ref, o_ref, st_ref):
    lt = q_ref.shape[1]
    C = GLA_CHUNK

    @pl.when(pl.program_id(1) == 0)
    def _():
        st_ref[...] = jnp.zeros_like(st_ref)

    ri = lax.broadcasted_iota(jnp.int32, (lt, lt), 0)
    ci = lax.broadcasted_iota(jnp.int32, (lt, lt), 1)
    tril = jnp.where(((ri >> 6) == (ci >> 6)) & (ci <= ri), 1.0, 0.0).astype(BF16)
    seqs = range(q_ref.shape[0])
    bcum, k, qi, ki = [], [], [], []
    for b in seqs:
        z = jnp.dot(glr_ref[b], wg_ref[...], preferred_element_type=F32) + bg_ref[...]
        log_a = (jnp.minimum(z, 0.0) - jnp.log(1.0 + jnp.exp(-jnp.abs(z)))) * (1.0 / GLA_TAU)
        la_hi = log_a.astype(BF16)
        la_lo = (log_a - la_hi.astype(F32)).astype(BF16)
        bcum.append(jnp.dot(tril, la_hi, preferred_element_type=F32)
                    + jnp.dot(tril, la_lo, preferred_element_type=F32))
        k.append(k_ref[b].astype(F32))
        qi.append(q_ref[b].astype(F32) * (GLA_DK ** -0.5) * jnp.exp(bcum[b]))
        ki.append(k[b] * jnp.exp(-bcum[b]))
    lane_head = lax.broadcasted_iota(jnp.int32, (1, GLA_QK), 1) >> 6
    causal = ((lax.broadcasted_iota(jnp.int32, (GLA_HEADS * C, C), 0) & (C - 1))
              >= lax.broadcasted_iota(jnp.int32, (GLA_HEADS * C, C), 1))
    same_head = ((lax.broadcasted_iota(jnp.int32, (D_GLA, GLA_QK), 0) >> 7)
                 == (lax.broadcasted_iota(jnp.int32, (D_GLA, GLA_QK), 1) >> 6))
    nt = (((1,), (1,)), ((), ()))
    ng = ng_ref[...]
    st = [st_ref[b] for b in seqs]
    for c in range(lt // C):
        sl = slice(c * C, (c + 1) * C)
        for b in seqs:
            bc = bcum[b][sl]
            bl = bc[C - 1:C, :]
            kd = k[b][sl] * jnp.exp(bl - bc)
            qic = qi[b][sl]
            qs = jnp.concatenate([jnp.where(lane_head == h, qic, 0.0) for h in range(GLA_HEADS)],
                                 axis=0).astype(BF16)
            sc = lax.dot_general(qs, ki[b][sl].astype(BF16), nt, preferred_element_type=F32)
            p = jnp.where(causal, sc, 0.0).astype(BF16)
            vc = v_ref[b, sl, :]
            o_intra = jnp.concatenate(
                [jnp.dot(p[h * C:(h + 1) * C], vc[:, h * GLA_DV:(h + 1) * GLA_DV], preferred_element_type=F32)
                 for h in range(GLA_HEADS)], axis=1)
            o_inter = lax.dot_general(qic.astype(BF16), st[b].astype(BF16), nt, preferred_element_type=F32)
            v_t = vc.astype(F32).T.astype(BF16)
            kv_t = jnp.dot(v_t, kd.astype(BF16), preferred_element_type=F32)
            st[b] = st[b] * jnp.exp(bl) + jnp.where(same_head, kv_t, 0.0)
            o = o_intra + o_inter
            parts = []
            for h in range(GLA_HEADS):
                oh = o[:, h * GLA_DV:(h + 1) * GLA_DV]
                oh = oh * lax.rsqrt(jnp.mean(oh * oh, axis=-1, keepdims=True) + EPS)
                parts.append(oh * ng)
            r = r_ref[b, sl, :].astype(F32)
            o_ref[b, sl, :] = (jnp.concatenate(parts, axis=1) * _silu(r)).astype(o_ref.dtype)
    for b in seqs:
        st_ref[b] = st[b]


def _gla(q, k, v, glr, r, wg, bg, ng):
    bsz, L, _ = q.shape
    lt = 256
    nb = 2
    tok = lambda n: pl.BlockSpec((nb, lt, n), lambda b, i: (b, i, 0))
    full = lambda a: pl.BlockSpec(a.shape, lambda b, i: (0,) * a.ndim)
    return pl.pallas_call(
        _gla_kernel,
        grid=(bsz // nb, L // lt),
        in_specs=[tok(GLA_QK), tok(GLA_QK), tok(D_GLA), tok(LANE), tok(D_GLA), full(wg), full(bg), full(ng)],
        out_specs=tok(D_GLA),
        out_shape=jax.ShapeDtypeStruct((bsz, L, D_GLA), BF16),
        scratch_shapes=[pltpu.VMEM((nb, D_GLA, GLA_QK), F32)],
        compiler_params=_params("arbitrary", "arbitrary"),
        name="gla",
    )(q, k, v, glr, r, wg, bg, ng)


def _pack_bf16_pairs(x):
    w = x.shape[1] // 2
    xr = x.astype(BF16).astype(F32)
    lo = lax.bitcast_convert_type(xr[:, :w], jnp.uint32) >> 16
    hi = lax.bitcast_convert_type(xr[:, w:], jnp.uint32) & jnp.uint32(0xFFFF0000)
    return lo | hi


def _unpack_bf16_pairs(p):
    lo = lax.bitcast_convert_type(p << 16, F32)
    hi = lax.bitcast_convert_type(p & jnp.uint32(0xFFFF0000), F32)
    return lo, hi


def _store_rows(ref, packed):
    n = packed.shape[0]
    for c in range(ROW_CHUNKS):
        ref[pl.ds(c, n, stride=ROW_CHUNKS), :] = packed[:, c * LANE:(c + 1) * LANE]


def _load_rows(ref, n):
    return jnp.concatenate([ref[pl.ds(c, n, stride=ROW_CHUNKS), :] for c in range(ROW_CHUNKS)], axis=1)


def _route_tile(lg_t, bias_col, tri, carry_ref):
    n_e, tt = lg_t.shape
    per_group = n_e // N_GROUPS
    scores = jax.nn.sigmoid(lg_t)
    biased = scores + bias_col
    row = lax.broadcasted_iota(jnp.int32, (n_e, tt), 0)
    neg = -jnp.inf
    group_score = []
    for g in range(N_GROUPS):
        b = biased[g * per_group:(g + 1) * per_group]
        r = lax.broadcasted_iota(jnp.int32, (per_group, tt), 0) + g * per_group
        m1 = jnp.max(b, axis=0, keepdims=True)
        i1 = jnp.min(jnp.where(b == m1, r, n_e), axis=0, keepdims=True)
        m2 = jnp.max(jnp.where(r == i1, neg, b), axis=0, keepdims=True)
        group_score.append(m1 + m2)
    parts = []
    for g in range(N_GROUPS):
        ahead = jnp.zeros((1, tt), jnp.int32)
        for g2 in range(N_GROUPS):
            if g2 != g:
                beats = (group_score[g2] >= group_score[g]) if g2 < g else (group_score[g2] > group_score[g])
                ahead = ahead + beats.astype(jnp.int32)
        parts.append(jnp.where(ahead < TOPK_GROUPS, biased[g * per_group:(g + 1) * per_group], neg))
    masked = jnp.concatenate(parts, axis=0)
    work = masked
    idxs = []
    for _ in range(TOP_K):
        m = jnp.max(work, axis=0, keepdims=True)
        ii = jnp.min(jnp.where(work == m, row, n_e), axis=0, keepdims=True)
        idxs.append(ii)
        work = jnp.where(row == ii, neg, work)
    sel = work != masked
    w = jnp.where(sel, scores, 0.0)
    gate_dense = w / jnp.sum(w, axis=0, keepdims=True) * ROUTED_SCALE
    mt = jnp.where(sel, 1.0, 0.0)
    rank_dense = jnp.dot(mt.astype(BF16), tri, preferred_element_type=F32) + carry_ref[...]
    carry_ref[...] += jnp.sum(mt, axis=1, keepdims=True)
    ranks, gts = [], []
    for ii in idxs:
        oh = row == ii
        ranks.append(jnp.sum(jnp.where(oh, rank_dense, 0.0), axis=0, keepdims=True))
        gts.append(jnp.sum(jnp.where(oh, gate_dense, 0.0), axis=0, keepdims=True))
    idx_t = jnp.concatenate(idxs, axis=0)
    rank_t = jnp.concatenate(ranks, axis=0).astype(jnp.int32)
    gate_t = jnp.concatenate(gts + [jnp.zeros((LANE - TOP_K, tt), F32)], axis=0)
    return idx_t, rank_t, gate_t.T[:, :TOP_K]


def _mix_kernel(ms_ref, yg_ref, x_ref, wo_ref, gt1_ref,
                gn_ref, sh2_ref, sc2_ref, gt2_ref, wrt_ref, rb_ref, tri_ref, wsg_ref, wsu_ref, wsd_ref,
                x1_ref, h2_ref, idx_ref, rank_ref, gate_ref, cnt_ref, carry_ref):
    @pl.when((pl.program_id(0) == 0) & (pl.program_id(1) == 0))
    def _():
        carry_ref[...] = jnp.zeros_like(carry_ref)

    mix_s5 = jnp.concatenate([ms_ref[0, c] for c in range(D_MODEL // LANE)], axis=1)
    mix = mix_s5 + jnp.dot(yg_ref[0], wo_ref[...], preferred_element_type=F32)
    x1 = x_ref[0] + gt1_ref[0] * mix
    h2 = x1 * lax.rsqrt(jnp.mean(x1 * x1, axis=-1, keepdims=True) + EPS) * gn_ref[...]
    h2 = h2 * (1.0 + sc2_ref[0]) + sh2_ref[0]
    hb = h2.astype(BF16)
    _store_rows(h2_ref, _pack_bf16_pairs(h2))
    lg_t = lax.dot_general(wrt_ref[...], hb, (((1,), (1,)), ((), ())), preferred_element_type=F32)
    idx_t, rank_t, gates = _route_tile(lg_t, rb_ref[...], tri_ref[...], carry_ref)
    idx_ref[...] = idx_t
    rank_ref[...] = rank_t
    gate_ref[...] = gates
    cnt_ref[...] = carry_ref[...]
    a = _silu(jnp.dot(hb, wsg_ref[...], preferred_element_type=F32)) * jnp.dot(
        hb, wsu_ref[...], preferred_element_type=F32)
    shared = jnp.dot(a.astype(BF16), wsd_ref[...], preferred_element_type=F32)
    x1_ref[0] = x1 + gt2_ref[0] * shared


def _mix(ms5, yg, x, wo, gt1, gn, sh2, sc2, gt2, wrt, rb, wsg, wsu, wsd):
    bsz, L, d = x.shape
    tt = 512
    nt = L // tt
    T = bsz * L
    tri = (jnp.arange(tt)[:, None] < jnp.arange(tt)[None, :]).astype(BF16)
    tok = lambda n: pl.BlockSpec((1, tt, n), lambda b, i: (b, i, 0))
    vec = pl.BlockSpec((1, 1, d), lambda b, i: (b, 0, 0))
    full = lambda a: pl.BlockSpec(a.shape, lambda b, i: (0,) * a.ndim)
    lanes = pl.BlockSpec((TOP_K, tt), lambda b, i: (0, b * nt + i))
    return pl.pallas_call(
        _mix_kernel,
        grid=(bsz, nt),
        in_specs=[pl.BlockSpec((1, d // LANE, tt, LANE), lambda b, i: (b, 0, i, 0)), tok(D_GLA), tok(d), full(wo), vec,
                  full(gn), vec, vec, vec, full(wrt), full(rb), full(tri), full(wsg), full(wsu), full(wsd)],
        out_specs=[tok(d),
                   pl.BlockSpec((tt * ROW_CHUNKS, LANE), lambda b, i: (b * nt + i, 0)),
                   lanes, lanes,
                   pl.BlockSpec((tt, TOP_K), lambda b, i: (b * nt + i, 0)),
                   pl.BlockSpec((N_EXPERTS, 1), lambda b, i: (0, 0))],
        out_shape=[jax.ShapeDtypeStruct((bsz, L, d), F32),
                   jax.ShapeDtypeStruct((T * ROW_CHUNKS, LANE), jnp.uint32),
                   jax.ShapeDtypeStruct((TOP_K, T), jnp.int32),
                   jax.ShapeDtypeStruct((TOP_K, T), jnp.int32),
                   jax.ShapeDtypeStruct((T, TOP_K), F32),
                   jax.ShapeDtypeStruct((N_EXPERTS, 1), F32)],
        scratch_shapes=[pltpu.VMEM((N_EXPERTS, 1), F32)],
        compiler_params=_params("arbitrary", "arbitrary"),
        name="mix",
    )(ms5, yg, x, wo, gt1, gn, sh2, sc2, gt2, wrt, rb, tri, wsg, wsu, wsd)


def _pos_kernel(idx_ref, rank_ref, ps_ref, pos_ref):
    n_e = ps_ref.shape[0]
    tt = idx_ref.shape[1]
    row = lax.broadcasted_iota(jnp.int32, (n_e, tt), 0)
    ps = ps_ref[...]
    starts = [jnp.sum(jnp.where(row == idx_ref[k:k + 1, :], ps, 0.0), axis=0, keepdims=True)
              for k in range(TOP_K)]
    pos = jnp.concatenate(starts, axis=0).astype(jnp.int32) + rank_ref[...]
    for jh in range(tt // LANE):
        pos_ref[jh * TOP_K:(jh + 1) * TOP_K, :] = pos[:, jh * LANE:(jh + 1) * LANE]


def _pos(idx_t, rank_t, pstart):
    T = idx_t.shape[1]
    tt = 2048
    blk = pl.BlockSpec((TOP_K, tt), lambda i: (0, i))
    return pl.pallas_call(
        _pos_kernel,
        grid=(T // tt,),
        in_specs=[blk, blk, pl.BlockSpec((N_EXPERTS, 1), lambda i: (0, 0))],
        out_specs=pl.BlockSpec((tt // LANE * TOP_K, LANE), lambda i: (i, 0)),
        out_shape=jax.ShapeDtypeStruct((T // LANE * TOP_K, LANE), jnp.int32),
        compiler_params=_params("arbitrary"),
        name="pos",
    )(idx_t, rank_t, pstart.astype(F32).reshape(N_EXPERTS, 1))


def _dispatch_kernel(pend_ref, padded_ref, pos_ref, h_ref, xs_hbm, zero_ref, pos_smem, sem_pos, sem_zero, sem_row):
    tt = h_ref.shape[0]
    step = pl.program_id(0)
    load_pos = pltpu.make_async_copy(pos_ref, pos_smem, sem_pos)
    load_pos.start()

    def zero_block(start):
        return pltpu.make_async_copy(zero_ref, xs_hbm.at[pl.ds(pl.multiple_of(start, MOE_BLOCK), MOE_BLOCK)],
                                     sem_zero)

    def for_each_zero_block(fn):
        def per_expert(e, c):
            @pl.when(padded_ref[e] > 0)
            def _():
                fn(zero_block(pend_ref[e] - MOE_BLOCK))
            return c

        def per_tail_block(g, c):
            fn(zero_block(g * MOE_BLOCK))
            return c

        lax.fori_loop(0, N_EXPERTS, per_expert, 0)
        lax.fori_loop(pend_ref[N_EXPERTS - 1] // MOE_BLOCK, xs_hbm.shape[0] // MOE_BLOCK, per_tail_block, 0)

    @pl.when(step == 0)
    def _():
        zero_ref[...] = jnp.zeros_like(zero_ref)
        for_each_zero_block(lambda cp: cp.start())
        for_each_zero_block(lambda cp: cp.wait())

    load_pos.wait()
    for jh in range(tt // LANE):
        def issue(jl, c, jh=jh):
            for k in range(TOP_K):
                slot = pos_smem[(jh * TOP_K + k) * LANE + jl]
                pltpu.make_async_copy(h_ref.at[jh * LANE + jl], xs_hbm.at[slot], sem_row).start(priority=k % 2)
            return c

        lax.fori_loop(0, LANE, issue, 0, unroll=4)
    for k in range(TOP_K):
        pltpu.make_async_copy(h_ref, xs_hbm.at[pl.ds(0, tt)], sem_row).wait()


def _dispatch(pends, padded, pos_t, h_rows, n_slots):
    T = h_rows.shape[0]
    tt = 512
    return pl.pallas_call(
        _dispatch_kernel,
        grid_spec=pltpu.PrefetchScalarGridSpec(
            num_scalar_prefetch=2,
            grid=(T // tt,),
            in_specs=[pl.BlockSpec((tt * TOP_K,), lambda i, pe, pa: (i,)),
                      pl.BlockSpec((tt, ROW_CHUNKS, LANE), lambda i, pe, pa: (i, 0, 0))],
            out_specs=pl.BlockSpec(memory_space=pl.ANY),
            scratch_shapes=[pltpu.VMEM((MOE_BLOCK, ROW_CHUNKS, LANE), jnp.uint32),
                            pltpu.SMEM((tt * TOP_K,), jnp.int32),
                            pltpu.SemaphoreType.DMA, pltpu.SemaphoreType.DMA, pltpu.SemaphoreType.DMA]),
        out_shape=jax.ShapeDtypeStruct((n_slots, ROW_CHUNKS, LANE), jnp.uint32),
        compiler_params=_params("arbitrary"),
        name="dispatch",
    )(pends, padded, pos_t, h_rows)


X_GROUP = 8
X_AHEAD = 8
X_BUFS = X_AHEAD + X_GROUP
Y_BUFS = X_GROUP


def _expert_kernel(b0_ref, nb_ref, nv_ref, wg_ref, wu_ref, wd_ref, xs_hbm, ye_hbm,
                   wg_s, wu_s, wd_s, xbuf, ybuf, semx, semy, *, n_blocks):
    e = pl.program_id(0)
    nb = nb_ref[e]
    b0 = b0_ref[e]
    nv = nv_ref[0]
    rows = MOE_BLOCK * ROW_CHUNKS

    def block(ref, g):
        return ref.at[pl.ds(pl.multiple_of(g * rows, rows), rows)]

    def x_copy(g):
        slot = g % X_BUFS
        return pltpu.make_async_copy(block(xs_hbm, g), xbuf.at[slot], semx.at[slot])

    def y_copy(g):
        slot = g % Y_BUFS
        return pltpu.make_async_copy(ybuf.at[slot], block(ye_hbm, g), semy.at[slot])

    @pl.when(e == 0)
    def _():
        for g in range(X_AHEAD):
            x_copy(g).start()

    @pl.when(nb > 0)
    def _():
        wg_s[...] = wg_ref[0].astype(BF16)
        wu_s[...] = wu_ref[0].astype(BF16)
        wd_s[...] = wd_ref[0].astype(BF16)

    def fetch(g):
        x_copy(g).wait()

        @pl.when(g + X_AHEAD < nv)
        def _():
            x_copy(g + X_AHEAD).start()

    def load_x(g):
        lo, hi = _unpack_bf16_pairs(_load_rows(xbuf.at[g % X_BUFS], MOE_BLOCK))
        return jnp.concatenate([lo, hi], axis=1).astype(BF16)

    def emit(g, y):
        @pl.when(g >= Y_BUFS)
        def _():
            y_copy(g - Y_BUFS).wait()

        _store_rows(ybuf.at[g % Y_BUFS], y)
        y_copy(g).start()

    def run(g, n):
        for b in range(n):
            fetch(g + b)
        x = jnp.concatenate([load_x(g + b) for b in range(n)], axis=0) if n > 1 else load_x(g)
        a = jnp.dot(x, wg_s[...], preferred_element_type=F32)
        u = jnp.dot(x, wu_s[...], preferred_element_type=F32)
        h = (_silu(a) * u).astype(BF16)
        y = _pack_bf16_pairs(jnp.dot(h, wd_s[...], preferred_element_type=F32))
        for b in range(n):
            emit(g + b, y[b * MOE_BLOCK:(b + 1) * MOE_BLOCK])

    def group(i, c):
        run(b0 + X_GROUP * i, X_GROUP)
        return c

    lax.fori_loop(0, nb // X_GROUP, group, 0)
    done = nb // X_GROUP * X_GROUP
    n = X_GROUP // 2
    while n >= 1:
        @pl.when((nb & n) != 0)
        def _(n=n, done=done):
            run(b0 + done, n)

        done = done + (nb & n)
        n //= 2

    @pl.when(e == pl.num_programs(0) - 1)
    def _():
        for back in range(Y_BUFS, 0, -1):
            y_copy(nv - back).wait()
        ybuf[0] = jnp.zeros(ybuf.shape[1:], ybuf.dtype)

        def fill(g):
            return pltpu.make_async_copy(ybuf.at[0], block(ye_hbm, g), semy.at[0])

        lax.fori_loop(nv, n_blocks, lambda g, c: (fill(g).start(), c)[1], 0)
        lax.fori_loop(nv, n_blocks, lambda g, c: (fill(g).wait(), c)[1], 0)


def _experts(first_block, num_blocks, n_valid, xs, wg, wu, wd):
    rows = MOE_BLOCK * ROW_CHUNKS
    n_blocks = xs.shape[0] // rows
    n_e, d, _ = wg.shape
    wsel = lambda e, b0, nb, nv: (e, 0, 0)
    return pl.pallas_call(
        functools.partial(_expert_kernel, n_blocks=n_blocks),
        grid_spec=pltpu.PrefetchScalarGridSpec(
            num_scalar_prefetch=3,
            grid=(n_e,),
            in_specs=[pl.BlockSpec((1, d, D_EXPERT), wsel),
                      pl.BlockSpec((1, d, D_EXPERT), wsel),
                      pl.BlockSpec((1, D_EXPERT, d), wsel),
                      pl.BlockSpec(memory_space=pl.ANY)],
            out_specs=pl.BlockSpec(memory_space=pl.ANY),
            scratch_shapes=[pltpu.VMEM((d, D_EXPERT), BF16), pltpu.VMEM((d, D_EXPERT), BF16),
                            pltpu.VMEM((D_EXPERT, d), BF16),
                            pltpu.VMEM((X_BUFS, rows, LANE), jnp.uint32),
                            pltpu.VMEM((Y_BUFS, rows, LANE), jnp.uint32),
                            pltpu.SemaphoreType.DMA((X_BUFS,)), pltpu.SemaphoreType.DMA((Y_BUFS,))]),
        out_shape=jax.ShapeDtypeStruct(xs.shape, jnp.uint32),
        compiler_params=_params("arbitrary"),
        name="experts",
    )(first_block, num_blocks, n_valid, wg, wu, wd, xs)


def _final_kernel(pos0_ref, posn_ref, gate_ref, x_ref, gt_ref, g_ref, ye_hbm, o_ref,
                  buf_ref, pos_smem, sem_pos, sem_row):
    tt = x_ref.shape[1]
    i = pl.program_id(0)

    def gather_tile(pos_ref, tile):
        base = (tile % 2) * (TOP_K * tt)
        sem = sem_row.at[tile % 2]
        load_pos = pltpu.make_async_copy(pos_ref, pos_smem, sem_pos)
        load_pos.start()
        load_pos.wait()
        for jh in range(tt // LANE):
            def issue(jl, c, jh=jh):
                for k in range(TOP_K):
                    slot = pos_smem[(jh * TOP_K + k) * LANE + jl]
                    pltpu.make_async_copy(ye_hbm.at[slot], buf_ref.at[base + k * tt + jh * LANE + jl],
                                          sem).start(priority=k % 2)
                return c

            lax.fori_loop(0, LANE, issue, 0, unroll=8)

    @pl.when(i == 0)
    def _():
        gather_tile(pos0_ref, i)

    @pl.when(i + 1 < pl.num_programs(0))
    def _():
        gather_tile(posn_ref, i + 1)

    base = pl.multiple_of((i % 2) * (TOP_K * tt), TOP_K * tt)
    for k in range(TOP_K):
        pltpu.make_async_copy(ye_hbm.at[pl.ds(0, tt)], buf_ref.at[pl.ds(base + k * tt, tt)],
                              sem_row.at[i % 2]).wait()
    gates = gate_ref[...]
    half = D_MODEL // 2
    acc_lo = jnp.zeros((tt, half), F32)
    acc_hi = jnp.zeros((tt, half), F32)
    rows_2d = buf_ref.reshape(buf_ref.shape[0] * ROW_CHUNKS, LANE)
    for k in range(TOP_K):
        lo, hi = _unpack_bf16_pairs(_load_rows(rows_2d.at[pl.ds((base + k * tt) * ROW_CHUNKS, tt * ROW_CHUNKS)], tt))
        gk = gates[:, k:k + 1]
        acc_lo += gk * lo
        acc_hi += gk * hi
    x = x_ref[0] + gt_ref[0] * jnp.concatenate([acc_lo, acc_hi], axis=1)
    o_ref[0] = x * lax.rsqrt(jnp.mean(x * x, axis=-1, keepdims=True) + EPS) * g_ref[...]


def _final(pos_t, gates, x1, gt2, g, ye_rows):
    bsz, L, d = x1.shape
    tt = 512
    nt = L // tt
    n = bsz * nt
    tok = pl.BlockSpec((1, tt, d), lambda i: (i // nt, i % nt, 0))
    return pl.pallas_call(
        _final_kernel,
        grid=(n,),
        in_specs=[pl.BlockSpec((tt * TOP_K,), lambda i: (0,)),
                  pl.BlockSpec((tt * TOP_K,), lambda i: (jnp.minimum(i + 1, n - 1),)),
                  pl.BlockSpec((tt, TOP_K), lambda i: (i, 0)),
                  tok, pl.BlockSpec((1, 1, d), lambda i: (i // nt, 0, 0)),
                  pl.BlockSpec((1, d), lambda i: (0, 0)),
                  pl.BlockSpec(memory_space=pl.ANY)],
        out_specs=tok,
        out_shape=jax.ShapeDtypeStruct((bsz, L, d), F32),
        scratch_shapes=[pltpu.VMEM((2 * TOP_K * tt, ROW_CHUNKS, LANE), jnp.uint32),
                        pltpu.SMEM((tt * TOP_K,), jnp.int32),
                        pltpu.SemaphoreType.DMA, pltpu.SemaphoreType.DMA((2,))],
        compiler_params=_params("arbitrary"),
        name="final",
    )(pos_t, pos_t, gates, x1, gt2, g.reshape(1, d), ye_rows)


def _block_table(counts):
    counts = counts.astype(jnp.int32)
    padded = (counts + MOE_BLOCK - 1) // MOE_BLOCK * MOE_BLOCK
    pends = jnp.cumsum(padded)
    pstarts = pends - padded
    n_valid = (pends[-1] // MOE_BLOCK).reshape(1)
    return padded, pends, pstarts, n_valid


def kernel(x, c, w_ada, b_ada, g_norm_mix, w_in, s5_lambda_re, s5_lambda_im, s5_log_dt, s5_b_re, s5_b_im, s5_c_re, s5_c_im, s5_d, s5_w_glu, s5_b_glu, s5_norm_g, gla_w_g2, gla_b_g2, gla_norm_g, w_out, g_norm_moe, w_router, router_bias, exp_w_gate, exp_w_up, exp_w_down, sh_w_gate, sh_w_up, sh_w_down, g_final):
    bsz, L, d = x.shape
    T = bsz * L
    assert w_ada.shape[0] == 1, "single-layer block"
    for l in range(1):
        mod = _ada(c, w_ada[l], b_ada[l])
        sh1, sc1, gt1, sh2, sc2, gt2 = [m.reshape(bsz, 1, d) for m in jnp.split(mod, 6, axis=-1)]

        wi = w_in[l]
        o_q, o_k, o_v, o_g, o_r = D_S5, D_S5 + GLA_QK, D_S5 + 2 * GLA_QK, D_S5 + 2 * GLA_QK + D_GLA, \
            D_S5 + 2 * GLA_QK + D_GLA + GLA_LOWRANK
        w_cat = jnp.concatenate([wi[:, :o_g], wi[:, o_r:], wi[:, o_g:o_r],
                                 jnp.zeros((d, LANE - GLA_LOWRANK), wi.dtype)], axis=1).astype(BF16)
        u, q, k, v, r, glr = _inproj(x, g_norm_mix[l], sh1, sc1, w_cat)

        mats = _s5_prep(s5_lambda_re[l], s5_lambda_im[l], s5_log_dt[l], s5_b_re[l], s5_b_im[l],
                        s5_c_re[l], s5_c_im[l], s5_d[l])
        ms5 = _s5(u, mats, s5_w_glu[l].T.astype(BF16), s5_b_glu[l].reshape(D_S5, 1),
                  s5_norm_g[l].reshape(D_S5, 1), w_out[l][:D_S5].astype(BF16))

        wg2 = jnp.concatenate([gla_w_g2[l], jnp.zeros((LANE - GLA_LOWRANK, GLA_QK), F32)], axis=0).astype(BF16)
        yg = _gla(q, k, v, glr, r, wg2, gla_b_g2[l].reshape(1, GLA_QK), gla_norm_g[l].reshape(1, GLA_DV))

        x1, h2p, idx_t, rank_t, gates, counts = _mix(
            ms5, yg, x, w_out[l][D_S5:].astype(BF16), gt1, g_norm_moe[l].reshape(1, d), sh2, sc2, gt2,
            w_router[l].T.astype(BF16), router_bias[l].reshape(N_EXPERTS, 1),
            sh_w_gate[l].astype(BF16), sh_w_up[l].astype(BF16), sh_w_down[l].astype(BF16))

        n_blocks = (T * TOP_K + N_EXPERTS * (MOE_BLOCK - 1) + MOE_BLOCK - 1) // MOE_BLOCK
        n_slots = n_blocks * MOE_BLOCK
        padded, pends, pstarts, n_valid = _block_table(counts[:, 0])
        pos_t = _pos(idx_t, rank_t, pstarts).reshape(-1)
        xs = _dispatch(pends, padded, pos_t, h2p.reshape(T, ROW_CHUNKS, LANE), n_slots)
        ye = _experts(pstarts // MOE_BLOCK, padded // MOE_BLOCK, n_valid,
                      xs.reshape(n_slots * ROW_CHUNKS, LANE), exp_w_gate[l], exp_w_up[l], exp_w_down[l])
    return _final(pos_t, gates, x1, gt2, g_final, ye.reshape(n_slots, ROW_CHUNKS, LANE))
```

```python
import functools

import jax
import jax.numpy as jnp
from jax import lax
from jax.experimental import pallas as pl
from jax.experimental.pallas import tpu as pltpu

F32 = jnp.float32
BF16 = jnp.bfloat16

D_MODEL = 1024
D_S5 = 512
S5_GROUP = 16
S5_GROUPS = 32
S5_STATE = 64
S5_CHUNK = 16
S5_STEP_GROUPS = 8
S5_LEVELS = 7
S5_APOW_COLS = 16
D_GLA = 512
GLA_HEADS = 4
GLA_DK = 64
GLA_DV = 128
GLA_QK = 256
GLA_LOWRANK = 16
GLA_TAU = 16.0
GLA_CHUNK = 64
LANE = 128
N_EXPERTS = 256
TOP_K = 8
N_GROUPS = 8
TOPK_GROUPS = 4
D_EXPERT = 256
ROUTED_SCALE = 2.5
EPS = 1e-6
MOE_BLOCK = 128
ROW_CHUNKS = D_MODEL // 2 // LANE
VMEM_LIMIT = 48 * 1024 * 1024


def _silu(x):
    return x * jax.nn.sigmoid(x)


def _params(*sem):
    return pltpu.CompilerParams(dimension_semantics=sem, vmem_limit_bytes=VMEM_LIMIT)


def _ada_kernel(c_ref, w_ref, b_ref, o_ref):
    s = _silu(c_ref[...]).astype(BF16)
    o_ref[...] = jnp.dot(s, w_ref[...].astype(BF16), preferred_element_type=F32) + b_ref[...]


def _ada(c, w, b):
    bsz, d = c.shape
    n = w.shape[1]
    tn = 1024
    return pl.pallas_call(
        _ada_kernel,
        grid=(n // tn,),
        in_specs=[pl.BlockSpec((bsz, d), lambda j: (0, 0)),
                  pl.BlockSpec((d, tn), lambda j: (0, j)),
                  pl.BlockSpec((1, tn), lambda j: (0, j))],
        out_specs=pl.BlockSpec((bsz, tn), lambda j: (0, j)),
        out_shape=jax.ShapeDtypeStruct((bsz, n), F32),
        compiler_params=_params("arbitrary"),
        name="ada",
    )(c, w, b.reshape(1, n))


def _inproj_kernel(x_ref, g_ref, sh_ref, sc_ref, w_ref,
                   u_ref, q_ref, k_ref, v_ref, r_ref, glr_ref):
    x = x_ref[0]
    ms = jnp.mean(x * x, axis=-1, keepdims=True)
    h = (x * lax.rsqrt(ms + EPS)) * g_ref[...]
    h = h * (1.0 + sc_ref[0]) + sh_ref[0]
    hb = h.astype(BF16)
    u = jnp.dot(hb, w_ref[:, 0:D_S5], preferred_element_type=F32)
    for c in range(D_S5 // LANE):
        u_ref[0, c] = u[:, c * LANE:(c + 1) * LANE]
    col = D_S5
    for ref in (q_ref, k_ref, v_ref, r_ref, glr_ref):
        n = ref.shape[-1]
        ref[0] = jnp.dot(hb, w_ref[:, col:col + n], preferred_element_type=F32).astype(ref.dtype)
        col += n


def _inproj(x, g, sh, sc, w):
    bsz, L, d = x.shape
    tt = 1024
    widths = (GLA_QK, GLA_QK, D_GLA, D_GLA, LANE)
    tok = lambda n: pl.BlockSpec((1, tt, n), lambda b, i: (b, i, 0))
    vec = pl.BlockSpec((1, 1, d), lambda b, i: (b, 0, 0))
    return pl.pallas_call(
        _inproj_kernel,
        grid=(bsz, L // tt),
        in_specs=[tok(d), pl.BlockSpec((1, d), lambda b, i: (0, 0)), vec, vec,
                  pl.BlockSpec(w.shape, lambda b, i: (0, 0))],
        out_specs=[pl.BlockSpec((1, D_S5 // LANE, tt, LANE), lambda b, i: (b, 0, i, 0))] + [tok(n) for n in widths],
        out_shape=[jax.ShapeDtypeStruct((bsz, D_S5 // LANE, L, LANE), F32)]
        + [jax.ShapeDtypeStruct((bsz, L, n), BF16) for n in widths],
        compiler_params=_params("arbitrary", "arbitrary"),
        name="inproj",
    )(x, g.reshape(1, d), sh, sc, w)


def _s5_prep(lam_re, lam_im, log_dt, b_re, b_im, c_re, c_im, d_skip):
    G, N, C, TC = S5_GROUPS, S5_STATE, S5_GROUP, S5_CHUNK
    hp = lax.Precision.HIGHEST
    dt = jnp.exp(log_dt)[:, None]
    lr, li = lam_re, lam_im
    mag = jnp.exp(lr * dt)
    ab_re, ab_im = mag * jnp.cos(li * dt), mag * jnp.sin(li * dt)
    den = lr * lr + li * li
    nr = ab_re - 1.0
    coef_re = ((nr * lr + ab_im * li) / den)[..., None]
    coef_im = ((ab_im * lr - nr * li) / den)[..., None]
    bb_re = coef_re * b_re - coef_im * b_im
    bb_im = coef_re * b_im + coef_im * b_re
    p = jnp.arange(TC + 1, dtype=F32)[:, None, None]
    pm = jnp.exp(lr * dt * p)
    pr, pi = pm * jnp.cos(li * dt * p), pm * jnp.sin(li * dt * p)
    ca_re = c_re[None] * pr[:, :, None, :] - c_im[None] * pi[:, :, None, :]
    ca_im = c_re[None] * pi[:, :, None, :] + c_im[None] * pr[:, :, None, :]
    ca = jnp.concatenate([ca_re[:TC], -ca_im[:TC]], axis=-1).transpose(1, 0, 2, 3).reshape(G, TC * C, 2 * N)
    kern = jnp.einsum('gxk,gki->gxi', ca, jnp.concatenate([bb_re, bb_im], axis=1),
                      precision=hp).reshape(G, TC, C, C)
    skip = jnp.eye(C, dtype=F32)[None] * d_skip[:, :, None]
    kern = kern + jnp.concatenate([skip[:, None], jnp.zeros((G, TC - 1, C, C), F32)], axis=1)
    rev = kern[:, ::-1].transpose(0, 2, 1, 3).reshape(G, C, TC * C)
    rev = jnp.concatenate([rev, jnp.zeros((G, C, (TC - 1) * C), F32)], axis=-1)
    toep_t = jnp.stack([rev[:, :, (TC - 1 - t) * C:(TC - 1 - t) * C + TC * C] for t in range(TC)],
                       axis=1).reshape(G, TC * C, TC * C)
    rr, ri = pr[TC - 1 - jnp.arange(TC)], pi[TC - 1 - jnp.arange(TC)]
    binc_re = rr[..., None] * bb_re[None] - ri[..., None] * bb_im[None]
    binc_im = rr[..., None] * bb_im[None] + ri[..., None] * bb_re[None]
    binc_re_t = binc_re.transpose(1, 2, 0, 3).reshape(G, N, TC * C)
    binc_im_t = binc_im.transpose(1, 2, 0, 3).reshape(G, N, TC * C)
    cm_re_t = ca_re[1:].transpose(1, 0, 2, 3).reshape(G, TC * C, N)
    cm_im_t = (-ca_im[1:]).transpose(1, 0, 2, 3).reshape(G, TC * C, N)
    q = (TC * 2.0 ** jnp.arange(S5_LEVELS, dtype=F32))[:, None, None]
    qm = jnp.exp(lr * dt * q)
    qr, qi = qm * jnp.cos(li * dt * q), qm * jnp.sin(li * dt * q)
    apow = jnp.stack([qr, qi], axis=1).reshape(2 * S5_LEVELS, G, N).transpose(1, 2, 0)
    apow = jnp.concatenate([apow, jnp.zeros((G, N, S5_APOW_COLS - 2 * S5_LEVELS), F32)], axis=-1)
    from_z = jnp.concatenate([toep_t, binc_re_t, binc_im_t], axis=1).astype(BF16)
    from_h = jnp.concatenate([cm_re_t, cm_im_t], axis=2).astype(BF16)
    return from_z, from_h, apow


def _s5_kernel(u_ref, eye_ref, fz_ref, fh_ref, apow_ref,
               wglu_ref, bglu_ref, sng_ref, wo_ref, o_ref, ut_ref, yt_ref, ot_ref):
    j = pl.program_id(1)
    n_chunks = u_ref.shape[2] // S5_CHUNK
    nt = (((1,), (1,)), ((), ()))

    @pl.when(j == 0)
    def _():
        for s in range(S5_CHUNK):
            us = jnp.concatenate([u_ref[0, c, pl.ds(s, n_chunks, stride=S5_CHUNK), :]
                                  for c in range(D_S5 // LANE)], axis=1).astype(BF16)
            ut_ref[:, s * n_chunks:(s + 1) * n_chunks] = lax.dot_general(
                eye_ref[...], us, nt, preferred_element_type=F32).astype(BF16)

    lane = lax.broadcasted_iota(jnp.int32, (S5_STATE, n_chunks), 1)
    groups = range(S5_STEP_GROUPS)
    rows = [pl.ds(pl.multiple_of(j * (S5_STEP_GROUPS * S5_GROUP) + gl * S5_GROUP, S5_GROUP), S5_GROUP)
            for gl in groups]
    zt = [jnp.concatenate([ut_ref[rows[gl], s * n_chunks:(s + 1) * n_chunks] for s in range(S5_CHUNK)], axis=0)
          for gl in groups]
    w = S5_CHUNK * S5_GROUP
    zprod = [jnp.dot(fz_ref[gl], zt[gl], preferred_element_type=F32) for gl in groups]
    xr = [jnp.where(lane >= 1, pltpu.roll(zprod[gl][w:w + S5_STATE], 1, axis=1), 0.0) for gl in groups]
    xi = [jnp.where(lane >= 1, pltpu.roll(zprod[gl][w + S5_STATE:], 1, axis=1), 0.0) for gl in groups]
    for lv in range(S5_LEVELS):
        d = 1 << lv
        for gl in groups:
            ar = apow_ref[gl, :, 2 * lv:2 * lv + 1]
            ai = apow_ref[gl, :, 2 * lv + 1:2 * lv + 2]
            sr = jnp.where(lane >= d, pltpu.roll(xr[gl], d, axis=1), 0.0)
            si = jnp.where(lane >= d, pltpu.roll(xi[gl], d, axis=1), 0.0)
            xr[gl], xi[gl] = xr[gl] + ar * sr - ai * si, xi[gl] + ar * si + ai * sr
    for gl in groups:
        state = jnp.concatenate([xr[gl], xi[gl]], axis=0).astype(BF16)
        yt = zprod[gl][:w] + jnp.dot(fh_ref[gl], state, preferred_element_type=F32)
        for t in range(S5_CHUNK):
            yt_ref[rows[gl], t * n_chunks:(t + 1) * n_chunks] = yt[t * S5_GROUP:(t + 1) * S5_GROUP, :]

    @pl.when(j == pl.num_programs(1) - 1)
    def _():
        cw = 2 * n_chunks
        for cc in range(yt_ref.shape[1] // cw):
            y = yt_ref[:, cc * cw:(cc + 1) * cw]
            g = y * (0.5 * (1.0 + jnp.tanh(0.7978845608028654 * (y + 0.044715 * (y * y * y)))))
            z = jnp.dot(wglu_ref[...], g.astype(BF16), preferred_element_type=F32) + bglu_ref[...]
            out = g * jax.nn.sigmoid(z)
            out = out * lax.rsqrt(jnp.mean(out * out, axis=0, keepdims=True) + EPS) * sng_ref[...]
            ot_ref[cc * cw:(cc + 1) * cw, :] = out.T.astype(BF16)
        rows = ot_ref.shape[0] // 2
        for half in range(2):
            mix = jnp.dot(ot_ref[half * rows:(half + 1) * rows, :], wo_ref[...], preferred_element_type=F32)
            for sl in range(rows // n_chunks):
                s = half * (rows // n_chunks) + sl
                for c in range(D_MODEL // LANE):
                    o_ref[0, c, pl.ds(s, n_chunks, stride=S5_CHUNK), :] = mix[sl * n_chunks:(sl + 1) * n_chunks,
                                                                              c * LANE:(c + 1) * LANE]


def _s5(u, mats, wglu_t, bglu, sng, wo_s5):
    bsz, _, L, _ = u.shape
    assert L // S5_CHUNK == LANE, "one lane tile of chunks per sequence"
    from_z, from_h, apow = mats
    sg = S5_STEP_GROUPS
    eye = jnp.eye(D_S5, dtype=BF16)
    grp = lambda a: pl.BlockSpec((sg,) + a.shape[1:], lambda bi, j: (j, 0, 0))
    full = lambda a: pl.BlockSpec(a.shape, lambda bi, j: (0,) * a.ndim)
    return pl.pallas_call(
        _s5_kernel,
        grid=(bsz, S5_GROUPS // sg),
        in_specs=[pl.BlockSpec((1, D_S5 // LANE, L, LANE), lambda bi, j: (bi, 0, 0, 0)), full(eye),
                  grp(from_z), grp(from_h), grp(apow), full(wglu_t), full(bglu), full(sng), full(wo_s5)],
        out_specs=pl.BlockSpec((1, D_MODEL // LANE, L, LANE), lambda bi, j: (bi, 0, 0, 0)),
        out_shape=jax.ShapeDtypeStruct((bsz, D_MODEL // LANE, L, LANE), F32),
        scratch_shapes=[pltpu.VMEM((D_S5, L), BF16), pltpu.VMEM((D_S5, L), F32), pltpu.VMEM((L, D_S5), BF16)],
        compiler_params=_params("arbitrary", "arbitrary"),
        name="s5",
    )(u, eye, from_z, from_h, apow, wglu_t, bglu, sng, wo_s5)


def _gla_kernel(q_ref, k_ref, v_ref, glr_ref, r_ref, wg_ref, bg_ref, ng_ref, o_ref, st_ref):
    lt = q_ref.shape[1]
    C = GLA_CHUNK

    @pl.when(pl.program_id(1) == 0)
    def _():
        st_ref[...] = jnp.zeros_like(st_ref)

    ri = lax.broadcasted_iota(jnp.int32, (lt, lt), 0)
    ci = lax.broadcasted_iota(jnp.int32, (lt, lt), 1)
    tril = jnp.where(((ri >> 6) == (ci >> 6)) & (ci <= ri), 1.0, 0.0).astype(BF16)
    seqs = range(q_ref.shape[0])
    bcum, k, qi, ki = [], [], [], []
    for b in seqs:
        z = jnp.dot(glr_ref[b], wg_ref[...], preferred_element_type=F32) + bg_ref[...]
        log_a = (jnp.minimum(z, 0.0) - jnp.log(1.0 + jnp.exp(-jnp.abs(z)))) * (1.0 / GLA_TAU)
        la_hi = log_a.astype(BF16)
        la_lo = (log_a - la_hi.astype(F32)).astype(BF16)
        bcum.append(jnp.dot(tril, la_hi, preferred_element_type=F32)
                    + jnp.dot(tril, la_lo, preferred_element_type=F32))
        k.append(k_ref[b].astype(F32))
        qi.append(q_ref[b].astype(F32) * (GLA_DK ** -0.5) * jnp.exp(bcum[b]))
        ki.append(k[b] * jnp.exp(-bcum[b]))
    lane_head = lax.broadcasted_iota(jnp.int32, (1, GLA_QK), 1) >> 6
    causal = ((lax.broadcasted_iota(jnp.int32, (GLA_HEADS * C, C), 0) & (C - 1))
              >= lax.broadcasted_iota(jnp.int32, (GLA_HEADS * C, C), 1))
    same_head = ((lax.broadcasted_iota(jnp.int32, (D_GLA, GLA_QK), 0) >> 7)
                 == (lax.broadcasted_iota(jnp.int32, (D_GLA, GLA_QK), 1) >> 6))
    nt = (((1,), (1,)), ((), ()))
    ng = ng_ref[...]
    st = [st_ref[b] for b in seqs]
    for c in range(lt // C):
        sl = slice(c * C, (c + 1) * C)
        for b in seqs:
            bc = bcum[b][sl]
            bl = bc[C - 1:C, :]
            kd = k[b][sl] * jnp.exp(bl - bc)
            qic = qi[b][sl]
            qs = jnp.concatenate([jnp.where(lane_head == h, qic, 0.0) for h in range(GLA_HEADS)],
                                 axis=0).astype(BF16)
            sc = lax.dot_general(qs, ki[b][sl].astype(BF16), nt, preferred_element_type=F32)
            p = jnp.where(causal, sc, 0.0).astype(BF16)
            vc = v_ref[b, sl, :]
            o_intra = jnp.concatenate(
                [jnp.dot(p[h * C:(h + 1) * C], vc[:, h * GLA_DV:(h + 1) * GLA_DV], preferred_element_type=F32)
                 for h in range(GLA_HEADS)], axis=1)
            o_inter = lax.dot_general(qic.astype(BF16), st[b].astype(BF16), nt, preferred_element_type=F32)
            v_t = vc.astype(F32).T.astype(BF16)
            kv_t = jnp.dot(v_t, kd.astype(BF16), preferred_element_type=F32)
            st[b] = st[b] * jnp.exp(bl) + jnp.where(same_head, kv_t, 0.0)
            o = o_intra + o_inter
            parts = []
            for h in range(GLA_HEADS):
                oh = o[:, h * GLA_DV:(h + 1) * GLA_DV]
                oh = oh * lax.rsqrt(jnp.mean(oh * oh, axis=-1, keepdims=True) + EPS)
                parts.append(oh * ng)
            r = r_ref[b, sl, :].astype(F32)
            o_ref[b, sl, :] = (jnp.concatenate(parts, axis=1) * _silu(r)).astype(o_ref.dtype)
    for b in seqs:
        st_ref[b] = st[b]


def _gla(q, k, v, glr, r, wg, bg, ng):
    bsz, L, _ = q.shape
    lt = 256
    nb = 2
    tok = lambda n: pl.BlockSpec((nb, lt, n), lambda b, i: (b, i, 0))
    full = lambda a: pl.BlockSpec(a.shape, lambda b, i: (0,) * a.ndim)
    return pl.pallas_call(
        _gla_kernel,
        grid=(bsz // nb, L // lt),
        in_specs=[tok(GLA_QK), tok(GLA_QK), tok(D_GLA), tok(LANE), tok(D_GLA), full(wg), full(bg), full(ng)],
        out_specs=tok(D_GLA),
        out_shape=jax.ShapeDtypeStruct((bsz, L, D_GLA), BF16),
        scratch_shapes=[pltpu.VMEM((nb, D_GLA, GLA_QK), F32)],
        compiler_params=_params("arbitrary", "arbitrary"),
        name="gla",
    )(q, k, v, glr, r, wg, bg, ng)


def _pack_bf16_pairs(x):
    w = x.shape[1] // 2
    xr = x.astype(BF16).astype(F32)
    lo = lax.bitcast_convert_type(xr[:, :w], jnp.uint32) >> 16
    hi = lax.bitcast_convert_type(xr[:, w:], jnp.uint32) & jnp.uint32(0xFFFF0000)
    return lo | hi


def _unpack_bf16_pairs(p):
    lo = lax.bitcast_convert_type(p << 16, F32)
    hi = lax.bitcast_convert_type(p & jnp.uint32(0xFFFF0000), F32)
    return lo, hi


def _store_rows(ref, packed):
    n = packed.shape[0]
    for c in range(ROW_CHUNKS):
        ref[pl.ds(c, n, stride=ROW_CHUNKS), :] = packed[:, c * LANE:(c + 1) * LANE]


def _load_rows(ref, n):
    return jnp.concatenate([ref[pl.ds(c, n, stride=ROW_CHUNKS), :] for c in range(ROW_CHUNKS)], axis=1)


def _route_tile(lg_t, bias_col, tri, carry_ref):
    n_e, tt = lg_t.shape
    per_group = n_e // N_GROUPS
    scores = jax.nn.sigmoid(lg_t)
    biased = scores + bias_col
    row = lax.broadcasted_iota(jnp.int32, (n_e, tt), 0)
    neg = -jnp.inf
    group_score = []
    for g in range(N_GROUPS):
        b = biased[g * per_group:(g + 1) * per_group]
        r = lax.broadcasted_iota(jnp.int32, (per_group, tt), 0) + g * per_group
        m1 = jnp.max(b, axis=0, keepdims=True)
        i1 = jnp.min(jnp.where(b == m1, r, n_e), axis=0, keepdims=True)
        m2 = jnp.max(jnp.where(r == i1, neg, b), axis=0, keepdims=True)
        group_score.append(m1 + m2)
    parts = []
    for g in range(N_GROUPS):
        ahead = jnp.zeros((1, tt), jnp.int32)
        for g2 in range(N_GROUPS):
            if g2 != g:
                beats = (group_score[g2] >= group_score[g]) if g2 < g else (group_score[g2] > group_score[g])
                ahead = ahead + beats.astype(jnp.int32)
        parts.append(jnp.where(ahead < TOPK_GROUPS, biased[g * per_group:(g + 1) * per_group], neg))
    masked = jnp.concatenate(parts, axis=0)
    work = masked
    idxs = []
    for _ in range(TOP_K):
        m = jnp.max(work, axis=0, keepdims=True)
        ii = jnp.min(jnp.where(work == m, row, n_e), axis=0, keepdims=True)
        idxs.append(ii)
        work = jnp.where(row == ii, neg, work)
    sel = work != masked
    w = jnp.where(sel, scores, 0.0)
    gate_dense = w / jnp.sum(w, axis=0, keepdims=True) * ROUTED_SCALE
    mt = jnp.where(sel, 1.0, 0.0)
    rank_dense = jnp.dot(mt.astype(BF16), tri, preferred_element_type=F32) + carry_ref[...]
    carry_ref[...] += jnp.sum(mt, axis=1, keepdims=True)
    ranks, gts = [], []
    for ii in idxs:
        oh = row == ii
        ranks.append(jnp.sum(jnp.where(oh, rank_dense, 0.0), axis=0, keepdims=True))
        gts.append(jnp.sum(jnp.where(oh, gate_dense, 0.0), axis=0, keepdims=True))
    idx_t = jnp.concatenate(idxs, axis=0)
    rank_t = jnp.concatenate(ranks, axis=0).astype(jnp.int32)
    gate_t = jnp.concatenate(gts + [jnp.zeros((LANE - TOP_K, tt), F32)], axis=0)
    return idx_t, rank_t, gate_t.T[:, :TOP_K]


def _mix_kernel(ms_ref, yg_ref, x_ref, wo_ref, gt1_ref,
                gn_ref, sh2_ref, sc2_ref, gt2_ref, wrt_ref, rb_ref, tri_ref, wsg_ref, wsu_ref, wsd_ref,
                x1_ref, h2_ref, idx_ref, rank_ref, gate_ref, cnt_ref, carry_ref):
    @pl.when((pl.program_id(0) == 0) & (pl.program_id(1) == 0))
    def _():
        carry_ref[...] = jnp.zeros_like(carry_ref)

    mix_s5 = jnp.concatenate([ms_ref[0, c] for c in range(D_MODEL // LANE)], axis=1)
    mix = mix_s5 + jnp.dot(yg_ref[0], wo_ref[...], preferred_element_type=F32)
    x1 = x_ref[0] + gt1_ref[0] * mix
    h2 = x1 * lax.rsqrt(jnp.mean(x1 * x1, axis=-1, keepdims=True) + EPS) * gn_ref[...]
    h2 = h2 * (1.0 + sc2_ref[0]) + sh2_ref[0]
    hb = h2.astype(BF16)
    _store_rows(h2_ref, _pack_bf16_pairs(h2))
    lg_t = lax.dot_general(wrt_ref[...], hb, (((1,), (1,)), ((), ())), preferred_element_type=F32)
    idx_t, rank_t, gates = _route_tile(lg_t, rb_ref[...], tri_ref[...], carry_ref)
    idx_ref[...] = idx_t
    rank_ref[...] = rank_t
    gate_ref[...] = gates
    cnt_ref[...] = carry_ref[...]
    a = _silu(jnp.dot(hb, wsg_ref[...], preferred_element_type=F32)) * jnp.dot(
        hb, wsu_ref[...], preferred_element_type=F32)
    shared = jnp.dot(a.astype(BF16), wsd_ref[...], preferred_element_type=F32)
    x1_ref[0] = x1 + gt2_ref[0] * shared


def _mix(ms5, yg, x, wo, gt1, gn, sh2, sc2, gt2, wrt, rb, wsg, wsu, wsd):
    bsz, L, d = x.shape
    tt = 512
    nt = L // tt
    T = bsz * L
    tri = (jnp.arange(tt)[:, None] < jnp.arange(tt)[None, :]).astype(BF16)
    tok = lambda n: pl.BlockSpec((1, tt, n), lambda b, i: (b, i, 0))
    vec = pl.BlockSpec((1, 1, d), lambda b, i: (b, 0, 0))
    full = lambda a: pl.BlockSpec(a.shape, lambda b, i: (0,) * a.ndim)
    lanes = pl.BlockSpec((TOP_K, tt), lambda b, i: (0, b * nt + i))
    return pl.pallas_call(
        _mix_kernel,
        grid=(bsz, nt),
        in_specs=[pl.BlockSpec((1, d // LANE, tt, LANE), lambda b, i: (b, 0, i, 0)), tok(D_GLA), tok(d), full(wo), vec,
                  full(gn), vec, vec, vec, full(wrt), full(rb), full(tri), full(wsg), full(wsu), full(wsd)],
        out_specs=[tok(d),
                   pl.BlockSpec((tt * ROW_CHUNKS, LANE), lambda b, i: (b * nt + i, 0)),
                   lanes, lanes,
                   pl.BlockSpec((tt, TOP_K), lambda b, i: (b * nt + i, 0)),
                   pl.BlockSpec((N_EXPERTS, 1), lambda b, i: (0, 0))],
        out_shape=[jax.ShapeDtypeStruct((bsz, L, d), F32),
                   jax.ShapeDtypeStruct((T * ROW_CHUNKS, LANE), jnp.uint32),
                   jax.ShapeDtypeStruct((TOP_K, T), jnp.int32),
                   jax.ShapeDtypeStruct((TOP_K, T), jnp.int32),
                   jax.ShapeDtypeStruct((T, TOP_K), F32),
                   jax.ShapeDtypeStruct((N_EXPERTS, 1), F32)],
        scratch_shapes=[pltpu.VMEM((N_EXPERTS, 1), F32)],
        compiler_params=_params("arbitrary", "arbitrary"),
        name="mix",
    )(ms5, yg, x, wo, gt1, gn, sh2, sc2, gt2, wrt, rb, tri, wsg, wsu, wsd)


def _pos_kernel(idx_ref, rank_ref, ps_ref, pos_ref):
    n_e = ps_ref.shape[0]
    tt = idx_ref.shape[1]
    row = lax.broadcasted_iota(jnp.int32, (n_e, tt), 0)
    ps = ps_ref[...]
    starts = [jnp.sum(jnp.where(row == idx_ref[k:k + 1, :], ps, 0.0), axis=0, keepdims=True)
              for k in range(TOP_K)]
    pos = jnp.concatenate(starts, axis=0).astype(jnp.int32) + rank_ref[...]
    for jh in range(tt // LANE):
        pos_ref[jh * TOP_K:(jh + 1) * TOP_K, :] = pos[:, jh * LANE:(jh + 1) * LANE]


def _pos(idx_t, rank_t, pstart):
    T = idx_t.shape[1]
    tt = 2048
    blk = pl.BlockSpec((TOP_K, tt), lambda i: (0, i))
    return pl.pallas_call(
        _pos_kernel,
        grid=(T // tt,),
        in_specs=[blk, blk, pl.BlockSpec((N_EXPERTS, 1), lambda i: (0, 0))],
        out_specs=pl.BlockSpec((tt // LANE * TOP_K, LANE), lambda i: (i, 0)),
        out_shape=jax.ShapeDtypeStruct((T // LANE * TOP_K, LANE), jnp.int32),
        compiler_params=_params("arbitrary"),
        name="pos",
    )(idx_t, rank_t, pstart.astype(F32).reshape(N_EXPERTS, 1))


def _dispatch_kernel(pend_ref, padded_ref, pos_ref, h_ref, xs_hbm, zero_ref, pos_smem, sem_pos, sem_zero, sem_row):
    tt = h_ref.shape[0]
    step = pl.program_id(0)
    load_pos = pltpu.make_async_copy(pos_ref, pos_smem, sem_pos)
    load_pos.start()

    def zero_block(start):
        return pltpu.make_async_copy(zero_ref, xs_hbm.at[pl.ds(pl.multiple_of(start, MOE_BLOCK), MOE_BLOCK)],
                                     sem_zero)

    def for_each_zero_block(fn):
        def per_expert(e, c):
            @pl.when(padded_ref[e] > 0)
            def _():
                fn(zero_block(pend_ref[e] - MOE_BLOCK))
            return c

        def per_tail_block(g, c):
            fn(zero_block(g * MOE_BLOCK))
            return c

        lax.fori_loop(0, N_EXPERTS, per_expert, 0)
        lax.fori_loop(pend_ref[N_EXPERTS - 1] // MOE_BLOCK, xs_hbm.shape[0] // MOE_BLOCK, per_tail_block, 0)

    @pl.when(step == 0)
    def _():
        zero_ref[...] = jnp.zeros_like(zero_ref)
        for_each_zero_block(lambda cp: cp.start())
        for_each_zero_block(lambda cp: cp.wait())

    load_pos.wait()
    for jh in range(tt // LANE):
        def issue(jl, c, jh=jh):
            for k in range(TOP_K):
                slot = pos_smem[(jh * TOP_K + k) * LANE + jl]
                pltpu.make_async_copy(h_ref.at[jh * LANE + jl], xs_hbm.at[slot], sem_row).start(priority=k % 2)
            return c

        lax.fori_loop(0, LANE, issue, 0, unroll=4)
    for k in range(TOP_K):
        pltpu.make_async_copy(h_ref, xs_hbm.at[pl.ds(0, tt)], sem_row).wait()


def _dispatch(pends, padded, pos_t, h_rows, n_slots):
    T = h_rows.shape[0]
    tt = 2048
    return pl.pallas_call(
        _dispatch_kernel,
        grid_spec=pltpu.PrefetchScalarGridSpec(
            num_scalar_prefetch=2,
            grid=(T // tt,),
            in_specs=[pl.BlockSpec((tt * TOP_K,), lambda i, pe, pa: (i,)),
                      pl.BlockSpec((tt, ROW_CHUNKS, LANE), lambda i, pe, pa: (i, 0, 0))],
            out_specs=pl.BlockSpec(memory_space=pl.ANY),
            scratch_shapes=[pltpu.VMEM((MOE_BLOCK, ROW_CHUNKS, LANE), jnp.uint32),
                            pltpu.SMEM((tt * TOP_K,), jnp.int32),
                            pltpu.SemaphoreType.DMA, pltpu.SemaphoreType.DMA, pltpu.SemaphoreType.DMA]),
        out_shape=jax.ShapeDtypeStruct((n_slots, ROW_CHUNKS, LANE), jnp.uint32),
        compiler_params=_params("arbitrary"),
        name="dispatch",
    )(pends, padded, pos_t, h_rows)


X_GROUP = 8
X_AHEAD = 8
X_BUFS = X_AHEAD + X_GROUP
Y_BUFS = X_GROUP


def _expert_kernel(b0_ref, nb_ref, nv_ref, wg_ref, wu_ref, wd_ref, xs_hbm, ye_hbm,
                   wg_s, wu_s, wd_s, xbuf, ybuf, semx, semy, *, n_blocks):
    e = pl.program_id(0)
    nb = nb_ref[e]
    b0 = b0_ref[e]
    nv = nv_ref[0]
    rows = MOE_BLOCK * ROW_CHUNKS

    def block(ref, g):
        return ref.at[pl.ds(pl.multiple_of(g * rows, rows), rows)]

    def x_copy(g):
        slot = g % X_BUFS
        return pltpu.make_async_copy(block(xs_hbm, g), xbuf.at[slot], semx.at[slot])

    def y_copy(g):
        slot = g % Y_BUFS
        return pltpu.make_async_copy(ybuf.at[slot], block(ye_hbm, g), semy.at[slot])

    @pl.when(e == 0)
    def _():
        for g in range(X_AHEAD):
            x_copy(g).start()

    @pl.when(nb > 0)
    def _():
        wg_s[...] = wg_ref[0].astype(BF16)
        wu_s[...] = wu_ref[0].astype(BF16)
        wd_s[...] = wd_ref[0].astype(BF16)

    def fetch(g):
        x_copy(g).wait()

        @pl.when(g + X_AHEAD < nv)
        def _():
            x_copy(g + X_AHEAD).start()

    def load_x(g):
        lo, hi = _unpack_bf16_pairs(_load_rows(xbuf.at[g % X_BUFS], MOE_BLOCK))
        return jnp.concatenate([lo, hi], axis=1).astype(BF16)

    def emit(g, y):
        @pl.when(g >= Y_BUFS)
        def _():
            y_copy(g - Y_BUFS).wait()

        _store_rows(ybuf.at[g % Y_BUFS], y)
        y_copy(g).start()

    def run(g, n):
        for b in range(n):
            fetch(g + b)
        x = jnp.concatenate([load_x(g + b) for b in range(n)], axis=0) if n > 1 else load_x(g)
        a = jnp.dot(x, wg_s[...], preferred_element_type=F32)
        u = jnp.dot(x, wu_s[...], preferred_element_type=F32)
        h = (_silu(a) * u).astype(BF16)
        y = _pack_bf16_pairs(jnp.dot(h, wd_s[...], preferred_element_type=F32))
        for b in range(n):
            emit(g + b, y[b * MOE_BLOCK:(b + 1) * MOE_BLOCK])

    def group(i, c):
        run(b0 + X_GROUP * i, X_GROUP)
        return c

    lax.fori_loop(0, nb // X_GROUP, group, 0)
    done = nb // X_GROUP * X_GROUP
    n = X_GROUP // 2
    while n >= 1:
        @pl.when((nb & n) != 0)
        def _(n=n, done=done):
            run(b0 + done, n)

        done = done + (nb & n)
        n //= 2

    @pl.when(e == pl.num_programs(0) - 1)
    def _():
        for back in range(Y_BUFS, 0, -1):
            y_copy(nv - back).wait()
        ybuf[0] = jnp.zeros(ybuf.shape[1:], ybuf.dtype)

        def fill(g):
            return pltpu.make_async_copy(ybuf.at[0], block(ye_hbm, g), semy.at[0])

        lax.fori_loop(nv, n_blocks, lambda g, c: (fill(g).start(), c)[1], 0)
        lax.fori_loop(nv, n_blocks, lambda g, c: (fill(g).wait(), c)[1], 0)


def _experts(first_block, num_blocks, n_valid, xs, wg, wu, wd):
    rows = MOE_BLOCK * ROW_CHUNKS
    n_blocks = xs.shape[0] // rows
    n_e, d, _ = wg.shape
    wsel = lambda e, b0, nb, nv: (e, 0, 0)
    return pl.pallas_call(
        functools.partial(_expert_kernel, n_blocks=n_blocks),
        grid_spec=pltpu.PrefetchScalarGridSpec(
            num_scalar_prefetch=3,
            grid=(n_e,),
            in_specs=[pl.BlockSpec((1, d, D_EXPERT), wsel),
                      pl.BlockSpec((1, d, D_EXPERT), wsel),
                      pl.BlockSpec((1, D_EXPERT, d), wsel),
                      pl.BlockSpec(memory_space=pl.ANY)],
            out_specs=pl.BlockSpec(memory_space=pl.ANY),
            scratch_shapes=[pltpu.VMEM((d, D_EXPERT), BF16), pltpu.VMEM((d, D_EXPERT), BF16),
                            pltpu.VMEM((D_EXPERT, d), BF16),
                            pltpu.VMEM((X_BUFS, rows, LANE), jnp.uint32),
                            pltpu.VMEM((Y_BUFS, rows, LANE), jnp.uint32),
                            pltpu.SemaphoreType.DMA((X_BUFS,)), pltpu.SemaphoreType.DMA((Y_BUFS,))]),
        out_shape=jax.ShapeDtypeStruct(xs.shape, jnp.uint32),
        compiler_params=_params("arbitrary"),
        name="experts",
    )(first_block, num_blocks, n_valid, wg, wu, wd, xs)


def _final_kernel(pos0_ref, posn_ref, gate_ref, x_ref, gt_ref, g_ref, ye_hbm, o_ref,
                  buf_ref, pos_smem, sem_pos, sem_row):
    tt = x_ref.shape[1]
    i = pl.program_id(0)

    def gather_tile(pos_ref, tile):
        base = (tile % 2) * (TOP_K * tt)
        sem = sem_row.at[tile % 2]
        load_pos = pltpu.make_async_copy(pos_ref, pos_smem, sem_pos)
        load_pos.start()
        load_pos.wait()
        for jh in range(tt // LANE):
            def issue(jl, c, jh=jh):
                for k in range(TOP_K):
                    slot = pos_smem[(jh * TOP_K + k) * LANE + jl]
                    pltpu.make_async_copy(ye_hbm.at[slot], buf_ref.at[base + k * tt + jh * LANE + jl],
                                          sem).start(priority=k % 2)
                return c

            lax.fori_loop(0, LANE, issue, 0, unroll=8)

    @pl.when(i == 0)
    def _():
        gather_tile(pos0_ref, i)

    @pl.when(i + 1 < pl.num_programs(0))
    def _():
        gather_tile(posn_ref, i + 1)

    base = pl.multiple_of((i % 2) * (TOP_K * tt), TOP_K * tt)
    for k in range(TOP_K):
        pltpu.make_async_copy(ye_hbm.at[pl.ds(0, tt)], buf_ref.at[pl.ds(base + k * tt, tt)],
                              sem_row.at[i % 2]).wait()
    gates = gate_ref[...]
    half = D_MODEL // 2
    acc_lo = jnp.zeros((tt, half), F32)
    acc_hi = jnp.zeros((tt, half), F32)
    rows_2d = buf_ref.reshape(buf_ref.shape[0] * ROW_CHUNKS, LANE)
    for k in range(TOP_K):
        lo, hi = _unpack_bf16_pairs(_load_rows(rows_2d.at[pl.ds((base + k * tt) * ROW_CHUNKS, tt * ROW_CHUNKS)], tt))
        gk = gates[:, k:k + 1]
        acc_lo += gk * lo
        acc_hi += gk * hi
    x = x_ref[0] + gt_ref[0] * jnp.concatenate([acc_lo, acc_hi], axis=1)
    o_ref[0] = x * lax.rsqrt(jnp.mean(x * x, axis=-1, keepdims=True) + EPS) * g_ref[...]


def _final(pos_t, gates, x1, gt2, g, ye_rows):
    bsz, L, d = x1.shape
    tt = 512
    nt = L // tt
    n = bsz * nt
    tok = pl.BlockSpec((1, tt, d), lambda i: (i // nt, i % nt, 0))
    return pl.pallas_call(
        _final_kernel,
        grid=(n,),
        in_specs=[pl.BlockSpec((tt * TOP_K,), lambda i: (0,)),
                  pl.BlockSpec((tt * TOP_K,), lambda i: (jnp.minimum(i + 1, n - 1),)),
                  pl.BlockSpec((tt, TOP_K), lambda i: (i, 0)),
                  tok, pl.BlockSpec((1, 1, d), lambda i: (i // nt, 0, 0)),
                  pl.BlockSpec((1, d), lambda i: (0, 0)),
                  pl.BlockSpec(memory_space=pl.ANY)],
        out_specs=tok,
        out_shape=jax.ShapeDtypeStruct((bsz, L, d), F32),
        scratch_shapes=[pltpu.VMEM((2 * TOP_K * tt, ROW_CHUNKS, LANE), jnp.uint32),
                        pltpu.SMEM((tt * TOP_K,), jnp.int32),
                        pltpu.SemaphoreType.DMA, pltpu.SemaphoreType.DMA((2,))],
        compiler_params=_params("arbitrary"),
        name="final",
    )(pos_t, pos_t, gates, x1, gt2, g.reshape(1, d), ye_rows)


def _block_table(counts):
    counts = counts.astype(jnp.int32)
    padded = (counts + MOE_BLOCK - 1) // MOE_BLOCK * MOE_BLOCK
    pends = jnp.cumsum(padded)
    pstarts = pends - padded
    n_valid = (pends[-1] // MOE_BLOCK).reshape(1)
    return padded, pends, pstarts, n_valid


def kernel(x, c, w_ada, b_ada, g_norm_mix, w_in, s5_lambda_re, s5_lambda_im, s5_log_dt, s5_b_re, s5_b_im, s5_c_re, s5_c_im, s5_d, s5_w_glu, s5_b_glu, s5_norm_g, gla_w_g2, gla_b_g2, gla_norm_g, w_out, g_norm_moe, w_router, router_bias, exp_w_gate, exp_w_up, exp_w_down, sh_w_gate, sh_w_up, sh_w_down, g_final):
    bsz, L, d = x.shape
    T = bsz * L
    assert w_ada.shape[0] == 1, "single-layer block"
    for l in range(1):
        mod = _ada(c, w_ada[l], b_ada[l])
        sh1, sc1, gt1, sh2, sc2, gt2 = [m.reshape(bsz, 1, d) for m in jnp.split(mod, 6, axis=-1)]

        wi = w_in[l]
        o_q, o_k, o_v, o_g, o_r = D_S5, D_S5 + GLA_QK, D_S5 + 2 * GLA_QK, D_S5 + 2 * GLA_QK + D_GLA, \
            D_S5 + 2 * GLA_QK + D_GLA + GLA_LOWRANK
        w_cat = jnp.concatenate([wi[:, :o_g], wi[:, o_r:], wi[:, o_g:o_r],
                                 jnp.zeros((d, LANE - GLA_LOWRANK), wi.dtype)], axis=1).astype(BF16)
        u, q, k, v, r, glr = _inproj(x, g_norm_mix[l], sh1, sc1, w_cat)

        mats = _s5_prep(s5_lambda_re[l], s5_lambda_im[l], s5_log_dt[l], s5_b_re[l], s5_b_im[l],
                        s5_c_re[l], s5_c_im[l], s5_d[l])
        ms5 = _s5(u, mats, s5_w_glu[l].T.astype(BF16), s5_b_glu[l].reshape(D_S5, 1),
                  s5_norm_g[l].reshape(D_S5, 1), w_out[l][:D_S5].astype(BF16))

        wg2 = jnp.concatenate([gla_w_g2[l], jnp.zeros((LANE - GLA_LOWRANK, GLA_QK), F32)], axis=0).astype(BF16)
        yg = _gla(q, k, v, glr, r, wg2, gla_b_g2[l].reshape(1, GLA_QK), gla_norm_g[l].reshape(1, GLA_DV))

        x1, h2p, idx_t, rank_t, gates, counts = _mix(
            ms5, yg, x, w_out[l][D_S5:].astype(BF16), gt1, g_norm_moe[l].reshape(1, d), sh2, sc2, gt2,
            w_router[l].T.astype(BF16), router_bias[l].reshape(N_EXPERTS, 1),
            sh_w_gate[l].astype(BF16), sh_w_up[l].astype(BF16), sh_w_down[l].astype(BF16))

        n_blocks = (T * TOP_K + N_EXPERTS * (MOE_BLOCK - 1) + MOE_BLOCK - 1) // MOE_BLOCK
        n_slots = n_blocks * MOE_BLOCK
        padded, pends, pstarts, n_valid = _block_table(counts[:, 0])
        pos_t = _pos(idx_t, rank_t, pstarts).reshape(-1)
        xs = _dispatch(pends, padded, pos_t, h2p.reshape(T, ROW_CHUNKS, LANE), n_slots)
        ye = _experts(pstarts // MOE_BLOCK, padded // MOE_BLOCK, n_valid,
                      xs.reshape(n_slots * ROW_CHUNKS, LANE), exp_w_gate[l], exp_w_up[l], exp_w_down[l])
    return _final(pos_t, gates, x1, gt2, g_final, ye.reshape(n_slots, ROW_CHUNKS, LANE))
```

```python
import functools

import jax
import jax.numpy as jnp
from jax import lax
from jax.experimental import pallas as pl
from jax.experimental.pallas import tpu as pltpu

F32 = jnp.float32
BF16 = jnp.bfloat16

D_MODEL = 1024
D_S5 = 512
S5_GROUP = 16
S5_GROUPS = 32
S5_STATE = 64
S5_CHUNK = 16
S5_STEP_GROUPS = 8
S5_LEVELS = 7
S5_APOW_COLS = 16
D_GLA = 512
GLA_HEADS = 4
GLA_DK = 64
GLA_DV = 128
GLA_QK = 256
GLA_LOWRANK = 16
GLA_TAU = 16.0
GLA_CHUNK = 64
LANE = 128
N_EXPERTS = 256
TOP_K = 8
N_GROUPS = 8
TOPK_GROUPS = 4
D_EXPERT = 256
ROUTED_SCALE = 2.5
EPS = 1e-6
MOE_BLOCK = 128
ROW_CHUNKS = D_MODEL // 2 // LANE
VMEM_LIMIT = 48 * 1024 * 1024


def _silu(x):
    return x * jax.nn.sigmoid(x)


def _params(*sem):
    return pltpu.CompilerParams(dimension_semantics=sem, vmem_limit_bytes=VMEM_LIMIT)


def _ada_kernel(c_ref, w_ref, b_ref, o_ref):
    s = _silu(c_ref[...]).astype(BF16)
    o_ref[...] = jnp.dot(s, w_ref[...].astype(BF16), preferred_element_type=F32) + b_ref[...]


def _ada(c, w, b):
    bsz, d = c.shape
    n = w.shape[1]
    tn = 1024
    return pl.pallas_call(
        _ada_kernel,
        grid=(n // tn,),
        in_specs=[pl.BlockSpec((bsz, d), lambda j: (0, 0)),
                  pl.BlockSpec((d, tn), lambda j: (0, j)),
                  pl.BlockSpec((1, tn), lambda j: (0, j))],
        out_specs=pl.BlockSpec((bsz, tn), lambda j: (0, j)),
        out_shape=jax.ShapeDtypeStruct((bsz, n), F32),
        compiler_params=_params("arbitrary"),
        name="ada",
    )(c, w, b.reshape(1, n))


def _inproj_kernel(x_ref, g_ref, sh_ref, sc_ref, w_ref,
                   u_ref, q_ref, k_ref, v_ref, r_ref, glr_ref):
    x = x_ref[0]
    ms = jnp.mean(x * x, axis=-1, keepdims=True)
    h = (x * lax.rsqrt(ms + EPS)) * g_ref[...]
    h = h * (1.0 + sc_ref[0]) + sh_ref[0]
    hb = h.astype(BF16)
    u = jnp.dot(hb, w_ref[:, 0:D_S5], preferred_element_type=F32)
    for c in range(D_S5 // LANE):
        u_ref[0, c] = u[:, c * LANE:(c + 1) * LANE]
    col = D_S5
    for ref in (q_ref, k_ref, v_ref, r_ref, glr_ref):
        n = ref.shape[-1]
        ref[0] = jnp.dot(hb, w_ref[:, col:col + n], preferred_element_type=F32).astype(ref.dtype)
        col += n


def _inproj(x, g, sh, sc, w):
    bsz, L, d = x.shape
    tt = 1024
    widths = (GLA_QK, GLA_QK, D_GLA, D_GLA, LANE)
    tok = lambda n: pl.BlockSpec((1, tt, n), lambda b, i: (b, i, 0))
    vec = pl.BlockSpec((1, 1, d), lambda b, i: (b, 0, 0))
    return pl.pallas_call(
        _inproj_kernel,
        grid=(bsz, L // tt),
        in_specs=[tok(d), pl.BlockSpec((1, d), lambda b, i: (0, 0)), vec, vec,
                  pl.BlockSpec(w.shape, lambda b, i: (0, 0))],
        out_specs=[pl.BlockSpec((1, D_S5 // LANE, tt, LANE), lambda b, i: (b, 0, i, 0))] + [tok(n) for n in widths],
        out_shape=[jax.ShapeDtypeStruct((bsz, D_S5 // LANE, L, LANE), F32)]
        + [jax.ShapeDtypeStruct((bsz, L, n), BF16) for n in widths],
        compiler_params=_params("arbitrary", "arbitrary"),
        name="inproj",
    )(x, g.reshape(1, d), sh, sc, w)


def _s5_prep(lam_re, lam_im, log_dt, b_re, b_im, c_re, c_im, d_skip):
    G, N, C, TC = S5_GROUPS, S5_STATE, S5_GROUP, S5_CHUNK
    hp = lax.Precision.HIGHEST
    dt = jnp.exp(log_dt)[:, None]
    lr, li = lam_re, lam_im
    mag = jnp.exp(lr * dt)
    ab_re, ab_im = mag * jnp.cos(li * dt), mag * jnp.sin(li * dt)
    den = lr * lr + li * li
    nr = ab_re - 1.0
    coef_re = ((nr * lr + ab_im * li) / den)[..., None]
    coef_im = ((ab_im * lr - nr * li) / den)[..., None]
    bb_re = coef_re * b_re - coef_im * b_im
    bb_im = coef_re * b_im + coef_im * b_re
    p = jnp.arange(TC + 1, dtype=F32)[:, None, None]
    pm = jnp.exp(lr * dt * p)
    pr, pi = pm * jnp.cos(li * dt * p), pm * jnp.sin(li * dt * p)
    ca_re = c_re[None] * pr[:, :, None, :] - c_im[None] * pi[:, :, None, :]
    ca_im = c_re[None] * pi[:, :, None, :] + c_im[None] * pr[:, :, None, :]
    ca = jnp.concatenate([ca_re[:TC], -ca_im[:TC]], axis=-1).transpose(1, 0, 2, 3).reshape(G, TC * C, 2 * N)
    kern = jnp.einsum('gxk,gki->gxi', ca, jnp.concatenate([bb_re, bb_im], axis=1),
                      precision=hp).reshape(G, TC, C, C)
    skip = jnp.eye(C, dtype=F32)[None] * d_skip[:, :, None]
    kern = kern + jnp.concatenate([skip[:, None], jnp.zeros((G, TC - 1, C, C), F32)], axis=1)
    rev = kern[:, ::-1].transpose(0, 2, 1, 3).reshape(G, C, TC * C)
    rev = jnp.concatenate([rev, jnp.zeros((G, C, (TC - 1) * C), F32)], axis=-1)
    toep_t = jnp.stack([rev[:, :, (TC - 1 - t) * C:(TC - 1 - t) * C + TC * C] for t in range(TC)],
                       axis=1).reshape(G, TC * C, TC * C)
    rr, ri = pr[TC - 1 - jnp.arange(TC)], pi[TC - 1 - jnp.arange(TC)]
    binc_re = rr[..., None] * bb_re[None] - ri[..., None] * bb_im[None]
    binc_im = rr[..., None] * bb_im[None] + ri[..., None] * bb_re[None]
    binc_re_t = binc_re.transpose(1, 2, 0, 3).reshape(G, N, TC * C)
    binc_im_t = binc_im.transpose(1, 2, 0, 3).reshape(G, N, TC * C)
    cm_re_t = ca_re[1:].transpose(1, 0, 2, 3).reshape(G, TC * C, N)
    cm_im_t = (-ca_im[1:]).transpose(1, 0, 2, 3).reshape(G, TC * C, N)
    q = (TC * 2.0 ** jnp.arange(S5_LEVELS, dtype=F32))[:, None, None]
    qm = jnp.exp(lr * dt * q)
    qr, qi = qm * jnp.cos(li * dt * q), qm * jnp.sin(li * dt * q)
    apow = jnp.stack([qr, qi], axis=1).reshape(2 * S5_LEVELS, G, N).transpose(1, 2, 0)
    apow = jnp.concatenate([apow, jnp.zeros((G, N, S5_APOW_COLS - 2 * S5_LEVELS), F32)], axis=-1)
    from_z = jnp.concatenate([toep_t, binc_re_t, binc_im_t], axis=1).astype(BF16)
    from_h = jnp.concatenate([cm_re_t, cm_im_t], axis=2).astype(BF16)
    return from_z, from_h, apow


def _s5_kernel(u_ref, eye_ref, fz_ref, fh_ref, apow_ref,
               wglu_ref, bglu_ref, sng_ref, wo_ref, o_ref, ut_ref, yt_ref, ot_ref):
    j = pl.program_id(1)
    n_chunks = u_ref.shape[2] // S5_CHUNK
    nt = (((1,), (1,)), ((), ()))

    @pl.when(j == 0)
    def _():
        for s in range(S5_CHUNK):
            us = jnp.concatenate([u_ref[0, c, pl.ds(s, n_chunks, stride=S5_CHUNK), :]
                                  for c in range(D_S5 // LANE)], axis=1).astype(BF16)
            ut_ref[:, s * n_chunks:(s + 1) * n_chunks] = lax.dot_general(
                eye_ref[...], us, nt, preferred_element_type=F32).astype(BF16)

    lane = lax.broadcasted_iota(jnp.int32, (S5_STATE, n_chunks), 1)
    groups = range(S5_STEP_GROUPS)
    rows = [pl.ds(pl.multiple_of(j * (S5_STEP_GROUPS * S5_GROUP) + gl * S5_GROUP, S5_GROUP), S5_GROUP)
            for gl in groups]
    zt = [jnp.concatenate([ut_ref[rows[gl], s * n_chunks:(s + 1) * n_chunks] for s in range(S5_CHUNK)], axis=0)
          for gl in groups]
    w = S5_CHUNK * S5_GROUP
    zprod = [jnp.dot(fz_ref[gl], zt[gl], preferred_element_type=F32) for gl in groups]
    xr = [jnp.where(lane >= 1, pltpu.roll(zprod[gl][w:w + S5_STATE], 1, axis=1), 0.0) for gl in groups]
    xi = [jnp.where(lane >= 1, pltpu.roll(zprod[gl][w + S5_STATE:], 1, axis=1), 0.0) for gl in groups]
    for lv in range(S5_LEVELS):
        d = 1 << lv
        for gl in groups:
            ar = apow_ref[gl, :, 2 * lv:2 * lv + 1]
            ai = apow_ref[gl, :, 2 * lv + 1:2 * lv + 2]
            sr = jnp.where(lane >= d, pltpu.roll(xr[gl], d, axis=1), 0.0)
            si = jnp.where(lane >= d, pltpu.roll(xi[gl], d, axis=1), 0.0)
            xr[gl], xi[gl] = xr[gl] + ar * sr - ai * si, xi[gl] + ar * si + ai * sr
    for gl in groups:
        state = jnp.concatenate([xr[gl], xi[gl]], axis=0).astype(BF16)
        yt = zprod[gl][:w] + jnp.dot(fh_ref[gl], state, preferred_element_type=F32)
        for t in range(S5_CHUNK):
            yt_ref[rows[gl], t * n_chunks:(t + 1) * n_chunks] = yt[t * S5_GROUP:(t + 1) * S5_GROUP, :]

    @pl.when(j == pl.num_programs(1) - 1)
    def _():
        cw = 2 * n_chunks
        for cc in range(yt_ref.shape[1] // cw):
            y = yt_ref[:, cc * cw:(cc + 1) * cw]
            g = y * (0.5 * (1.0 + jnp.tanh(0.7978845608028654 * (y + 0.044715 * (y * y * y)))))
            z = jnp.dot(wglu_ref[...], g.astype(BF16), preferred_element_type=F32) + bglu_ref[...]
            out = g * jax.nn.sigmoid(z)
            out = out * lax.rsqrt(jnp.mean(out * out, axis=0, keepdims=True) + EPS) * sng_ref[...]
            ot_ref[cc * cw:(cc + 1) * cw, :] = out.T.astype(BF16)
        rows = ot_ref.shape[0] // 2
        for half in range(2):
            mix = jnp.dot(ot_ref[half * rows:(half + 1) * rows, :], wo_ref[...], preferred_element_type=F32)
            for sl in range(rows // n_chunks):
                s = half * (rows // n_chunks) + sl
                for c in range(D_MODEL // LANE):
                    o_ref[0, c, pl.ds(s, n_chunks, stride=S5_CHUNK), :] = mix[sl * n_chunks:(sl + 1) * n_chunks,
                                                                              c * LANE:(c + 1) * LANE]


def _s5(u, mats, wglu_t, bglu, sng, wo_s5):
    bsz, _, L, _ = u.shape
    assert L // S5_CHUNK == LANE, "one lane tile of chunks per sequence"
    from_z, from_h, apow = mats
    sg = S5_STEP_GROUPS
    eye = jnp.eye(D_S5, dtype=BF16)
    grp = lambda a: pl.BlockSpec((sg,) + a.shape[1:], lambda bi, j: (j, 0, 0))
    full = lambda a: pl.BlockSpec(a.shape, lambda bi, j: (0,) * a.ndim)
    return pl.pallas_call(
        _s5_kernel,
        grid=(bsz, S5_GROUPS // sg),
        in_specs=[pl.BlockSpec((1, D_S5 // LANE, L, LANE), lambda bi, j: (bi, 0, 0, 0)), full(eye),
                  grp(from_z), grp(from_h), grp(apow), full(wglu_t), full(bglu), full(sng), full(wo_s5)],
        out_specs=pl.BlockSpec((1, D_MODEL // LANE, L, LANE), lambda bi, j: (bi, 0, 0, 0)),
        out_shape=jax.ShapeDtypeStruct((bsz, D_MODEL // LANE, L, LANE), F32),
        scratch_shapes=[pltpu.VMEM((D_S5, L), BF16), pltpu.VMEM((D_S5, L), F32), pltpu.VMEM((L, D_S5), BF16)],
        compiler_params=_params("arbitrary", "arbitrary"),
        name="s5",
    )(u, eye, from_z, from_h, apow, wglu_t, bglu, sng, wo_s5)


def _gla_kernel(q_ref, k_ref, v_ref, glr_ref, r_ref, wg_ref, bg_ref, ng_ref, o_ref, st_ref):
    lt = q_ref.shape[1]
    C = GLA_CHUNK

    @pl.when(pl.program_id(1) == 0)
    def _():
        st_ref[...] = jnp.zeros_like(st_ref)

    ri = lax.broadcasted_iota(jnp.int32, (lt, lt), 0)
    ci = lax.broadcasted_iota(jnp.int32, (lt, lt), 1)
    tril = jnp.where(((ri >> 6) == (ci >> 6)) & (ci <= ri), 1.0, 0.0).astype(BF16)
    seqs = range(q_ref.shape[0])
    bcum, k, qi, ki = [], [], [], []
    for b in seqs:
        z = jnp.dot(glr_ref[b], wg_ref[...], preferred_element_type=F32) + bg_ref[...]
        log_a = (jnp.minimum(z, 0.0) - jnp.log(1.0 + jnp.exp(-jnp.abs(z)))) * (1.0 / GLA_TAU)
        la_hi = log_a.astype(BF16)
        la_lo = (log_a - la_hi.astype(F32)).astype(BF16)
        bcum.append(jnp.dot(tril, la_hi, preferred_element_type=F32)
                    + jnp.dot(tril, la_lo, preferred_element_type=F32))
        k.append(k_ref[b].astype(F32))
        qi.append(q_ref[b].astype(F32) * (GLA_DK ** -0.5) * jnp.exp(bcum[b]))
        ki.append(k[b] * jnp.exp(-bcum[b]))
    lane_head = lax.broadcasted_iota(jnp.int32, (1, GLA_QK), 1) >> 6
    causal = ((lax.broadcasted_iota(jnp.int32, (GLA_HEADS * C, C), 0) & (C - 1))
              >= lax.broadcasted_iota(jnp.int32, (GLA_HEADS * C, C), 1))
    same_head = ((lax.broadcasted_iota(jnp.int32, (D_GLA, GLA_QK), 0) >> 7)
                 == (lax.broadcasted_iota(jnp.int32, (D_GLA, GLA_QK), 1) >> 6))
    nt = (((1,), (1,)), ((), ()))
    ng = ng_ref[...]
    st = [st_ref[b] for b in seqs]
    for c in range(lt // C):
        sl = slice(c * C, (c + 1) * C)
        for b in seqs:
            bc = bcum[b][sl]
            bl = bc[C - 1:C, :]
            kd = k[b][sl] * jnp.exp(bl - bc)
            qic = qi[b][sl]
            qs = jnp.concatenate([jnp.where(lane_head == h, qic, 0.0) for h in range(GLA_HEADS)],
                                 axis=0).astype(BF16)
            sc = lax.dot_general(qs, ki[b][sl].astype(BF16), nt, preferred_element_type=F32)
            p = jnp.where(causal, sc, 0.0).astype(BF16)
            vc = v_ref[b, sl, :]
            o_intra = jnp.concatenate(
                [jnp.dot(p[h * C:(h + 1) * C], vc[:, h * GLA_DV:(h + 1) * GLA_DV], preferred_element_type=F32)
                 for h in range(GLA_HEADS)], axis=1)
            o_inter = lax.dot_general(qic.astype(BF16), st[b].astype(BF16), nt, preferred_element_type=F32)
            v_t = vc.astype(F32).T.astype(BF16)
            kv_t = jnp.dot(v_t, kd.astype(BF16), preferred_element_type=F32)
            st[b] = st[b] * jnp.exp(bl) + jnp.where(same_head, kv_t, 0.0)
            o = o_intra + o_inter
            parts = []
            for h in range(GLA_HEADS):
                oh = o[:, h * GLA_DV:(h + 1) * GLA_DV]
                oh = oh * lax.rsqrt(jnp.mean(oh * oh, axis=-1, keepdims=True) + EPS)
                parts.append(oh * ng)
            r = r_ref[b, sl, :].astype(F32)
            o_ref[b, sl, :] = (jnp.concatenate(parts, axis=1) * _silu(r)).astype(o_ref.dtype)
    for b in seqs:
        st_ref[b] = st[b]


def _gla(q, k, v, glr, r, wg, bg, ng):
    bsz, L, _ = q.shape
    lt = 256
    nb = 2
    tok = lambda n: pl.BlockSpec((nb, lt, n), lambda b, i: (b, i, 0))
    full = lambda a: pl.BlockSpec(a.shape, lambda b, i: (0,) * a.ndim)
    return pl.pallas_call(
        _gla_kernel,
        grid=(bsz // nb, L // lt),
        in_specs=[tok(GLA_QK), tok(GLA_QK), tok(D_GLA), tok(LANE), tok(D_GLA), full(wg), full(bg), full(ng)],
        out_specs=tok(D_GLA),
        out_shape=jax.ShapeDtypeStruct((bsz, L, D_GLA), BF16),
        scratch_shapes=[pltpu.VMEM((nb, D_GLA, GLA_QK), F32)],
        compiler_params=_params("arbitrary", "arbitrary"),
        name="gla",
    )(q, k, v, glr, r, wg, bg, ng)


def _pack_bf16_pairs(x):
    w = x.shape[1] // 2
    xr = x.astype(BF16).astype(F32)
    lo = lax.bitcast_convert_type(xr[:, :w], jnp.uint32) >> 16
    hi = lax.bitcast_convert_type(xr[:, w:], jnp.uint32) & jnp.uint32(0xFFFF0000)
    return lo | hi


def _unpack_bf16_pairs(p):
    lo = lax.bitcast_convert_type(p << 16, F32)
    hi = lax.bitcast_convert_type(p & jnp.uint32(0xFFFF0000), F32)
    return lo, hi


def _store_rows(ref, packed):
    n = packed.shape[0]
    for c in range(ROW_CHUNKS):
        ref[pl.ds(c, n, stride=ROW_CHUNKS), :] = packed[:, c * LANE:(c + 1) * LANE]


def _load_rows(ref, n):
    return jnp.concatenate([ref[pl.ds(c, n, stride=ROW_CHUNKS), :] for c in range(ROW_CHUNKS)], axis=1)


def _route_tile(lg_t, bias_col, tri, carry_ref):
    n_e, tt = lg_t.shape
    per_group = n_e // N_GROUPS
    scores = jax.nn.sigmoid(lg_t)
    biased = scores + bias_col
    row = lax.broadcasted_iota(jnp.int32, (n_e, tt), 0)
    neg = -jnp.inf
    group_score = []
    for g in range(N_GROUPS):
        b = biased[g * per_group:(g + 1) * per_group]
        r = lax.broadcasted_iota(jnp.int32, (per_group, tt), 0) + g * per_group
        m1 = jnp.max(b, axis=0, keepdims=True)
        i1 = jnp.min(jnp.where(b == m1, r, n_e), axis=0, keepdims=True)
        m2 = jnp.max(jnp.where(r == i1, neg, b), axis=0, keepdims=True)
        group_score.append(m1 + m2)
    parts = []
    for g in range(N_GROUPS):
        ahead = jnp.zeros((1, tt), jnp.int32)
        for g2 in range(N_GROUPS):
            if g2 != g:
                beats = (group_score[g2] >= group_score[g]) if g2 < g else (group_score[g2] > group_score[g])
                ahead = ahead + beats.astype(jnp.int32)
        parts.append(jnp.where(ahead < TOPK_GROUPS, biased[g * per_group:(g + 1) * per_group], neg))
    masked = jnp.concatenate(parts, axis=0)
    work = masked
    idxs = []
    for _ in range(TOP_K):
        m = jnp.max(work, axis=0, keepdims=True)
        ii = jnp.min(jnp.where(work == m, row, n_e), axis=0, keepdims=True)
        idxs.append(ii)
        work = jnp.where(row == ii, neg, work)
    sel = work != masked
    w = jnp.where(sel, scores, 0.0)
    gate_dense = w / jnp.sum(w, axis=0, keepdims=True) * ROUTED_SCALE
    mt = jnp.where(sel, 1.0, 0.0)
    rank_dense = jnp.dot(mt.astype(BF16), tri, preferred_element_type=F32) + carry_ref[...]
    carry_ref[...] += jnp.sum(mt, axis=1, keepdims=True)
    ranks, gts = [], []
    for ii in idxs:
        oh = row == ii
        ranks.append(jnp.sum(jnp.where(oh, rank_dense, 0.0), axis=0, keepdims=True))
        gts.append(jnp.sum(jnp.where(oh, gate_dense, 0.0), axis=0, keepdims=True))
    idx_t = jnp.concatenate(idxs, axis=0)
    rank_t = jnp.concatenate(ranks, axis=0).astype(jnp.int32)
    gate_t = jnp.concatenate(gts + [jnp.zeros((LANE - TOP_K, tt), F32)], axis=0)
    return idx_t, rank_t, gate_t.T[:, :TOP_K]


def _mix_kernel(ms_ref, yg_ref, x_ref, wo_ref, gt1_ref,
                gn_ref, sh2_ref, sc2_ref, gt2_ref, wrt_ref, rb_ref, tri_ref, wsg_ref, wsu_ref, wsd_ref,
                x1_ref, h2_ref, idx_ref, rank_ref, gate_ref, cnt_ref, carry_ref):
    @pl.when((pl.program_id(0) == 0) & (pl.program_id(1) == 0))
    def _():
        carry_ref[...] = jnp.zeros_like(carry_ref)

    mix_s5 = jnp.concatenate([ms_ref[0, c] for c in range(D_MODEL // LANE)], axis=1)
    mix = mix_s5 + jnp.dot(yg_ref[0], wo_ref[...], preferred_element_type=F32)
    x1 = x_ref[0] + gt1_ref[0] * mix
    h2 = x1 * lax.rsqrt(jnp.mean(x1 * x1, axis=-1, keepdims=True) + EPS) * gn_ref[...]
    h2 = h2 * (1.0 + sc2_ref[0]) + sh2_ref[0]
    hb = h2.astype(BF16)
    _store_rows(h2_ref, _pack_bf16_pairs(h2))
    lg_t = lax.dot_general(wrt_ref[...], hb, (((1,), (1,)), ((), ())), preferred_element_type=F32)
    idx_t, rank_t, gates = _route_tile(lg_t, rb_ref[...], tri_ref[...], carry_ref)
    idx_ref[...] = idx_t
    rank_ref[...] = rank_t
    gate_ref[...] = gates
    cnt_ref[...] = carry_ref[...]
    a = _silu(jnp.dot(hb, wsg_ref[...], preferred_element_type=F32)) * jnp.dot(
        hb, wsu_ref[...], preferred_element_type=F32)
    shared = jnp.dot(a.astype(BF16), wsd_ref[...], preferred_element_type=F32)
    x1_ref[0] = x1 + gt2_ref[0] * shared


def _mix(ms5, yg, x, wo, gt1, gn, sh2, sc2, gt2, wrt, rb, wsg, wsu, wsd):
    bsz, L, d = x.shape
    tt = 512
    nt = L // tt
    T = bsz * L
    tri = (jnp.arange(tt)[:, None] < jnp.arange(tt)[None, :]).astype(BF16)
    tok = lambda n: pl.BlockSpec((1, tt, n), lambda b, i: (b, i, 0))
    vec = pl.BlockSpec((1, 1, d), lambda b, i: (b, 0, 0))
    full = lambda a: pl.BlockSpec(a.shape, lambda b, i: (0,) * a.ndim)
    lanes = pl.BlockSpec((TOP_K, tt), lambda b, i: (0, b * nt + i))
    return pl.pallas_call(
        _mix_kernel,
        grid=(bsz, nt),
        in_specs=[pl.BlockSpec((1, d // LANE, tt, LANE), lambda b, i: (b, 0, i, 0)), tok(D_GLA), tok(d), full(wo), vec,
                  full(gn), vec, vec, vec, full(wrt), full(rb), full(tri), full(wsg), full(wsu), full(wsd)],
        out_specs=[tok(d),
                   pl.BlockSpec((tt * ROW_CHUNKS, LANE), lambda b, i: (b * nt + i, 0)),
                   lanes, lanes,
                   pl.BlockSpec((tt, TOP_K), lambda b, i: (b * nt + i, 0)),
                   pl.BlockSpec((N_EXPERTS, 1), lambda b, i: (0, 0))],
        out_shape=[jax.ShapeDtypeStruct((bsz, L, d), F32),
                   jax.ShapeDtypeStruct((T * ROW_CHUNKS, LANE), jnp.uint32),
                   jax.ShapeDtypeStruct((TOP_K, T), jnp.int32),
                   jax.ShapeDtypeStruct((TOP_K, T), jnp.int32),
                   jax.ShapeDtypeStruct((T, TOP_K), F32),
                   jax.ShapeDtypeStruct((N_EXPERTS, 1), F32)],
        scratch_shapes=[pltpu.VMEM((N_EXPERTS, 1), F32)],
        compiler_params=_params("arbitrary", "arbitrary"),
        name="mix",
    )(ms5, yg, x, wo, gt1, gn, sh2, sc2, gt2, wrt, rb, tri, wsg, wsu, wsd)


def _pos_kernel(idx_ref, rank_ref, ps_ref, pos_ref):
    n_e = ps_ref.shape[0]
    tt = idx_ref.shape[1]
    row = lax.broadcasted_iota(jnp.int32, (n_e, tt), 0)
    ps = ps_ref[...]
    starts = [jnp.sum(jnp.where(row == idx_ref[k:k + 1, :], ps, 0.0), axis=0, keepdims=True)
              for k in range(TOP_K)]
    pos = jnp.concatenate(starts, axis=0).astype(jnp.int32) + rank_ref[...]
    for jh in range(tt // LANE):
        pos_ref[jh * TOP_K:(jh + 1) * TOP_K, :] = pos[:, jh * LANE:(jh + 1) * LANE]


def _pos(idx_t, rank_t, pstart):
    T = idx_t.shape[1]
    tt = 2048
    blk = pl.BlockSpec((TOP_K, tt), lambda i: (0, i))
    return pl.pallas_call(
        _pos_kernel,
        grid=(T // tt,),
        in_specs=[blk, blk, pl.BlockSpec((N_EXPERTS, 1), lambda i: (0, 0))],
        out_specs=pl.BlockSpec((tt // LANE * TOP_K, LANE), lambda i: (i, 0)),
        out_shape=jax.ShapeDtypeStruct((T // LANE * TOP_K, LANE), jnp.int32),
        compiler_params=_params("arbitrary"),
        name="pos",
    )(idx_t, rank_t, pstart.astype(F32).reshape(N_EXPERTS, 1))


def _dispatch_kernel(pend_ref, padded_ref, pos_ref, h_ref, xs_hbm, zero_ref, pos_smem, sem_pos, sem_zero, sem_row):
    tt = h_ref.shape[0]
    step = pl.program_id(0)
    load_pos = pltpu.make_async_copy(pos_ref, pos_smem, sem_pos)
    load_pos.start()

    def zero_block(start):
        return pltpu.make_async_copy(zero_ref, xs_hbm.at[pl.ds(pl.multiple_of(start, MOE_BLOCK), MOE_BLOCK)],
                                     sem_zero)

    def for_each_zero_block(fn):
        def per_expert(e, c):
            @pl.when(padded_ref[e] > 0)
            def _():
                fn(zero_block(pend_ref[e] - MOE_BLOCK))
            return c

        def per_tail_block(g, c):
            fn(zero_block(g * MOE_BLOCK))
            return c

        lax.fori_loop(0, N_EXPERTS, per_expert, 0)
        lax.fori_loop(pend_ref[N_EXPERTS - 1] // MOE_BLOCK, xs_hbm.shape[0] // MOE_BLOCK, per_tail_block, 0)

    @pl.when(step == 0)
    def _():
        zero_ref[...] = jnp.zeros_like(zero_ref)
        for_each_zero_block(lambda cp: cp.start())
        for_each_zero_block(lambda cp: cp.wait())

    load_pos.wait()
    for jh in range(tt // LANE):
        def issue(jl, c, jh=jh):
            for k in range(TOP_K):
                slot = pos_smem[(jh * TOP_K + k) * LANE + jl]
                pltpu.make_async_copy(h_ref.at[jh * LANE + jl], xs_hbm.at[slot], sem_row).start(priority=k % 2)
            return c

        lax.fori_loop(0, LANE, issue, 0, unroll=4)
    for k in range(TOP_K):
        pltpu.make_async_copy(h_ref, xs_hbm.at[pl.ds(0, tt)], sem_row).wait()


def _dispatch(pends, padded, pos_t, h_rows, n_slots):
    T = h_rows.shape[0]
    tt = 4096
    return pl.pallas_call(
        _dispatch_kernel,
        grid_spec=pltpu.PrefetchScalarGridSpec(
            num_scalar_prefetch=2,
            grid=(T // tt,),
            in_specs=[pl.BlockSpec((tt * TOP_K,), lambda i, pe, pa: (i,)),
                      pl.BlockSpec((tt, ROW_CHUNKS, LANE), lambda i, pe, pa: (i, 0, 0))],
            out_specs=pl.BlockSpec(memory_space=pl.ANY),
            scratch_shapes=[pltpu.VMEM((MOE_BLOCK, ROW_CHUNKS, LANE), jnp.uint32),
                            pltpu.SMEM((tt * TOP_K,), jnp.int32),
                            pltpu.SemaphoreType.DMA, pltpu.SemaphoreType.DMA, pltpu.SemaphoreType.DMA]),
        out_shape=jax.ShapeDtypeStruct((n_slots, ROW_CHUNKS, LANE), jnp.uint32),
        compiler_params=_params("arbitrary"),
        name="dispatch",
    )(pends, padded, pos_t, h_rows)


X_GROUP = 8
X_AHEAD = 8
X_BUFS = X_AHEAD + X_GROUP
Y_BUFS = X_GROUP


def _expert_kernel(b0_ref, nb_ref, nv_ref, wg_ref, wu_ref, wd_ref, xs_hbm, ye_hbm,
                   wg_s, wu_s, wd_s, xbuf, ybuf, semx, semy, *, n_blocks):
    e = pl.program_id(0)
    nb = nb_ref[e]
    b0 = b0_ref[e]
    nv = nv_ref[0]
    rows = MOE_BLOCK * ROW_CHUNKS

    def block(ref, g):
        return ref.at[pl.ds(pl.multiple_of(g * rows, rows), rows)]

    def x_copy(g):
        slot = g % X_BUFS
        return pltpu.make_async_copy(block(xs_hbm, g), xbuf.at[slot], semx.at[slot])

    def y_copy(g):
        slot = g % Y_BUFS
        return pltpu.make_async_copy(ybuf.at[slot], block(ye_hbm, g), semy.at[slot])

    @pl.when(e == 0)
    def _():
        for g in range(X_AHEAD):
            x_copy(g).start()

    @pl.when(nb > 0)
    def _():
        wg_s[...] = wg_ref[0].astype(BF16)
        wu_s[...] = wu_ref[0].astype(BF16)
        wd_s[...] = wd_ref[0].astype(BF16)

    def fetch(g):
        x_copy(g).wait()

        @pl.when(g + X_AHEAD < nv)
        def _():
            x_copy(g + X_AHEAD).start()

    def load_x(g):
        lo, hi = _unpack_bf16_pairs(_load_rows(xbuf.at[g % X_BUFS], MOE_BLOCK))
        return jnp.concatenate([lo, hi], axis=1).astype(BF16)

    def emit(g, y):
        @pl.when(g >= Y_BUFS)
        def _():
            y_copy(g - Y_BUFS).wait()

        _store_rows(ybuf.at[g % Y_BUFS], y)
        y_copy(g).start()

    def run(g, n):
        for b in range(n):
            fetch(g + b)
        x = jnp.concatenate([load_x(g + b) for b in range(n)], axis=0) if n > 1 else load_x(g)
        a = jnp.dot(x, wg_s[...], preferred_element_type=F32)
        u = jnp.dot(x, wu_s[...], preferred_element_type=F32)
        h = (_silu(a) * u).astype(BF16)
        y = _pack_bf16_pairs(jnp.dot(h, wd_s[...], preferred_element_type=F32))
        for b in range(n):
            emit(g + b, y[b * MOE_BLOCK:(b + 1) * MOE_BLOCK])

    def group(i, c):
        run(b0 + X_GROUP * i, X_GROUP)
        return c

    lax.fori_loop(0, nb // X_GROUP, group, 0)
    done = nb // X_GROUP * X_GROUP
    n = X_GROUP // 2
    while n >= 1:
        @pl.when((nb & n) != 0)
        def _(n=n, done=done):
            run(b0 + done, n)

        done = done + (nb & n)
        n //= 2

    @pl.when(e == pl.num_programs(0) - 1)
    def _():
        for back in range(Y_BUFS, 0, -1):
            y_copy(nv - back).wait()
        ybuf[0] = jnp.zeros(ybuf.shape[1:], ybuf.dtype)

        def fill(g):
            return pltpu.make_async_copy(ybuf.at[0], block(ye_hbm, g), semy.at[0])

        lax.fori_loop(nv, n_blocks, lambda g, c: (fill(g).start(), c)[1], 0)
        lax.fori_loop(nv, n_blocks, lambda g, c: (fill(g).wait(), c)[1], 0)


def _experts(first_block, num_blocks, n_valid, xs, wg, wu, wd):
    rows = MOE_BLOCK * ROW_CHUNKS
    n_blocks = xs.shape[0] // rows
    n_e, d, _ = wg.shape
    wsel = lambda e, b0, nb, nv: (e, 0, 0)
    return pl.pallas_call(
        functools.partial(_expert_kernel, n_blocks=n_blocks),
        grid_spec=pltpu.PrefetchScalarGridSpec(
            num_scalar_prefetch=3,
            grid=(n_e,),
            in_specs=[pl.BlockSpec((1, d, D_EXPERT), wsel),
                      pl.BlockSpec((1, d, D_EXPERT), wsel),
                      pl.BlockSpec((1, D_EXPERT, d), wsel),
                      pl.BlockSpec(memory_space=pl.ANY)],
            out_specs=pl.BlockSpec(memory_space=pl.ANY),
            scratch_shapes=[pltpu.VMEM((d, D_EXPERT), BF16), pltpu.VMEM((d, D_EXPERT), BF16),
                            pltpu.VMEM((D_EXPERT, d), BF16),
                            pltpu.VMEM((X_BUFS, rows, LANE), jnp.uint32),
                            pltpu.VMEM((Y_BUFS, rows, LANE), jnp.uint32),
                            pltpu.SemaphoreType.DMA((X_BUFS,)), pltpu.SemaphoreType.DMA((Y_BUFS,))]),
        out_shape=jax.ShapeDtypeStruct(xs.shape, jnp.uint32),
        compiler_params=_params("arbitrary"),
        name="experts",
    )(first_block, num_blocks, n_valid, wg, wu, wd, xs)


def _final_kernel(pos0_ref, posn_ref, gate_ref, x_ref, gt_ref, g_ref, ye_hbm, o_ref,
                  buf_ref, pos_smem, sem_pos, sem_row):
    tt = x_ref.shape[1]
    i = pl.program_id(0)

    def gather_tile(pos_ref, tile):
        base = (tile % 2) * (TOP_K * tt)
        sem = sem_row.at[tile % 2]
        load_pos = pltpu.make_async_copy(pos_ref, pos_smem, sem_pos)
        load_pos.start()
        load_pos.wait()
        for jh in range(tt // LANE):
            def issue(jl, c, jh=jh):
                for k in range(TOP_K):
                    slot = pos_smem[(jh * TOP_K + k) * LANE + jl]
                    pltpu.make_async_copy(ye_hbm.at[slot], buf_ref.at[base + k * tt + jh * LANE + jl],
                                          sem).start(priority=k % 2)
                return c

            lax.fori_loop(0, LANE, issue, 0, unroll=8)

    @pl.when(i == 0)
    def _():
        gather_tile(pos0_ref, i)

    @pl.when(i + 1 < pl.num_programs(0))
    def _():
        gather_tile(posn_ref, i + 1)

    base = pl.multiple_of((i % 2) * (TOP_K * tt), TOP_K * tt)
    for k in range(TOP_K):
        pltpu.make_async_copy(ye_hbm.at[pl.ds(0, tt)], buf_ref.at[pl.ds(base + k * tt, tt)],
                              sem_row.at[i % 2]).wait()
    gates = gate_ref[...]
    half = D_MODEL // 2
    acc_lo = jnp.zeros((tt, half), F32)
    acc_hi = jnp.zeros((tt, half), F32)
    rows_2d = buf_ref.reshape(buf_ref.shape[0] * ROW_CHUNKS, LANE)
    for k in range(TOP_K):
        lo, hi = _unpack_bf16_pairs(_load_rows(rows_2d.at[pl.ds((base + k * tt) * ROW_CHUNKS, tt * ROW_CHUNKS)], tt))
        gk = gates[:, k:k + 1]
        acc_lo += gk * lo
        acc_hi += gk * hi
    x = x_ref[0] + gt_ref[0] * jnp.concatenate([acc_lo, acc_hi], axis=1)
    o_ref[0] = x * lax.rsqrt(jnp.mean(x * x, axis=-1, keepdims=True) + EPS) * g_ref[...]


def _final(pos_t, gates, x1, gt2, g, ye_rows):
    bsz, L, d = x1.shape
    tt = 512
    nt = L // tt
    n = bsz * nt
    tok = pl.BlockSpec((1, tt, d), lambda i: (i // nt, i % nt, 0))
    return pl.pallas_call(
        _final_kernel,
        grid=(n,),
        in_specs=[pl.BlockSpec((tt * TOP_K,), lambda i: (0,)),
                  pl.BlockSpec((tt * TOP_K,), lambda i: (jnp.minimum(i + 1, n - 1),)),
                  pl.BlockSpec((tt, TOP_K), lambda i: (i, 0)),
                  tok, pl.BlockSpec((1, 1, d), lambda i: (i // nt, 0, 0)),
                  pl.BlockSpec((1, d), lambda i: (0, 0)),
                  pl.BlockSpec(memory_space=pl.ANY)],
        out_specs=tok,
        out_shape=jax.ShapeDtypeStruct((bsz, L, d), F32),
        scratch_shapes=[pltpu.VMEM((2 * TOP_K * tt, ROW_CHUNKS, LANE), jnp.uint32),
                        pltpu.SMEM((tt * TOP_K,), jnp.int32),
                        pltpu.SemaphoreType.DMA, pltpu.SemaphoreType.DMA((2,))],
        compiler_params=_params("arbitrary"),
        name="final",
    )(pos_t, pos_t, gates, x1, gt2, g.reshape(1, d), ye_rows)


def _block_table(counts):
    counts = counts.astype(jnp.int32)
    padded = (counts + MOE_BLOCK - 1) // MOE_BLOCK * MOE_BLOCK
    pends = jnp.cumsum(padded)
    pstarts = pends - padded
    n_valid = (pends[-1] // MOE_BLOCK).reshape(1)
    return padded, pends, pstarts, n_valid


def kernel(x, c, w_ada, b_ada, g_norm_mix, w_in, s5_lambda_re, s5_lambda_im, s5_log_dt, s5_b_re, s5_b_im, s5_c_re, s5_c_im, s5_d, s5_w_glu, s5_b_glu, s5_norm_g, gla_w_g2, gla_b_g2, gla_norm_g, w_out, g_norm_moe, w_router, router_bias, exp_w_gate, exp_w_up, exp_w_down, sh_w_gate, sh_w_up, sh_w_down, g_final):
    bsz, L, d = x.shape
    T = bsz * L
    assert w_ada.shape[0] == 1, "single-layer block"
    for l in range(1):
        mod = _ada(c, w_ada[l], b_ada[l])
        sh1, sc1, gt1, sh2, sc2, gt2 = [m.reshape(bsz, 1, d) for m in jnp.split(mod, 6, axis=-1)]

        wi = w_in[l]
        o_q, o_k, o_v, o_g, o_r = D_S5, D_S5 + GLA_QK, D_S5 + 2 * GLA_QK, D_S5 + 2 * GLA_QK + D_GLA, \
            D_S5 + 2 * GLA_QK + D_GLA + GLA_LOWRANK
        w_cat = jnp.concatenate([wi[:, :o_g], wi[:, o_r:], wi[:, o_g:o_r],
                                 jnp.zeros((d, LANE - GLA_LOWRANK), wi.dtype)], axis=1).astype(BF16)
        u, q, k, v, r, glr = _inproj(x, g_norm_mix[l], sh1, sc1, w_cat)

        mats = _s5_prep(s5_lambda_re[l], s5_lambda_im[l], s5_log_dt[l], s5_b_re[l], s5_b_im[l],
                        s5_c_re[l], s5_c_im[l], s5_d[l])
        ms5 = _s5(u, mats, s5_w_glu[l].T.astype(BF16), s5_b_glu[l].reshape(D_S5, 1),
                  s5_norm_g[l].reshape(D_S5, 1), w_out[l][:D_S5].astype(BF16))

        wg2 = jnp.concatenate([gla_w_g2[l], jnp.zeros((LANE - GLA_LOWRANK, GLA_QK), F32)], axis=0).astype(BF16)
        yg = _gla(q, k, v, glr, r, wg2, gla_b_g2[l].reshape(1, GLA_QK), gla_norm_g[l].reshape(1, GLA_DV))

        x1, h2p, idx_t, rank_t, gates, counts = _mix(
            ms5, yg, x, w_out[l][D_S5:].astype(BF16), gt1, g_norm_moe[l].reshape(1, d), sh2, sc2, gt2,
            w_router[l].T.astype(BF16), router_bias[l].reshape(N_EXPERTS, 1),
            sh_w_gate[l].astype(BF16), sh_w_up[l].astype(BF16), sh_w_down[l].astype(BF16))

        n_blocks = (T * TOP_K + N_EXPERTS * (MOE_BLOCK - 1) + MOE_BLOCK - 1) // MOE_BLOCK
        n_slots = n_blocks * MOE_BLOCK
        padded, pends, pstarts, n_valid = _block_table(counts[:, 0])
        pos_t = _pos(idx_t, rank_t, pstarts).reshape(-1)
        xs = _dispatch(pends, padded, pos_t, h2p.reshape(T, ROW_CHUNKS, LANE), n_slots)
        ye = _experts(pstarts // MOE_BLOCK, padded // MOE_BLOCK, n_valid,
                      xs.reshape(n_slots * ROW_CHUNKS, LANE), exp_w_gate[l], exp_w_up[l], exp_w_down[l])
    return _final(pos_t, gates, x1, gt2, g_final, ye.reshape(n_slots, ROW_CHUNKS, LANE))
```
